```python
import jax, jax.numpy as jnp
from jax import lax
import numpy as np

D_MODEL = 1024
BATCH = 1
SEQ = 16384
DEPTH = 1

HEAD_DIM = 64
SB_HEADS = 8
SB_WIDTH = SB_HEADS * HEAD_DIM
CONV_GROUPS = 4
CONV_WIDTH_CH = CONV_GROUPS * HEAD_DIM
MEM_HEADS = 4
MEM_WIDTH = MEM_HEADS * HEAD_DIM
MIX_WIDTH = SB_WIDTH + CONV_WIDTH_CH + MEM_WIDTH
N_MEM = 256
CONV_K = 3
SB_BLOCK = 128
EPS = 1e-6

SPLIT_POINTS = [
    SB_WIDTH,
    2 * SB_WIDTH,
    3 * SB_WIDTH,
    3 * SB_WIDTH + CONV_WIDTH_CH,
    3 * SB_WIDTH + 2 * CONV_WIDTH_CH,
    3 * SB_WIDTH + 3 * CONV_WIDTH_CH,
    3 * SB_WIDTH + 3 * CONV_WIDTH_CH + MEM_WIDTH,
]
PROJ_WIDTH = 3 * SB_WIDTH + 3 * CONV_WIDTH_CH + MEM_WIDTH + MIX_WIDTH

kernel_name = "hymba_sbattn_shortconv_memxattn"


def _rmsnorm(x, g):
    xf = x.astype(jnp.float32)
    y = xf * lax.rsqrt(jnp.mean(xf * xf, axis=-1, keepdims=True) + EPS)
    return (y * g.astype(jnp.float32)).astype(x.dtype)


def _stick_breaking_attention(q, k, v):
    b, t, h, dh = q.shape
    nblk = t // SB_BLOCK
    scale = dh ** -0.5
    key_pos = jnp.arange(t)
    q_blocks = q.reshape(b, nblk, SB_BLOCK, h, dh).transpose(1, 0, 2, 3, 4)
    starts = jnp.arange(nblk) * SB_BLOCK

    def block(args):
        q_blk, start = args
        q_pos = start + jnp.arange(SB_BLOCK)
        z = jnp.einsum('bqhd,bkhd->bhqk', q_blk, k,
                       preferred_element_type=jnp.float32) * scale
        before = key_pos[None, :] < q_pos[:, None]
        log_not = jnp.where(before, jax.nn.log_sigmoid(-z), 0.0)
        suffix = lax.cumsum(log_not, axis=3, reverse=True) - log_not
        w = jnp.where(before, jnp.exp(jax.nn.log_sigmoid(z) + suffix), 0.0)
        return jnp.einsum('bhqk,bkhd->bqhd', w.astype(v.dtype), v)

    out = lax.map(block, (q_blocks, starts))
    return out.transpose(1, 0, 2, 3, 4).reshape(b, t, h, dh)


def _causal_depthwise_conv(u, w, bias):
    c = u.shape[-1]
    y = lax.conv_general_dilated(
        u, w[:, None, :].astype(u.dtype), window_strides=(1,),
        padding=[(CONV_K - 1, 0)], dimension_numbers=('NWC', 'WIO', 'NWC'),
        feature_group_count=c)
    return y + bias.astype(u.dtype)


def _memory_attention(q, mk, mv):
    s = jnp.einsum('bthd,bmhd->bhtm', q, mk,
                   preferred_element_type=jnp.float32) * (q.shape[-1] ** -0.5)
    p = jax.nn.softmax(s, axis=-1)
    return jnp.einsum('bhtm,bmhd->bthd', p.astype(mv.dtype), mv)


def setup_inputs(seed: int = 0) -> dict:
    key = jax.random.key(seed)
    ks = jax.random.split(key, 14)
    f32 = jnp.float32
    x = jax.random.normal(ks[0], (BATCH, SEQ, D_MODEL), f32)
    mem = jax.random.normal(ks[1], (BATCH, N_MEM, D_MODEL), f32)
    g_in = 1.0 + 0.02 * jax.random.normal(ks[2], (DEPTH, D_MODEL), f32)
    w_in = jax.random.normal(ks[3], (DEPTH, D_MODEL, PROJ_WIDTH), f32) * D_MODEL ** -0.5
    conv_w = jax.random.normal(ks[4], (DEPTH, CONV_K, CONV_WIDTH_CH), f32) * CONV_K ** -0.5
    conv_b = 0.01 * jax.random.normal(ks[5], (DEPTH, CONV_WIDTH_CH), f32)
    g_mem = 1.0 + 0.02 * jax.random.normal(ks[6], (DEPTH, D_MODEL), f32)
    w_mem_kv = jax.random.normal(ks[7], (DEPTH, D_MODEL, 2 * MEM_WIDTH), f32) * D_MODEL ** -0.5
    g_sb_out = 1.0 + 0.02 * jax.random.normal(ks[8], (DEPTH, SB_WIDTH), f32)
    g_conv_out = 1.0 + 0.02 * jax.random.normal(ks[9], (DEPTH, CONV_WIDTH_CH), f32)
    g_mem_out = 1.0 + 0.02 * jax.random.normal(ks[10], (DEPTH, MEM_WIDTH), f32)
    w_out = jax.random.normal(ks[11], (DEPTH, MIX_WIDTH, D_MODEL), f32) * MIX_WIDTH ** -0.5
    g_final = 1.0 + 0.02 * jax.random.normal(ks[12], (D_MODEL,), f32)
    return {"x": x, "mem": mem, "g_in": g_in, "w_in": w_in, "conv_w": conv_w,
            "conv_b": conv_b, "g_mem": g_mem, "w_mem_kv": w_mem_kv,
            "g_sb_out": g_sb_out, "g_conv_out": g_conv_out, "g_mem_out": g_mem_out,
            "w_out": w_out, "g_final": g_final}


def reference(x, mem, g_in, w_in, conv_w, conv_b, g_mem, w_mem_kv,
              g_sb_out, g_conv_out, g_mem_out, w_out, g_final):
    b, t, _ = x.shape
    for l in range(DEPTH):
        h = _rmsnorm(x, g_in[l])
        proj = h @ w_in[l]
        q_sb, k_sb, v_sb, u_c, b_c, c_c, q_mem, gate = jnp.split(proj, SPLIT_POINTS, axis=-1)

        y_sb = _stick_breaking_attention(
            q_sb.reshape(b, t, SB_HEADS, HEAD_DIM),
            k_sb.reshape(b, t, SB_HEADS, HEAD_DIM),
            v_sb.reshape(b, t, SB_HEADS, HEAD_DIM)).reshape(b, t, SB_WIDTH)

        y_conv = b_c * _causal_depthwise_conv(c_c * u_c, conv_w[l], conv_b[l])

        m = _rmsnorm(mem, g_mem[l])
        mk, mv = jnp.split(m @ w_mem_kv[l], 2, axis=-1)
        nm = mem.shape[1]
        y_mem = _memory_attention(
            q_mem.reshape(b, t, MEM_HEADS, HEAD_DIM),
            mk.reshape(b, nm, MEM_HEADS, HEAD_DIM),
            mv.reshape(b, nm, MEM_HEADS, HEAD_DIM)).reshape(b, t, MEM_WIDTH)

        y = jnp.concatenate([_rmsnorm(y_sb, g_sb_out[l]),
                             _rmsnorm(y_conv, g_conv_out[l]),
                             _rmsnorm(y_mem, g_mem_out[l])], axis=-1)
        x = x + (y * jax.nn.silu(gate)) @ w_out[l]
    return _rmsnorm(x, g_final)
```

```python
import functools

import jax
import jax.numpy as jnp
from jax import lax
from jax.experimental import pallas as pl
from jax.experimental.pallas import tpu as pltpu

HEAD_DIM = 64
SB_HEADS = 8
SB_WIDTH = SB_HEADS * HEAD_DIM
CONV_WIDTH = 4 * HEAD_DIM
MEM_HEADS = 4
MEM_WIDTH = MEM_HEADS * HEAD_DIM
CONV_K = 3
EPS = 1e-6
SCALE = HEAD_DIM ** -0.5

LANES = 128
SUBLANES = 8
PROJ_ROWS = 512
PROJ_COLS = 512
Q_BLOCK = 256
K_BLOCK = 256
SWEEP_STOP = 104.0
VMEM_LIMIT_BYTES = 56 * 1024 * 1024


def _rms_scale(xf, eps=EPS):
    return lax.rsqrt(jnp.mean(xf * xf, axis=-1, keepdims=True) + eps)


def _proj_kernel(x_ref, g_ref, w_ref, q_ref, k_ref, v_ref, rest_ref):
    x = x_ref[...]
    h = (x * _rms_scale(x) * g_ref[...]).astype(jnp.bfloat16)

    def mm(c0, width):
        return jnp.dot(h, w_ref[:, c0:c0 + width], preferred_element_type=jnp.float32)

    c_q, c_k, c_v = 0, SB_WIDTH, 2 * SB_WIDTH
    c_u = 3 * SB_WIDTH
    c_qm = c_u + 3 * CONV_WIDTH
    c_gate = c_qm + MEM_WIDTH
    q_ref[:, :SB_WIDTH] = (mm(c_q, SB_WIDTH) * SCALE).astype(jnp.bfloat16)
    q_ref[:, SB_WIDTH:] = (mm(c_qm, MEM_WIDTH) * SCALE).astype(jnp.bfloat16)
    k_ref[...] = mm(c_k, SB_WIDTH).astype(jnp.bfloat16)
    v_ref[...] = mm(c_v, SB_WIDTH).astype(jnp.bfloat16)
    rest_ref[:, :3 * CONV_WIDTH] = mm(c_u, 3 * CONV_WIDTH)
    for c0 in range(0, 2 * SB_WIDTH, PROJ_COLS):
        rest_ref[:, 3 * CONV_WIDTH + c0:3 * CONV_WIDTH + c0 + PROJ_COLS] = mm(c_gate + c0, PROJ_COLS)


def _project(x2, g_in, w_in_bf16):
    t, d = x2.shape
    mix = SB_WIDTH + CONV_WIDTH + MEM_WIDTH
    rows = min(PROJ_ROWS, t)
    return pl.pallas_call(
        _proj_kernel,
        grid=(t // rows,),
        in_specs=[
            pl.BlockSpec((rows, d), lambda i: (i, 0)),
            pl.BlockSpec((1, d), lambda i: (0, 0)),
            pl.BlockSpec(w_in_bf16.shape, lambda i: (0, 0)),
        ],
        out_specs=[
            pl.BlockSpec((rows, SB_WIDTH + MEM_WIDTH), lambda i: (i, 0)),
            pl.BlockSpec((rows, SB_WIDTH), lambda i: (i, 0)),
            pl.BlockSpec((rows, SB_WIDTH), lambda i: (i, 0)),
            pl.BlockSpec((rows, 3 * CONV_WIDTH + mix), lambda i: (i, 0)),
        ],
        out_shape=[
            jax.ShapeDtypeStruct((t, SB_WIDTH + MEM_WIDTH), jnp.bfloat16),
            jax.ShapeDtypeStruct((t, SB_WIDTH), jnp.bfloat16),
            jax.ShapeDtypeStruct((t, SB_WIDTH), jnp.bfloat16),
            jax.ShapeDtypeStruct((t, 3 * CONV_WIDTH + mix), jnp.float32),
        ],
        compiler_params=pltpu.CompilerParams(
            dimension_semantics=("arbitrary",), vmem_limit_bytes=VMEM_LIMIT_BYTES),
        name="input_projection",
    )(x2, g_in, w_in_bf16)


def _mem_kv_kernel(mem_ref, g_ref, w_ref, kv_ref):
    m = mem_ref[...]
    h = (m * _rms_scale(m) * g_ref[...]).astype(jnp.bfloat16)
    kv_ref[...] = jnp.dot(h, w_ref[...], preferred_element_type=jnp.float32).astype(jnp.bfloat16)


def _memory_kv(mem2, g_mem, w_kv_bf16):
    n_mem = mem2.shape[0]
    return pl.pallas_call(
        _mem_kv_kernel,
        out_shape=jax.ShapeDtypeStruct((n_mem, 2 * MEM_WIDTH), jnp.bfloat16),
        name="memory_kv",
    )(mem2, g_mem, w_kv_bf16)


def _softplus(z):
    return jnp.maximum(z, 0.0) + jnp.log(1.0 + jnp.exp(-jnp.abs(z)))


def _dot_nt(a, b):
    return lax.dot_general(a, b, (((1,), (1,)), ((), ())), preferred_element_type=jnp.float32)


def _suffix_sums(l, tri):
    hi = l.astype(jnp.bfloat16)
    lo = (l - hi.astype(jnp.float32)).astype(jnp.bfloat16)
    return (jnp.dot(hi, tri, preferred_element_type=jnp.float32)
            + jnp.dot(lo, tri, preferred_element_type=jnp.float32))


def _sb_head(qm, k_ref, v_ref, lane0, i, tri, strict):
    q_rows = qm.shape[0]

    def block(j, carry, masked):
        rows = pl.ds(pl.multiple_of(j * K_BLOCK, K_BLOCK), K_BLOCK)
        k_blk = k_ref[rows, lane0:lane0 + LANES]
        v_blk = v_ref[rows, lane0:lane0 + LANES]
        z = _dot_nt(qm, k_blk)
        sp = _softplus(z)
        l = jnp.where(strict, sp, 0.0) if masked else sp
        cs = _suffix_sums(l, tri)
        w = jnp.exp(z - cs - carry)
        if masked:
            w = jnp.where(strict, w, 0.0)
        out = jnp.dot(w.astype(jnp.bfloat16), v_blk, preferred_element_type=jnp.float32)
        return out, carry + cs[:, 0:1]

    acc0, carry0 = block(i, jnp.zeros((q_rows, 1), jnp.float32), True)

    def cond(state):
        j, _, carry = state
        return jnp.logical_and(j >= 0, jnp.min(carry) < SWEEP_STOP)

    def body(state):
        j, acc, carry = state
        out, carry = block(j, carry, False)
        return j - 1, acc + out, carry

    _, acc, _ = lax.while_loop(cond, body, (i - 1, acc0, carry0))
    return acc


def _mem_head(qm, mk, mv):
    s = _dot_nt(qm, mk)
    e = jnp.exp(s - jnp.max(s, axis=-1, keepdims=True))
    num = jnp.dot(e.astype(jnp.bfloat16), mv, preferred_element_type=jnp.float32)
    return num / jnp.sum(e, axis=-1, keepdims=True)


def _mixer_kernel(x_ref, q_ref, k_ref, v_ref, rest_ref, mkv_ref, convw_ref, convb_ref,
                  g_sb_ref, g_conv_ref, g_mem_ref, w_out_ref, g_final_ref,
                  out_ref, cu_ref):
    i = pl.program_id(0)
    q_rows = x_ref.shape[0]
    assert q_rows == K_BLOCK

    lane = lax.broadcasted_iota(jnp.int32, (q_rows, LANES), 1)
    first_half = lane < HEAD_DIM
    row = lax.broadcasted_iota(jnp.int32, (q_rows, K_BLOCK), 0)
    col = lax.broadcasted_iota(jnp.int32, (q_rows, K_BLOCK), 1)
    strict = col < row
    tri = (lax.broadcasted_iota(jnp.int32, (K_BLOCK, K_BLOCK), 0)
           >= lax.broadcasted_iota(jnp.int32, (K_BLOCK, K_BLOCK), 1)).astype(jnp.bfloat16)

    def head_pair(attend, lane0):
        qp = q_ref[:, lane0:lane0 + LANES]
        zero = jnp.zeros_like(qp)
        ya = attend(jnp.where(first_half, qp, zero))
        yb = attend(jnp.where(first_half, zero, qp))
        return jnp.where(first_half, ya, yb)

    y_sb = jnp.concatenate(
        [head_pair(lambda qm, p=p: _sb_head(qm, k_ref, v_ref, p * LANES, i, tri, strict), p * LANES)
         for p in range(SB_WIDTH // LANES)], axis=-1)

    u = rest_ref[:, 0:CONV_WIDTH]
    b = rest_ref[:, CONV_WIDTH:2 * CONV_WIDTH]
    c = rest_ref[:, 2 * CONV_WIDTH:3 * CONV_WIDTH]
    cu = c * u

    @pl.when(i == 0)
    def _():
        cu_ref[0:SUBLANES, :] = jnp.zeros((SUBLANES, CONV_WIDTH), jnp.float32)

    cu_ref[SUBLANES:, :] = cu
    cu_1 = cu_ref[SUBLANES - 1:SUBLANES - 1 + q_rows, :]
    cu_2 = cu_ref[SUBLANES - 2:SUBLANES - 2 + q_rows, :]
    cu_ref[0:SUBLANES, :] = cu[q_rows - SUBLANES:, :]
    conv = (convw_ref[0:1, :] * cu_2 + convw_ref[1:2, :] * cu_1 + convw_ref[2:3, :] * cu
            + convb_ref[...])
    y_conv = b * conv

    y_mem = jnp.concatenate(
        [head_pair(lambda qm, p=p: _mem_head(qm, mkv_ref[:, p * LANES:(p + 1) * LANES],
                                             mkv_ref[:, MEM_WIDTH + p * LANES:MEM_WIDTH + (p + 1) * LANES]),
                   SB_WIDTH + p * LANES)
         for p in range(MEM_WIDTH // LANES)], axis=-1)

    y = jnp.concatenate([y_sb * _rms_scale(y_sb) * g_sb_ref[...],
                         y_conv * _rms_scale(y_conv) * g_conv_ref[...],
                         y_mem * _rms_scale(y_mem) * g_mem_ref[...]], axis=-1)
    gate = rest_ref[:, 3 * CONV_WIDTH:]
    gated = (y * (gate / (1.0 + jnp.exp(-gate)))).astype(jnp.bfloat16)
    res = x_ref[...] + jnp.dot(gated, w_out_ref[...], preferred_element_type=jnp.float32)
    out_ref[...] = res * _rms_scale(res) * g_final_ref[...]


def _mix(x2, q_arr, k_arr, v_arr, rest, mkv, conv_w, conv_b, g_sb, g_conv, g_mem, w_out_bf16, g_final):
    t, d = x2.shape
    assert t % Q_BLOCK == 0

    def rows(width):
        return pl.BlockSpec((Q_BLOCK, width), lambda i: (i, 0))

    def whole(a):
        return pl.BlockSpec(a.shape, lambda i: (0,) * a.ndim)

    resident = pl.BlockSpec(memory_space=pltpu.VMEM)
    return pl.pallas_call(
        _mixer_kernel,
        grid=(t // Q_BLOCK,),
        in_specs=[rows(d), rows(q_arr.shape[1]), resident, resident, rows(rest.shape[1]),
                  whole(mkv), whole(conv_w), whole(conv_b), whole(g_sb), whole(g_conv),
                  whole(g_mem), whole(w_out_bf16), whole(g_final)],
        out_specs=rows(d),
        out_shape=jax.ShapeDtypeStruct((t, d), jnp.float32),
        scratch_shapes=[pltpu.VMEM((SUBLANES + Q_BLOCK, CONV_WIDTH), jnp.float32)],
        compiler_params=pltpu.CompilerParams(
            dimension_semantics=("arbitrary",), vmem_limit_bytes=VMEM_LIMIT_BYTES),
        name="mixer",
    )(x2, q_arr, k_arr, v_arr, rest, mkv, conv_w, conv_b, g_sb, g_conv, g_mem, w_out_bf16, g_final)


def kernel(x, mem, g_in, w_in, conv_w, conv_b, g_mem, w_mem_kv, g_sb_out, g_conv_out,
           g_mem_out, w_out, g_final):
    batch, t, d = x.shape
    assert batch == 1 and g_in.shape[0] == 1
    x2 = x.reshape(t, d)
    q_arr, k_arr, v_arr, rest = _project(x2, g_in[0][None, :], w_in[0].astype(jnp.bfloat16))
    mkv = _memory_kv(mem.reshape(mem.shape[1], d), g_mem[0][None, :], w_mem_kv[0].astype(jnp.bfloat16))
    out = _mix(x2, q_arr, k_arr, v_arr, rest, mkv, conv_w[0], conv_b[0][None, :],
               g_sb_out[0][None, :], g_conv_out[0][None, :], g_mem_out[0][None, :],
               w_out[0].astype(jnp.bfloat16), g_final[None, :])
    return out.reshape(batch, t, d)
```

```python
import jax
import jax.numpy as jnp
from jax import lax
from jax.experimental import pallas as pl
from jax.experimental.pallas import tpu as pltpu

HEAD_DIM = 64
SB_HEADS = 8
SB_WIDTH = SB_HEADS * HEAD_DIM
CONV_WIDTH = 4 * HEAD_DIM
MEM_HEADS = 4
MEM_WIDTH = MEM_HEADS * HEAD_DIM
CONV_K = 3
EPS = 1e-6
SCALE = HEAD_DIM ** -0.5

LANES = 128
SUBLANES = 8
PROJ_ROWS = 512
PROJ_COLS = 512
Q_BLOCK = 256
SB_TILE = 128
SWEEP_STOP = 104.0
VMEM_LIMIT_BYTES = 56 * 1024 * 1024


def _rms_scale(xf, eps=EPS):
    return lax.rsqrt(jnp.mean(xf * xf, axis=-1, keepdims=True) + eps)


def _proj_kernel(x_ref, g_ref, w_ref, q_ref, k_ref, v_ref, rest_ref):
    x = x_ref[...]
    h = (x * _rms_scale(x) * g_ref[...]).astype(jnp.bfloat16)

    def mm(c0, width):
        return jnp.dot(h, w_ref[:, c0:c0 + width], preferred_element_type=jnp.float32)

    c_q, c_k, c_v = 0, SB_WIDTH, 2 * SB_WIDTH
    c_u = 3 * SB_WIDTH
    c_qm = c_u + 3 * CONV_WIDTH
    c_gate = c_qm + MEM_WIDTH
    q_ref[:, :SB_WIDTH] = (mm(c_q, SB_WIDTH) * SCALE).astype(jnp.bfloat16)
    q_ref[:, SB_WIDTH:] = (mm(c_qm, MEM_WIDTH) * SCALE).astype(jnp.bfloat16)
    k_ref[...] = mm(c_k, SB_WIDTH).astype(jnp.bfloat16)
    v_ref[...] = mm(c_v, SB_WIDTH).astype(jnp.bfloat16)
    rest_ref[:, :3 * CONV_WIDTH] = mm(c_u, 3 * CONV_WIDTH)
    for c0 in range(0, 2 * SB_WIDTH, PROJ_COLS):
        rest_ref[:, 3 * CONV_WIDTH + c0:3 * CONV_WIDTH + c0 + PROJ_COLS] = mm(c_gate + c0, PROJ_COLS)


def _project(x2, g_in, w_in_bf16):
    t, d = x2.shape
    mix = SB_WIDTH + CONV_WIDTH + MEM_WIDTH
    rows = min(PROJ_ROWS, t)
    return pl.pallas_call(
        _proj_kernel,
        grid=(t // rows,),
        in_specs=[
            pl.BlockSpec((rows, d), lambda i: (i, 0)),
            pl.BlockSpec((1, d), lambda i: (0, 0)),
            pl.BlockSpec(w_in_bf16.shape, lambda i: (0, 0)),
        ],
        out_specs=[
            pl.BlockSpec((rows, SB_WIDTH + MEM_WIDTH), lambda i: (i, 0)),
            pl.BlockSpec((rows, SB_WIDTH), lambda i: (i, 0)),
            pl.BlockSpec((rows, SB_WIDTH), lambda i: (i, 0)),
            pl.BlockSpec((rows, 3 * CONV_WIDTH + mix), lambda i: (i, 0)),
        ],
        out_shape=[
            jax.ShapeDtypeStruct((t, SB_WIDTH + MEM_WIDTH), jnp.bfloat16),
            jax.ShapeDtypeStruct((t, SB_WIDTH), jnp.bfloat16),
            jax.ShapeDtypeStruct((t, SB_WIDTH), jnp.bfloat16),
            jax.ShapeDtypeStruct((t, 3 * CONV_WIDTH + mix), jnp.float32),
        ],
        compiler_params=pltpu.CompilerParams(
            dimension_semantics=("arbitrary",), vmem_limit_bytes=VMEM_LIMIT_BYTES),
        name="input_projection",
    )(x2, g_in, w_in_bf16)


def _mem_kv_kernel(mem_ref, g_ref, w_ref, kv_ref):
    m = mem_ref[...]
    h = (m * _rms_scale(m) * g_ref[...]).astype(jnp.bfloat16)
    kv_ref[...] = jnp.dot(h, w_ref[...], preferred_element_type=jnp.float32).astype(jnp.bfloat16)


def _memory_kv(mem2, g_mem, w_kv_bf16):
    n_mem = mem2.shape[0]
    return pl.pallas_call(
        _mem_kv_kernel,
        out_shape=jax.ShapeDtypeStruct((n_mem, 2 * MEM_WIDTH), jnp.bfloat16),
        name="memory_kv",
    )(mem2, g_mem, w_kv_bf16)


def _softplus(z):
    return jnp.maximum(z, 0.0) + jnp.log(1.0 + jnp.exp(-jnp.abs(z)))


def _dot_nt(a, b):
    return lax.dot_general(a, b, (((1,), (1,)), ((), ())), preferred_element_type=jnp.float32)


def _split_heads(a, first_half):
    zero = jnp.zeros_like(a)
    return jnp.concatenate([jnp.where(first_half, a, zero), jnp.where(first_half, zero, a)], axis=0)


def _sb_sweep(i, q_ref, k_ref, v_ref, acc_ref, carry_ref):
    n_sub = q_ref.shape[0] // SB_TILE
    n_pair = SB_WIDTH // LANES
    chains = [(sub, p) for sub in range(n_sub) for p in range(n_pair)]

    lane = lax.broadcasted_iota(jnp.int32, (SB_TILE, LANES), 1)
    first_half = lane < HEAD_DIM
    r2 = lax.broadcasted_iota(jnp.int32, (2 * SB_TILE, SB_TILE), 0)
    c2 = lax.broadcasted_iota(jnp.int32, (2 * SB_TILE, SB_TILE), 1)
    strict = c2 < jnp.where(r2 >= SB_TILE, r2 - SB_TILE, r2)
    rt = lax.broadcasted_iota(jnp.int32, (2 * SB_TILE, 2 * SB_TILE), 0)
    ct = lax.broadcasted_iota(jnp.int32, (2 * SB_TILE, 2 * SB_TILE), 1)
    tri_ext = jnp.logical_or(ct >= SB_TILE, jnp.where(rt >= SB_TILE, rt - SB_TILE, rt) >= ct
                             ).astype(jnp.bfloat16)

    q_stack = [_split_heads(q_ref[sub * SB_TILE:(sub + 1) * SB_TILE, p * LANES:(p + 1) * LANES],
                            first_half) for sub, p in chains]

    def sweep(m, masked):
        rows = []
        for sub, _ in chains:
            jb = n_sub * i + sub - m
            rows.append((jb, pl.ds(pl.multiple_of(jnp.maximum(jb, 0) * SB_TILE, SB_TILE), SB_TILE)))
        zs = [_dot_nt(q_stack[c], k_ref[rows[c][1], p * LANES:(p + 1) * LANES])
              for c, (_, p) in enumerate(chains)]
        hls = []
        for z in zs:
            sp = _softplus(z)
            l = jnp.where(strict, sp, 0.0) if masked else sp
            hi = l.astype(jnp.bfloat16)
            lo = (l - hi.astype(jnp.float32)).astype(jnp.bfloat16)
            hls.append(jnp.concatenate([hi, lo], axis=1))
        cs_all = jnp.dot(jnp.concatenate(hls, axis=0), tri_ext,
                         preferred_element_type=jnp.float32)
        floor = None
        w_cats = []
        for c, z in enumerate(zs):
            cs = cs_all[c * 2 * SB_TILE:(c + 1) * 2 * SB_TILE]
            if masked:
                s, carry = cs[:, :SB_TILE], cs[:, SB_TILE:]
            else:
                old = carry_ref[c]
                s, carry = cs[:, :SB_TILE] + old, cs[:, SB_TILE:] + old
            carry_ref[c] = carry
            floor = carry if floor is None else jnp.minimum(floor, carry)
            w = jnp.exp(z - s)
            if masked:
                w = jnp.where(strict, w, 0.0)
            wb = w.astype(jnp.bfloat16)
            w_cats.append(jnp.concatenate([wb[:SB_TILE], wb[SB_TILE:]], axis=1))
        for c, (sub, p) in enumerate(chains):
            v_blk = v_ref[rows[c][1], p * LANES:(p + 1) * LANES]
            if (not masked) and sub < n_sub - 1:
                v_blk = jnp.where(rows[c][0] >= 0, v_blk, jnp.zeros_like(v_blk))
            out = jnp.dot(w_cats[c], _split_heads(v_blk, first_half),
                          preferred_element_type=jnp.float32)
            if masked:
                acc_ref[c] = out
            else:
                acc_ref[c] += out
        return jnp.min(floor) >= SWEEP_STOP

    done0 = sweep(0, True)
    last = n_sub * i + n_sub - 1

    def cond(state):
        m, done = state
        return jnp.logical_and(m <= last, jnp.logical_not(done))

    def body(state):
        m, _ = state
        return m + 1, sweep(m, False)

    lax.while_loop(cond, body, (jnp.int32(1), done0))
    return jnp.concatenate(
        [jnp.concatenate([acc_ref[sub * n_pair + p] for p in range(n_pair)], axis=1)
         for sub in range(n_sub)], axis=0)


def _mem_attention(q_ref, mkv_ref):
    q_rows = q_ref.shape[0]
    n_mem = mkv_ref.shape[0]
    first_half = lax.broadcasted_iota(jnp.int32, (q_rows, LANES), 1) < HEAD_DIM
    first_half_m = lax.broadcasted_iota(jnp.int32, (n_mem, LANES), 1) < HEAD_DIM
    head_ones = ((lax.broadcasted_iota(jnp.int32, (2 * n_mem, LANES), 1) < HEAD_DIM)
                 == (lax.broadcasted_iota(jnp.int32, (2 * n_mem, LANES), 0) < n_mem)
                 ).astype(jnp.float32).astype(jnp.bfloat16)
    outs = []
    for p in range(MEM_WIDTH // LANES):
        qs = _split_heads(q_ref[:, SB_WIDTH + p * LANES:SB_WIDTH + (p + 1) * LANES], first_half)
        mk = mkv_ref[:, p * LANES:(p + 1) * LANES]
        mv = mkv_ref[:, MEM_WIDTH + p * LANES:MEM_WIDTH + (p + 1) * LANES]
        s = _dot_nt(qs, mk)
        e = jnp.exp(s - jnp.max(s, axis=-1, keepdims=True)).astype(jnp.bfloat16)
        e_cat = jnp.concatenate([e[:q_rows], e[q_rows:]], axis=1)
        num = jnp.dot(e_cat, _split_heads(mv, first_half_m), preferred_element_type=jnp.float32)
        den = jnp.dot(e_cat, head_ones, preferred_element_type=jnp.float32)
        outs.append(num / den)
    return jnp.concatenate(outs, axis=-1)


def _mixer_kernel(x_ref, q_ref, k_ref, v_ref, rest_ref, mkv_ref, convw_ref, convb_ref,
                  g_sb_ref, g_conv_ref, g_mem_ref, w_out_ref, g_final_ref,
                  out_ref, cu_ref, acc_ref, carry_ref):
    i = pl.program_id(0)
    q_rows = x_ref.shape[0]

    y_sb = _sb_sweep(i, q_ref, k_ref, v_ref, acc_ref, carry_ref)

    u = rest_ref[:, 0:CONV_WIDTH]
    b = rest_ref[:, CONV_WIDTH:2 * CONV_WIDTH]
    c = rest_ref[:, 2 * CONV_WIDTH:3 * CONV_WIDTH]
    cu = c * u

    @pl.when(i == 0)
    def _():
        cu_ref[0:SUBLANES, :] = jnp.zeros((SUBLANES, CONV_WIDTH), jnp.float32)

    cu_ref[SUBLANES:, :] = cu
    cu_1 = cu_ref[SUBLANES - 1:SUBLANES - 1 + q_rows, :]
    cu_2 = cu_ref[SUBLANES - 2:SUBLANES - 2 + q_rows, :]
    cu_ref[0:SUBLANES, :] = cu[q_rows - SUBLANES:, :]
    conv = (convw_ref[0:1, :] * cu_2 + convw_ref[1:2, :] * cu_1 + convw_ref[2:3, :] * cu
            + convb_ref[...])
    y_conv = b * conv

    y_mem = _mem_attention(q_ref, mkv_ref)

    y = jnp.concatenate([y_sb * _rms_scale(y_sb) * g_sb_ref[...],
                         y_conv * _rms_scale(y_conv) * g_conv_ref[...],
                         y_mem * _rms_scale(y_mem) * g_mem_ref[...]], axis=-1)
    gate = rest_ref[:, 3 * CONV_WIDTH:]
    gated = (y * (gate / (1.0 + jnp.exp(-gate)))).astype(jnp.bfloat16)
    res = x_ref[...] + jnp.dot(gated, w_out_ref[...], preferred_element_type=jnp.float32)
    out_ref[...] = res * _rms_scale(res) * g_final_ref[...]


def _mix(x2, q_arr, k_arr, v_arr, rest, mkv, conv_w, conv_b, g_sb, g_conv, g_mem, w_out_bf16, g_final):
    t, d = x2.shape
    assert t % Q_BLOCK == 0 and Q_BLOCK % SB_TILE == 0
    n_chain = (Q_BLOCK // SB_TILE) * (SB_WIDTH // LANES)

    def rows(width):
        return pl.BlockSpec((Q_BLOCK, width), lambda i: (i, 0))

    def whole(a):
        return pl.BlockSpec(a.shape, lambda i: (0,) * a.ndim)

    resident = pl.BlockSpec(memory_space=pltpu.VMEM)
    return pl.pallas_call(
        _mixer_kernel,
        grid=(t // Q_BLOCK,),
        in_specs=[rows(d), rows(q_arr.shape[1]), resident, resident, rows(rest.shape[1]),
                  whole(mkv), whole(conv_w), whole(conv_b), whole(g_sb), whole(g_conv),
                  whole(g_mem), whole(w_out_bf16), whole(g_final)],
        out_specs=rows(d),
        out_shape=jax.ShapeDtypeStruct((t, d), jnp.float32),
        scratch_shapes=[pltpu.VMEM((SUBLANES + Q_BLOCK, CONV_WIDTH), jnp.float32),
                        pltpu.VMEM((n_chain, SB_TILE, LANES), jnp.float32),
                        pltpu.VMEM((n_chain, 2 * SB_TILE, LANES), jnp.float32)],
        compiler_params=pltpu.CompilerParams(
            dimension_semantics=("arbitrary",), vmem_limit_bytes=VMEM_LIMIT_BYTES),
        name="mixer",
    )(x2, q_arr, k_arr, v_arr, rest, mkv, conv_w, conv_b, g_sb, g_conv, g_mem, w_out_bf16, g_final)


def kernel(x, mem, g_in, w_in, conv_w, conv_b, g_mem, w_mem_kv, g_sb_out, g_conv_out,
           g_mem_out, w_out, g_final):
    batch, t, d = x.shape
    assert batch == 1 and g_in.shape[0] == 1
    x2 = x.reshape(t, d)
    q_arr, k_arr, v_arr, rest = _project(x2, g_in[0][None, :], w_in[0].astype(jnp.bfloat16))
    mkv = _memory_kv(mem.reshape(mem.shape[1], d), g_mem[0][None, :], w_mem_kv[0].astype(jnp.bfloat16))
    out = _mix(x2, q_arr, k_arr, v_arr, rest, mkv, conv_w[0], conv_b[0][None, :],
               g_sb_out[0][None, :], g_conv_out[0][None, :], g_mem_out[0][None, :],
               w_out[0].astype(jnp.bfloat16), g_final[None, :])
    return out.reshape(batch, t, d)
```

```python
import jax
import jax.numpy as jnp
from jax import lax
from jax.experimental import pallas as pl
from jax.experimental.pallas import tpu as pltpu

HEAD_DIM = 64
SB_HEADS = 8
SB_WIDTH = SB_HEADS * HEAD_DIM
CONV_WIDTH = 4 * HEAD_DIM
MEM_HEADS = 4
MEM_WIDTH = MEM_HEADS * HEAD_DIM
CONV_K = 3
EPS = 1e-6
SCALE = HEAD_DIM ** -0.5

LANES = 128
SUBLANES = 8
PROJ_ROWS = 512
PROJ_COLS = 512
Q_BLOCK = 256
SB_TILE = 128
SWEEP_STOP = 104.0
PROLOGUE_TILES = 3
LOG2_E = 1.4426950408889634
LOGIT2_CLAMP = 126.0
VMEM_LIMIT_BYTES = 56 * 1024 * 1024


def _rms_scale(xf, eps=EPS):
    return lax.rsqrt(jnp.mean(xf * xf, axis=-1, keepdims=True) + eps)


def _proj_kernel(x_ref, g_ref, w_ref, q_ref, k_ref, v_ref, rest_ref):
    x = x_ref[...]
    h = (x * _rms_scale(x) * g_ref[...]).astype(jnp.bfloat16)

    def mm(c0, width):
        return jnp.dot(h, w_ref[:, c0:c0 + width], preferred_element_type=jnp.float32)

    c_q, c_k, c_v = 0, SB_WIDTH, 2 * SB_WIDTH
    c_u = 3 * SB_WIDTH
    c_qm = c_u + 3 * CONV_WIDTH
    c_gate = c_qm + MEM_WIDTH
    q_ref[:, :SB_WIDTH] = (mm(c_q, SB_WIDTH) * SCALE).astype(jnp.bfloat16)
    q_ref[:, SB_WIDTH:] = (mm(c_qm, MEM_WIDTH) * SCALE).astype(jnp.bfloat16)
    k_ref[...] = mm(c_k, SB_WIDTH).astype(jnp.bfloat16)
    v_ref[...] = mm(c_v, SB_WIDTH).astype(jnp.bfloat16)
    rest_ref[:, :3 * CONV_WIDTH] = mm(c_u, 3 * CONV_WIDTH)
    for c0 in range(0, 2 * SB_WIDTH, PROJ_COLS):
        rest_ref[:, 3 * CONV_WIDTH + c0:3 * CONV_WIDTH + c0 + PROJ_COLS] = mm(c_gate + c0, PROJ_COLS)


def _project(x2, g_in, w_in_bf16):
    t, d = x2.shape
    mix = SB_WIDTH + CONV_WIDTH + MEM_WIDTH
    rows = min(PROJ_ROWS, t)
    return pl.pallas_call(
        _proj_kernel,
        grid=(t // rows,),
        in_specs=[
            pl.BlockSpec((rows, d), lambda i: (i, 0)),
            pl.BlockSpec((1, d), lambda i: (0, 0)),
            pl.BlockSpec(w_in_bf16.shape, lambda i: (0, 0)),
        ],
        out_specs=[
            pl.BlockSpec((rows, SB_WIDTH + MEM_WIDTH), lambda i: (i, 0)),
            pl.BlockSpec((rows, SB_WIDTH), lambda i: (i, 0)),
            pl.BlockSpec((rows, SB_WIDTH), lambda i: (i, 0)),
            pl.BlockSpec((rows, 3 * CONV_WIDTH + mix), lambda i: (i, 0)),
        ],
        out_shape=[
            jax.ShapeDtypeStruct((t, SB_WIDTH + MEM_WIDTH), jnp.bfloat16),
            jax.ShapeDtypeStruct((t, SB_WIDTH), jnp.bfloat16),
            jax.ShapeDtypeStruct((t, SB_WIDTH), jnp.bfloat16),
            jax.ShapeDtypeStruct((t, 3 * CONV_WIDTH + mix), jnp.float32),
        ],
        compiler_params=pltpu.CompilerParams(
            dimension_semantics=("arbitrary",), vmem_limit_bytes=VMEM_LIMIT_BYTES),
        name="input_projection",
    )(x2, g_in, w_in_bf16)


def _mem_kv_kernel(mem_ref, g_ref, w_ref, kv_ref):
    m = mem_ref[...]
    h = (m * _rms_scale(m) * g_ref[...]).astype(jnp.bfloat16)
    kv_ref[...] = jnp.dot(h, w_ref[...], preferred_element_type=jnp.float32).astype(jnp.bfloat16)


def _memory_kv(mem2, g_mem, w_kv_bf16):
    n_mem = mem2.shape[0]
    return pl.pallas_call(
        _mem_kv_kernel,
        out_shape=jax.ShapeDtypeStruct((n_mem, 2 * MEM_WIDTH), jnp.bfloat16),
        name="memory_kv",
    )(mem2, g_mem, w_kv_bf16)


def _dot_nt(a, b):
    return lax.dot_general(a, b, (((1,), (1,)), ((), ())), preferred_element_type=jnp.float32)


def _split_heads(a, first_half):
    zero = jnp.zeros_like(a)
    return jnp.concatenate([jnp.where(first_half, a, zero), jnp.where(first_half, zero, a)], axis=0)


def _sb_sweep(i, q_ref, k_ref, v_ref, acc_ref, carry_ref):
    n_sub = q_ref.shape[0] // SB_TILE
    n_pair = SB_WIDTH // LANES
    chains = [(sub, p) for sub in range(n_sub) for p in range(n_pair)]

    lane = lax.broadcasted_iota(jnp.int32, (SB_TILE, LANES), 1)
    first_half = lane < HEAD_DIM
    r2 = lax.broadcasted_iota(jnp.int32, (2 * SB_TILE, SB_TILE), 0)
    c2 = lax.broadcasted_iota(jnp.int32, (2 * SB_TILE, SB_TILE), 1)
    strict = c2 < jnp.where(r2 >= SB_TILE, r2 - SB_TILE, r2)
    rt = lax.broadcasted_iota(jnp.int32, (2 * SB_TILE, 2 * SB_TILE), 0)
    ct = lax.broadcasted_iota(jnp.int32, (2 * SB_TILE, 2 * SB_TILE), 1)
    c_hi = jnp.full((1, 1), LOG2_E, jnp.float32).astype(jnp.bfloat16).astype(jnp.float32)
    c_lo = (LOG2_E - c_hi).astype(jnp.bfloat16).astype(jnp.float32)
    tri_ext = jnp.where(jnp.logical_or(ct >= SB_TILE, jnp.where(rt >= SB_TILE, rt - SB_TILE, rt) >= ct),
                        jnp.where(rt >= SB_TILE, c_lo, c_hi), 0.0).astype(jnp.bfloat16)

    q_stack = [_split_heads(q_ref[sub * SB_TILE:(sub + 1) * SB_TILE, p * LANES:(p + 1) * LANES],
                            first_half) for sub, p in chains]

    def sweep(ms, from_diagonal):
        tiles = []
        for t, m in enumerate(ms):
            for c, (sub, p) in enumerate(chains):
                jb = n_sub * i + sub - m
                diag = from_diagonal and t == 0
                may_underrun = not diag and (from_diagonal or sub < n_sub - 1)
                tiles.append(dict(c=c, p=p, diag=diag, valid=(jb >= 0) if may_underrun else None,
                                  rows=pl.ds(pl.multiple_of(jnp.maximum(jb, 0) * SB_TILE, SB_TILE), SB_TILE)))
        for tl in tiles:
            k_blk = k_ref[tl["rows"], tl["p"] * LANES:(tl["p"] + 1) * LANES]
            tl["z"] = jnp.minimum(_dot_nt(q_stack[tl["c"]], k_blk) * LOG2_E, LOGIT2_CLAMP)
        ls = []
        for tl in tiles:
            sp = jnp.log(1.0 + jnp.exp2(tl["z"]))
            sp = (jnp.where(strict, sp, 0.0) if tl["diag"] else sp).astype(jnp.bfloat16)
            ls.append(jnp.concatenate([sp, sp], axis=1))
        cs_all = jnp.dot(jnp.concatenate(ls, axis=0), tri_ext,
                         preferred_element_type=jnp.float32)
        run = [None if from_diagonal else carry_ref[c] for c in range(len(chains))]
        for n, tl in enumerate(tiles):
            cs = cs_all[n * 2 * SB_TILE:(n + 1) * 2 * SB_TILE]
            s, total = cs[:, :SB_TILE], cs[:, SB_TILE:]
            old = run[tl["c"]]
            if old is not None:
                s, total = s + old, total + old
            run[tl["c"]] = total
            w = jnp.exp2(tl["z"] - s)
            if tl["diag"]:
                w = jnp.where(strict, w, 0.0)
            wb = w.astype(jnp.bfloat16)
            tl["w"] = jnp.concatenate([wb[:SB_TILE], wb[SB_TILE:]], axis=1)
        floor = None
        for c, (sub, p) in enumerate(chains):
            carry_ref[c] = run[c]
            floor = run[c] if floor is None else jnp.minimum(floor, run[c])
            mine = [tl for tl in tiles if tl["c"] == c]
            vs = []
            for tl in mine:
                v_blk = v_ref[tl["rows"], p * LANES:(p + 1) * LANES]
                if tl["valid"] is not None:
                    v_blk = jnp.where(tl["valid"], v_blk, jnp.zeros_like(v_blk))
                vs.append(_split_heads(v_blk, first_half))
            out = jnp.dot(jnp.concatenate([tl["w"] for tl in mine], axis=1), jnp.concatenate(vs, axis=0),
                          preferred_element_type=jnp.float32)
            if from_diagonal:
                acc_ref[c] = out
            else:
                acc_ref[c] += out
        return jnp.min(floor) >= SWEEP_STOP * LOG2_E

    done0 = sweep(list(range(PROLOGUE_TILES)), True)
    last = n_sub * i + n_sub - 1

    def cond(state):
        m, done = state
        return jnp.logical_and(m <= last, jnp.logical_not(done))

    def body(state):
        m, _ = state
        return m + 1, sweep([m], False)

    lax.while_loop(cond, body, (jnp.int32(PROLOGUE_TILES), done0))
    return jnp.concatenate(
        [jnp.concatenate([acc_ref[sub * n_pair + p] for p in range(n_pair)], axis=1)
         for sub in range(n_sub)], axis=0)


def _mem_attention(q_ref, mkv_ref):
    q_rows = q_ref.shape[0]
    n_mem = mkv_ref.shape[0]
    first_half = lax.broadcasted_iota(jnp.int32, (q_rows, LANES), 1) < HEAD_DIM
    first_half_m = lax.broadcasted_iota(jnp.int32, (n_mem, LANES), 1) < HEAD_DIM
    head_ones = ((lax.broadcasted_iota(jnp.int32, (2 * n_mem, LANES), 1) < HEAD_DIM)
                 == (lax.broadcasted_iota(jnp.int32, (2 * n_mem, LANES), 0) < n_mem)
                 ).astype(jnp.float32).astype(jnp.bfloat16)
    outs = []
    for p in range(MEM_WIDTH // LANES):
        qs = _split_heads(q_ref[:, SB_WIDTH + p * LANES:SB_WIDTH + (p + 1) * LANES], first_half)
        mk = mkv_ref[:, p * LANES:(p + 1) * LANES]
        mv = mkv_ref[:, MEM_WIDTH + p * LANES:MEM_WIDTH + (p + 1) * LANES]
        s = _dot_nt(qs, mk)
        e = jnp.exp(s - jnp.max(s, axis=-1, keepdims=True)).astype(jnp.bfloat16)
        e_cat = jnp.concatenate([e[:q_rows], e[q_rows:]], axis=1)
        num = jnp.dot(e_cat, _split_heads(mv, first_half_m), preferred_element_type=jnp.float32)
        den = jnp.dot(e_cat, head_ones, preferred_element_type=jnp.float32)
        outs.append(num / den)
    return jnp.concatenate(outs, axis=-1)


def _mixer_kernel(x_ref, q_ref, k_ref, v_ref, rest_ref, mkv_ref, convw_ref, convb_ref,
                  g_sb_ref, g_conv_ref, g_mem_ref, w_out_ref, g_final_ref,
                  out_ref, cu_ref, acc_ref, carry_ref):
    i = pl.program_id(0)
    q_rows = x_ref.shape[0]

    y_sb = _sb_sweep(i, q_ref, k_ref, v_ref, acc_ref, carry_ref)

    u = rest_ref[:, 0:CONV_WIDTH]
    b = rest_ref[:, CONV_WIDTH:2 * CONV_WIDTH]
    c = rest_ref[:, 2 * CONV_WIDTH:3 * CONV_WIDTH]
    cu = c * u

    @pl.when(i == 0)
    def _():
        cu_ref[0:SUBLANES, :] = jnp.zeros((SUBLANES, CONV_WIDTH), jnp.float32)

    cu_ref[SUBLANES:, :] = cu
    cu_1 = cu_ref[SUBLANES - 1:SUBLANES - 1 + q_rows, :]
    cu_2 = cu_ref[SUBLANES - 2:SUBLANES - 2 + q_rows, :]
    cu_ref[0:SUBLANES, :] = cu[q_rows - SUBLANES:, :]
    conv = (convw_ref[0:1, :] * cu_2 + convw_ref[1:2, :] * cu_1 + convw_ref[2:3, :] * cu
            + convb_ref[...])
    y_conv = b * conv

    y_mem = _mem_attention(q_ref, mkv_ref)

    y = jnp.concatenate([y_sb * _rms_scale(y_sb) * g_sb_ref[...],
                         y_conv * _rms_scale(y_conv) * g_conv_ref[...],
                         y_mem * _rms_scale(y_mem) * g_mem_ref[...]], axis=-1)
    gate = rest_ref[:, 3 * CONV_WIDTH:]
    gated = (y * (gate / (1.0 + jnp.exp(-gate)))).astype(jnp.bfloat16)
    res = x_ref[...] + jnp.dot(gated, w_out_ref[...], preferred_element_type=jnp.float32)
    out_ref[...] = res * _rms_scale(res) * g_final_ref[...]


def _mix(x2, q_arr, k_arr, v_arr, rest, mkv, conv_w, conv_b, g_sb, g_conv, g_mem, w_out_bf16, g_final):
    t, d = x2.shape
    assert t % Q_BLOCK == 0 and Q_BLOCK % SB_TILE == 0
    n_chain = (Q_BLOCK // SB_TILE) * (SB_WIDTH // LANES)

    def rows(width):
        return pl.BlockSpec((Q_BLOCK, width), lambda i: (i, 0))

    def whole(a):
        return pl.BlockSpec(a.shape, lambda i: (0,) * a.ndim)

    resident = pl.BlockSpec(memory_space=pltpu.VMEM)
    return pl.pallas_call(
        _mixer_kernel,
        grid=(t // Q_BLOCK,),
        in_specs=[rows(d), rows(q_arr.shape[1]), resident, resident, rows(rest.shape[1]),
                  whole(mkv), whole(conv_w), whole(conv_b), whole(g_sb), whole(g_conv),
                  whole(g_mem), whole(w_out_bf16), whole(g_final)],
        out_specs=rows(d),
        out_shape=jax.ShapeDtypeStruct((t, d), jnp.float32),
        scratch_shapes=[pltpu.VMEM((SUBLANES + Q_BLOCK, CONV_WIDTH), jnp.float32),
                        pltpu.VMEM((n_chain, SB_TILE, LANES), jnp.float32),
                        pltpu.VMEM((n_chain, 2 * SB_TILE, LANES), jnp.float32)],
        compiler_params=pltpu.CompilerParams(
            dimension_semantics=("arbitrary",), vmem_limit_bytes=VMEM_LIMIT_BYTES),
        name="mixer",
    )(x2, q_arr, k_arr, v_arr, rest, mkv, conv_w, conv_b, g_sb, g_conv, g_mem, w_out_bf16, g_final)


def kernel(x, mem, g_in, w_in, conv_w, conv_b, g_mem, w_mem_kv, g_sb_out, g_conv_out,
           g_mem_out, w_out, g_final):
    batch, t, d = x.shape
    assert batch == 1 and g_in.shape[0] == 1
    x2 = x.reshape(t, d)
    q_arr, k_arr, v_arr, rest = _project(x2, g_in[0][None, :], w_in[0].astype(jnp.bfloat16))
    mkv = _memory_kv(mem.reshape(mem.shape[1], d), g_mem[0][None, :], w_mem_kv[0].astype(jnp.bfloat16))
    out = _mix(x2, q_arr, k_arr, v_arr, rest, mkv, conv_w[0], conv_b[0][None, :],
               g_sb_out[0][None, :], g_conv_out[0][None, :], g_mem_out[0][None, :],
               w_out[0].astype(jnp.bfloat16), g_final[None, :])
    return out.reshape(batch, t, d)
```

```python
import jax
import jax.numpy as jnp
from jax import lax
from jax.experimental import pallas as pl
from jax.experimental.pallas import tpu as pltpu

HEAD_DIM = 64
SB_HEADS = 8
SB_WIDTH = SB_HEADS * HEAD_DIM
CONV_WIDTH = 4 * HEAD_DIM
MEM_HEADS = 4
MEM_WIDTH = MEM_HEADS * HEAD_DIM
CONV_K = 3
EPS = 1e-6
SCALE = HEAD_DIM ** -0.5

LANES = 128
SUBLANES = 8
MIX_WIDTH = SB_WIDTH + CONV_WIDTH + MEM_WIDTH
PROJ_COLS = 512
Q_BLOCK = 256
SB_TILE = 128
SWEEP_STOP = 104.0
PROLOGUE_TILES = 3
LOG2_E = 1.4426950408889634
LOGIT2_CLAMP = 126.0
VMEM_LIMIT_BYTES = 60 * 1024 * 1024


def _rms_scale(xf, eps=EPS):
    return lax.rsqrt(jnp.mean(xf * xf, axis=-1, keepdims=True) + eps)


def _mem_kv_kernel(mem_ref, g_ref, w_ref, kv_ref):
    m = mem_ref[...]
    h = (m * _rms_scale(m) * g_ref[...]).astype(jnp.bfloat16)
    kv_ref[...] = jnp.dot(h, w_ref[...], preferred_element_type=jnp.float32).astype(jnp.bfloat16)


def _memory_kv(mem2, g_mem, w_kv_bf16):
    n_mem = mem2.shape[0]
    return pl.pallas_call(
        _mem_kv_kernel,
        out_shape=jax.ShapeDtypeStruct((n_mem, 2 * MEM_WIDTH), jnp.bfloat16),
        name="memory_kv",
    )(mem2, g_mem, w_kv_bf16)


def _dot_nt(a, b):
    return lax.dot_general(a, b, (((1,), (1,)), ((), ())), preferred_element_type=jnp.float32)


def _split_heads(a, first_half):
    zero = jnp.zeros_like(a)
    return jnp.concatenate([jnp.where(first_half, a, zero), jnp.where(first_half, zero, a)], axis=0)


def _sb_sweep(i, q_ref, k_ref, v_ref, acc_ref, carry_ref):
    n_sub = q_ref.shape[0] // SB_TILE
    n_pair = SB_WIDTH // LANES
    chains = [(sub, p) for sub in range(n_sub) for p in range(n_pair)]

    lane = lax.broadcasted_iota(jnp.int32, (SB_TILE, LANES), 1)
    first_half = lane < HEAD_DIM
    r2 = lax.broadcasted_iota(jnp.int32, (2 * SB_TILE, SB_TILE), 0)
    c2 = lax.broadcasted_iota(jnp.int32, (2 * SB_TILE, SB_TILE), 1)
    strict = c2 < jnp.where(r2 >= SB_TILE, r2 - SB_TILE, r2)
    rt = lax.broadcasted_iota(jnp.int32, (2 * SB_TILE, 2 * SB_TILE), 0)
    ct = lax.broadcasted_iota(jnp.int32, (2 * SB_TILE, 2 * SB_TILE), 1)
    c_hi = jnp.full((1, 1), LOG2_E, jnp.float32).astype(jnp.bfloat16).astype(jnp.float32)
    c_lo = (LOG2_E - c_hi).astype(jnp.bfloat16).astype(jnp.float32)
    tri_ext = jnp.where(jnp.logical_or(ct >= SB_TILE, jnp.where(rt >= SB_TILE, rt - SB_TILE, rt) >= ct),
                        jnp.where(rt >= SB_TILE, c_lo, c_hi), 0.0).astype(jnp.bfloat16)

    q_stack = [_split_heads(q_ref[sub * SB_TILE:(sub + 1) * SB_TILE, p * LANES:(p + 1) * LANES],
                            first_half) for sub, p in chains]

    def sweep(ms, from_diagonal):
        tiles = []
        for t, m in enumerate(ms):
            for c, (sub, p) in enumerate(chains):
                jb = n_sub * i + sub - m
                diag = from_diagonal and t == 0
                may_underrun = not diag and (from_diagonal or sub < n_sub - 1)
                tiles.append(dict(c=c, p=p, diag=diag, valid=(jb >= 0) if may_underrun else None,
                                  rows=pl.ds(pl.multiple_of(jnp.maximum(jb, 0) * SB_TILE, SB_TILE), SB_TILE)))
        for tl in tiles:
            k_blk = k_ref[tl["rows"], tl["p"] * LANES:(tl["p"] + 1) * LANES]
            tl["z"] = jnp.minimum(_dot_nt(q_stack[tl["c"]], k_blk), LOGIT2_CLAMP)
        ls = []
        for tl in tiles:
            sp = jnp.log(1.0 + jnp.exp2(tl["z"]))
            sp = (jnp.where(strict, sp, 0.0) if tl["diag"] else sp).astype(jnp.bfloat16)
            ls.append(jnp.concatenate([sp, sp], axis=1))
        cs_all = jnp.dot(jnp.concatenate(ls, axis=0), tri_ext,
                         preferred_element_type=jnp.float32)
        run = [None if from_diagonal else carry_ref[c] for c in range(len(chains))]
        for n, tl in enumerate(tiles):
            cs = cs_all[n * 2 * SB_TILE:(n + 1) * 2 * SB_TILE]
            s, total = cs[:, :SB_TILE], cs[:, SB_TILE:]
            old = run[tl["c"]]
            if old is not None:
                s, total = s + old, total + old
            run[tl["c"]] = total
            w = jnp.exp2(tl["z"] - s)
            if tl["diag"]:
                w = jnp.where(strict, w, 0.0)
            wb = w.astype(jnp.bfloat16)
            tl["w"] = jnp.concatenate([wb[:SB_TILE], wb[SB_TILE:]], axis=1)
        floor = None
        for c, (sub, p) in enumerate(chains):
            carry_ref[c] = run[c]
            floor = run[c] if floor is None else jnp.minimum(floor, run[c])
            mine = [tl for tl in tiles if tl["c"] == c]
            vs = []
            for tl in mine:
                v_blk = v_ref[tl["rows"], p * LANES:(p + 1) * LANES]
                if tl["valid"] is not None:
                    v_blk = jnp.where(tl["valid"], v_blk, jnp.zeros_like(v_blk))
                vs.append(_split_heads(v_blk, first_half))
            out = jnp.dot(jnp.concatenate([tl["w"] for tl in mine], axis=1), jnp.concatenate(vs, axis=0),
                          preferred_element_type=jnp.float32)
            if from_diagonal:
                acc_ref[c] = out
            else:
                acc_ref[c] += out
        return jnp.min(floor) >= SWEEP_STOP * LOG2_E

    done0 = sweep(list(range(PROLOGUE_TILES)), True)
    last = n_sub * i + n_sub - 1

    def cond(state):
        m, done = state
        return jnp.logical_and(m <= last, jnp.logical_not(done))

    def body(state):
        m, _ = state
        return m + 1, sweep([m], False)

    lax.while_loop(cond, body, (jnp.int32(PROLOGUE_TILES), done0))
    return jnp.concatenate(
        [jnp.concatenate([acc_ref[sub * n_pair + p] for p in range(n_pair)], axis=1)
         for sub in range(n_sub)], axis=0)


def _mem_attention(q_ref, mkv_ref):
    q_rows = q_ref.shape[0]
    n_mem = mkv_ref.shape[0]
    first_half = lax.broadcasted_iota(jnp.int32, (q_rows, LANES), 1) < HEAD_DIM
    first_half_m = lax.broadcasted_iota(jnp.int32, (n_mem, LANES), 1) < HEAD_DIM
    head_ones = ((lax.broadcasted_iota(jnp.int32, (2 * n_mem, LANES), 1) < HEAD_DIM)
                 == (lax.broadcasted_iota(jnp.int32, (2 * n_mem, LANES), 0) < n_mem)
                 ).astype(jnp.float32).astype(jnp.bfloat16)
    outs = []
    for p in range(MEM_WIDTH // LANES):
        qs = _split_heads(q_ref[:, SB_WIDTH + p * LANES:SB_WIDTH + (p + 1) * LANES], first_half)
        mk = mkv_ref[:, p * LANES:(p + 1) * LANES]
        mv = mkv_ref[:, MEM_WIDTH + p * LANES:MEM_WIDTH + (p + 1) * LANES]
        s = _dot_nt(qs, mk)
        e = jnp.exp(s - jnp.max(s, axis=-1, keepdims=True)).astype(jnp.bfloat16)
        e_cat = jnp.concatenate([e[:q_rows], e[q_rows:]], axis=1)
        num = jnp.dot(e_cat, _split_heads(mv, first_half_m), preferred_element_type=jnp.float32)
        den = jnp.dot(e_cat, head_ones, preferred_element_type=jnp.float32)
        outs.append(num / den)
    return jnp.concatenate(outs, axis=-1)


def _layer_kernel(x_ref, g_in_ref, w_in_ref, mkv_ref, convw_ref, convb_ref,
                  g_sb_ref, g_conv_ref, g_mem_ref, w_out_ref, g_final_ref,
                  out_ref, k_ref, v_ref, q_ref, rest_ref, cu_ref, acc_ref, carry_ref):
    i = pl.program_id(0)
    q_rows = x_ref.shape[0]

    x = x_ref[...]
    h = (x * _rms_scale(x) * g_in_ref[...]).astype(jnp.bfloat16)

    def proj(c0, width):
        return jnp.dot(h, w_in_ref[:, c0:c0 + width], preferred_element_type=jnp.float32)

    c_u = 3 * SB_WIDTH
    c_qm = c_u + 3 * CONV_WIDTH
    c_gate = c_qm + MEM_WIDTH
    blk = pl.ds(pl.multiple_of(i * q_rows, q_rows), q_rows)
    k_ref[blk, :] = proj(SB_WIDTH, SB_WIDTH).astype(jnp.bfloat16)
    v_ref[blk, :] = proj(2 * SB_WIDTH, SB_WIDTH).astype(jnp.bfloat16)
    q_ref[:, :SB_WIDTH] = (proj(0, SB_WIDTH) * (SCALE * LOG2_E)).astype(jnp.bfloat16)
    q_ref[:, SB_WIDTH:] = (proj(c_qm, MEM_WIDTH) * SCALE).astype(jnp.bfloat16)
    rest_ref[:, :3 * CONV_WIDTH] = proj(c_u, 3 * CONV_WIDTH)
    for c0 in range(0, MIX_WIDTH, PROJ_COLS):
        rest_ref[:, 3 * CONV_WIDTH + c0:3 * CONV_WIDTH + c0 + PROJ_COLS] = proj(c_gate + c0, PROJ_COLS)

    y_sb = _sb_sweep(i, q_ref, k_ref, v_ref, acc_ref, carry_ref)

    u = rest_ref[:, 0:CONV_WIDTH]
    b = rest_ref[:, CONV_WIDTH:2 * CONV_WIDTH]
    c = rest_ref[:, 2 * CONV_WIDTH:3 * CONV_WIDTH]
    cu = c * u

    @pl.when(i == 0)
    def _():
        cu_ref[0:SUBLANES, :] = jnp.zeros((SUBLANES, CONV_WIDTH), jnp.float32)

    cu_ref[SUBLANES:, :] = cu
    cu_1 = cu_ref[SUBLANES - 1:SUBLANES - 1 + q_rows, :]
    cu_2 = cu_ref[SUBLANES - 2:SUBLANES - 2 + q_rows, :]
    cu_ref[0:SUBLANES, :] = cu[q_rows - SUBLANES:, :]
    conv = (convw_ref[0:1, :] * cu_2 + convw_ref[1:2, :] * cu_1 + convw_ref[2:3, :] * cu
            + convb_ref[...])
    y_conv = b * conv

    y_mem = _mem_attention(q_ref, mkv_ref)

    y = jnp.concatenate([y_sb * _rms_scale(y_sb) * g_sb_ref[...],
                         y_conv * _rms_scale(y_conv) * g_conv_ref[...],
                         y_mem * _rms_scale(y_mem) * g_mem_ref[...]], axis=-1)
    gate = rest_ref[:, 3 * CONV_WIDTH:]
    gated = (y * (gate / (1.0 + jnp.exp(-gate)))).astype(jnp.bfloat16)
    res = x_ref[...] + jnp.dot(gated, w_out_ref[...], preferred_element_type=jnp.float32)
    out_ref[...] = res * _rms_scale(res) * g_final_ref[...]


def _layer(x2, g_in, w_in_bf16, mkv, conv_w, conv_b, g_sb, g_conv, g_mem, w_out_bf16, g_final):
    t, d = x2.shape
    assert t % Q_BLOCK == 0 and Q_BLOCK % SB_TILE == 0
    n_chain = (Q_BLOCK // SB_TILE) * (SB_WIDTH // LANES)

    def whole(a):
        return pl.BlockSpec(a.shape, lambda i: (0,) * a.ndim)

    resident = pl.BlockSpec(memory_space=pltpu.VMEM)
    return pl.pallas_call(
        _layer_kernel,
        grid=(t // Q_BLOCK,),
        in_specs=[pl.BlockSpec((Q_BLOCK, d), lambda i: (i, 0)), whole(g_in), resident,
                  whole(mkv), whole(conv_w), whole(conv_b), whole(g_sb), whole(g_conv),
                  whole(g_mem), resident, whole(g_final)],
        out_specs=pl.BlockSpec((Q_BLOCK, d), lambda i: (i, 0)),
        out_shape=jax.ShapeDtypeStruct((t, d), jnp.float32),
        scratch_shapes=[pltpu.VMEM((t, SB_WIDTH), jnp.bfloat16),
                        pltpu.VMEM((t, SB_WIDTH), jnp.bfloat16),
                        pltpu.VMEM((Q_BLOCK, SB_WIDTH + MEM_WIDTH), jnp.bfloat16),
                        pltpu.VMEM((Q_BLOCK, 3 * CONV_WIDTH + MIX_WIDTH), jnp.float32),
                        pltpu.VMEM((SUBLANES + Q_BLOCK, CONV_WIDTH), jnp.float32),
                        pltpu.VMEM((n_chain, SB_TILE, LANES), jnp.float32),
                        pltpu.VMEM((n_chain, 2 * SB_TILE, LANES), jnp.float32)],
        compiler_params=pltpu.CompilerParams(
            dimension_semantics=("arbitrary",), vmem_limit_bytes=VMEM_LIMIT_BYTES),
        name="layer",
    )(x2, g_in, w_in_bf16, mkv, conv_w, conv_b, g_sb, g_conv, g_mem, w_out_bf16, g_final)


def kernel(x, mem, g_in, w_in, conv_w, conv_b, g_mem, w_mem_kv, g_sb_out, g_conv_out,
           g_mem_out, w_out, g_final):
    batch, t, d = x.shape
    assert batch == 1 and g_in.shape[0] == 1
    mkv = _memory_kv(mem.reshape(mem.shape[1], d), g_mem[0][None, :], w_mem_kv[0].astype(jnp.bfloat16))
    out = _layer(x.reshape(t, d), g_in[0][None, :], w_in[0].astype(jnp.bfloat16), mkv,
                 conv_w[0], conv_b[0][None, :], g_sb_out[0][None, :], g_conv_out[0][None, :],
                 g_mem_out[0][None, :], w_out[0].astype(jnp.bfloat16), g_final[None, :])
    return out.reshape(batch, t, d)
```

```python
import jax
import jax.numpy as jnp
from jax import lax
from jax.experimental import pallas as pl
from jax.experimental.pallas import tpu as pltpu

HEAD_DIM = 64
SB_HEADS = 8
SB_WIDTH = SB_HEADS * HEAD_DIM
CONV_WIDTH = 4 * HEAD_DIM
MEM_HEADS = 4
MEM_WIDTH = MEM_HEADS * HEAD_DIM
CONV_K = 3
EPS = 1e-6
SCALE = HEAD_DIM ** -0.5

LANES = 128
SUBLANES = 8
MIX_WIDTH = SB_WIDTH + CONV_WIDTH + MEM_WIDTH
PROJ_COLS = 512
Q_BLOCK = 256
SB_TILE = 128
SWEEP_STOP = 104.0
PROLOGUE_TILES = 3
LOG2_E = 1.4426950408889634
LOGIT2_CLAMP = 126.0
VMEM_LIMIT_BYTES = 62 * 1024 * 1024


def _rms_scale(xf, eps=EPS):
    return lax.rsqrt(jnp.mean(xf * xf, axis=-1, keepdims=True) + eps)


def _mem_kv_kernel(mem_ref, g_ref, w_ref, kv_ref):
    m = mem_ref[...]
    h = (m * _rms_scale(m) * g_ref[...]).astype(jnp.bfloat16)
    kv_ref[...] = jnp.dot(h, w_ref[...], preferred_element_type=jnp.float32).astype(jnp.bfloat16)


def _memory_kv(mem2, g_mem, w_kv_bf16):
    n_mem = mem2.shape[0]
    return pl.pallas_call(
        _mem_kv_kernel,
        out_shape=jax.ShapeDtypeStruct((n_mem, 2 * MEM_WIDTH), jnp.bfloat16),
        name="memory_kv",
    )(mem2, g_mem, w_kv_bf16)


def _dot_nt(a, b):
    return lax.dot_general(a, b, (((1,), (1,)), ((), ())), preferred_element_type=jnp.float32)


def _split_heads(a, first_half):
    zero = jnp.zeros_like(a)
    return jnp.concatenate([jnp.where(first_half, a, zero), jnp.where(first_half, zero, a)], axis=0)


def _sb_sweep(i, q_ref, k_ref, v_ref, acc_ref, carry_ref, fillers):
    n_sub = q_ref.shape[0] // SB_TILE
    n_pair = SB_WIDTH // LANES
    chains = [(sub, p) for sub in range(n_sub) for p in range(n_pair)]

    lane = lax.broadcasted_iota(jnp.int32, (SB_TILE, LANES), 1)
    first_half = lane < HEAD_DIM
    r2 = lax.broadcasted_iota(jnp.int32, (2 * SB_TILE, SB_TILE), 0)
    c2 = lax.broadcasted_iota(jnp.int32, (2 * SB_TILE, SB_TILE), 1)
    strict = c2 < jnp.where(r2 >= SB_TILE, r2 - SB_TILE, r2)
    rt = lax.broadcasted_iota(jnp.int32, (2 * SB_TILE, 2 * SB_TILE), 0)
    ct = lax.broadcasted_iota(jnp.int32, (2 * SB_TILE, 2 * SB_TILE), 1)
    c_hi = jnp.full((1, 1), LOG2_E, jnp.float32).astype(jnp.bfloat16).astype(jnp.float32)
    c_lo = (LOG2_E - c_hi).astype(jnp.bfloat16).astype(jnp.float32)
    tri_ext = jnp.where(jnp.logical_or(ct >= SB_TILE, jnp.where(rt >= SB_TILE, rt - SB_TILE, rt) >= ct),
                        jnp.where(rt >= SB_TILE, c_lo, c_hi), 0.0).astype(jnp.bfloat16)

    q_stack = [_split_heads(q_ref[sub * SB_TILE:(sub + 1) * SB_TILE, p * LANES:(p + 1) * LANES],
                            first_half) for sub, p in chains]

    def sweep(ms, from_diagonal, fillers=()):
        fillers = list(fillers)
        tiles = []
        for t, m in enumerate(ms):
            for c, (sub, p) in enumerate(chains):
                jb = n_sub * i + sub - m
                diag = from_diagonal and t == 0
                may_underrun = not diag and (from_diagonal or sub < n_sub - 1)
                tiles.append(dict(c=c, p=p, diag=diag, valid=(jb >= 0) if may_underrun else None,
                                  rows=pl.ds(pl.multiple_of(jnp.maximum(jb, 0) * SB_TILE, SB_TILE), SB_TILE)))
        for tl in tiles:
            k_blk = k_ref[tl["rows"], tl["p"] * LANES:(tl["p"] + 1) * LANES]
            tl["z"] = jnp.minimum(_dot_nt(q_stack[tl["c"]], k_blk), LOGIT2_CLAMP)
        if fillers:
            fillers.pop(0)()
        ls = []
        for tl in tiles:
            sp = jnp.log(1.0 + jnp.exp2(tl["z"]))
            sp = (jnp.where(strict, sp, 0.0) if tl["diag"] else sp).astype(jnp.bfloat16)
            ls.append(jnp.concatenate([sp, sp], axis=1))
        cs_all = jnp.dot(jnp.concatenate(ls, axis=0), tri_ext,
                         preferred_element_type=jnp.float32)
        if fillers:
            fillers.pop(0)()
        run = [None if from_diagonal else carry_ref[c] for c in range(len(chains))]
        for n, tl in enumerate(tiles):
            cs = cs_all[n * 2 * SB_TILE:(n + 1) * 2 * SB_TILE]
            s, total = cs[:, :SB_TILE], cs[:, SB_TILE:]
            old = run[tl["c"]]
            if old is not None:
                s, total = s + old, total + old
            run[tl["c"]] = total
            w = jnp.exp2(tl["z"] - s)
            if tl["diag"]:
                w = jnp.where(strict, w, 0.0)
            wb = w.astype(jnp.bfloat16)
            tl["w"] = jnp.concatenate([wb[:SB_TILE], wb[SB_TILE:]], axis=1)
        floor = None
        for c, (sub, p) in enumerate(chains):
            carry_ref[c] = run[c]
            floor = run[c] if floor is None else jnp.minimum(floor, run[c])
            mine = [tl for tl in tiles if tl["c"] == c]
            vs = []
            for tl in mine:
                v_blk = v_ref[tl["rows"], p * LANES:(p + 1) * LANES]
                if tl["valid"] is not None:
                    v_blk = jnp.where(tl["valid"], v_blk, jnp.zeros_like(v_blk))
                vs.append(_split_heads(v_blk, first_half))
            out = jnp.dot(jnp.concatenate([tl["w"] for tl in mine], axis=1), jnp.concatenate(vs, axis=0),
                          preferred_element_type=jnp.float32)
            if from_diagonal:
                acc_ref[c] = out
            else:
                acc_ref[c] += out
        return jnp.min(floor) >= SWEEP_STOP * LOG2_E

    done0 = sweep(list(range(PROLOGUE_TILES)), True, fillers)
    last = n_sub * i + n_sub - 1

    def cond(state):
        m, done = state
        return jnp.logical_and(m <= last, jnp.logical_not(done))

    def body(state):
        m, _ = state
        return m + 1, sweep([m], False)

    lax.while_loop(cond, body, (jnp.int32(PROLOGUE_TILES), done0))
    return jnp.concatenate(
        [jnp.concatenate([acc_ref[sub * n_pair + p] for p in range(n_pair)], axis=1)
         for sub in range(n_sub)], axis=0)


def _mem_attention(q_ref, mkv_ref):
    q_rows = q_ref.shape[0]
    n_mem = mkv_ref.shape[0]
    first_half = lax.broadcasted_iota(jnp.int32, (q_rows, LANES), 1) < HEAD_DIM
    first_half_m = lax.broadcasted_iota(jnp.int32, (n_mem, LANES), 1) < HEAD_DIM
    head_ones = ((lax.broadcasted_iota(jnp.int32, (2 * n_mem, LANES), 1) < HEAD_DIM)
                 == (lax.broadcasted_iota(jnp.int32, (2 * n_mem, LANES), 0) < n_mem)
                 ).astype(jnp.float32).astype(jnp.bfloat16)
    outs = []
    for p in range(MEM_WIDTH // LANES):
        qs = _split_heads(q_ref[:, SB_WIDTH + p * LANES:SB_WIDTH + (p + 1) * LANES], first_half)
        mk = mkv_ref[:, p * LANES:(p + 1) * LANES]
        mv = mkv_ref[:, MEM_WIDTH + p * LANES:MEM_WIDTH + (p + 1) * LANES]
        s = _dot_nt(qs, mk)
        e = jnp.exp(s - jnp.max(s, axis=-1, keepdims=True)).astype(jnp.bfloat16)
        e_cat = jnp.concatenate([e[:q_rows], e[q_rows:]], axis=1)
        num = jnp.dot(e_cat, _split_heads(mv, first_half_m), preferred_element_type=jnp.float32)
        den = jnp.dot(e_cat, head_ones, preferred_element_type=jnp.float32)
        outs.append(num / den)
    return jnp.concatenate(outs, axis=-1)


def _layer_kernel(x_ref, g_in_ref, w_in_ref, mkv_ref, convw_ref, convb_ref,
                  g_sb_ref, g_conv_ref, g_mem_ref, w_out_ref, g_final_ref,
                  out_ref, k_ref, v_ref, kv_next_ref, q2_ref, rest2_ref, x_prev_ref,
                  cu_ref, acc_ref, carry_ref):
    s = pl.program_id(0)
    q_rows = x_ref.shape[0]
    c_u = 3 * SB_WIDTH
    c_qm = c_u + 3 * CONV_WIDTH
    c_gate = c_qm + MEM_WIDTH

    def normed_input():
        x = x_ref[...]
        return (x * _rms_scale(x) * g_in_ref[...]).astype(jnp.bfloat16)

    def proj(h, c0, width):
        return jnp.dot(h, w_in_ref[:, c0:c0 + width], preferred_element_type=jnp.float32)

    def project_kv(h):
        kv_next_ref[...] = proj(h, SB_WIDTH, 2 * SB_WIDTH).astype(jnp.bfloat16)

    def project_q_u(h, slot):
        q2_ref[slot, :, :SB_WIDTH] = (proj(h, 0, SB_WIDTH) * (SCALE * LOG2_E)).astype(jnp.bfloat16)
        q2_ref[slot, :, SB_WIDTH:] = (proj(h, c_qm, MEM_WIDTH) * SCALE).astype(jnp.bfloat16)
        rest2_ref[slot, :, :CONV_WIDTH] = proj(h, c_u, CONV_WIDTH)

    def project_b_c_gate(h, slot):
        rest2_ref[slot, :, CONV_WIDTH:3 * CONV_WIDTH] = proj(h, c_u + CONV_WIDTH, 2 * CONV_WIDTH)
        for c0 in range(0, MIX_WIDTH, PROJ_COLS):
            rest2_ref[slot, :, 3 * CONV_WIDTH + c0:3 * CONV_WIDTH + c0 + PROJ_COLS] = proj(h, c_gate + c0, PROJ_COLS)

    @pl.when(s == 0)
    def _():
        h = normed_input()
        project_kv(h)
        project_q_u(h, 0)
        project_b_c_gate(h, 0)
        x_prev_ref[...] = x_ref[...]
        cu_ref[0:SUBLANES, :] = jnp.zeros((SUBLANES, CONV_WIDTH), jnp.float32)

    @pl.when(s > 0)
    def _():
        i = s - 1
        cur = lax.rem(i, 2)
        nxt = 1 - cur
        q_ref = q2_ref.at[cur]
        rest_ref = rest2_ref.at[cur]
        blk = pl.ds(pl.multiple_of(i * q_rows, q_rows), q_rows)
        k_ref[blk, :] = kv_next_ref[:, :SB_WIDTH]
        v_ref[blk, :] = kv_next_ref[:, SB_WIDTH:]

        h_next = normed_input()
        y_sb = _sb_sweep(i, q_ref, k_ref, v_ref, acc_ref, carry_ref,
                         (lambda: project_kv(h_next), lambda: project_q_u(h_next, nxt)))

        u = rest_ref[:, 0:CONV_WIDTH]
        b = rest_ref[:, CONV_WIDTH:2 * CONV_WIDTH]
        c = rest_ref[:, 2 * CONV_WIDTH:3 * CONV_WIDTH]
        cu = c * u
        cu_ref[SUBLANES:, :] = cu
        cu_1 = cu_ref[SUBLANES - 1:SUBLANES - 1 + q_rows, :]
        cu_2 = cu_ref[SUBLANES - 2:SUBLANES - 2 + q_rows, :]
        cu_ref[0:SUBLANES, :] = cu[q_rows - SUBLANES:, :]
        conv = (convw_ref[0:1, :] * cu_2 + convw_ref[1:2, :] * cu_1 + convw_ref[2:3, :] * cu
                + convb_ref[...])
        y_conv = b * conv

        y_mem = _mem_attention(q_ref, mkv_ref)

        y = jnp.concatenate([y_sb * _rms_scale(y_sb) * g_sb_ref[...],
                             y_conv * _rms_scale(y_conv) * g_conv_ref[...],
                             y_mem * _rms_scale(y_mem) * g_mem_ref[...]], axis=-1)
        gate = rest_ref[:, 3 * CONV_WIDTH:]
        gated = (y * (gate / (1.0 + jnp.exp(-gate)))).astype(jnp.bfloat16)
        project_b_c_gate(h_next, nxt)
        res = x_prev_ref[...] + jnp.dot(gated, w_out_ref[...], preferred_element_type=jnp.float32)
        out_ref[...] = res * _rms_scale(res) * g_final_ref[...]
        x_prev_ref[...] = x_ref[...]


def _layer(x2, g_in, w_in_bf16, mkv, conv_w, conv_b, g_sb, g_conv, g_mem, w_out_bf16, g_final):
    t, d = x2.shape
    assert t % Q_BLOCK == 0 and Q_BLOCK % SB_TILE == 0
    n_chain = (Q_BLOCK // SB_TILE) * (SB_WIDTH // LANES)

    def whole(a):
        return pl.BlockSpec(a.shape, lambda s: (0,) * a.ndim)

    resident = pl.BlockSpec(memory_space=pltpu.VMEM)
    n_blk = t // Q_BLOCK
    return pl.pallas_call(
        _layer_kernel,
        grid=(n_blk + 1,),
        in_specs=[pl.BlockSpec((Q_BLOCK, d), lambda s: (jnp.minimum(s, n_blk - 1), 0)), whole(g_in), resident,
                  whole(mkv), whole(conv_w), whole(conv_b), whole(g_sb), whole(g_conv),
                  whole(g_mem), resident, whole(g_final)],
        out_specs=pl.BlockSpec((Q_BLOCK, d), lambda s: (jnp.maximum(s - 1, 0), 0)),
        out_shape=jax.ShapeDtypeStruct((t, d), jnp.float32),
        scratch_shapes=[pltpu.VMEM((t, SB_WIDTH), jnp.bfloat16),
                        pltpu.VMEM((t, SB_WIDTH), jnp.bfloat16),
                        pltpu.VMEM((Q_BLOCK, 2 * SB_WIDTH), jnp.bfloat16),
                        pltpu.VMEM((2, Q_BLOCK, SB_WIDTH + MEM_WIDTH), jnp.bfloat16),
                        pltpu.VMEM((2, Q_BLOCK, 3 * CONV_WIDTH + MIX_WIDTH), jnp.float32),
                        pltpu.VMEM((Q_BLOCK, d), jnp.float32),
                        pltpu.VMEM((SUBLANES + Q_BLOCK, CONV_WIDTH), jnp.float32),
                        pltpu.VMEM((n_chain, SB_TILE, LANES), jnp.float32),
                        pltpu.VMEM((n_chain, 2 * SB_TILE, LANES), jnp.float32)],
        compiler_params=pltpu.CompilerParams(
            dimension_semantics=("arbitrary",), vmem_limit_bytes=VMEM_LIMIT_BYTES),
        name="layer",
    )(x2, g_in, w_in_bf16, mkv, conv_w, conv_b, g_sb, g_conv, g_mem, w_out_bf16, g_final)


def kernel(x, mem, g_in, w_in, conv_w, conv_b, g_mem, w_mem_kv, g_sb_out, g_conv_out,
           g_mem_out, w_out, g_final):
    batch, t, d = x.shape
    assert batch == 1 and g_in.shape[0] == 1
    mkv = _memory_kv(mem.reshape(mem.shape[1], d), g_mem[0][None, :], w_mem_kv[0].astype(jnp.bfloat16))
    out = _layer(x.reshape(t, d), g_in[0][None, :], w_in[0].astype(jnp.bfloat16), mkv,
                 conv_w[0], conv_b[0][None, :], g_sb_out[0][None, :], g_conv_out[0][None, :],
                 g_mem_out[0][None, :], w_out[0].astype(jnp.bfloat16), g_final[None, :])
    return out.reshape(batch, t, d)
```

```python
import jax
import jax.numpy as jnp
from jax import lax
from jax.experimental import pallas as pl
from jax.experimental.pallas import tpu as pltpu

HEAD_DIM = 64
SB_HEADS = 8
SB_WIDTH = SB_HEADS * HEAD_DIM
CONV_WIDTH = 4 * HEAD_DIM
MEM_HEADS = 4
MEM_WIDTH = MEM_HEADS * HEAD_DIM
CONV_K = 3
EPS = 1e-6
SCALE = HEAD_DIM ** -0.5

LANES = 128
SUBLANES = 8
MIX_WIDTH = SB_WIDTH + CONV_WIDTH + MEM_WIDTH
PROJ_COLS = 512
Q_BLOCK = 256
SB_TILE = 128
SWEEP_STOP = 104.0
LOG2_E = 1.4426950408889634
LOGIT2_CLAMP = 126.0
VMEM_LIMIT_BYTES = 62 * 1024 * 1024


def _rms_scale(xf, eps=EPS):
    return lax.rsqrt(jnp.mean(xf * xf, axis=-1, keepdims=True) + eps)


def _mem_kv_kernel(mem_ref, g_ref, w_ref, kv_ref):
    m = mem_ref[...]
    h = (m * _rms_scale(m) * g_ref[...]).astype(jnp.bfloat16)
    kv_ref[...] = jnp.dot(h, w_ref[...], preferred_element_type=jnp.float32).astype(jnp.bfloat16)


def _memory_kv(mem2, g_mem, w_kv_bf16):
    n_mem = mem2.shape[0]
    return pl.pallas_call(
        _mem_kv_kernel,
        out_shape=jax.ShapeDtypeStruct((n_mem, 2 * MEM_WIDTH), jnp.bfloat16),
        name="memory_kv",
    )(mem2, g_mem, w_kv_bf16)


def _dot_nt(a, b):
    return lax.dot_general(a, b, (((1,), (1,)), ((), ())), preferred_element_type=jnp.float32)


def _split_heads(a, first_half):
    zero = jnp.zeros_like(a)
    return jnp.concatenate([jnp.where(first_half, a, zero), jnp.where(first_half, zero, a)], axis=0)


def _sb_sweep(i, q_ref, k_ref, v_ref, acc_ref, carry_ref, fillers):
    n_sub = q_ref.shape[0] // SB_TILE
    n_pair = SB_WIDTH // LANES
    chains = [(sub, p) for sub in range(n_sub) for p in range(n_pair)]
    fillers = list(fillers)

    first_half = lax.broadcasted_iota(jnp.int32, (SB_TILE, LANES), 1) < HEAD_DIM
    r2 = lax.broadcasted_iota(jnp.int32, (2 * SB_TILE, SB_TILE), 0)
    c2 = lax.broadcasted_iota(jnp.int32, (2 * SB_TILE, SB_TILE), 1)
    strict = c2 < jnp.where(r2 >= SB_TILE, r2 - SB_TILE, r2)
    rt = lax.broadcasted_iota(jnp.int32, (2 * SB_TILE, 2 * SB_TILE), 0)
    ct = lax.broadcasted_iota(jnp.int32, (2 * SB_TILE, 2 * SB_TILE), 1)
    top, left = rt < SB_TILE, ct < SB_TILE
    tri2 = jnp.where(top, rt, rt - SB_TILE) >= jnp.where(left, ct, ct - SB_TILE)

    def as_matrix(cond):
        return jnp.where(cond, 1.0, 0.0).astype(jnp.bfloat16)

    w_01 = as_matrix(jnp.logical_or(jnp.logical_and(top, jnp.logical_not(left)),
                                    jnp.logical_and(tri2, jnp.logical_not(jnp.logical_xor(top, left)))))
    w_2 = as_matrix(jnp.logical_or(top, jnp.logical_or(tri2, jnp.logical_not(left))))
    w_1 = w_2[SB_TILE:]

    q_stack = [_split_heads(q_ref[sub * SB_TILE:(sub + 1) * SB_TILE, p * LANES:(p + 1) * LANES],
                            first_half) for sub, p in chains]

    def tile_rows(c, m):
        jb = n_sub * i + chains[c][0] - m
        return jb >= 0, pl.ds(pl.multiple_of(jnp.maximum(jb, 0) * SB_TILE, SB_TILE), SB_TILE)

    def logits(c, rows):
        p = chains[c][1]
        return jnp.minimum(_dot_nt(q_stack[c], k_ref[rows, p * LANES:(p + 1) * LANES]), LOGIT2_CLAMP)

    def neg_log2_not_beta(z):
        return jnp.log(1.0 + jnp.exp2(z)) * LOG2_E

    def weights(z, s, diagonal=False):
        w = jnp.exp2(z - s)
        if diagonal:
            w = jnp.where(strict, w, 0.0)
        wb = w.astype(jnp.bfloat16)
        return jnp.concatenate([wb[:SB_TILE], wb[SB_TILE:]], axis=1)

    def values(c, rows, valid=None):
        p = chains[c][1]
        v_blk = v_ref[rows, p * LANES:(p + 1) * LANES]
        if valid is not None:
            v_blk = jnp.where(valid, v_blk, jnp.zeros_like(v_blk))
        return _split_heads(v_blk, first_half)

    def all_done(carries):
        floor = carries[0]
        for carry in carries[1:]:
            floor = jnp.minimum(floor, carry)
        return jnp.min(floor) >= SWEEP_STOP * LOG2_E

    def first_sweep():
        n_tiles = 3
        rows = [[tile_rows(c, m) for m in range(n_tiles)] for c in range(len(chains))]
        zs = [[logits(c, rows[c][m][1]) for m in range(n_tiles)] for c in range(len(chains))]
        if fillers:
            fillers.pop(0)()
        lhs_01, lhs_2 = [], []
        for z0, z1, z2 in zs:
            l0 = jnp.where(strict, neg_log2_not_beta(z0), 0.0)
            l1, l2 = neg_log2_not_beta(z1), neg_log2_not_beta(z2)
            lhs_01.append(jnp.concatenate([l0.astype(jnp.bfloat16), l1.astype(jnp.bfloat16)], axis=1))
            lhs_2.append(jnp.concatenate([(l0 + l1).astype(jnp.bfloat16), l2.astype(jnp.bfloat16)], axis=1))
        s_01 = jnp.dot(jnp.concatenate(lhs_01, axis=0), w_01, preferred_element_type=jnp.float32)
        s_2 = jnp.dot(jnp.concatenate(lhs_2, axis=0), w_2, preferred_element_type=jnp.float32)
        if fillers:
            fillers.pop(0)()
        carries, w_cats = [], []
        for c, (z0, z1, z2) in enumerate(zs):
            part = slice(c * 2 * SB_TILE, (c + 1) * 2 * SB_TILE)
            w_cats.append(jnp.concatenate([weights(z0, s_01[part, :SB_TILE], diagonal=True),
                                           weights(z1, s_01[part, SB_TILE:]),
                                           weights(z2, s_2[part, :SB_TILE])], axis=1))
            carries.append(s_2[part, SB_TILE:])
            carry_ref[c] = carries[c]
        for c in range(len(chains)):
            vs = [values(c, rows[c][0][1])] + [values(c, rows[c][m][1], rows[c][m][0]) for m in (1, 2)]
            acc_ref[c] = jnp.dot(w_cats[c], jnp.concatenate(vs, axis=0), preferred_element_type=jnp.float32)
        return n_tiles, all_done(carries)

    def next_sweep(m):
        rows = [tile_rows(c, m) for c in range(len(chains))]
        zs = [logits(c, rows[c][1]) for c in range(len(chains))]
        ls = [neg_log2_not_beta(z).astype(jnp.bfloat16) for z in zs]
        cs = jnp.dot(jnp.concatenate(ls, axis=0), w_1, preferred_element_type=jnp.float32)
        carries, w_cats = [], []
        for c, z in enumerate(zs):
            part = slice(c * 2 * SB_TILE, (c + 1) * 2 * SB_TILE)
            old = carry_ref[c]
            w_cats.append(weights(z, cs[part, :SB_TILE] + old))
            carries.append(cs[part, SB_TILE:] + old)
            carry_ref[c] = carries[c]
        for c, (sub, _) in enumerate(chains):
            valid = rows[c][0] if sub < n_sub - 1 else None
            acc_ref[c] += jnp.dot(w_cats[c], values(c, rows[c][1], valid), preferred_element_type=jnp.float32)
        return all_done(carries)

    m0, done0 = first_sweep()
    last = n_sub * i + n_sub - 1

    def cond(state):
        m, done = state
        return jnp.logical_and(m <= last, jnp.logical_not(done))

    def body(state):
        m, _ = state
        return m + 1, next_sweep(m)

    lax.while_loop(cond, body, (jnp.int32(m0), done0))
    return jnp.concatenate(
        [jnp.concatenate([acc_ref[sub * n_pair + p] for p in range(n_pair)], axis=1)
         for sub in range(n_sub)], axis=0)


def _mem_attention(q_ref, mkv_ref):
    q_rows = q_ref.shape[0]
    n_mem = mkv_ref.shape[0]
    first_half = lax.broadcasted_iota(jnp.int32, (q_rows, LANES), 1) < HEAD_DIM
    first_half_m = lax.broadcasted_iota(jnp.int32, (n_mem, LANES), 1) < HEAD_DIM
    head_ones = ((lax.broadcasted_iota(jnp.int32, (2 * n_mem, LANES), 1) < HEAD_DIM)
                 == (lax.broadcasted_iota(jnp.int32, (2 * n_mem, LANES), 0) < n_mem)
                 ).astype(jnp.float32).astype(jnp.bfloat16)
    outs = []
    for p in range(MEM_WIDTH // LANES):
        qs = _split_heads(q_ref[:, SB_WIDTH + p * LANES:SB_WIDTH + (p + 1) * LANES], first_half)
        mk = mkv_ref[:, p * LANES:(p + 1) * LANES]
        mv = mkv_ref[:, MEM_WIDTH + p * LANES:MEM_WIDTH + (p + 1) * LANES]
        s = _dot_nt(qs, mk)
        e = jnp.exp(s - jnp.max(s, axis=-1, keepdims=True)).astype(jnp.bfloat16)
        e_cat = jnp.concatenate([e[:q_rows], e[q_rows:]], axis=1)
        num_den = jnp.dot(e_cat, jnp.concatenate([_split_heads(mv, first_half_m), head_ones], axis=1),
                          preferred_element_type=jnp.float32)
        outs.append(num_den[:, :LANES] / num_den[:, LANES:])
    return jnp.concatenate(outs, axis=-1)


def _layer_kernel(x_ref, g_in_ref, w_in_ref, mkv_ref, convw_ref, convb_ref,
                  g_sb_ref, g_conv_ref, g_mem_ref, w_out_ref, g_final_ref,
                  out_ref, k_ref, v_ref, kv_next_ref, q2_ref, rest2_ref, x_prev_ref,
                  cu_ref, acc_ref, carry_ref):
    s = pl.program_id(0)
    q_rows = x_ref.shape[0]
    c_u = 3 * SB_WIDTH
    c_qm = c_u + 3 * CONV_WIDTH
    c_gate = c_qm + MEM_WIDTH

    def normed_input():
        x = x_ref[...]
        return (x * _rms_scale(x) * g_in_ref[...]).astype(jnp.bfloat16)

    def proj(h, c0, width):
        return jnp.dot(h, w_in_ref[:, c0:c0 + width], preferred_element_type=jnp.float32)

    def project_kv(h):
        kv_next_ref[...] = proj(h, SB_WIDTH, 2 * SB_WIDTH).astype(jnp.bfloat16)

    def project_q_conv(h, slot):
        q2_ref[slot, :, :SB_WIDTH] = (proj(h, 0, SB_WIDTH) * (SCALE * LOG2_E)).astype(jnp.bfloat16)
        q2_ref[slot, :, SB_WIDTH:] = (proj(h, c_qm, MEM_WIDTH) * SCALE).astype(jnp.bfloat16)
        rest2_ref[slot, :, :3 * CONV_WIDTH] = proj(h, c_u, 3 * CONV_WIDTH)

    def project_gate(h, slot):
        for c0 in range(0, MIX_WIDTH, PROJ_COLS):
            rest2_ref[slot, :, 3 * CONV_WIDTH + c0:3 * CONV_WIDTH + c0 + PROJ_COLS] = proj(h, c_gate + c0, PROJ_COLS)

    @pl.when(s == 0)
    def _():
        h = normed_input()
        project_kv(h)
        project_q_conv(h, 0)
        project_gate(h, 0)
        x_prev_ref[...] = x_ref[...]
        cu_ref[0:SUBLANES, :] = jnp.zeros((SUBLANES, CONV_WIDTH), jnp.float32)

    @pl.when(s > 0)
    def _():
        i = s - 1
        cur = lax.rem(i, 2)
        nxt = 1 - cur
        q_ref = q2_ref.at[cur]
        rest_ref = rest2_ref.at[cur]
        blk = pl.ds(pl.multiple_of(i * q_rows, q_rows), q_rows)
        k_ref[blk, :] = kv_next_ref[:, :SB_WIDTH]
        v_ref[blk, :] = kv_next_ref[:, SB_WIDTH:]

        h_next = normed_input()
        y_sb = _sb_sweep(i, q_ref, k_ref, v_ref, acc_ref, carry_ref,
                         (lambda: project_kv(h_next), lambda: project_q_conv(h_next, nxt)))

        u = rest_ref[:, 0:CONV_WIDTH]
        b = rest_ref[:, CONV_WIDTH:2 * CONV_WIDTH]
        c = rest_ref[:, 2 * CONV_WIDTH:3 * CONV_WIDTH]
        cu = c * u
        cu_ref[SUBLANES:, :] = cu
        cu_1 = cu_ref[SUBLANES - 1:SUBLANES - 1 + q_rows, :]
        cu_2 = cu_ref[SUBLANES - 2:SUBLANES - 2 + q_rows, :]
        cu_ref[0:SUBLANES, :] = cu[q_rows - SUBLANES:, :]
        conv = (convw_ref[0:1, :] * cu_2 + convw_ref[1:2, :] * cu_1 + convw_ref[2:3, :] * cu
                + convb_ref[...])
        y_conv = b * conv

        y_mem = _mem_attention(q_ref, mkv_ref)

        y = jnp.concatenate([y_sb * _rms_scale(y_sb) * g_sb_ref[...],
                             y_conv * _rms_scale(y_conv) * g_conv_ref[...],
                             y_mem * _rms_scale(y_mem) * g_mem_ref[...]], axis=-1)
        gate = rest_ref[:, 3 * CONV_WIDTH:]
        gated = (y * (gate / (1.0 + jnp.exp(-gate)))).astype(jnp.bfloat16)
        project_gate(h_next, nxt)
        res = x_prev_ref[...] + jnp.dot(gated, w_out_ref[...], preferred_element_type=jnp.float32)
        out_ref[...] = res * _rms_scale(res) * g_final_ref[...]
        x_prev_ref[...] = x_ref[...]


def _layer(x2, g_in, w_in_bf16, mkv, conv_w, conv_b, g_sb, g_conv, g_mem, w_out_bf16, g_final):
    t, d = x2.shape
    assert t % Q_BLOCK == 0 and Q_BLOCK % SB_TILE == 0
    n_chain = (Q_BLOCK // SB_TILE) * (SB_WIDTH // LANES)

    def whole(a):
        return pl.BlockSpec(a.shape, lambda s: (0,) * a.ndim)

    resident = pl.BlockSpec(memory_space=pltpu.VMEM)
    n_blk = t // Q_BLOCK
    return pl.pallas_call(
        _layer_kernel,
        grid=(n_blk + 1,),
        in_specs=[pl.BlockSpec((Q_BLOCK, d), lambda s: (jnp.minimum(s, n_blk - 1), 0)), whole(g_in), resident,
                  whole(mkv), whole(conv_w), whole(conv_b), whole(g_sb), whole(g_conv),
                  whole(g_mem), resident, whole(g_final)],
        out_specs=pl.BlockSpec((Q_BLOCK, d), lambda s: (jnp.maximum(s - 1, 0), 0)),
        out_shape=jax.ShapeDtypeStruct((t, d), jnp.float32),
        scratch_shapes=[pltpu.VMEM((t, SB_WIDTH), jnp.bfloat16),
                        pltpu.VMEM((t, SB_WIDTH), jnp.bfloat16),
                        pltpu.VMEM((Q_BLOCK, 2 * SB_WIDTH), jnp.bfloat16),
                        pltpu.VMEM((2, Q_BLOCK, SB_WIDTH + MEM_WIDTH), jnp.bfloat16),
                        pltpu.VMEM((2, Q_BLOCK, 3 * CONV_WIDTH + MIX_WIDTH), jnp.float32),
                        pltpu.VMEM((Q_BLOCK, d), jnp.float32),
                        pltpu.VMEM((SUBLANES + Q_BLOCK, CONV_WIDTH), jnp.float32),
                        pltpu.VMEM((n_chain, SB_TILE, LANES), jnp.float32),
                        pltpu.VMEM((n_chain, 2 * SB_TILE, LANES), jnp.float32)],
        compiler_params=pltpu.CompilerParams(
            dimension_semantics=("arbitrary",), vmem_limit_bytes=VMEM_LIMIT_BYTES),
        name="layer",
    )(x2, g_in, w_in_bf16, mkv, conv_w, conv_b, g_sb, g_conv, g_mem, w_out_bf16, g_final)


def kernel(x, mem, g_in, w_in, conv_w, conv_b, g_mem, w_mem_kv, g_sb_out, g_conv_out,
           g_mem_out, w_out, g_final):
    batch, t, d = x.shape
    assert batch == 1 and g_in.shape[0] == 1
    mkv = _memory_kv(mem.reshape(mem.shape[1], d), g_mem[0][None, :], w_mem_kv[0].astype(jnp.bfloat16))
    out = _layer(x.reshape(t, d), g_in[0][None, :], w_in[0].astype(jnp.bfloat16), mkv,
                 conv_w[0], conv_b[0][None, :], g_sb_out[0][None, :], g_conv_out[0][None, :],
                 g_mem_out[0][None, :], w_out[0].astype(jnp.bfloat16), g_final[None, :])
    return out.reshape(batch, t, d)
```

```python
import jax
import jax.numpy as jnp
from jax import lax
from jax.experimental import pallas as pl
from jax.experimental.pallas import tpu as pltpu

HEAD_DIM = 64
SB_HEADS = 8
SB_WIDTH = SB_HEADS * HEAD_DIM
CONV_WIDTH = 4 * HEAD_DIM
MEM_HEADS = 4
MEM_WIDTH = MEM_HEADS * HEAD_DIM
CONV_K = 3
EPS = 1e-6
SCALE = HEAD_DIM ** -0.5

LANES = 128
SUBLANES = 8
MIX_WIDTH = SB_WIDTH + CONV_WIDTH + MEM_WIDTH
PROJ_COLS = 512
Q_BLOCK = 256
SB_TILE = 128
SWEEP_STOP = 104.0
TILE2_ROWS = 32
LOG2_E = 1.4426950408889634
LOGIT2_CLAMP = 126.0
VMEM_LIMIT_BYTES = 62 * 1024 * 1024


def _rms_scale(xf, eps=EPS):
    return lax.rsqrt(jnp.mean(xf * xf, axis=-1, keepdims=True) + eps)


def _mem_kv_kernel(mem_ref, g_ref, w_ref, kv_ref):
    m = mem_ref[...]
    h = (m * _rms_scale(m) * g_ref[...]).astype(jnp.bfloat16)
    kv_ref[...] = jnp.dot(h, w_ref[...], preferred_element_type=jnp.float32).astype(jnp.bfloat16)


def _memory_kv(mem2, g_mem, w_kv_bf16):
    n_mem = mem2.shape[0]
    return pl.pallas_call(
        _mem_kv_kernel,
        out_shape=jax.ShapeDtypeStruct((n_mem, 2 * MEM_WIDTH), jnp.bfloat16),
        name="memory_kv",
    )(mem2, g_mem, w_kv_bf16)


def _dot_nt(a, b):
    return lax.dot_general(a, b, (((1,), (1,)), ((), ())), preferred_element_type=jnp.float32)


def _split_heads(a, first_half):
    zero = jnp.zeros_like(a)
    return jnp.concatenate([jnp.where(first_half, a, zero), jnp.where(first_half, zero, a)], axis=0)


def _sb_sweep(i, q_ref, k_ref, v_ref, acc_ref, carry_ref, fillers):
    n_sub = q_ref.shape[0] // SB_TILE
    n_pair = SB_WIDTH // LANES
    chains = [(sub, p) for sub in range(n_sub) for p in range(n_pair)]
    fillers = list(fillers)

    first_half = lax.broadcasted_iota(jnp.int32, (SB_TILE, LANES), 1) < HEAD_DIM
    r2 = lax.broadcasted_iota(jnp.int32, (2 * SB_TILE, SB_TILE), 0)
    c2 = lax.broadcasted_iota(jnp.int32, (2 * SB_TILE, SB_TILE), 1)
    strict = c2 < jnp.where(r2 >= SB_TILE, r2 - SB_TILE, r2)
    rt = lax.broadcasted_iota(jnp.int32, (2 * SB_TILE, 2 * SB_TILE), 0)
    ct = lax.broadcasted_iota(jnp.int32, (2 * SB_TILE, 2 * SB_TILE), 1)
    top, left = rt < SB_TILE, ct < SB_TILE
    tri2 = jnp.where(top, rt, rt - SB_TILE) >= jnp.where(left, ct, ct - SB_TILE)

    def as_matrix(cond):
        return jnp.where(cond, 1.0, 0.0).astype(jnp.bfloat16)

    w_01 = as_matrix(jnp.logical_or(jnp.logical_and(top, jnp.logical_not(left)),
                                    jnp.logical_and(tri2, jnp.logical_not(jnp.logical_xor(top, left)))))
    w_2 = as_matrix(jnp.logical_or(top, jnp.logical_or(tri2, jnp.logical_not(left))))
    w_1 = w_2[SB_TILE:]

    q_stack = [_split_heads(q_ref[sub * SB_TILE:(sub + 1) * SB_TILE, p * LANES:(p + 1) * LANES],
                            first_half) for sub, p in chains]

    def tile_rows(c, m):
        jb = n_sub * i + chains[c][0] - m
        return jb >= 0, pl.ds(pl.multiple_of(jnp.maximum(jb, 0) * SB_TILE, SB_TILE), SB_TILE)

    def head_rows(a, n):
        return jnp.concatenate([a[:n], a[SB_TILE:SB_TILE + n]], axis=0)

    def logits(q, c, rows):
        p = chains[c][1]
        return jnp.minimum(_dot_nt(q, k_ref[rows, p * LANES:(p + 1) * LANES]), LOGIT2_CLAMP)

    def neg_log2_not_beta(z):
        return jnp.log(1.0 + jnp.exp2(z)) * LOG2_E

    def weights(z, s, diagonal=False):
        w = jnp.exp2(z - s)
        if diagonal:
            w = jnp.where(strict, w, 0.0)
        wb = w.astype(jnp.bfloat16)
        n = wb.shape[0] // 2
        return jnp.concatenate([wb[:n], wb[n:]], axis=1)

    def values(c, rows, valid=None):
        p = chains[c][1]
        v_blk = v_ref[rows, p * LANES:(p + 1) * LANES]
        if valid is not None:
            v_blk = jnp.where(valid, v_blk, jnp.zeros_like(v_blk))
        return _split_heads(v_blk, first_half)

    def all_done(carries):
        floor = carries[0]
        for carry in carries[1:]:
            floor = jnp.minimum(floor, carry)
        return jnp.min(floor) >= SWEEP_STOP * LOG2_E

    def first_sweep():
        n = TILE2_ROWS
        rows = [[tile_rows(c, m) for m in range(3)] for c in range(len(chains))]
        zs = [(logits(q_stack[c], c, rows[c][0][1]), logits(q_stack[c], c, rows[c][1][1]),
               logits(head_rows(q_stack[c], n), c, rows[c][2][1])) for c in range(len(chains))]
        if fillers:
            fillers.pop(0)()
        lhs_01, lhs_2 = [], []
        for z0, z1, z2 in zs:
            l0 = jnp.where(strict, neg_log2_not_beta(z0), 0.0)
            l1, l2 = neg_log2_not_beta(z1), neg_log2_not_beta(z2)
            lhs_01.append(jnp.concatenate([l0.astype(jnp.bfloat16), l1.astype(jnp.bfloat16)], axis=1))
            lhs_2.append(jnp.concatenate([head_rows(l0 + l1, n).astype(jnp.bfloat16),
                                          l2.astype(jnp.bfloat16)], axis=1))
        s_01 = jnp.dot(jnp.concatenate(lhs_01, axis=0), w_01, preferred_element_type=jnp.float32)
        s_2 = jnp.dot(jnp.concatenate(lhs_2, axis=0), w_2, preferred_element_type=jnp.float32)
        if fillers:
            fillers.pop(0)()
        reached, w_01s, w_2s = [], [], []
        for c, (z0, z1, z2) in enumerate(zs):
            part = slice(c * 2 * SB_TILE, (c + 1) * 2 * SB_TILE)
            part2 = slice(c * 2 * n, (c + 1) * 2 * n)
            w_01s.append(jnp.concatenate([weights(z0, s_01[part, :SB_TILE], diagonal=True),
                                          weights(z1, s_01[part, SB_TILE:])], axis=1))
            w_2s.append(weights(z2, s_2[part2, :SB_TILE]))
            after_1 = s_01[part, SB_TILE:SB_TILE + 1]
            after_2 = s_2[part2, SB_TILE:SB_TILE + 1]
            carry_ref[c] = after_1
            reached.append(jnp.concatenate([after_2[:n], after_1[n:SB_TILE],
                                            after_2[n:], after_1[SB_TILE + n:]], axis=0))
        done = all_done(reached)
        for c in range(len(chains)):
            acc = jnp.dot(w_01s[c], jnp.concatenate([values(c, rows[c][0][1]),
                                                     values(c, rows[c][1][1], rows[c][1][0])], axis=0),
                          preferred_element_type=jnp.float32)
            tile_2 = jnp.dot(w_2s[c], values(c, rows[c][2][1], rows[c][2][0]),
                             preferred_element_type=jnp.float32)
            acc_ref[c, n:, :] = acc[n:]
            acc_ref[c, :n, :] = acc[:n] + jnp.where(done, tile_2, 0.0)
        return 2, done

    def next_sweep(m):
        rows = [tile_rows(c, m) for c in range(len(chains))]
        zs = [logits(q_stack[c], c, rows[c][1]) for c in range(len(chains))]
        ls = [neg_log2_not_beta(z).astype(jnp.bfloat16) for z in zs]
        cs = jnp.dot(jnp.concatenate(ls, axis=0), w_1, preferred_element_type=jnp.float32)
        carries, w_cats = [], []
        for c, z in enumerate(zs):
            part = slice(c * 2 * SB_TILE, (c + 1) * 2 * SB_TILE)
            old = carry_ref[c]
            w_cats.append(weights(z, cs[part, :SB_TILE] + old))
            carries.append(cs[part, SB_TILE:SB_TILE + 1] + old)
            carry_ref[c] = carries[c]
        for c, (sub, _) in enumerate(chains):
            valid = rows[c][0] if sub < n_sub - 1 else None
            acc_ref[c] += jnp.dot(w_cats[c], values(c, rows[c][1], valid), preferred_element_type=jnp.float32)
        return all_done(carries)

    m0, done0 = first_sweep()
    last = n_sub * i + n_sub - 1

    def cond(state):
        m, done = state
        return jnp.logical_and(m <= last, jnp.logical_not(done))

    def body(state):
        m, _ = state
        return m + 1, next_sweep(m)

    lax.while_loop(cond, body, (jnp.int32(m0), done0))
    return jnp.concatenate(
        [jnp.concatenate([acc_ref[sub * n_pair + p] for p in range(n_pair)], axis=1)
         for sub in range(n_sub)], axis=0)


def _mem_attention(q_ref, mkv_ref):
    q_rows = q_ref.shape[0]
    n_mem = mkv_ref.shape[0]
    first_half = lax.broadcasted_iota(jnp.int32, (q_rows, LANES), 1) < HEAD_DIM
    first_half_m = lax.broadcasted_iota(jnp.int32, (n_mem, LANES), 1) < HEAD_DIM
    head_ones = ((lax.broadcasted_iota(jnp.int32, (2 * n_mem, LANES), 1) < HEAD_DIM)
                 == (lax.broadcasted_iota(jnp.int32, (2 * n_mem, LANES), 0) < n_mem)
                 ).astype(jnp.float32).astype(jnp.bfloat16)
    outs = []
    for p in range(MEM_WIDTH // LANES):
        qs = _split_heads(q_ref[:, SB_WIDTH + p * LANES:SB_WIDTH + (p + 1) * LANES], first_half)
        mk = mkv_ref[:, p * LANES:(p + 1) * LANES]
        mv = mkv_ref[:, MEM_WIDTH + p * LANES:MEM_WIDTH + (p + 1) * LANES]
        s = _dot_nt(qs, mk)
        e = jnp.exp(s - jnp.max(s, axis=-1, keepdims=True)).astype(jnp.bfloat16)
        e_cat = jnp.concatenate([e[:q_rows], e[q_rows:]], axis=1)
        num_den = jnp.dot(e_cat, jnp.concatenate([_split_heads(mv, first_half_m), head_ones], axis=1),
                          preferred_element_type=jnp.float32)
        outs.append(num_den[:, :LANES] / num_den[:, LANES:])
    return jnp.concatenate(outs, axis=-1)


def _layer_kernel(x_ref, g_in_ref, w_in_ref, mkv_ref, convw_ref, convb_ref,
                  g_sb_ref, g_conv_ref, g_mem_ref, w_out_ref, g_final_ref,
                  out_ref, k_ref, v_ref, kv_next_ref, q2_ref, rest2_ref, x_prev_ref,
                  cu_ref, acc_ref, carry_ref):
    s = pl.program_id(0)
    q_rows = x_ref.shape[0]
    c_u = 3 * SB_WIDTH
    c_qm = c_u + 3 * CONV_WIDTH
    c_gate = c_qm + MEM_WIDTH

    def normed_input():
        x = x_ref[...]
        return (x * _rms_scale(x) * g_in_ref[...]).astype(jnp.bfloat16)

    def proj(h, c0, width):
        return jnp.dot(h, w_in_ref[:, c0:c0 + width], preferred_element_type=jnp.float32)

    def project_kv(h):
        kv_next_ref[...] = proj(h, SB_WIDTH, 2 * SB_WIDTH).astype(jnp.bfloat16)

    def project_q_conv(h, slot):
        q2_ref[slot, :, :SB_WIDTH] = (proj(h, 0, SB_WIDTH) * (SCALE * LOG2_E)).astype(jnp.bfloat16)
        q2_ref[slot, :, SB_WIDTH:] = (proj(h, c_qm, MEM_WIDTH) * SCALE).astype(jnp.bfloat16)
        rest2_ref[slot, :, :3 * CONV_WIDTH] = proj(h, c_u, 3 * CONV_WIDTH)

    def project_gate(h, slot):
        for c0 in range(0, MIX_WIDTH, PROJ_COLS):
            rest2_ref[slot, :, 3 * CONV_WIDTH + c0:3 * CONV_WIDTH + c0 + PROJ_COLS] = proj(h, c_gate + c0, PROJ_COLS)

    @pl.when(s == 0)
    def _():
        h = normed_input()
        project_kv(h)
        project_q_conv(h, 0)
        project_gate(h, 0)
        x_prev_ref[...] = x_ref[...]
        cu_ref[0:SUBLANES, :] = jnp.zeros((SUBLANES, CONV_WIDTH), jnp.float32)

    @pl.when(s > 0)
    def _():
        i = s - 1
        cur = lax.rem(i, 2)
        nxt = 1 - cur
        q_ref = q2_ref.at[cur]
        rest_ref = rest2_ref.at[cur]
        blk = pl.ds(pl.multiple_of(i * q_rows, q_rows), q_rows)
        k_ref[blk, :] = kv_next_ref[:, :SB_WIDTH]
        v_ref[blk, :] = kv_next_ref[:, SB_WIDTH:]

        h_next = normed_input()
        y_sb = _sb_sweep(i, q_ref, k_ref, v_ref, acc_ref, carry_ref,
                         (lambda: project_kv(h_next), lambda: project_q_conv(h_next, nxt)))

        u = rest_ref[:, 0:CONV_WIDTH]
        b = rest_ref[:, CONV_WIDTH:2 * CONV_WIDTH]
        c = rest_ref[:, 2 * CONV_WIDTH:3 * CONV_WIDTH]
        cu = c * u
        cu_ref[SUBLANES:, :] = cu
        cu_1 = cu_ref[SUBLANES - 1:SUBLANES - 1 + q_rows, :]
        cu_2 = cu_ref[SUBLANES - 2:SUBLANES - 2 + q_rows, :]
        cu_ref[0:SUBLANES, :] = cu[q_rows - SUBLANES:, :]
        conv = (convw_ref[0:1, :] * cu_2 + convw_ref[1:2, :] * cu_1 + convw_ref[2:3, :] * cu
                + convb_ref[...])
        y_conv = b * conv

        y_mem = _mem_attention(q_ref, mkv_ref)

        y = jnp.concatenate([y_sb * _rms_scale(y_sb) * g_sb_ref[...],
                             y_conv * _rms_scale(y_conv) * g_conv_ref[...],
                             y_mem * _rms_scale(y_mem) * g_mem_ref[...]], axis=-1)
        gate = rest_ref[:, 3 * CONV_WIDTH:]
        gated = (y * (gate / (1.0 + jnp.exp(-gate)))).astype(jnp.bfloat16)
        project_gate(h_next, nxt)
        res = x_prev_ref[...] + jnp.dot(gated, w_out_ref[...], preferred_element_type=jnp.float32)
        out_ref[...] = res * _rms_scale(res) * g_final_ref[...]
        x_prev_ref[...] = x_ref[...]


def _layer(x2, g_in, w_in_bf16, mkv, conv_w, conv_b, g_sb, g_conv, g_mem, w_out_bf16, g_final):
    t, d = x2.shape
    assert t % Q_BLOCK == 0 and Q_BLOCK % SB_TILE == 0
    n_chain = (Q_BLOCK // SB_TILE) * (SB_WIDTH // LANES)

    def whole(a):
        return pl.BlockSpec(a.shape, lambda s: (0,) * a.ndim)

    resident = pl.BlockSpec(memory_space=pltpu.VMEM)
    n_blk = t // Q_BLOCK
    return pl.pallas_call(
        _layer_kernel,
        grid=(n_blk + 1,),
        in_specs=[pl.BlockSpec((Q_BLOCK, d), lambda s: (jnp.minimum(s, n_blk - 1), 0)), whole(g_in), resident,
                  whole(mkv), whole(conv_w), whole(conv_b), whole(g_sb), whole(g_conv),
                  whole(g_mem), resident, whole(g_final)],
        out_specs=pl.BlockSpec((Q_BLOCK, d), lambda s: (jnp.maximum(s - 1, 0), 0)),
        out_shape=jax.ShapeDtypeStruct((t, d), jnp.float32),
        scratch_shapes=[pltpu.VMEM((t, SB_WIDTH), jnp.bfloat16),
                        pltpu.VMEM((t, SB_WIDTH), jnp.bfloat16),
                        pltpu.VMEM((Q_BLOCK, 2 * SB_WIDTH), jnp.bfloat16),
                        pltpu.VMEM((2, Q_BLOCK, SB_WIDTH + MEM_WIDTH), jnp.bfloat16),
                        pltpu.VMEM((2, Q_BLOCK, 3 * CONV_WIDTH + MIX_WIDTH), jnp.float32),
                        pltpu.VMEM((Q_BLOCK, d), jnp.float32),
                        pltpu.VMEM((SUBLANES + Q_BLOCK, CONV_WIDTH), jnp.float32),
                        pltpu.VMEM((n_chain, SB_TILE, LANES), jnp.float32),
                        pltpu.VMEM((n_chain, 2 * SB_TILE, 1), jnp.float32)],
        compiler_params=pltpu.CompilerParams(
            dimension_semantics=("arbitrary",), vmem_limit_bytes=VMEM_LIMIT_BYTES),
        name="layer",
    )(x2, g_in, w_in_bf16, mkv, conv_w, conv_b, g_sb, g_conv, g_mem, w_out_bf16, g_final)


def kernel(x, mem, g_in, w_in, conv_w, conv_b, g_mem, w_mem_kv, g_sb_out, g_conv_out,
           g_mem_out, w_out, g_final):
    batch, t, d = x.shape
    assert batch == 1 and g_in.shape[0] == 1
    mkv = _memory_kv(mem.reshape(mem.shape[1], d), g_mem[0][None, :], w_mem_kv[0].astype(jnp.bfloat16))
    out = _layer(x.reshape(t, d), g_in[0][None, :], w_in[0].astype(jnp.bfloat16), mkv,
                 conv_w[0], conv_b[0][None, :], g_sb_out[0][None, :], g_conv_out[0][None, :],
                 g_mem_out[0][None, :], w_out[0].astype(jnp.bfloat16), g_final[None, :])
    return out.reshape(batch, t, d)
```

```python
import jax
import jax.numpy as jnp
from jax import lax
from jax.experimental import pallas as pl
from jax.experimental.pallas import tpu as pltpu

HEAD_DIM = 64
SB_HEADS = 8
SB_WIDTH = SB_HEADS * HEAD_DIM
CONV_WIDTH = 4 * HEAD_DIM
MEM_HEADS = 4
MEM_WIDTH = MEM_HEADS * HEAD_DIM
CONV_K = 3
EPS = 1e-6
SCALE = HEAD_DIM ** -0.5

LANES = 128
SUBLANES = 8
MIX_WIDTH = SB_WIDTH + CONV_WIDTH + MEM_WIDTH
PROJ_COLS = MIX_WIDTH // 2
Q_BLOCK = 256
SB_TILE = 128
SWEEP_STOP = 104.0
TILE2_ROWS = 32
KV_PAD = 2 * SB_TILE
LOG2_E = 1.4426950408889634
LOGIT2_CLAMP = 126.0
VMEM_LIMIT_BYTES = 62 * 1024 * 1024


def _rms_scale(xf, eps=EPS):
    return lax.rsqrt(jnp.mean(xf * xf, axis=-1, keepdims=True) + eps)


def _mem_kv_kernel(mem_ref, g_ref, w_ref, kv_ref):
    m = mem_ref[...]
    h = (m * _rms_scale(m) * g_ref[...]).astype(jnp.bfloat16)
    kv_ref[...] = jnp.dot(h, w_ref[...], preferred_element_type=jnp.float32).astype(jnp.bfloat16)


def _memory_kv(mem2, g_mem, w_kv_bf16):
    n_mem = mem2.shape[0]
    return pl.pallas_call(
        _mem_kv_kernel,
        out_shape=jax.ShapeDtypeStruct((n_mem, 2 * MEM_WIDTH), jnp.bfloat16),
        name="memory_kv",
    )(mem2, g_mem, w_kv_bf16)


def _dot_nt(a, b):
    return lax.dot_general(a, b, (((1,), (1,)), ((), ())), preferred_element_type=jnp.float32)


def _split_heads(a):
    first_half = lax.broadcasted_iota(jnp.int32, a.shape, 1) < HEAD_DIM
    zero = jnp.zeros_like(a)
    return jnp.concatenate([jnp.where(first_half, a, zero), jnp.where(first_half, zero, a)], axis=0)


def _sb_sweep(i, q_ref, k_ref, v_ref, acc_ref, carry_ref, fillers):
    n_sub = q_ref.shape[0] // SB_TILE
    n_pair = SB_WIDTH // LANES
    chains = [(sub, p) for sub in range(n_sub) for p in range(n_pair)]
    fillers = list(fillers)

    rt = lax.broadcasted_iota(jnp.int32, (2 * SB_TILE, 2 * SB_TILE), 0)
    ct = lax.broadcasted_iota(jnp.int32, (2 * SB_TILE, 2 * SB_TILE), 1)
    top, left = rt < SB_TILE, ct < SB_TILE
    tri2 = jnp.where(top, rt, rt - SB_TILE) >= jnp.where(left, ct, ct - SB_TILE)
    causal = jnp.logical_or(left, jnp.where(left, ct, ct - SB_TILE) < jnp.where(top, rt, rt - SB_TILE))

    def as_matrix(cond):
        return jnp.where(cond, 1.0, 0.0).astype(jnp.bfloat16)

    w_10 = as_matrix(jnp.logical_or(jnp.logical_and(jnp.logical_not(top), left),
                                    jnp.logical_and(tri2, jnp.logical_not(jnp.logical_xor(top, left)))))
    w_2 = as_matrix(jnp.logical_or(top, jnp.logical_or(tri2, jnp.logical_not(left))))
    w_1 = w_2[SB_TILE:]

    q_stack = [_split_heads(q_ref[sub * SB_TILE:(sub + 1) * SB_TILE, p * LANES:(p + 1) * LANES])
               for sub, p in chains]

    def tile_rows(c, m, n_tiles=1):
        oldest = n_sub * i + chains[c][0] - (m + n_tiles - 1)
        return pl.ds(pl.multiple_of(KV_PAD + oldest * SB_TILE, SB_TILE), n_tiles * SB_TILE)

    def head_rows(a, n):
        return jnp.concatenate([a[:n], a[SB_TILE:SB_TILE + n]], axis=0)

    def logits(q, c, rows):
        p = chains[c][1]
        return jnp.minimum(_dot_nt(q, k_ref[rows, p * LANES:(p + 1) * LANES]), LOGIT2_CLAMP)

    def neg_log2_not_beta(z):
        return jnp.log(1.0 + jnp.exp2(z)) * LOG2_E

    def weights(z, s, mask=None):
        w = jnp.exp2(z - s)
        if mask is not None:
            w = jnp.where(mask, w, 0.0)
        wb = w.astype(jnp.bfloat16)
        n = wb.shape[0] // 2
        return jnp.concatenate([wb[:n], wb[n:]], axis=1)

    def values(c, rows):
        p = chains[c][1]
        return _split_heads(v_ref[rows, p * LANES:(p + 1) * LANES])

    def all_done(carries):
        floor = carries[0]
        for carry in carries[1:]:
            floor = jnp.minimum(floor, carry)
        return jnp.min(floor) >= SWEEP_STOP * LOG2_E

    def first_sweep():
        n = TILE2_ROWS
        rows_10 = [tile_rows(c, 0, 2) for c in range(len(chains))]
        rows_2 = [tile_rows(c, 2) for c in range(len(chains))]
        zs = [(logits(q_stack[c], c, rows_10[c]),
               logits(head_rows(q_stack[c], n), c, rows_2[c])) for c in range(len(chains))]
        if fillers:
            fillers.pop(0)()
        lhs_10, lhs_2 = [], []
        for z10, z2 in zs:
            l10 = jnp.where(causal, neg_log2_not_beta(z10), 0.0)
            lhs_10.append(l10.astype(jnp.bfloat16))
            lhs_2.append(jnp.concatenate(
                [head_rows(l10[:, :SB_TILE] + l10[:, SB_TILE:], n).astype(jnp.bfloat16),
                 neg_log2_not_beta(z2).astype(jnp.bfloat16)], axis=1))
        s_10 = jnp.dot(jnp.concatenate(lhs_10, axis=0), w_10, preferred_element_type=jnp.float32)
        s_2 = jnp.dot(jnp.concatenate(lhs_2, axis=0), w_2, preferred_element_type=jnp.float32)
        if fillers:
            fillers.pop(0)()
        reached, w_10s, w_2s = [], [], []
        for c, (z10, z2) in enumerate(zs):
            part = slice(c * 2 * SB_TILE, (c + 1) * 2 * SB_TILE)
            part2 = slice(c * 2 * n, (c + 1) * 2 * n)
            w_10s.append(weights(z10, s_10[part], causal))
            w_2s.append(weights(z2, s_2[part2, :SB_TILE]))
            after_1 = s_10[part, 0:1]
            after_2 = s_2[part2, SB_TILE:SB_TILE + 1]
            carry_ref[c] = after_1
            reached.append(jnp.concatenate([after_2[:n], after_1[n:SB_TILE],
                                            after_2[n:], after_1[SB_TILE + n:]], axis=0))
        done = all_done(reached)
        for c in range(len(chains)):
            acc = jnp.dot(w_10s[c], values(c, rows_10[c]), preferred_element_type=jnp.float32)
            tile_2 = jnp.dot(w_2s[c], values(c, rows_2[c]), preferred_element_type=jnp.float32)
            acc_ref[c, n:, :] = acc[n:]
            acc_ref[c, :n, :] = acc[:n] + jnp.where(done, tile_2, 0.0)
        return 2, done

    def next_sweep(m):
        rows = [tile_rows(c, m) for c in range(len(chains))]
        zs = [logits(q_stack[c], c, rows[c]) for c in range(len(chains))]
        ls = [neg_log2_not_beta(z).astype(jnp.bfloat16) for z in zs]
        cs = jnp.dot(jnp.concatenate(ls, axis=0), w_1, preferred_element_type=jnp.float32)
        carries, w_cats = [], []
        for c, z in enumerate(zs):
            part = slice(c * 2 * SB_TILE, (c + 1) * 2 * SB_TILE)
            old = carry_ref[c]
            w_cats.append(weights(z, cs[part, :SB_TILE] + old))
            carries.append(cs[part, SB_TILE:SB_TILE + 1] + old)
            carry_ref[c] = carries[c]
        for c in range(len(chains)):
            acc_ref[c] += jnp.dot(w_cats[c], values(c, rows[c]), preferred_element_type=jnp.float32)
        return all_done(carries)

    m0, done0 = first_sweep()
    last = n_sub * i + n_sub - 1

    def cond(state):
        m, done = state
        return jnp.logical_and(m <= last, jnp.logical_not(done))

    def body(state):
        m, _ = state
        return m + 1, next_sweep(m)

    lax.while_loop(cond, body, (jnp.int32(m0), done0))
    return jnp.concatenate(
        [jnp.concatenate([acc_ref[sub * n_pair + p] for p in range(n_pair)], axis=1)
         for sub in range(n_sub)], axis=0)


def _mem_attention(q_ref, mkv_ref, filler):
    q_rows = q_ref.shape[0]
    n_mem = mkv_ref.shape[0]
    n_pair = MEM_WIDTH // LANES
    head_ones = ((lax.broadcasted_iota(jnp.int32, (2 * n_mem, LANES), 1) < HEAD_DIM)
                 == (lax.broadcasted_iota(jnp.int32, (2 * n_mem, LANES), 0) < n_mem)
                 ).astype(jnp.float32).astype(jnp.bfloat16)
    logits = [_dot_nt(_split_heads(q_ref[:, SB_WIDTH + p * LANES:SB_WIDTH + (p + 1) * LANES]),
                      mkv_ref[:, p * LANES:(p + 1) * LANES]) for p in range(n_pair)]
    filler()
    outs = []
    for p, s in enumerate(logits):
        mv = mkv_ref[:, MEM_WIDTH + p * LANES:MEM_WIDTH + (p + 1) * LANES]
        e = jnp.exp(s - jnp.max(s, axis=-1, keepdims=True)).astype(jnp.bfloat16)
        e_cat = jnp.concatenate([e[:q_rows], e[q_rows:]], axis=1)
        num_den = jnp.dot(e_cat, jnp.concatenate([_split_heads(mv), head_ones], axis=1),
                          preferred_element_type=jnp.float32)
        outs.append(num_den[:, :LANES] / num_den[:, LANES:])
    return jnp.concatenate(outs, axis=-1)


def _layer_kernel(x_ref, g_in_ref, w_in_ref, mkv_ref, convw_ref, convb_ref,
                  g_sb_ref, g_conv_ref, g_mem_ref, w_out_ref, g_final_ref,
                  out_ref, k_ref, v_ref, kv_next_ref, q2_ref, rest2_ref, x_prev_ref,
                  cu_ref, acc_ref, carry_ref):
    s = pl.program_id(0)
    q_rows = x_ref.shape[0]
    c_u = 3 * SB_WIDTH
    c_qm = c_u + 3 * CONV_WIDTH
    c_gate = c_qm + MEM_WIDTH

    def normed_input():
        x = x_ref[...]
        return (x * _rms_scale(x) * g_in_ref[...]).astype(jnp.bfloat16)

    def proj(h, c0, width):
        return jnp.dot(h, w_in_ref[:, c0:c0 + width], preferred_element_type=jnp.float32)

    def project_kv(h):
        kv_next_ref[...] = proj(h, SB_WIDTH, 2 * SB_WIDTH).astype(jnp.bfloat16)

    def project_q_conv(h, slot):
        q2_ref[slot, :, :SB_WIDTH] = (proj(h, 0, SB_WIDTH) * (SCALE * LOG2_E)).astype(jnp.bfloat16)
        q2_ref[slot, :, SB_WIDTH:] = (proj(h, c_qm, MEM_WIDTH) * SCALE).astype(jnp.bfloat16)
        rest2_ref[slot, :, :3 * CONV_WIDTH] = proj(h, c_u, 3 * CONV_WIDTH)

    def project_gate(h, slot, half):
        c0 = half * PROJ_COLS
        rest2_ref[slot, :, 3 * CONV_WIDTH + c0:3 * CONV_WIDTH + c0 + PROJ_COLS] = proj(h, c_gate + c0, PROJ_COLS)

    @pl.when(s == 0)
    def _():
        h = normed_input()
        project_kv(h)
        project_q_conv(h, 0)
        project_gate(h, 0, 0)
        project_gate(h, 0, 1)
        x_prev_ref[...] = x_ref[...]
        cu_ref[0:SUBLANES, :] = jnp.zeros((SUBLANES, CONV_WIDTH), jnp.float32)
        k_ref[0:KV_PAD, :] = jnp.zeros((KV_PAD, SB_WIDTH), jnp.bfloat16)
        v_ref[0:KV_PAD, :] = jnp.zeros((KV_PAD, SB_WIDTH), jnp.bfloat16)

    @pl.when(s > 0)
    def _():
        i = s - 1
        cur = lax.rem(i, 2)
        nxt = 1 - cur
        q_ref = q2_ref.at[cur]
        rest_ref = rest2_ref.at[cur]
        blk = pl.ds(pl.multiple_of(KV_PAD + i * q_rows, q_rows), q_rows)
        k_ref[blk, :] = kv_next_ref[:, :SB_WIDTH]
        v_ref[blk, :] = kv_next_ref[:, SB_WIDTH:]

        h_next = normed_input()
        y_sb = _sb_sweep(i, q_ref, k_ref, v_ref, acc_ref, carry_ref,
                         (lambda: project_kv(h_next), lambda: project_q_conv(h_next, nxt)))

        u = rest_ref[:, 0:CONV_WIDTH]
        b = rest_ref[:, CONV_WIDTH:2 * CONV_WIDTH]
        c = rest_ref[:, 2 * CONV_WIDTH:3 * CONV_WIDTH]
        cu = c * u
        cu_ref[SUBLANES:, :] = cu
        cu_1 = cu_ref[SUBLANES - 1:SUBLANES - 1 + q_rows, :]
        cu_2 = cu_ref[SUBLANES - 2:SUBLANES - 2 + q_rows, :]
        cu_ref[0:SUBLANES, :] = cu[q_rows - SUBLANES:, :]
        conv = (convw_ref[0:1, :] * cu_2 + convw_ref[1:2, :] * cu_1 + convw_ref[2:3, :] * cu
                + convb_ref[...])
        y_conv = b * conv

        y_mem = _mem_attention(q_ref, mkv_ref, lambda: project_gate(h_next, nxt, 0))

        y = jnp.concatenate([y_sb * _rms_scale(y_sb) * g_sb_ref[...],
                             y_conv * _rms_scale(y_conv) * g_conv_ref[...],
                             y_mem * _rms_scale(y_mem) * g_mem_ref[...]], axis=-1)
        gate = rest_ref[:, 3 * CONV_WIDTH:]
        gated = (y * (gate / (1.0 + jnp.exp(-gate)))).astype(jnp.bfloat16)
        res = x_prev_ref[...] + jnp.dot(gated, w_out_ref[...], preferred_element_type=jnp.float32)
        project_gate(h_next, nxt, 1)
        out_ref[...] = res * _rms_scale(res) * g_final_ref[...]
        x_prev_ref[...] = x_ref[...]


def _layer(x2, g_in, w_in_bf16, mkv, conv_w, conv_b, g_sb, g_conv, g_mem, w_out_bf16, g_final):
    t, d = x2.shape
    assert t % Q_BLOCK == 0 and Q_BLOCK % SB_TILE == 0
    n_chain = (Q_BLOCK // SB_TILE) * (SB_WIDTH // LANES)

    def whole(a):
        return pl.BlockSpec(a.shape, lambda s: (0,) * a.ndim)

    resident = pl.BlockSpec(memory_space=pltpu.VMEM)
    n_blk = t // Q_BLOCK
    return pl.pallas_call(
        _layer_kernel,
        grid=(n_blk + 1,),
        in_specs=[pl.BlockSpec((Q_BLOCK, d), lambda s: (jnp.minimum(s, n_blk - 1), 0)), whole(g_in), resident,
                  whole(mkv), whole(conv_w), whole(conv_b), whole(g_sb), whole(g_conv),
                  whole(g_mem), resident, whole(g_final)],
        out_specs=pl.BlockSpec((Q_BLOCK, d), lambda s: (jnp.maximum(s - 1, 0), 0)),
        out_shape=jax.ShapeDtypeStruct((t, d), jnp.float32),
        scratch_shapes=[pltpu.VMEM((KV_PAD + t, SB_WIDTH), jnp.bfloat16),
                        pltpu.VMEM((KV_PAD + t, SB_WIDTH), jnp.bfloat16),
                        pltpu.VMEM((Q_BLOCK, 2 * SB_WIDTH), jnp.bfloat16),
                        pltpu.VMEM((2, Q_BLOCK, SB_WIDTH + MEM_WIDTH), jnp.bfloat16),
                        pltpu.VMEM((2, Q_BLOCK, 3 * CONV_WIDTH + MIX_WIDTH), jnp.float32),
                        pltpu.VMEM((Q_BLOCK, d), jnp.float32),
                        pltpu.VMEM((SUBLANES + Q_BLOCK, CONV_WIDTH), jnp.float32),
                        pltpu.VMEM((n_chain, SB_TILE, LANES), jnp.float32),
                        pltpu.VMEM((n_chain, 2 * SB_TILE, 1), jnp.float32)],
        compiler_params=pltpu.CompilerParams(
            dimension_semantics=("arbitrary",), vmem_limit_bytes=VMEM_LIMIT_BYTES),
        name="layer",
    )(x2, g_in, w_in_bf16, mkv, conv_w, conv_b, g_sb, g_conv, g_mem, w_out_bf16, g_final)


def kernel(x, mem, g_in, w_in, conv_w, conv_b, g_mem, w_mem_kv, g_sb_out, g_conv_out,
           g_mem_out, w_out, g_final):
    batch, t, d = x.shape
    assert batch == 1 and g_in.shape[0] == 1
    mkv = _memory_kv(mem.reshape(mem.shape[1], d), g_mem[0][None, :], w_mem_kv[0].astype(jnp.bfloat16))
    out = _layer(x.reshape(t, d), g_in[0][None, :], w_in[0].astype(jnp.bfloat16), mkv,
                 conv_w[0], conv_b[0][None, :], g_sb_out[0][None, :], g_conv_out[0][None, :],
                 g_mem_out[0][None, :], w_out[0].astype(jnp.bfloat16), g_final[None, :])
    return out.reshape(batch, t, d)
```

```python
import jax
import jax.numpy as jnp
from jax import lax
from jax.experimental import pallas as pl
from jax.experimental.pallas import tpu as pltpu

HEAD_DIM = 64
SB_HEADS = 8
SB_WIDTH = SB_HEADS * HEAD_DIM
CONV_WIDTH = 4 * HEAD_DIM
MEM_HEADS = 4
MEM_WIDTH = MEM_HEADS * HEAD_DIM
CONV_K = 3
EPS = 1e-6
SCALE = HEAD_DIM ** -0.5

LANES = 128
SUBLANES = 8
MIX_WIDTH = SB_WIDTH + CONV_WIDTH + MEM_WIDTH
PROJ_COLS = 512
Q_BLOCK = 256
SB_TILE = 128
SWEEP_STOP = 104.0
TILE2_ROWS = 32
KV_PAD = 2 * SB_TILE
LOG2_E = 1.4426950408889634
LOGIT2_CLAMP = 126.0
VMEM_LIMIT_BYTES = 62 * 1024 * 1024


def _rms_scale(xf, eps=EPS):
    return lax.rsqrt(jnp.mean(xf * xf, axis=-1, keepdims=True) + eps)


def _mem_kv_kernel(mem_ref, g_ref, w_ref, kv_ref):
    m = mem_ref[...]
    h = (m * _rms_scale(m) * g_ref[...]).astype(jnp.bfloat16)
    kv_ref[...] = jnp.dot(h, w_ref[...], preferred_element_type=jnp.float32).astype(jnp.bfloat16)


def _memory_kv(mem2, g_mem, w_kv_bf16):
    n_mem = mem2.shape[0]
    return pl.pallas_call(
        _mem_kv_kernel,
        out_shape=jax.ShapeDtypeStruct((n_mem, 2 * MEM_WIDTH), jnp.bfloat16),
        name="memory_kv",
    )(mem2, g_mem, w_kv_bf16)


def _dot_nt(a, b):
    return lax.dot_general(a, b, (((1,), (1,)), ((), ())), preferred_element_type=jnp.float32)


def _split_heads(a):
    first_half = lax.broadcasted_iota(jnp.int32, a.shape, 1) < HEAD_DIM
    zero = jnp.zeros_like(a)
    return jnp.concatenate([jnp.where(first_half, a, zero), jnp.where(first_half, zero, a)], axis=0)


def _sb_sweep(i, q_ref, k_ref, v_ref, acc_ref, carry_ref, fillers):
    n_sub = q_ref.shape[0] // SB_TILE
    n_pair = SB_WIDTH // LANES
    chains = [(sub, p) for sub in range(n_sub) for p in range(n_pair)]
    fillers = list(fillers)

    rt = lax.broadcasted_iota(jnp.int32, (2 * SB_TILE, 2 * SB_TILE), 0)
    ct = lax.broadcasted_iota(jnp.int32, (2 * SB_TILE, 2 * SB_TILE), 1)
    top, left = rt < SB_TILE, ct < SB_TILE
    tri2 = jnp.where(top, rt, rt - SB_TILE) >= jnp.where(left, ct, ct - SB_TILE)
    causal = jnp.logical_or(left, jnp.where(left, ct, ct - SB_TILE) < jnp.where(top, rt, rt - SB_TILE))

    def as_matrix(cond):
        return jnp.where(cond, 1.0, 0.0).astype(jnp.bfloat16)

    w_10 = as_matrix(jnp.logical_or(jnp.logical_and(jnp.logical_not(top), left),
                                    jnp.logical_and(tri2, jnp.logical_not(jnp.logical_xor(top, left)))))
    w_2 = as_matrix(jnp.logical_or(top, jnp.logical_or(tri2, jnp.logical_not(left))))
    w_1 = w_2[SB_TILE:]

    q_stack = [_split_heads(q_ref[sub * SB_TILE:(sub + 1) * SB_TILE, p * LANES:(p + 1) * LANES])
               for sub, p in chains]

    def tile_rows(c, m, n_tiles=1):
        oldest = n_sub * i + chains[c][0] - (m + n_tiles - 1)
        return pl.ds(pl.multiple_of(KV_PAD + oldest * SB_TILE, SB_TILE), n_tiles * SB_TILE)

    def head_rows(a, n):
        return jnp.concatenate([a[:n], a[SB_TILE:SB_TILE + n]], axis=0)

    def logits(q, c, rows):
        p = chains[c][1]
        return jnp.minimum(_dot_nt(q, k_ref[rows, p * LANES:(p + 1) * LANES]), LOGIT2_CLAMP)

    def neg_log2_not_beta(z):
        return jnp.log(1.0 + jnp.exp2(z)) * LOG2_E

    def weights(z, s, mask=None):
        w = jnp.exp2(z - s)
        if mask is not None:
            w = jnp.where(mask, w, 0.0)
        wb = w.astype(jnp.bfloat16)
        n = wb.shape[0] // 2
        return jnp.concatenate([wb[:n], wb[n:]], axis=1)

    def values(c, rows):
        p = chains[c][1]
        return _split_heads(v_ref[rows, p * LANES:(p + 1) * LANES])

    def all_done(carries):
        floor = carries[0]
        for carry in carries[1:]:
            floor = jnp.minimum(floor, carry)
        return jnp.min(floor) >= SWEEP_STOP * LOG2_E

    def first_sweep():
        n = TILE2_ROWS
        rows_10 = [tile_rows(c, 0, 2) for c in range(len(chains))]
        rows_2 = [tile_rows(c, 2) for c in range(len(chains))]
        zs = [(logits(q_stack[c], c, rows_10[c]),
               logits(head_rows(q_stack[c], n), c, rows_2[c])) for c in range(len(chains))]
        if fillers:
            fillers.pop(0)()
        lhs_10, lhs_2 = [], []
        for z10, z2 in zs:
            l10 = jnp.where(causal, neg_log2_not_beta(z10), 0.0)
            lhs_10.append(l10.astype(jnp.bfloat16))
            lhs_2.append(jnp.concatenate(
                [head_rows(l10[:, :SB_TILE] + l10[:, SB_TILE:], n).astype(jnp.bfloat16),
                 neg_log2_not_beta(z2).astype(jnp.bfloat16)], axis=1))
        s_10 = jnp.dot(jnp.concatenate(lhs_10, axis=0), w_10, preferred_element_type=jnp.float32)
        s_2 = jnp.dot(jnp.concatenate(lhs_2, axis=0), w_2, preferred_element_type=jnp.float32)
        if fillers:
            fillers.pop(0)()
        reached, w_10s, w_2s = [], [], []
        for c, (z10, z2) in enumerate(zs):
            part = slice(c * 2 * SB_TILE, (c + 1) * 2 * SB_TILE)
            part2 = slice(c * 2 * n, (c + 1) * 2 * n)
            w_10s.append(weights(z10, s_10[part], causal))
            w_2s.append(weights(z2, s_2[part2, :SB_TILE]))
            after_1 = s_10[part, 0:1]
            after_2 = s_2[part2, SB_TILE:SB_TILE + 1]
            carry_ref[c] = after_1
            reached.append(jnp.concatenate([after_2[:n], after_1[n:SB_TILE],
                                            after_2[n:], after_1[SB_TILE + n:]], axis=0))
        done = all_done(reached)
        for c in range(len(chains)):
            acc = jnp.dot(w_10s[c], values(c, rows_10[c]), preferred_element_type=jnp.float32)
            tile_2 = jnp.dot(w_2s[c], values(c, rows_2[c]), preferred_element_type=jnp.float32)
            acc_ref[c, n:, :] = acc[n:]
            acc_ref[c, :n, :] = acc[:n] + jnp.where(done, tile_2, 0.0)
        return 2, done

    def next_sweep(m):
        rows = [tile_rows(c, m) for c in range(len(chains))]
        zs = [logits(q_stack[c], c, rows[c]) for c in range(len(chains))]
        ls = [neg_log2_not_beta(z).astype(jnp.bfloat16) for z in zs]
        cs = jnp.dot(jnp.concatenate(ls, axis=0), w_1, preferred_element_type=jnp.float32)
        carries, w_cats = [], []
        for c, z in enumerate(zs):
            part = slice(c * 2 * SB_TILE, (c + 1) * 2 * SB_TILE)
            old = carry_ref[c]
            w_cats.append(weights(z, cs[part, :SB_TILE] + old))
            carries.append(cs[part, SB_TILE:SB_TILE + 1] + old)
            carry_ref[c] = carries[c]
        for c in range(len(chains)):
            acc_ref[c] += jnp.dot(w_cats[c], values(c, rows[c]), preferred_element_type=jnp.float32)
        return all_done(carries)

    m0, done0 = first_sweep()
    last = n_sub * i + n_sub - 1

    def cond(state):
        m, done = state
        return jnp.logical_and(m <= last, jnp.logical_not(done))

    def body(state):
        m, _ = state
        return m + 1, next_sweep(m)

    lax.while_loop(cond, body, (jnp.int32(m0), done0))
    return jnp.concatenate(
        [jnp.concatenate([acc_ref[sub * n_pair + p] for p in range(n_pair)], axis=1)
         for sub in range(n_sub)], axis=0)


def _mem_attention(q_ref, mkv_ref, filler):
    q_rows = q_ref.shape[0]
    n_mem = mkv_ref.shape[0]
    n_pair = MEM_WIDTH // LANES
    head_ones = ((lax.broadcasted_iota(jnp.int32, (2 * n_mem, LANES), 1) < HEAD_DIM)
                 == (lax.broadcasted_iota(jnp.int32, (2 * n_mem, LANES), 0) < n_mem)
                 ).astype(jnp.float32).astype(jnp.bfloat16)
    logits = [_dot_nt(_split_heads(q_ref[:, SB_WIDTH + p * LANES:SB_WIDTH + (p + 1) * LANES]),
                      mkv_ref[:, p * LANES:(p + 1) * LANES]) for p in range(n_pair)]
    filler()
    outs = []
    for p, s in enumerate(logits):
        mv = mkv_ref[:, MEM_WIDTH + p * LANES:MEM_WIDTH + (p + 1) * LANES]
        e = jnp.exp(s - jnp.max(s, axis=-1, keepdims=True)).astype(jnp.bfloat16)
        e_cat = jnp.concatenate([e[:q_rows], e[q_rows:]], axis=1)
        num_den = jnp.dot(e_cat, jnp.concatenate([_split_heads(mv), head_ones], axis=1),
                          preferred_element_type=jnp.float32)
        outs.append(num_den[:, :LANES] / num_den[:, LANES:])
    return jnp.concatenate(outs, axis=-1)


def _layer_kernel(x_ref, g_in_ref, w_in_ref, mkv_ref, convw_ref, convb_ref,
                  g_sb_ref, g_conv_ref, g_mem_ref, w_out_ref, g_final_ref,
                  out_ref, k_ref, v_ref, kv_next_ref, q2_ref, rest2_ref, x_prev_ref,
                  cu_ref, acc_ref, carry_ref):
    s = pl.program_id(0)
    q_rows = x_ref.shape[0]
    c_u = 3 * SB_WIDTH
    c_qm = c_u + 3 * CONV_WIDTH
    c_gate = c_qm + MEM_WIDTH

    def normed_input():
        x = x_ref[...]
        return (x * _rms_scale(x) * g_in_ref[...]).astype(jnp.bfloat16)

    def proj(h, c0, width):
        return jnp.dot(h, w_in_ref[:, c0:c0 + width], preferred_element_type=jnp.float32)

    def project_kv(h):
        kv_next_ref[...] = proj(h, SB_WIDTH, 2 * SB_WIDTH).astype(jnp.bfloat16)

    def project_q_conv(h, slot):
        q2_ref[slot, :, :SB_WIDTH] = (proj(h, 0, SB_WIDTH) * (SCALE * LOG2_E)).astype(jnp.bfloat16)
        q2_ref[slot, :, SB_WIDTH:] = (proj(h, c_qm, MEM_WIDTH) * SCALE).astype(jnp.bfloat16)
        rest2_ref[slot, :, :3 * CONV_WIDTH] = proj(h, c_u, 3 * CONV_WIDTH)

    def project_gate(h, slot):
        for c0 in range(0, MIX_WIDTH, PROJ_COLS):
            rest2_ref[slot, :, 3 * CONV_WIDTH + c0:3 * CONV_WIDTH + c0 + PROJ_COLS] = proj(h, c_gate + c0, PROJ_COLS)

    @pl.when(s == 0)
    def _():
        h = normed_input()
        project_kv(h)
        project_q_conv(h, 0)
        project_gate(h, 0)
        x_prev_ref[...] = x_ref[...]
        cu_ref[0:SUBLANES, :] = jnp.zeros((SUBLANES, CONV_WIDTH), jnp.float32)
        k_ref[0:KV_PAD, :] = jnp.zeros((KV_PAD, SB_WIDTH), jnp.bfloat16)
        v_ref[0:KV_PAD, :] = jnp.zeros((KV_PAD, SB_WIDTH), jnp.bfloat16)

    @pl.when(s > 0)
    def _():
        i = s - 1
        cur = lax.rem(i, 2)
        nxt = 1 - cur
        q_ref = q2_ref.at[cur]
        rest_ref = rest2_ref.at[cur]
        blk = pl.ds(pl.multiple_of(KV_PAD + i * q_rows, q_rows), q_rows)
        k_ref[blk, :] = kv_next_ref[:, :SB_WIDTH]
        v_ref[blk, :] = kv_next_ref[:, SB_WIDTH:]

        h_next = normed_input()
        y_sb = _sb_sweep(i, q_ref, k_ref, v_ref, acc_ref, carry_ref,
                         (lambda: project_kv(h_next), lambda: project_q_conv(h_next, nxt)))

        u = rest_ref[:, 0:CONV_WIDTH]
        b = rest_ref[:, CONV_WIDTH:2 * CONV_WIDTH]
        c = rest_ref[:, 2 * CONV_WIDTH:3 * CONV_WIDTH]
        cu = c * u
        cu_ref[SUBLANES:, :] = cu
        cu_1 = cu_ref[SUBLANES - 1:SUBLANES - 1 + q_rows, :]
        cu_2 = cu_ref[SUBLANES - 2:SUBLANES - 2 + q_rows, :]
        cu_ref[0:SUBLANES, :] = cu[q_rows - SUBLANES:, :]
        conv = (convw_ref[0:1, :] * cu_2 + convw_ref[1:2, :] * cu_1 + convw_ref[2:3, :] * cu
                + convb_ref[...])
        y_conv = b * conv

        y_mem = _mem_attention(q_ref, mkv_ref, lambda: project_gate(h_next, nxt))

        y = jnp.concatenate([y_sb * _rms_scale(y_sb) * g_sb_ref[...],
                             y_conv * _rms_scale(y_conv) * g_conv_ref[...],
                             y_mem * _rms_scale(y_mem) * g_mem_ref[...]], axis=-1)
        gate = rest_ref[:, 3 * CONV_WIDTH:]
        gated = (y * (gate / (1.0 + jnp.exp(-gate)))).astype(jnp.bfloat16)
        res = x_prev_ref[...] + jnp.dot(gated, w_out_ref[...], preferred_element_type=jnp.float32)
        out_ref[...] = res * _rms_scale(res) * g_final_ref[...]
        x_prev_ref[...] = x_ref[...]


def _layer(x2, g_in, w_in_bf16, mkv, conv_w, conv_b, g_sb, g_conv, g_mem, w_out_bf16, g_final):
    t, d = x2.shape
    assert t % Q_BLOCK == 0 and Q_BLOCK % SB_TILE == 0
    n_chain = (Q_BLOCK // SB_TILE) * (SB_WIDTH // LANES)

    def whole(a):
        return pl.BlockSpec(a.shape, lambda s: (0,) * a.ndim)

    resident = pl.BlockSpec(memory_space=pltpu.VMEM)
    n_blk = t // Q_BLOCK
    return pl.pallas_call(
        _layer_kernel,
        grid=(n_blk + 1,),
        in_specs=[pl.BlockSpec((Q_BLOCK, d), lambda s: (jnp.minimum(s, n_blk - 1), 0)), whole(g_in), resident,
                  whole(mkv), whole(conv_w), whole(conv_b), whole(g_sb), whole(g_conv),
                  whole(g_mem), resident, whole(g_final)],
        out_specs=pl.BlockSpec((Q_BLOCK, d), lambda s: (jnp.maximum(s - 1, 0), 0)),
        out_shape=jax.ShapeDtypeStruct((t, d), jnp.float32),
        scratch_shapes=[pltpu.VMEM((KV_PAD + t, SB_WIDTH), jnp.bfloat16),
                        pltpu.VMEM((KV_PAD + t, SB_WIDTH), jnp.bfloat16),
                        pltpu.VMEM((Q_BLOCK, 2 * SB_WIDTH), jnp.bfloat16),
                        pltpu.VMEM((2, Q_BLOCK, SB_WIDTH + MEM_WIDTH), jnp.bfloat16),
                        pltpu.VMEM((2, Q_BLOCK, 3 * CONV_WIDTH + MIX_WIDTH), jnp.float32),
                        pltpu.VMEM((Q_BLOCK, d), jnp.float32),
                        pltpu.VMEM((SUBLANES + Q_BLOCK, CONV_WIDTH), jnp.float32),
                        pltpu.VMEM((n_chain, SB_TILE, LANES), jnp.float32),
                        pltpu.VMEM((n_chain, 2 * SB_TILE, 1), jnp.float32)],
        compiler_params=pltpu.CompilerParams(
            dimension_semantics=("arbitrary",), vmem_limit_bytes=VMEM_LIMIT_BYTES),
        name="layer",
    )(x2, g_in, w_in_bf16, mkv, conv_w, conv_b, g_sb, g_conv, g_mem, w_out_bf16, g_final)


def kernel(x, mem, g_in, w_in, conv_w, conv_b, g_mem, w_mem_kv, g_sb_out, g_conv_out,
           g_mem_out, w_out, g_final):
    batch, t, d = x.shape
    assert batch == 1 and g_in.shape[0] == 1
    mkv = _memory_kv(mem.reshape(mem.shape[1], d), g_mem[0][None, :], w_mem_kv[0].astype(jnp.bfloat16))
    out = _layer(x.reshape(t, d), g_in[0][None, :], w_in[0].astype(jnp.bfloat16), mkv,
                 conv_w[0], conv_b[0][None, :], g_sb_out[0][None, :], g_conv_out[0][None, :],
                 g_mem_out[0][None, :], w_out[0].astype(jnp.bfloat16), g_final[None, :])
    return out.reshape(batch, t, d)
```

```python
import jax
import jax.numpy as jnp
from jax import lax
from jax.experimental import pallas as pl
from jax.experimental.pallas import tpu as pltpu

HEAD_DIM = 64
SB_HEADS = 8
SB_WIDTH = SB_HEADS * HEAD_DIM
CONV_WIDTH = 4 * HEAD_DIM
MEM_HEADS = 4
MEM_WIDTH = MEM_HEADS * HEAD_DIM
CONV_K = 3
EPS = 1e-6
SCALE = HEAD_DIM ** -0.5

LANES = 128
SUBLANES = 8
MIX_WIDTH = SB_WIDTH + CONV_WIDTH + MEM_WIDTH
PROJ_COLS = 512
Q_BLOCK = 256
SB_TILE = 128
SWEEP_STOP = 104.0
TILE2_ROWS = 32
KV_PAD = 2 * SB_TILE
LOG2_E = 1.4426950408889634
LOGIT2_CLAMP = 126.0
WEIGHT_CHUNK_ROWS = 256
WEIGHT_CHUNK_COLS = 1024
VMEM_LIMIT_BYTES = 62 * 1024 * 1024


def _rms_scale(xf, eps=EPS):
    return lax.rsqrt(jnp.mean(xf * xf, axis=-1, keepdims=True) + eps)


def _mem_kv_kernel(mem_ref, g_ref, w_ref, kv_ref):
    m = mem_ref[...]
    h = (m * _rms_scale(m) * g_ref[...]).astype(jnp.bfloat16)
    kv_ref[...] = jnp.dot(h, w_ref[...].astype(jnp.bfloat16),
                          preferred_element_type=jnp.float32).astype(jnp.bfloat16)


def _memory_kv(mem2, g_mem, w_kv):
    n_mem = mem2.shape[0]
    return pl.pallas_call(
        _mem_kv_kernel,
        out_shape=jax.ShapeDtypeStruct((n_mem, 2 * MEM_WIDTH), jnp.bfloat16),
        name="memory_kv",
    )(mem2, g_mem, w_kv)


def _dot_nt(a, b):
    return lax.dot_general(a, b, (((1,), (1,)), ((), ())), preferred_element_type=jnp.float32)


def _split_heads(a):
    first_half = lax.broadcasted_iota(jnp.int32, a.shape, 1) < HEAD_DIM
    zero = jnp.zeros_like(a)
    return jnp.concatenate([jnp.where(first_half, a, zero), jnp.where(first_half, zero, a)], axis=0)


def _sb_sweep(i, q_ref, k_ref, v_ref, acc_ref, carry_ref, fillers):
    n_sub = q_ref.shape[0] // SB_TILE
    n_pair = SB_WIDTH // LANES
    chains = [(sub, p) for sub in range(n_sub) for p in range(n_pair)]
    fillers = list(fillers)

    rt = lax.broadcasted_iota(jnp.int32, (2 * SB_TILE, 2 * SB_TILE), 0)
    ct = lax.broadcasted_iota(jnp.int32, (2 * SB_TILE, 2 * SB_TILE), 1)
    top, left = rt < SB_TILE, ct < SB_TILE
    tri2 = jnp.where(top, rt, rt - SB_TILE) >= jnp.where(left, ct, ct - SB_TILE)
    causal = jnp.logical_or(left, jnp.where(left, ct, ct - SB_TILE) < jnp.where(top, rt, rt - SB_TILE))

    def as_matrix(cond):
        return jnp.where(cond, 1.0, 0.0).astype(jnp.bfloat16)

    w_10 = as_matrix(jnp.logical_or(jnp.logical_and(jnp.logical_not(top), left),
                                    jnp.logical_and(tri2, jnp.logical_not(jnp.logical_xor(top, left)))))
    w_2 = as_matrix(jnp.logical_or(top, jnp.logical_or(tri2, jnp.logical_not(left))))
    w_1 = w_2[SB_TILE:]

    q_stack = [_split_heads(q_ref[sub * SB_TILE:(sub + 1) * SB_TILE, p * LANES:(p + 1) * LANES])
               for sub, p in chains]

    def tile_rows(c, m, n_tiles=1):
        oldest = n_sub * i + chains[c][0] - (m + n_tiles - 1)
        return pl.ds(pl.multiple_of(KV_PAD + oldest * SB_TILE, SB_TILE), n_tiles * SB_TILE)

    def head_rows(a, n):
        return jnp.concatenate([a[:n], a[SB_TILE:SB_TILE + n]], axis=0)

    def logits(q, c, rows):
        p = chains[c][1]
        return jnp.minimum(_dot_nt(q, k_ref[rows, p * LANES:(p + 1) * LANES]), LOGIT2_CLAMP)

    def neg_log2_not_beta(z):
        return jnp.log(1.0 + jnp.exp2(z)) * LOG2_E

    def weights(z, s, mask=None):
        w = jnp.exp2(z - s)
        if mask is not None:
            w = jnp.where(mask, w, 0.0)
        wb = w.astype(jnp.bfloat16)
        n = wb.shape[0] // 2
        return jnp.concatenate([wb[:n], wb[n:]], axis=1)

    def values(c, rows):
        p = chains[c][1]
        return _split_heads(v_ref[rows, p * LANES:(p + 1) * LANES])

    def all_done(carries):
        floor = carries[0]
        for carry in carries[1:]:
            floor = jnp.minimum(floor, carry)
        return jnp.min(floor) >= SWEEP_STOP * LOG2_E

    def first_sweep():
        n = TILE2_ROWS
        rows_10 = [tile_rows(c, 0, 2) for c in range(len(chains))]
        rows_2 = [tile_rows(c, 2) for c in range(len(chains))]
        zs = [(logits(q_stack[c], c, rows_10[c]),
               logits(head_rows(q_stack[c], n), c, rows_2[c])) for c in range(len(chains))]
        if fillers:
            fillers.pop(0)()
        lhs_10, lhs_2 = [], []
        for z10, z2 in zs:
            l10 = jnp.where(causal, neg_log2_not_beta(z10), 0.0)
            lhs_10.append(l10.astype(jnp.bfloat16))
            lhs_2.append(jnp.concatenate(
                [head_rows(l10[:, :SB_TILE] + l10[:, SB_TILE:], n).astype(jnp.bfloat16),
                 neg_log2_not_beta(z2).astype(jnp.bfloat16)], axis=1))
        s_10 = jnp.dot(jnp.concatenate(lhs_10, axis=0), w_10, preferred_element_type=jnp.float32)
        s_2 = jnp.dot(jnp.concatenate(lhs_2, axis=0), w_2, preferred_element_type=jnp.float32)
        if fillers:
            fillers.pop(0)()
        reached, w_10s, w_2s = [], [], []
        for c, (z10, z2) in enumerate(zs):
            part = slice(c * 2 * SB_TILE, (c + 1) * 2 * SB_TILE)
            part2 = slice(c * 2 * n, (c + 1) * 2 * n)
            w_10s.append(weights(z10, s_10[part], causal))
            w_2s.append(weights(z2, s_2[part2, :SB_TILE]))
            after_1 = s_10[part, 0:1]
            after_2 = s_2[part2, SB_TILE:SB_TILE + 1]
            carry_ref[c] = after_1
            reached.append(jnp.concatenate([after_2[:n], after_1[n:SB_TILE],
                                            after_2[n:], after_1[SB_TILE + n:]], axis=0))
        done = all_done(reached)
        for c in range(len(chains)):
            acc = jnp.dot(w_10s[c], values(c, rows_10[c]), preferred_element_type=jnp.float32)
            tile_2 = jnp.dot(w_2s[c], values(c, rows_2[c]), preferred_element_type=jnp.float32)
            acc_ref[c, n:, :] = acc[n:]
            acc_ref[c, :n, :] = acc[:n] + jnp.where(done, tile_2, 0.0)
        return 2, done

    def next_sweep(m):
        rows = [tile_rows(c, m) for c in range(len(chains))]
        zs = [logits(q_stack[c], c, rows[c]) for c in range(len(chains))]
        ls = [neg_log2_not_beta(z).astype(jnp.bfloat16) for z in zs]
        cs = jnp.dot(jnp.concatenate(ls, axis=0), w_1, preferred_element_type=jnp.float32)
        carries, w_cats = [], []
        for c, z in enumerate(zs):
            part = slice(c * 2 * SB_TILE, (c + 1) * 2 * SB_TILE)
            old = carry_ref[c]
            w_cats.append(weights(z, cs[part, :SB_TILE] + old))
            carries.append(cs[part, SB_TILE:SB_TILE + 1] + old)
            carry_ref[c] = carries[c]
        for c in range(len(chains)):
            acc_ref[c] += jnp.dot(w_cats[c], values(c, rows[c]), preferred_element_type=jnp.float32)
        return all_done(carries)

    m0, done0 = first_sweep()
    last = n_sub * i + n_sub - 1

    def cond(state):
        m, done = state
        return jnp.logical_and(m <= last, jnp.logical_not(done))

    def body(state):
        m, _ = state
        return m + 1, next_sweep(m)

    lax.while_loop(cond, body, (jnp.int32(m0), done0))
    return jnp.concatenate(
        [jnp.concatenate([acc_ref[sub * n_pair + p] for p in range(n_pair)], axis=1)
         for sub in range(n_sub)], axis=0)


def _mem_attention(q_ref, mkv_ref, filler):
    q_rows = q_ref.shape[0]
    n_mem = mkv_ref.shape[0]
    n_pair = MEM_WIDTH // LANES
    head_ones = ((lax.broadcasted_iota(jnp.int32, (2 * n_mem, LANES), 1) < HEAD_DIM)
                 == (lax.broadcasted_iota(jnp.int32, (2 * n_mem, LANES), 0) < n_mem)
                 ).astype(jnp.float32).astype(jnp.bfloat16)
    logits = [_dot_nt(_split_heads(q_ref[:, SB_WIDTH + p * LANES:SB_WIDTH + (p + 1) * LANES]),
                      mkv_ref[:, p * LANES:(p + 1) * LANES]) for p in range(n_pair)]
    filler()
    outs = []
    for p, s in enumerate(logits):
        mv = mkv_ref[:, MEM_WIDTH + p * LANES:MEM_WIDTH + (p + 1) * LANES]
        e = jnp.exp(s - jnp.max(s, axis=-1, keepdims=True)).astype(jnp.bfloat16)
        e_cat = jnp.concatenate([e[:q_rows], e[q_rows:]], axis=1)
        num_den = jnp.dot(e_cat, jnp.concatenate([_split_heads(mv), head_ones], axis=1),
                          preferred_element_type=jnp.float32)
        outs.append(num_den[:, :LANES] / num_den[:, LANES:])
    return jnp.concatenate(outs, axis=-1)


def _load_weights(w_in_hbm, w_out_hbm, w_in_ref, w_out_ref, stages, sems):
    chunks = []
    for src, dst in ((w_in_hbm, w_in_ref), (w_out_hbm, w_out_ref)):
        n_rows, n_cols = dst.shape
        for r0 in range(0, n_rows, WEIGHT_CHUNK_ROWS):
            for c0 in range(0, n_cols, WEIGHT_CHUNK_COLS):
                chunks.append((src, dst, r0, c0, min(WEIGHT_CHUNK_COLS, n_cols - c0)))

    def copy(j):
        src, _, r0, c0, width = chunks[j]
        return pltpu.make_async_copy(src.at[0, pl.ds(r0, WEIGHT_CHUNK_ROWS), pl.ds(c0, width)],
                                     stages[j % 2].at[:, pl.ds(0, width)], sems.at[j % 2])

    copy(0).start()
    for j, (_, dst, r0, c0, width) in enumerate(chunks):
        if j + 1 < len(chunks):
            copy(j + 1).start()
        copy(j).wait()
        dst[r0:r0 + WEIGHT_CHUNK_ROWS, c0:c0 + width] = stages[j % 2][:, :width].astype(jnp.bfloat16)


def _layer_kernel(x_ref, g_in_ref, w_in_hbm, mkv_ref, convw_ref, convb_ref,
                  g_sb_ref, g_conv_ref, g_mem_ref, w_out_hbm, g_final_ref,
                  out_ref, w_in_ref, w_out_ref, weight_sems, k_ref, v_ref, kv_next_ref, q2_ref, rest2_ref,
                  x_prev_ref, cu_ref, acc_ref, carry_ref):
    s = pl.program_id(0)
    q_rows = x_ref.shape[0]
    c_u = 3 * SB_WIDTH
    c_qm = c_u + 3 * CONV_WIDTH
    c_gate = c_qm + MEM_WIDTH

    def normed_input():
        x = x_ref[...]
        return (x * _rms_scale(x) * g_in_ref[...]).astype(jnp.bfloat16)

    def proj(h, c0, width):
        return jnp.dot(h, w_in_ref[:, c0:c0 + width], preferred_element_type=jnp.float32)

    def project_kv(h):
        kv_next_ref[...] = proj(h, SB_WIDTH, 2 * SB_WIDTH).astype(jnp.bfloat16)

    def project_q_conv(h, slot):
        q2_ref[slot, :, :SB_WIDTH] = (proj(h, 0, SB_WIDTH) * (SCALE * LOG2_E)).astype(jnp.bfloat16)
        q2_ref[slot, :, SB_WIDTH:] = (proj(h, c_qm, MEM_WIDTH) * SCALE).astype(jnp.bfloat16)
        rest2_ref[slot, :, :3 * CONV_WIDTH] = proj(h, c_u, 3 * CONV_WIDTH)

    def project_gate(h, slot):
        for c0 in range(0, MIX_WIDTH, PROJ_COLS):
            rest2_ref[slot, :, 3 * CONV_WIDTH + c0:3 * CONV_WIDTH + c0 + PROJ_COLS] = proj(h, c_gate + c0, PROJ_COLS)

    @pl.when(s == 0)
    def _():
        _load_weights(w_in_hbm, w_out_hbm, w_in_ref, w_out_ref,
                      (x_prev_ref, rest2_ref.at[1, :, pl.ds(0, WEIGHT_CHUNK_COLS)]), weight_sems)
        h = normed_input()
        project_kv(h)
        project_q_conv(h, 0)
        project_gate(h, 0)
        x_prev_ref[...] = x_ref[...]
        cu_ref[0:SUBLANES, :] = jnp.zeros((SUBLANES, CONV_WIDTH), jnp.float32)
        k_ref[0:KV_PAD, :] = jnp.zeros((KV_PAD, SB_WIDTH), jnp.bfloat16)
        v_ref[0:KV_PAD, :] = jnp.zeros((KV_PAD, SB_WIDTH), jnp.bfloat16)

    @pl.when(s > 0)
    def _():
        i = s - 1
        cur = lax.rem(i, 2)
        nxt = 1 - cur
        q_ref = q2_ref.at[cur]
        rest_ref = rest2_ref.at[cur]
        blk = pl.ds(pl.multiple_of(KV_PAD + i * q_rows, q_rows), q_rows)
        k_ref[blk, :] = kv_next_ref[:, :SB_WIDTH]
        v_ref[blk, :] = kv_next_ref[:, SB_WIDTH:]

        h_next = normed_input()
        y_sb = _sb_sweep(i, q_ref, k_ref, v_ref, acc_ref, carry_ref,
                         (lambda: project_kv(h_next), lambda: project_q_conv(h_next, nxt)))

        u = rest_ref[:, 0:CONV_WIDTH]
        b = rest_ref[:, CONV_WIDTH:2 * CONV_WIDTH]
        c = rest_ref[:, 2 * CONV_WIDTH:3 * CONV_WIDTH]
        cu = c * u
        cu_ref[SUBLANES:, :] = cu
        cu_1 = cu_ref[SUBLANES - 1:SUBLANES - 1 + q_rows, :]
        cu_2 = cu_ref[SUBLANES - 2:SUBLANES - 2 + q_rows, :]
        cu_ref[0:SUBLANES, :] = cu[q_rows - SUBLANES:, :]
        conv = (convw_ref[0:1, :] * cu_2 + convw_ref[1:2, :] * cu_1 + convw_ref[2:3, :] * cu
                + convb_ref[...])
        y_conv = b * conv

        y_mem = _mem_attention(q_ref, mkv_ref, lambda: project_gate(h_next, nxt))

        y = jnp.concatenate([y_sb * _rms_scale(y_sb) * g_sb_ref[...],
                             y_conv * _rms_scale(y_conv) * g_conv_ref[...],
                             y_mem * _rms_scale(y_mem) * g_mem_ref[...]], axis=-1)
        gate = rest_ref[:, 3 * CONV_WIDTH:]
        gated = (y * (gate / (1.0 + jnp.exp(-gate)))).astype(jnp.bfloat16)
        res = x_prev_ref[...] + jnp.dot(gated, w_out_ref[...], preferred_element_type=jnp.float32)
        out_ref[...] = res * _rms_scale(res) * g_final_ref[...]
        x_prev_ref[...] = x_ref[...]


def _layer(x2, g_in, w_in, mkv, conv_w, conv_b, g_sb, g_conv, g_mem, w_out, g_final):
    t, d = x2.shape
    assert t % Q_BLOCK == 0 and Q_BLOCK % SB_TILE == 0
    assert Q_BLOCK == WEIGHT_CHUNK_ROWS and d == WEIGHT_CHUNK_COLS <= 3 * CONV_WIDTH + MIX_WIDTH
    assert w_in.shape[1] % WEIGHT_CHUNK_ROWS == 0 and w_out.shape[1] % WEIGHT_CHUNK_ROWS == 0
    n_chain = (Q_BLOCK // SB_TILE) * (SB_WIDTH // LANES)

    def whole(a):
        return pl.BlockSpec(a.shape, lambda s: (0,) * a.ndim)

    in_hbm = pl.BlockSpec(memory_space=pl.ANY)
    n_blk = t // Q_BLOCK
    return pl.pallas_call(
        _layer_kernel,
        grid=(n_blk + 1,),
        in_specs=[pl.BlockSpec((Q_BLOCK, d), lambda s: (jnp.minimum(s, n_blk - 1), 0)), whole(g_in), in_hbm,
                  whole(mkv), whole(conv_w), whole(conv_b), whole(g_sb), whole(g_conv),
                  whole(g_mem), in_hbm, whole(g_final)],
        out_specs=pl.BlockSpec((Q_BLOCK, d), lambda s: (jnp.maximum(s - 1, 0), 0)),
        out_shape=jax.ShapeDtypeStruct((t, d), jnp.float32),
        scratch_shapes=[pltpu.VMEM(w_in.shape[1:], jnp.bfloat16),
                        pltpu.VMEM(w_out.shape[1:], jnp.bfloat16),
                        pltpu.SemaphoreType.DMA((2,)),
                        pltpu.VMEM((KV_PAD + t, SB_WIDTH), jnp.bfloat16),
                        pltpu.VMEM((KV_PAD + t, SB_WIDTH), jnp.bfloat16),
                        pltpu.VMEM((Q_BLOCK, 2 * SB_WIDTH), jnp.bfloat16),
                        pltpu.VMEM((2, Q_BLOCK, SB_WIDTH + MEM_WIDTH), jnp.bfloat16),
                        pltpu.VMEM((2, Q_BLOCK, 3 * CONV_WIDTH + MIX_WIDTH), jnp.float32),
                        pltpu.VMEM((Q_BLOCK, d), jnp.float32),
                        pltpu.VMEM((SUBLANES + Q_BLOCK, CONV_WIDTH), jnp.float32),
                        pltpu.VMEM((n_chain, SB_TILE, LANES), jnp.float32),
                        pltpu.VMEM((n_chain, 2 * SB_TILE, 1), jnp.float32)],
        compiler_params=pltpu.CompilerParams(
            dimension_semantics=("arbitrary",), vmem_limit_bytes=VMEM_LIMIT_BYTES),
        name="layer",
    )(x2, g_in, w_in, mkv, conv_w, conv_b, g_sb, g_conv, g_mem, w_out, g_final)


def kernel(x, mem, g_in, w_in, conv_w, conv_b, g_mem, w_mem_kv, g_sb_out, g_conv_out,
           g_mem_out, w_out, g_final):
    batch, t, d = x.shape
    assert batch == 1 and g_in.shape[0] == 1
    mkv = _memory_kv(mem.reshape(mem.shape[1], d), g_mem[0][None, :], w_mem_kv[0])
    out = _layer(x.reshape(t, d), g_in[0][None, :], w_in, mkv,
                 conv_w[0], conv_b[0][None, :], g_sb_out[0][None, :], g_conv_out[0][None, :],
                 g_mem_out[0][None, :], w_out, g_final[None, :])
    return out.reshape(batch, t, d)
```

```python
import jax
import jax.numpy as jnp
from jax import lax
from jax.experimental import pallas as pl
from jax.experimental.pallas import tpu as pltpu

HEAD_DIM = 64
SB_HEADS = 8
SB_WIDTH = SB_HEADS * HEAD_DIM
CONV_WIDTH = 4 * HEAD_DIM
MEM_HEADS = 4
MEM_WIDTH = MEM_HEADS * HEAD_DIM
CONV_K = 3
EPS = 1e-6
SCALE = HEAD_DIM ** -0.5

LANES = 128
SUBLANES = 8
MIX_WIDTH = SB_WIDTH + CONV_WIDTH + MEM_WIDTH
PROJ_COLS = 512
Q_BLOCK = 256
SB_TILE = 128
SWEEP_STOP = 104.0
TILE2_ROWS = 32
KV_PAD = 2 * SB_TILE
LOG2_E = 1.4426950408889634
LOGIT2_CLAMP = 126.0
WEIGHT_CHUNK_ROWS = 256
WEIGHT_CHUNK_COLS = 512
WEIGHT_STAGES = 5
VMEM_LIMIT_BYTES = 62 * 1024 * 1024


def _rms_scale(xf, eps=EPS):
    return lax.rsqrt(jnp.mean(xf * xf, axis=-1, keepdims=True) + eps)


def _mem_kv_kernel(mem_ref, g_ref, w_ref, kv_ref):
    m = mem_ref[...]
    h = (m * _rms_scale(m) * g_ref[...]).astype(jnp.bfloat16)
    kv_ref[...] = jnp.dot(h, w_ref[...].astype(jnp.bfloat16),
                          preferred_element_type=jnp.float32).astype(jnp.bfloat16)


def _memory_kv(mem2, g_mem, w_kv):
    n_mem = mem2.shape[0]
    return pl.pallas_call(
        _mem_kv_kernel,
        out_shape=jax.ShapeDtypeStruct((n_mem, 2 * MEM_WIDTH), jnp.bfloat16),
        name="memory_kv",
    )(mem2, g_mem, w_kv)


def _dot_nt(a, b):
    return lax.dot_general(a, b, (((1,), (1,)), ((), ())), preferred_element_type=jnp.float32)


def _split_heads(a):
    first_half = lax.broadcasted_iota(jnp.int32, a.shape, 1) < HEAD_DIM
    zero = jnp.zeros_like(a)
    return jnp.concatenate([jnp.where(first_half, a, zero), jnp.where(first_half, zero, a)], axis=0)


def _sb_sweep(i, q_ref, k_ref, v_ref, acc_ref, carry_ref, fillers):
    n_sub = q_ref.shape[0] // SB_TILE
    n_pair = SB_WIDTH // LANES
    chains = [(sub, p) for sub in range(n_sub) for p in range(n_pair)]
    fillers = list(fillers)

    rt = lax.broadcasted_iota(jnp.int32, (2 * SB_TILE, 2 * SB_TILE), 0)
    ct = lax.broadcasted_iota(jnp.int32, (2 * SB_TILE, 2 * SB_TILE), 1)
    top, left = rt < SB_TILE, ct < SB_TILE
    tri2 = jnp.where(top, rt, rt - SB_TILE) >= jnp.where(left, ct, ct - SB_TILE)
    causal = jnp.logical_or(left, jnp.where(left, ct, ct - SB_TILE) < jnp.where(top, rt, rt - SB_TILE))

    def as_matrix(cond):
        return jnp.where(cond, 1.0, 0.0).astype(jnp.bfloat16)

    w_10 = as_matrix(jnp.logical_or(jnp.logical_and(jnp.logical_not(top), left),
                                    jnp.logical_and(tri2, jnp.logical_not(jnp.logical_xor(top, left)))))
    w_2 = as_matrix(jnp.logical_or(top, jnp.logical_or(tri2, jnp.logical_not(left))))
    w_1 = w_2[SB_TILE:]

    q_stack = [_split_heads(q_ref[sub * SB_TILE:(sub + 1) * SB_TILE, p * LANES:(p + 1) * LANES])
               for sub, p in chains]

    def tile_rows(c, m, n_tiles=1):
        oldest = n_sub * i + chains[c][0] - (m + n_tiles - 1)
        return pl.ds(pl.multiple_of(KV_PAD + oldest * SB_TILE, SB_TILE), n_tiles * SB_TILE)

    def head_rows(a, n):
        return jnp.concatenate([a[:n], a[SB_TILE:SB_TILE + n]], axis=0)

    def logits(q, c, rows):
        p = chains[c][1]
        return jnp.minimum(_dot_nt(q, k_ref[rows, p * LANES:(p + 1) * LANES]), LOGIT2_CLAMP)

    def neg_log2_not_beta(z):
        return jnp.log(1.0 + jnp.exp2(z)) * LOG2_E

    def weights(z, s, mask=None):
        w = jnp.exp2(z - s)
        if mask is not None:
            w = jnp.where(mask, w, 0.0)
        wb = w.astype(jnp.bfloat16)
        n = wb.shape[0] // 2
        return jnp.concatenate([wb[:n], wb[n:]], axis=1)

    def values(c, rows):
        p = chains[c][1]
        return _split_heads(v_ref[rows, p * LANES:(p + 1) * LANES])

    def all_done(carries):
        floor = carries[0]
        for carry in carries[1:]:
            floor = jnp.minimum(floor, carry)
        return jnp.min(floor) >= SWEEP_STOP * LOG2_E

    def first_sweep():
        n = TILE2_ROWS
        rows_10 = [tile_rows(c, 0, 2) for c in range(len(chains))]
        rows_2 = [tile_rows(c, 2) for c in range(len(chains))]
        zs = [(logits(q_stack[c], c, rows_10[c]),
               logits(head_rows(q_stack[c], n), c, rows_2[c])) for c in range(len(chains))]
        if fillers:
            fillers.pop(0)()
        lhs_10, lhs_2 = [], []
        for z10, z2 in zs:
            l10 = jnp.where(causal, neg_log2_not_beta(z10), 0.0)
            lhs_10.append(l10.astype(jnp.bfloat16))
            lhs_2.append(jnp.concatenate(
                [head_rows(l10[:, :SB_TILE] + l10[:, SB_TILE:], n).astype(jnp.bfloat16),
                 neg_log2_not_beta(z2).astype(jnp.bfloat16)], axis=1))
        s_10 = jnp.dot(jnp.concatenate(lhs_10, axis=0), w_10, preferred_element_type=jnp.float32)
        s_2 = jnp.dot(jnp.concatenate(lhs_2, axis=0), w_2, preferred_element_type=jnp.float32)
        if fillers:
            fillers.pop(0)()
        reached, w_10s, w_2s = [], [], []
        for c, (z10, z2) in enumerate(zs):
            part = slice(c * 2 * SB_TILE, (c + 1) * 2 * SB_TILE)
            part2 = slice(c * 2 * n, (c + 1) * 2 * n)
            w_10s.append(weights(z10, s_10[part], causal))
            w_2s.append(weights(z2, s_2[part2, :SB_TILE]))
            after_1 = s_10[part, 0:1]
            after_2 = s_2[part2, SB_TILE:SB_TILE + 1]
            carry_ref[c] = after_1
            reached.append(jnp.concatenate([after_2[:n], after_1[n:SB_TILE],
                                            after_2[n:], after_1[SB_TILE + n:]], axis=0))
        done = all_done(reached)
        for c in range(len(chains)):
            acc = jnp.dot(w_10s[c], values(c, rows_10[c]), preferred_element_type=jnp.float32)
            tile_2 = jnp.dot(w_2s[c], values(c, rows_2[c]), preferred_element_type=jnp.float32)
            acc_ref[c, n:, :] = acc[n:]
            acc_ref[c, :n, :] = acc[:n] + jnp.where(done, tile_2, 0.0)
        return 2, done

    def next_sweep(m):
        rows = [tile_rows(c, m) for c in range(len(chains))]
        zs = [logits(q_stack[c], c, rows[c]) for c in range(len(chains))]
        ls = [neg_log2_not_beta(z).astype(jnp.bfloat16) for z in zs]
        cs = jnp.dot(jnp.concatenate(ls, axis=0), w_1, preferred_element_type=jnp.float32)
        carries, w_cats = [], []
        for c, z in enumerate(zs):
            part = slice(c * 2 * SB_TILE, (c + 1) * 2 * SB_TILE)
            old = carry_ref[c]
            w_cats.append(weights(z, cs[part, :SB_TILE] + old))
            carries.append(cs[part, SB_TILE:SB_TILE + 1] + old)
            carry_ref[c] = carries[c]
        for c in range(len(chains)):
            acc_ref[c] += jnp.dot(w_cats[c], values(c, rows[c]), preferred_element_type=jnp.float32)
        return all_done(carries)

    m0, done0 = first_sweep()
    last = n_sub * i + n_sub - 1

    def cond(state):
        m, done = state
        return jnp.logical_and(m <= last, jnp.logical_not(done))

    def body(state):
        m, _ = state
        return m + 1, next_sweep(m)

    lax.while_loop(cond, body, (jnp.int32(m0), done0))
    return jnp.concatenate(
        [jnp.concatenate([acc_ref[sub * n_pair + p] for p in range(n_pair)], axis=1)
         for sub in range(n_sub)], axis=0)


def _mem_attention(q_ref, mkv_ref, filler):
    q_rows = q_ref.shape[0]
    n_mem = mkv_ref.shape[0]
    n_pair = MEM_WIDTH // LANES
    head_ones = ((lax.broadcasted_iota(jnp.int32, (2 * n_mem, LANES), 1) < HEAD_DIM)
                 == (lax.broadcasted_iota(jnp.int32, (2 * n_mem, LANES), 0) < n_mem)
                 ).astype(jnp.float32).astype(jnp.bfloat16)
    logits = [_dot_nt(_split_heads(q_ref[:, SB_WIDTH + p * LANES:SB_WIDTH + (p + 1) * LANES]),
                      mkv_ref[:, p * LANES:(p + 1) * LANES]) for p in range(n_pair)]
    filler()
    outs = []
    for p, s in enumerate(logits):
        mv = mkv_ref[:, MEM_WIDTH + p * LANES:MEM_WIDTH + (p + 1) * LANES]
        e = jnp.exp(s - jnp.max(s, axis=-1, keepdims=True)).astype(jnp.bfloat16)
        e_cat = jnp.concatenate([e[:q_rows], e[q_rows:]], axis=1)
        num_den = jnp.dot(e_cat, jnp.concatenate([_split_heads(mv), head_ones], axis=1),
                          preferred_element_type=jnp.float32)
        outs.append(num_den[:, :LANES] / num_den[:, LANES:])
    return jnp.concatenate(outs, axis=-1)


def _load_weights(w_in_hbm, w_out_hbm, w_in_ref, w_out_ref, stages, sems):
    chunks = []
    for src, dst in ((w_in_hbm, w_in_ref), (w_out_hbm, w_out_ref)):
        n_rows, n_cols = dst.shape
        for r0 in range(0, n_rows, WEIGHT_CHUNK_ROWS):
            for c0 in range(0, n_cols, WEIGHT_CHUNK_COLS):
                chunks.append((src, dst, r0, c0))
    depth = len(stages)

    def copy(j):
        src, _, r0, c0 = chunks[j]
        return pltpu.make_async_copy(src.at[0, pl.ds(r0, WEIGHT_CHUNK_ROWS), pl.ds(c0, WEIGHT_CHUNK_COLS)],
                                     stages[j % depth], sems.at[j % depth])

    for j in range(min(depth - 1, len(chunks))):
        copy(j).start()
    for j, (_, dst, r0, c0) in enumerate(chunks):
        if j + depth - 1 < len(chunks):
            copy(j + depth - 1).start()
        copy(j).wait()
        dst[r0:r0 + WEIGHT_CHUNK_ROWS, c0:c0 + WEIGHT_CHUNK_COLS] = stages[j % depth][...].astype(jnp.bfloat16)


def _layer_kernel(x_ref, g_in_ref, w_in_hbm, mkv_ref, convw_ref, convb_ref,
                  g_sb_ref, g_conv_ref, g_mem_ref, w_out_hbm, g_final_ref,
                  out_ref, w_in_ref, w_out_ref, weight_sems, k_ref, v_ref, kv_next_ref, q2_ref, rest2_ref,
                  x_prev_ref, cu_ref, acc_ref, carry_ref):
    s = pl.program_id(0)
    q_rows = x_ref.shape[0]
    c_u = 3 * SB_WIDTH
    c_qm = c_u + 3 * CONV_WIDTH
    c_gate = c_qm + MEM_WIDTH

    def normed_input():
        x = x_ref[...]
        return (x * _rms_scale(x) * g_in_ref[...]).astype(jnp.bfloat16)

    def proj(h, c0, width):
        return jnp.dot(h, w_in_ref[:, c0:c0 + width], preferred_element_type=jnp.float32)

    def project_kv(h):
        kv_next_ref[...] = proj(h, SB_WIDTH, 2 * SB_WIDTH).astype(jnp.bfloat16)

    def project_q_conv(h, slot):
        q2_ref[slot, :, :SB_WIDTH] = (proj(h, 0, SB_WIDTH) * (SCALE * LOG2_E)).astype(jnp.bfloat16)
        q2_ref[slot, :, SB_WIDTH:] = (proj(h, c_qm, MEM_WIDTH) * SCALE).astype(jnp.bfloat16)
        rest2_ref[slot, :, :3 * CONV_WIDTH] = proj(h, c_u, 3 * CONV_WIDTH)

    def project_gate(h, slot):
        for c0 in range(0, MIX_WIDTH, PROJ_COLS):
            rest2_ref[slot, :, 3 * CONV_WIDTH + c0:3 * CONV_WIDTH + c0 + PROJ_COLS] = proj(h, c_gate + c0, PROJ_COLS)

    @pl.when(s == 0)
    def _():
        stages = [ref.at[:, pl.ds(c0, WEIGHT_CHUNK_COLS)] for ref in (x_prev_ref, rest2_ref.at[1])
                  for c0 in range(0, ref.shape[-1] - WEIGHT_CHUNK_COLS + 1, WEIGHT_CHUNK_COLS)]
        _load_weights(w_in_hbm, w_out_hbm, w_in_ref, w_out_ref, stages[:WEIGHT_STAGES], weight_sems)
        h = normed_input()
        project_kv(h)
        project_q_conv(h, 0)
        project_gate(h, 0)
        x_prev_ref[...] = x_ref[...]
        cu_ref[0:SUBLANES, :] = jnp.zeros((SUBLANES, CONV_WIDTH), jnp.float32)
        k_ref[0:KV_PAD, :] = jnp.zeros((KV_PAD, SB_WIDTH), jnp.bfloat16)
        v_ref[0:KV_PAD, :] = jnp.zeros((KV_PAD, SB_WIDTH), jnp.bfloat16)

    @pl.when(s > 0)
    def _():
        i = s - 1
        cur = lax.rem(i, 2)
        nxt = 1 - cur
        q_ref = q2_ref.at[cur]
        rest_ref = rest2_ref.at[cur]
        blk = pl.ds(pl.multiple_of(KV_PAD + i * q_rows, q_rows), q_rows)
        k_ref[blk, :] = kv_next_ref[:, :SB_WIDTH]
        v_ref[blk, :] = kv_next_ref[:, SB_WIDTH:]

        h_next = normed_input()
        y_sb = _sb_sweep(i, q_ref, k_ref, v_ref, acc_ref, carry_ref,
                         (lambda: project_kv(h_next), lambda: project_q_conv(h_next, nxt)))

        u = rest_ref[:, 0:CONV_WIDTH]
        b = rest_ref[:, CONV_WIDTH:2 * CONV_WIDTH]
        c = rest_ref[:, 2 * CONV_WIDTH:3 * CONV_WIDTH]
        cu = c * u
        cu_ref[SUBLANES:, :] = cu
        cu_1 = cu_ref[SUBLANES - 1:SUBLANES - 1 + q_rows, :]
        cu_2 = cu_ref[SUBLANES - 2:SUBLANES - 2 + q_rows, :]
        cu_ref[0:SUBLANES, :] = cu[q_rows - SUBLANES:, :]
        conv = (convw_ref[0:1, :] * cu_2 + convw_ref[1:2, :] * cu_1 + convw_ref[2:3, :] * cu
                + convb_ref[...])
        y_conv = b * conv

        y_mem = _mem_attention(q_ref, mkv_ref, lambda: project_gate(h_next, nxt))

        y = jnp.concatenate([y_sb * _rms_scale(y_sb) * g_sb_ref[...],
                             y_conv * _rms_scale(y_conv) * g_conv_ref[...],
                             y_mem * _rms_scale(y_mem) * g_mem_ref[...]], axis=-1)
        gate = rest_ref[:, 3 * CONV_WIDTH:]
        gated = (y * (gate / (1.0 + jnp.exp(-gate)))).astype(jnp.bfloat16)
        res = x_prev_ref[...] + jnp.dot(gated, w_out_ref[...], preferred_element_type=jnp.float32)
        out_ref[...] = res * _rms_scale(res) * g_final_ref[...]
        x_prev_ref[...] = x_ref[...]


def _layer(x2, g_in, w_in, mkv, conv_w, conv_b, g_sb, g_conv, g_mem, w_out, g_final):
    t, d = x2.shape
    assert t % Q_BLOCK == 0 and Q_BLOCK % SB_TILE == 0
    assert Q_BLOCK == WEIGHT_CHUNK_ROWS
    assert d // WEIGHT_CHUNK_COLS + (3 * CONV_WIDTH + MIX_WIDTH) // WEIGHT_CHUNK_COLS >= WEIGHT_STAGES
    for w in (w_in, w_out):
        assert w.shape[1] % WEIGHT_CHUNK_ROWS == 0 and w.shape[2] % WEIGHT_CHUNK_COLS == 0
    n_chain = (Q_BLOCK // SB_TILE) * (SB_WIDTH // LANES)

    def whole(a):
        return pl.BlockSpec(a.shape, lambda s: (0,) * a.ndim)

    in_hbm = pl.BlockSpec(memory_space=pl.ANY)
    n_blk = t // Q_BLOCK
    return pl.pallas_call(
        _layer_kernel,
        grid=(n_blk + 1,),
        in_specs=[pl.BlockSpec((Q_BLOCK, d), lambda s: (jnp.minimum(s, n_blk - 1), 0)), whole(g_in), in_hbm,
                  whole(mkv), whole(conv_w), whole(conv_b), whole(g_sb), whole(g_conv),
                  whole(g_mem), in_hbm, whole(g_final)],
        out_specs=pl.BlockSpec((Q_BLOCK, d), lambda s: (jnp.maximum(s - 1, 0), 0)),
        out_shape=jax.ShapeDtypeStruct((t, d), jnp.float32),
        scratch_shapes=[pltpu.VMEM(w_in.shape[1:], jnp.bfloat16),
                        pltpu.VMEM(w_out.shape[1:], jnp.bfloat16),
                        pltpu.SemaphoreType.DMA((WEIGHT_STAGES,)),
                        pltpu.VMEM((KV_PAD + t, SB_WIDTH), jnp.bfloat16),
                        pltpu.VMEM((KV_PAD + t, SB_WIDTH), jnp.bfloat16),
                        pltpu.VMEM((Q_BLOCK, 2 * SB_WIDTH), jnp.bfloat16),
                        pltpu.VMEM((2, Q_BLOCK, SB_WIDTH + MEM_WIDTH), jnp.bfloat16),
                        pltpu.VMEM((2, Q_BLOCK, 3 * CONV_WIDTH + MIX_WIDTH), jnp.float32),
                        pltpu.VMEM((Q_BLOCK, d), jnp.float32),
                        pltpu.VMEM((SUBLANES + Q_BLOCK, CONV_WIDTH), jnp.float32),
                        pltpu.VMEM((n_chain, SB_TILE, LANES), jnp.float32),
                        pltpu.VMEM((n_chain, 2 * SB_TILE, 1), jnp.float32)],
        compiler_params=pltpu.CompilerParams(
            dimension_semantics=("arbitrary",), vmem_limit_bytes=VMEM_LIMIT_BYTES),
        name="layer",
    )(x2, g_in, w_in, mkv, conv_w, conv_b, g_sb, g_conv, g_mem, w_out, g_final)


def kernel(x, mem, g_in, w_in, conv_w, conv_b, g_mem, w_mem_kv, g_sb_out, g_conv_out,
           g_mem_out, w_out, g_final):
    batch, t, d = x.shape
    assert batch == 1 and g_in.shape[0] == 1
    mkv = _memory_kv(mem.reshape(mem.shape[1], d), g_mem[0][None, :], w_mem_kv[0])
    out = _layer(x.reshape(t, d), g_in[0][None, :], w_in, mkv,
                 conv_w[0], conv_b[0][None, :], g_sb_out[0][None, :], g_conv_out[0][None, :],
                 g_mem_out[0][None, :], w_out, g_final[None, :])
    return out.reshape(batch, t, d)
```

```python
import jax
import jax.numpy as jnp
from jax import lax
from jax.experimental import pallas as pl
from jax.experimental.pallas import tpu as pltpu

HEAD_DIM = 64
SB_HEADS = 8
SB_WIDTH = SB_HEADS * HEAD_DIM
CONV_WIDTH = 4 * HEAD_DIM
MEM_HEADS = 4
MEM_WIDTH = MEM_HEADS * HEAD_DIM
CONV_K = 3
EPS = 1e-6
SCALE = HEAD_DIM ** -0.5

LANES = 128
SUBLANES = 8
MIX_WIDTH = SB_WIDTH + CONV_WIDTH + MEM_WIDTH
PROJ_COLS = 512
Q_BLOCK = 256
SB_TILE = 128
SWEEP_STOP = 104.0
TILE2_ROWS = 32
KV_PAD = 2 * SB_TILE
LOG2_E = 1.4426950408889634
LOGIT2_CLAMP = 126.0
STREAM_ROWS = 256
STREAM_COLS = 512
STREAM_STAGES = 5
VMEM_LIMIT_BYTES = 62 * 1024 * 1024


def _rms_scale(xf, eps=EPS):
    return lax.rsqrt(jnp.mean(xf * xf, axis=-1, keepdims=True) + eps)


def _dot_nt(a, b):
    return lax.dot_general(a, b, (((1,), (1,)), ((), ())), preferred_element_type=jnp.float32)


def _split_heads(a):
    first_half = lax.broadcasted_iota(jnp.int32, a.shape, 1) < HEAD_DIM
    zero = jnp.zeros_like(a)
    return jnp.concatenate([jnp.where(first_half, a, zero), jnp.where(first_half, zero, a)], axis=0)


def _sb_sweep(i, q_ref, k_ref, v_ref, acc_ref, carry_ref, fillers):
    n_sub = q_ref.shape[0] // SB_TILE
    n_pair = SB_WIDTH // LANES
    chains = [(sub, p) for sub in range(n_sub) for p in range(n_pair)]
    fillers = list(fillers)

    rt = lax.broadcasted_iota(jnp.int32, (2 * SB_TILE, 2 * SB_TILE), 0)
    ct = lax.broadcasted_iota(jnp.int32, (2 * SB_TILE, 2 * SB_TILE), 1)
    top, left = rt < SB_TILE, ct < SB_TILE
    tri2 = jnp.where(top, rt, rt - SB_TILE) >= jnp.where(left, ct, ct - SB_TILE)
    causal = jnp.logical_or(left, jnp.where(left, ct, ct - SB_TILE) < jnp.where(top, rt, rt - SB_TILE))

    def as_matrix(cond):
        return jnp.where(cond, 1.0, 0.0).astype(jnp.bfloat16)

    w_10 = as_matrix(jnp.logical_or(jnp.logical_and(jnp.logical_not(top), left),
                                    jnp.logical_and(tri2, jnp.logical_not(jnp.logical_xor(top, left)))))
    w_2 = as_matrix(jnp.logical_or(top, jnp.logical_or(tri2, jnp.logical_not(left))))
    w_1 = w_2[SB_TILE:]

    q_stack = [_split_heads(q_ref[sub * SB_TILE:(sub + 1) * SB_TILE, p * LANES:(p + 1) * LANES])
               for sub, p in chains]

    def tile_rows(c, m, n_tiles=1):
        oldest = n_sub * i + chains[c][0] - (m + n_tiles - 1)
        return pl.ds(pl.multiple_of(KV_PAD + oldest * SB_TILE, SB_TILE), n_tiles * SB_TILE)

    def head_rows(a, n):
        return jnp.concatenate([a[:n], a[SB_TILE:SB_TILE + n]], axis=0)

    def logits(q, c, rows):
        p = chains[c][1]
        return jnp.minimum(_dot_nt(q, k_ref[rows, p * LANES:(p + 1) * LANES]), LOGIT2_CLAMP)

    def neg_log2_not_beta(z):
        return jnp.log(1.0 + jnp.exp2(z)) * LOG2_E

    def weights(z, s, mask=None):
        w = jnp.exp2(z - s)
        if mask is not None:
            w = jnp.where(mask, w, 0.0)
        wb = w.astype(jnp.bfloat16)
        n = wb.shape[0] // 2
        return jnp.concatenate([wb[:n], wb[n:]], axis=1)

    def values(c, rows):
        p = chains[c][1]
        return _split_heads(v_ref[rows, p * LANES:(p + 1) * LANES])

    def all_done(carries):
        floor = carries[0]
        for carry in carries[1:]:
            floor = jnp.minimum(floor, carry)
        return jnp.min(floor) >= SWEEP_STOP * LOG2_E

    def first_sweep():
        n = TILE2_ROWS
        rows_10 = [tile_rows(c, 0, 2) for c in range(len(chains))]
        rows_2 = [tile_rows(c, 2) for c in range(len(chains))]
        zs = [(logits(q_stack[c], c, rows_10[c]),
               logits(head_rows(q_stack[c], n), c, rows_2[c])) for c in range(len(chains))]
        if fillers:
            fillers.pop(0)()
        lhs_10, lhs_2 = [], []
        for z10, z2 in zs:
            l10 = jnp.where(causal, neg_log2_not_beta(z10), 0.0)
            lhs_10.append(l10.astype(jnp.bfloat16))
            lhs_2.append(jnp.concatenate(
                [head_rows(l10[:, :SB_TILE] + l10[:, SB_TILE:], n).astype(jnp.bfloat16),
                 neg_log2_not_beta(z2).astype(jnp.bfloat16)], axis=1))
        s_10 = jnp.dot(jnp.concatenate(lhs_10, axis=0), w_10, preferred_element_type=jnp.float32)
        s_2 = jnp.dot(jnp.concatenate(lhs_2, axis=0), w_2, preferred_element_type=jnp.float32)
        if fillers:
            fillers.pop(0)()
        reached, w_10s, w_2s = [], [], []
        for c, (z10, z2) in enumerate(zs):
            part = slice(c * 2 * SB_TILE, (c + 1) * 2 * SB_TILE)
            part2 = slice(c * 2 * n, (c + 1) * 2 * n)
            w_10s.append(weights(z10, s_10[part], causal))
            w_2s.append(weights(z2, s_2[part2, :SB_TILE]))
            after_1 = s_10[part, 0:1]
            after_2 = s_2[part2, SB_TILE:SB_TILE + 1]
            carry_ref[c] = after_1
            reached.append(jnp.concatenate([after_2[:n], after_1[n:SB_TILE],
                                            after_2[n:], after_1[SB_TILE + n:]], axis=0))
        done = all_done(reached)
        for c in range(len(chains)):
            acc = jnp.dot(w_10s[c], values(c, rows_10[c]), preferred_element_type=jnp.float32)
            tile_2 = jnp.dot(w_2s[c], values(c, rows_2[c]), preferred_element_type=jnp.float32)
            acc_ref[c, n:, :] = acc[n:]
            acc_ref[c, :n, :] = acc[:n] + jnp.where(done, tile_2, 0.0)
        return 2, done

    def next_sweep(m):
        rows = [tile_rows(c, m) for c in range(len(chains))]
        zs = [logits(q_stack[c], c, rows[c]) for c in range(len(chains))]
        ls = [neg_log2_not_beta(z).astype(jnp.bfloat16) for z in zs]
        cs = jnp.dot(jnp.concatenate(ls, axis=0), w_1, preferred_element_type=jnp.float32)
        carries, w_cats = [], []
        for c, z in enumerate(zs):
            part = slice(c * 2 * SB_TILE, (c + 1) * 2 * SB_TILE)
            old = carry_ref[c]
            w_cats.append(weights(z, cs[part, :SB_TILE] + old))
            carries.append(cs[part, SB_TILE:SB_TILE + 1] + old)
            carry_ref[c] = carries[c]
        for c in range(len(chains)):
            acc_ref[c] += jnp.dot(w_cats[c], values(c, rows[c]), preferred_element_type=jnp.float32)
        return all_done(carries)

    m0, done0 = first_sweep()
    last = n_sub * i + n_sub - 1

    def cond(state):
        m, done = state
        return jnp.logical_and(m <= last, jnp.logical_not(done))

    def body(state):
        m, _ = state
        return m + 1, next_sweep(m)

    lax.while_loop(cond, body, (jnp.int32(m0), done0))
    return jnp.concatenate(
        [jnp.concatenate([acc_ref[sub * n_pair + p] for p in range(n_pair)], axis=1)
         for sub in range(n_sub)], axis=0)


def _mem_attention(q_ref, mkv_ref, filler):
    q_rows = q_ref.shape[0]
    n_mem = mkv_ref.shape[0]
    n_pair = MEM_WIDTH // LANES
    head_ones = ((lax.broadcasted_iota(jnp.int32, (2 * n_mem, LANES), 1) < HEAD_DIM)
                 == (lax.broadcasted_iota(jnp.int32, (2 * n_mem, LANES), 0) < n_mem)
                 ).astype(jnp.float32).astype(jnp.bfloat16)
    logits = [_dot_nt(_split_heads(q_ref[:, SB_WIDTH + p * LANES:SB_WIDTH + (p + 1) * LANES]),
                      mkv_ref[:, p * LANES:(p + 1) * LANES]) for p in range(n_pair)]
    filler()
    outs = []
    for p, s in enumerate(logits):
        mv = mkv_ref[:, MEM_WIDTH + p * LANES:MEM_WIDTH + (p + 1) * LANES]
        e = jnp.exp(s - jnp.max(s, axis=-1, keepdims=True)).astype(jnp.bfloat16)
        e_cat = jnp.concatenate([e[:q_rows], e[q_rows:]], axis=1)
        num_den = jnp.dot(e_cat, jnp.concatenate([_split_heads(mv), head_ones], axis=1),
                          preferred_element_type=jnp.float32)
        outs.append(num_den[:, :LANES] / num_den[:, LANES:])
    return jnp.concatenate(outs, axis=-1)


def _stream_chunks(jobs, stages, sems):
    depth = len(stages)

    def copy(j):
        return pltpu.make_async_copy(jobs[j][0], stages[j % depth], sems.at[j % depth])

    for j in range(min(depth - 1, len(jobs))):
        copy(j).start()
    for j, (_, consume) in enumerate(jobs):
        if j + depth - 1 < len(jobs):
            copy(j + depth - 1).start()
        copy(j).wait()
        consume(stages[j % depth])


def _chunks(hbm):
    _, n_rows, n_cols = hbm.shape
    return [(r0, c0, hbm.at[0, pl.ds(r0, STREAM_ROWS), pl.ds(c0, STREAM_COLS)])
            for r0 in range(0, n_rows, STREAM_ROWS) for c0 in range(0, n_cols, STREAM_COLS)]


def _load_parameters(mem_hbm, g_mem_ref, w_kv_hbm, w_in_hbm, w_out_hbm, mkv_ref, w_in_ref, w_out_ref,
                     stages, sems):
    jobs = []
    state = {"mem": [], "h": None, "kv": None}

    def take_mem(window):
        state["mem"].append(window[...])

    def memory_kv(window, r0):
        if state["h"] is None:
            m = jnp.concatenate(state["mem"], axis=1)
            state["h"] = (m * _rms_scale(m) * g_mem_ref[...]).astype(jnp.bfloat16)
        part = jnp.dot(state["h"][:, r0:r0 + STREAM_ROWS], window[...].astype(jnp.bfloat16),
                       preferred_element_type=jnp.float32)
        state["kv"] = part if state["kv"] is None else state["kv"] + part
        if r0 + STREAM_ROWS == w_kv_hbm.shape[1]:
            mkv_ref[...] = state["kv"].astype(jnp.bfloat16)

    def cast_into(dst, r0, c0):
        def consume(window):
            dst[r0:r0 + STREAM_ROWS, c0:c0 + STREAM_COLS] = window[...].astype(jnp.bfloat16)
        return consume

    assert mem_hbm.shape[1] == STREAM_ROWS and w_kv_hbm.shape[2] == STREAM_COLS
    jobs += [(chunk, take_mem) for _, _, chunk in _chunks(mem_hbm)]
    jobs += [(chunk, lambda window, r0=r0: memory_kv(window, r0)) for r0, _, chunk in _chunks(w_kv_hbm)]
    for hbm, dst in ((w_in_hbm, w_in_ref), (w_out_hbm, w_out_ref)):
        jobs += [(chunk, cast_into(dst, r0, c0)) for r0, c0, chunk in _chunks(hbm)]
    _stream_chunks(jobs, stages, sems)


def _layer_kernel(x_ref, mem_hbm, g_in_ref, w_in_hbm, convw_ref, convb_ref, g_memin_ref, w_kv_hbm,
                  g_sb_ref, g_conv_ref, g_mem_ref, w_out_hbm, g_final_ref,
                  out_ref, w_in_ref, w_out_ref, mkv_ref, stream_sems, k_ref, v_ref, kv_next_ref, q2_ref,
                  rest2_ref, x_prev_ref, cu_ref, acc_ref, carry_ref):
    s = pl.program_id(0)
    q_rows = x_ref.shape[0]
    c_u = 3 * SB_WIDTH
    c_qm = c_u + 3 * CONV_WIDTH
    c_gate = c_qm + MEM_WIDTH

    def normed_input():
        x = x_ref[...]
        return (x * _rms_scale(x) * g_in_ref[...]).astype(jnp.bfloat16)

    def proj(h, c0, width):
        return jnp.dot(h, w_in_ref[:, c0:c0 + width], preferred_element_type=jnp.float32)

    def project_kv(h):
        kv_next_ref[...] = proj(h, SB_WIDTH, 2 * SB_WIDTH).astype(jnp.bfloat16)

    def project_q_conv(h, slot):
        q2_ref[slot, :, :SB_WIDTH] = (proj(h, 0, SB_WIDTH) * (SCALE * LOG2_E)).astype(jnp.bfloat16)
        q2_ref[slot, :, SB_WIDTH:] = (proj(h, c_qm, MEM_WIDTH) * SCALE).astype(jnp.bfloat16)
        rest2_ref[slot, :, :3 * CONV_WIDTH] = proj(h, c_u, 3 * CONV_WIDTH)

    def project_gate(h, slot):
        for c0 in range(0, MIX_WIDTH, PROJ_COLS):
            rest2_ref[slot, :, 3 * CONV_WIDTH + c0:3 * CONV_WIDTH + c0 + PROJ_COLS] = proj(h, c_gate + c0, PROJ_COLS)

    @pl.when(s == 0)
    def _():
        stages = [ref.at[:, pl.ds(c0, STREAM_COLS)] for ref in (x_prev_ref, rest2_ref.at[1])
                  for c0 in range(0, ref.shape[-1] - STREAM_COLS + 1, STREAM_COLS)]
        _load_parameters(mem_hbm, g_memin_ref, w_kv_hbm, w_in_hbm, w_out_hbm, mkv_ref, w_in_ref, w_out_ref,
                         stages[:STREAM_STAGES], stream_sems)
        h = normed_input()
        project_kv(h)
        project_q_conv(h, 0)
        project_gate(h, 0)
        x_prev_ref[...] = x_ref[...]
        cu_ref[0:SUBLANES, :] = jnp.zeros((SUBLANES, CONV_WIDTH), jnp.float32)
        k_ref[0:KV_PAD, :] = jnp.zeros((KV_PAD, SB_WIDTH), jnp.bfloat16)
        v_ref[0:KV_PAD, :] = jnp.zeros((KV_PAD, SB_WIDTH), jnp.bfloat16)

    @pl.when(s > 0)
    def _():
        i = s - 1
        cur = lax.rem(i, 2)
        nxt = 1 - cur
        q_ref = q2_ref.at[cur]
        rest_ref = rest2_ref.at[cur]
        blk = pl.ds(pl.multiple_of(KV_PAD + i * q_rows, q_rows), q_rows)
        k_ref[blk, :] = kv_next_ref[:, :SB_WIDTH]
        v_ref[blk, :] = kv_next_ref[:, SB_WIDTH:]

        h_next = normed_input()
        y_sb = _sb_sweep(i, q_ref, k_ref, v_ref, acc_ref, carry_ref,
                         (lambda: project_kv(h_next), lambda: project_q_conv(h_next, nxt)))

        u = rest_ref[:, 0:CONV_WIDTH]
        b = rest_ref[:, CONV_WIDTH:2 * CONV_WIDTH]
        c = rest_ref[:, 2 * CONV_WIDTH:3 * CONV_WIDTH]
        cu = c * u
        cu_ref[SUBLANES:, :] = cu
        cu_1 = cu_ref[SUBLANES - 1:SUBLANES - 1 + q_rows, :]
        cu_2 = cu_ref[SUBLANES - 2:SUBLANES - 2 + q_rows, :]
        cu_ref[0:SUBLANES, :] = cu[q_rows - SUBLANES:, :]
        conv = (convw_ref[0, 0:1, :] * cu_2 + convw_ref[0, 1:2, :] * cu_1 + convw_ref[0, 2:3, :] * cu
                + convb_ref[...])
        y_conv = b * conv

        y_mem = _mem_attention(q_ref, mkv_ref, lambda: project_gate(h_next, nxt))

        y = jnp.concatenate([y_sb * _rms_scale(y_sb) * g_sb_ref[...],
                             y_conv * _rms_scale(y_conv) * g_conv_ref[...],
                             y_mem * _rms_scale(y_mem) * g_mem_ref[...]], axis=-1)
        gate = rest_ref[:, 3 * CONV_WIDTH:]
        gated = (y * (gate / (1.0 + jnp.exp(-gate)))).astype(jnp.bfloat16)
        res = x_prev_ref[...] + jnp.dot(gated, w_out_ref[...], preferred_element_type=jnp.float32)
        out_ref[...] = res * _rms_scale(res) * g_final_ref[...]
        x_prev_ref[...] = x_ref[...]


def _layer(x2, mem, g_in, w_in, conv_w, conv_b, g_memin, w_kv, g_sb, g_conv, g_mem, w_out, g_final):
    t, d = x2.shape
    assert t % Q_BLOCK == 0 and Q_BLOCK % SB_TILE == 0
    assert Q_BLOCK == STREAM_ROWS
    assert d // STREAM_COLS + (3 * CONV_WIDTH + MIX_WIDTH) // STREAM_COLS >= STREAM_STAGES
    for a in (mem, w_kv, w_in, w_out):
        assert a.shape[1] % STREAM_ROWS == 0 and a.shape[2] % STREAM_COLS == 0
    n_chain = (Q_BLOCK // SB_TILE) * (SB_WIDTH // LANES)

    def whole(a):
        return pl.BlockSpec(a.shape, lambda s: (0,) * a.ndim)

    in_hbm = pl.BlockSpec(memory_space=pl.ANY)
    n_blk = t // Q_BLOCK
    return pl.pallas_call(
        _layer_kernel,
        grid=(n_blk + 1,),
        in_specs=[pl.BlockSpec((Q_BLOCK, d), lambda s: (jnp.minimum(s, n_blk - 1), 0)), in_hbm, whole(g_in),
                  in_hbm, whole(conv_w), whole(conv_b), whole(g_memin), in_hbm, whole(g_sb), whole(g_conv),
                  whole(g_mem), in_hbm, whole(g_final)],
        out_specs=pl.BlockSpec((Q_BLOCK, d), lambda s: (jnp.maximum(s - 1, 0), 0)),
        out_shape=jax.ShapeDtypeStruct((t, d), jnp.float32),
        scratch_shapes=[pltpu.VMEM(w_in.shape[1:], jnp.bfloat16),
                        pltpu.VMEM(w_out.shape[1:], jnp.bfloat16),
                        pltpu.VMEM((mem.shape[1], w_kv.shape[2]), jnp.bfloat16),
                        pltpu.SemaphoreType.DMA((STREAM_STAGES,)),
                        pltpu.VMEM((KV_PAD + t, SB_WIDTH), jnp.bfloat16),
                        pltpu.VMEM((KV_PAD + t, SB_WIDTH), jnp.bfloat16),
                        pltpu.VMEM((Q_BLOCK, 2 * SB_WIDTH), jnp.bfloat16),
                        pltpu.VMEM((2, Q_BLOCK, SB_WIDTH + MEM_WIDTH), jnp.bfloat16),
                        pltpu.VMEM((2, Q_BLOCK, 3 * CONV_WIDTH + MIX_WIDTH), jnp.float32),
                        pltpu.VMEM((Q_BLOCK, d), jnp.float32),
                        pltpu.VMEM((SUBLANES + Q_BLOCK, CONV_WIDTH), jnp.float32),
                        pltpu.VMEM((n_chain, SB_TILE, LANES), jnp.float32),
                        pltpu.VMEM((n_chain, 2 * SB_TILE, 1), jnp.float32)],
        compiler_params=pltpu.CompilerParams(
            dimension_semantics=("arbitrary",), vmem_limit_bytes=VMEM_LIMIT_BYTES),
        name="layer",
    )(x2, mem, g_in, w_in, conv_w, conv_b, g_memin, w_kv, g_sb, g_conv, g_mem, w_out, g_final)


def kernel(x, mem, g_in, w_in, conv_w, conv_b, g_mem, w_mem_kv, g_sb_out, g_conv_out,
           g_mem_out, w_out, g_final):
    batch, t, d = x.shape
    assert batch == 1 and g_in.shape[0] == 1
    out = _layer(x.reshape(t, d), mem, g_in[0][None, :], w_in, conv_w, conv_b[0][None, :],
                 g_mem[0][None, :], w_mem_kv, g_sb_out[0][None, :], g_conv_out[0][None, :],
                 g_mem_out[0][None, :], w_out, g_final[None, :])
    return out.reshape(batch, t, d)
```

```python
import jax
import jax.numpy as jnp
from jax import lax
from jax.experimental import pallas as pl
from jax.experimental.pallas import tpu as pltpu

HEAD_DIM = 64
SB_HEADS = 8
SB_WIDTH = SB_HEADS * HEAD_DIM
CONV_WIDTH = 4 * HEAD_DIM
MEM_HEADS = 4
MEM_WIDTH = MEM_HEADS * HEAD_DIM
CONV_K = 3
EPS = 1e-6
SCALE = HEAD_DIM ** -0.5

LANES = 128
SUBLANES = 8
MIX_WIDTH = SB_WIDTH + CONV_WIDTH + MEM_WIDTH
PROJ_COLS = 512
Q_BLOCK = 512
SB_TILE = 128
SWEEP_STOP = 104.0
TILE2_ROWS = 32
KV_PAD = Q_BLOCK
KV_PREV = 2 * SB_TILE
LOG2_E = 1.4426950408889634
LOGIT2_CLAMP = 126.0
STREAM_ROWS = 256
STREAM_COLS = 512
STREAM_STAGES = 5
VMEM_LIMIT_BYTES = 52 * 1024 * 1024


def _rms_scale(xf, eps=EPS):
    return lax.rsqrt(jnp.mean(xf * xf, axis=-1, keepdims=True) + eps)


def _dot_nt(a, b):
    return lax.dot_general(a, b, (((1,), (1,)), ((), ())), preferred_element_type=jnp.float32)


def _split_heads(a):
    first_half = lax.broadcasted_iota(jnp.int32, a.shape, 1) < HEAD_DIM
    zero = jnp.zeros_like(a)
    return jnp.concatenate([jnp.where(first_half, a, zero), jnp.where(first_half, zero, a)], axis=0)


def _sb_sweep(i, q_ref, kwin_ref, vwin_ref, kv_hbm, ktile_ref, vtile_ref, tile_sems, acc_ref, carry_ref, fillers,
              before_loop):
    n_sub = q_ref.shape[0] // SB_TILE
    n_pair = SB_WIDTH // LANES
    chains = [(sub, p) for sub in range(n_sub) for p in range(n_pair)]
    fillers = list(fillers)

    rt = lax.broadcasted_iota(jnp.int32, (2 * SB_TILE, 2 * SB_TILE), 0)
    ct = lax.broadcasted_iota(jnp.int32, (2 * SB_TILE, 2 * SB_TILE), 1)
    top, left = rt < SB_TILE, ct < SB_TILE
    tri2 = jnp.where(top, rt, rt - SB_TILE) >= jnp.where(left, ct, ct - SB_TILE)
    causal = jnp.logical_or(left, jnp.where(left, ct, ct - SB_TILE) < jnp.where(top, rt, rt - SB_TILE))

    def as_matrix(cond):
        return jnp.where(cond, 1.0, 0.0).astype(jnp.bfloat16)

    w_10 = as_matrix(jnp.logical_or(jnp.logical_and(jnp.logical_not(top), left),
                                    jnp.logical_and(tri2, jnp.logical_not(jnp.logical_xor(top, left)))))
    w_2 = as_matrix(jnp.logical_or(top, jnp.logical_or(tri2, jnp.logical_not(left))))
    w_1 = w_2[SB_TILE:]

    q_stack = [_split_heads(q_ref[sub * SB_TILE:(sub + 1) * SB_TILE, p * LANES:(p + 1) * LANES])
               for sub, p in chains]

    def window_rows(c, m, n_tiles=1):
        start = KV_PREV + (chains[c][0] - (m + n_tiles - 1)) * SB_TILE
        assert start >= 0
        return slice(start, start + n_tiles * SB_TILE)

    def head_rows(a, n):
        return jnp.concatenate([a[:n], a[SB_TILE:SB_TILE + n]], axis=0)

    def logits(q, keys):
        return jnp.minimum(_dot_nt(q, keys), LOGIT2_CLAMP)

    def window_keys(c, rows):
        p = chains[c][1]
        return kwin_ref[rows, p * LANES:(p + 1) * LANES]

    def window_values(c, rows):
        p = chains[c][1]
        return _split_heads(vwin_ref[rows, p * LANES:(p + 1) * LANES])

    def neg_log2_not_beta(z):
        return jnp.log(1.0 + jnp.exp2(z)) * LOG2_E

    def weights(z, s, mask=None):
        w = jnp.exp2(z - s)
        if mask is not None:
            w = jnp.where(mask, w, 0.0)
        wb = w.astype(jnp.bfloat16)
        n = wb.shape[0] // 2
        return jnp.concatenate([wb[:n], wb[n:]], axis=1)

    def all_done(carries):
        floor = carries[0]
        for carry in carries[1:]:
            floor = jnp.minimum(floor, carry)
        return jnp.min(floor) >= SWEEP_STOP * LOG2_E

    def first_sweep():
        n = TILE2_ROWS
        rows_10 = [window_rows(c, 0, 2) for c in range(len(chains))]
        rows_2 = [window_rows(c, 2) for c in range(len(chains))]
        zs = [(logits(q_stack[c], window_keys(c, rows_10[c])),
               logits(head_rows(q_stack[c], n), window_keys(c, rows_2[c]))) for c in range(len(chains))]
        if fillers:
            fillers.pop(0)()
        lhs_10, lhs_2 = [], []
        for z10, z2 in zs:
            l10 = jnp.where(causal, neg_log2_not_beta(z10), 0.0)
            lhs_10.append(l10.astype(jnp.bfloat16))
            lhs_2.append(jnp.concatenate(
                [head_rows(l10[:, :SB_TILE] + l10[:, SB_TILE:], n).astype(jnp.bfloat16),
                 neg_log2_not_beta(z2).astype(jnp.bfloat16)], axis=1))
        s_10 = jnp.dot(jnp.concatenate(lhs_10, axis=0), w_10, preferred_element_type=jnp.float32)
        s_2 = jnp.dot(jnp.concatenate(lhs_2, axis=0), w_2, preferred_element_type=jnp.float32)
        if fillers:
            fillers.pop(0)()
        reached, w_10s, w_2s = [], [], []
        for c, (z10, z2) in enumerate(zs):
            part = slice(c * 2 * SB_TILE, (c + 1) * 2 * SB_TILE)
            part2 = slice(c * 2 * n, (c + 1) * 2 * n)
            w_10s.append(weights(z10, s_10[part], causal))
            w_2s.append(weights(z2, s_2[part2, :SB_TILE]))
            after_1 = s_10[part, 0:1]
            after_2 = s_2[part2, SB_TILE:SB_TILE + 1]
            carry_ref[c] = after_1
            reached.append(jnp.concatenate([after_2[:n], after_1[n:SB_TILE],
                                            after_2[n:], after_1[SB_TILE + n:]], axis=0))
        done = all_done(reached)
        for c in range(len(chains)):
            acc = jnp.dot(w_10s[c], window_values(c, rows_10[c]), preferred_element_type=jnp.float32)
            tile_2 = jnp.dot(w_2s[c], window_values(c, rows_2[c]), preferred_element_type=jnp.float32)
            acc_ref[c, n:, :] = acc[n:]
            acc_ref[c, :n, :] = acc[:n] + jnp.where(done, tile_2, 0.0)
        return 2, done

    def tile_copies(m):
        copies = []
        for c, (sub, p) in enumerate(chains):
            row0 = pl.multiple_of(KV_PAD + (n_sub * i + sub - m) * SB_TILE, SB_TILE)
            for which, dst in ((0, ktile_ref), (1, vtile_ref)):
                copies.append(pltpu.make_async_copy(
                    kv_hbm.at[which, pl.ds(row0, SB_TILE), pl.ds(p * LANES, LANES)], dst.at[c],
                    tile_sems.at[which, c]))
        return copies

    def next_sweep(m):
        for copy in tile_copies(m):
            copy.start()
        for copy in tile_copies(m):
            copy.wait()
        zs = [logits(q_stack[c], ktile_ref[c]) for c in range(len(chains))]
        ls = [neg_log2_not_beta(z).astype(jnp.bfloat16) for z in zs]
        cs = jnp.dot(jnp.concatenate(ls, axis=0), w_1, preferred_element_type=jnp.float32)
        carries, w_cats = [], []
        for c, z in enumerate(zs):
            part = slice(c * 2 * SB_TILE, (c + 1) * 2 * SB_TILE)
            old = carry_ref[c]
            w_cats.append(weights(z, cs[part, :SB_TILE] + old))
            carries.append(cs[part, SB_TILE:SB_TILE + 1] + old)
            carry_ref[c] = carries[c]
        for c in range(len(chains)):
            acc_ref[c] += jnp.dot(w_cats[c], _split_heads(vtile_ref[c]), preferred_element_type=jnp.float32)
        return all_done(carries)

    m0, done0 = first_sweep()
    before_loop()
    last = n_sub * i + n_sub - 1

    def cond(state):
        m, done = state
        return jnp.logical_and(m <= last, jnp.logical_not(done))

    def body(state):
        m, _ = state
        return m + 1, next_sweep(m)

    lax.while_loop(cond, body, (jnp.int32(m0), done0))
    return jnp.concatenate(
        [jnp.concatenate([acc_ref[sub * n_pair + p] for p in range(n_pair)], axis=1)
         for sub in range(n_sub)], axis=0)


def _mem_attention(q_ref, mkv_ref, filler):
    q_rows = q_ref.shape[0]
    n_mem = mkv_ref.shape[0]
    n_pair = MEM_WIDTH // LANES
    head_ones = ((lax.broadcasted_iota(jnp.int32, (2 * n_mem, LANES), 1) < HEAD_DIM)
                 == (lax.broadcasted_iota(jnp.int32, (2 * n_mem, LANES), 0) < n_mem)
                 ).astype(jnp.float32).astype(jnp.bfloat16)
    logits = [_dot_nt(_split_heads(q_ref[:, SB_WIDTH + p * LANES:SB_WIDTH + (p + 1) * LANES]),
                      mkv_ref[:, p * LANES:(p + 1) * LANES]) for p in range(n_pair)]
    filler()
    outs = []
    for p, s in enumerate(logits):
        mv = mkv_ref[:, MEM_WIDTH + p * LANES:MEM_WIDTH + (p + 1) * LANES]
        e = jnp.exp(s - jnp.max(s, axis=-1, keepdims=True)).astype(jnp.bfloat16)
        e_cat = jnp.concatenate([e[:q_rows], e[q_rows:]], axis=1)
        num_den = jnp.dot(e_cat, jnp.concatenate([_split_heads(mv), head_ones], axis=1),
                          preferred_element_type=jnp.float32)
        outs.append(num_den[:, :LANES] / num_den[:, LANES:])
    return jnp.concatenate(outs, axis=-1)


def _stream_chunks(jobs, stages, sems):
    depth = len(stages)

    def copy(j):
        return pltpu.make_async_copy(jobs[j][0], stages[j % depth], sems.at[j % depth])

    for j in range(min(depth - 1, len(jobs))):
        copy(j).start()
    for j, (_, consume) in enumerate(jobs):
        if j + depth - 1 < len(jobs):
            copy(j + depth - 1).start()
        copy(j).wait()
        consume(stages[j % depth])


def _chunks(hbm):
    _, n_rows, n_cols = hbm.shape
    return [(r0, c0, hbm.at[0, pl.ds(r0, STREAM_ROWS), pl.ds(c0, STREAM_COLS)])
            for r0 in range(0, n_rows, STREAM_ROWS) for c0 in range(0, n_cols, STREAM_COLS)]


def _load_parameters(mem_hbm, g_mem_ref, w_kv_hbm, w_in_hbm, w_out_hbm, mkv_ref, w_in_ref, w_out_ref,
                     stages, sems):
    jobs = []
    state = {"mem": [], "h": None, "kv": None}

    def take_mem(window):
        state["mem"].append(window[...])

    def memory_kv(window, r0):
        if state["h"] is None:
            m = jnp.concatenate(state["mem"], axis=1)
            state["h"] = (m * _rms_scale(m) * g_mem_ref[...]).astype(jnp.bfloat16)
        part = jnp.dot(state["h"][:, r0:r0 + STREAM_ROWS], window[...].astype(jnp.bfloat16),
                       preferred_element_type=jnp.float32)
        state["kv"] = part if state["kv"] is None else state["kv"] + part
        if r0 + STREAM_ROWS == w_kv_hbm.shape[1]:
            mkv_ref[...] = state["kv"].astype(jnp.bfloat16)

    def cast_into(dst, r0, c0):
        def consume(window):
            dst[r0:r0 + STREAM_ROWS, c0:c0 + STREAM_COLS] = window[...].astype(jnp.bfloat16)
        return consume

    assert mem_hbm.shape[1] == STREAM_ROWS and w_kv_hbm.shape[2] == STREAM_COLS
    jobs += [(chunk, take_mem) for _, _, chunk in _chunks(mem_hbm)]
    jobs += [(chunk, lambda window, r0=r0: memory_kv(window, r0)) for r0, _, chunk in _chunks(w_kv_hbm)]
    for hbm, dst in ((w_in_hbm, w_in_ref), (w_out_hbm, w_out_ref)):
        jobs += [(chunk, cast_into(dst, r0, c0)) for r0, c0, chunk in _chunks(hbm)]
    _stream_chunks(jobs, stages, sems)


def _layer_kernel(x_ref, mem_hbm, g_in_ref, w_in_hbm, convw_ref, convb_ref, g_memin_ref, w_kv_hbm,
                  g_sb_ref, g_conv_ref, g_mem_ref, w_out_hbm, g_final_ref,
                  out_ref, kv_hbm, w_in_ref, w_out_ref, mkv_ref, stream_sems, kwin_ref, vwin_ref, kv_next_ref,
                  ktile_ref, vtile_ref, kv_sems, tile_sems, q2_ref, rest2_ref, x_prev_ref, cu_ref, acc_ref,
                  carry_ref):
    s = pl.program_id(0)
    q_rows = x_ref.shape[0]
    c_u = 3 * SB_WIDTH
    c_qm = c_u + 3 * CONV_WIDTH
    c_gate = c_qm + MEM_WIDTH

    def normed_input():
        x = x_ref[...]
        return (x * _rms_scale(x) * g_in_ref[...]).astype(jnp.bfloat16)

    def proj(h, c0, width):
        return jnp.dot(h, w_in_ref[:, c0:c0 + width], preferred_element_type=jnp.float32)

    def project_kv(h):
        kv_next_ref[...] = proj(h, SB_WIDTH, 2 * SB_WIDTH).astype(jnp.bfloat16)

    def project_q_conv(h, slot):
        q2_ref[slot, :, :SB_WIDTH] = (proj(h, 0, SB_WIDTH) * (SCALE * LOG2_E)).astype(jnp.bfloat16)
        q2_ref[slot, :, SB_WIDTH:] = (proj(h, c_qm, MEM_WIDTH) * SCALE).astype(jnp.bfloat16)
        rest2_ref[slot, :, :3 * CONV_WIDTH] = proj(h, c_u, 3 * CONV_WIDTH)

    def project_gate(h, slot):
        for c0 in range(0, MIX_WIDTH, PROJ_COLS):
            rest2_ref[slot, :, 3 * CONV_WIDTH + c0:3 * CONV_WIDTH + c0 + PROJ_COLS] = proj(h, c_gate + c0, PROJ_COLS)

    @pl.when(s == 0)
    def _():
        stages = [ref.at[pl.ds(0, STREAM_ROWS), pl.ds(c0, STREAM_COLS)] for ref in (x_prev_ref, rest2_ref.at[1])
                  for c0 in range(0, ref.shape[-1] - STREAM_COLS + 1, STREAM_COLS)]
        _load_parameters(mem_hbm, g_memin_ref, w_kv_hbm, w_in_hbm, w_out_hbm, mkv_ref, w_in_ref, w_out_ref,
                         stages[:STREAM_STAGES], stream_sems)
        h = normed_input()
        project_kv(h)
        project_q_conv(h, 0)
        project_gate(h, 0)
        x_prev_ref[...] = x_ref[...]
        cu_ref[0:SUBLANES, :] = jnp.zeros((SUBLANES, CONV_WIDTH), jnp.float32)
        kwin_ref[...] = jnp.zeros(kwin_ref.shape, jnp.bfloat16)
        vwin_ref[...] = jnp.zeros(vwin_ref.shape, jnp.bfloat16)
        zero_rows = [pltpu.make_async_copy(win.at[pl.ds(0, KV_PAD)], kv_hbm.at[which, pl.ds(0, KV_PAD)],
                                           kv_sems.at[which])
                     for which, win in ((0, kwin_ref), (1, vwin_ref))]
        for copy in zero_rows:
            copy.start()
        for copy in zero_rows:
            copy.wait()

    @pl.when(s > 0)
    def _():
        i = s - 1
        cur = lax.rem(i, 2)
        nxt = 1 - cur
        q_ref = q2_ref.at[cur]
        rest_ref = rest2_ref.at[cur]

        def history_writes(block):
            rows = pl.ds(pl.multiple_of(KV_PAD + block * q_rows, q_rows), q_rows)
            return [pltpu.make_async_copy(win.at[pl.ds(KV_PREV, q_rows)], kv_hbm.at[which, rows],
                                          kv_sems.at[which])
                    for which, win in ((0, kwin_ref), (1, vwin_ref))]

        for win, c0 in ((kwin_ref, 0), (vwin_ref, SB_WIDTH)):
            win[0:KV_PREV, :] = win[q_rows:q_rows + KV_PREV, :]
            win[KV_PREV:, :] = kv_next_ref[:, c0:c0 + SB_WIDTH]
        for copy in history_writes(i):
            copy.start()

        h_next = normed_input()
        y_sb = _sb_sweep(i, q_ref, kwin_ref, vwin_ref, kv_hbm, ktile_ref, vtile_ref, tile_sems, acc_ref, carry_ref,
                         (lambda: project_kv(h_next), lambda: project_q_conv(h_next, nxt)),
                         lambda: [copy.wait() for copy in history_writes(i)])

        u = rest_ref[:, 0:CONV_WIDTH]
        b = rest_ref[:, CONV_WIDTH:2 * CONV_WIDTH]
        c = rest_ref[:, 2 * CONV_WIDTH:3 * CONV_WIDTH]
        cu = c * u
        cu_ref[SUBLANES:, :] = cu
        cu_1 = cu_ref[SUBLANES - 1:SUBLANES - 1 + q_rows, :]
        cu_2 = cu_ref[SUBLANES - 2:SUBLANES - 2 + q_rows, :]
        cu_ref[0:SUBLANES, :] = cu[q_rows - SUBLANES:, :]
        conv = (convw_ref[0, 0:1, :] * cu_2 + convw_ref[0, 1:2, :] * cu_1 + convw_ref[0, 2:3, :] * cu
                + convb_ref[...])
        y_conv = b * conv

        y_mem = _mem_attention(q_ref, mkv_ref, lambda: project_gate(h_next, nxt))

        y = jnp.concatenate([y_sb * _rms_scale(y_sb) * g_sb_ref[...],
                             y_conv * _rms_scale(y_conv) * g_conv_ref[...],
                             y_mem * _rms_scale(y_mem) * g_mem_ref[...]], axis=-1)
        gate = rest_ref[:, 3 * CONV_WIDTH:]
        gated = (y * (gate / (1.0 + jnp.exp(-gate)))).astype(jnp.bfloat16)
        res = x_prev_ref[...] + jnp.dot(gated, w_out_ref[...], preferred_element_type=jnp.float32)
        out_ref[...] = res * _rms_scale(res) * g_final_ref[...]
        x_prev_ref[...] = x_ref[...]


def _layer(x2, mem, g_in, w_in, conv_w, conv_b, g_memin, w_kv, g_sb, g_conv, g_mem, w_out, g_final):
    t, d = x2.shape
    assert t % Q_BLOCK == 0 and Q_BLOCK % SB_TILE == 0 and KV_PREV <= Q_BLOCK
    assert Q_BLOCK >= STREAM_ROWS
    assert d // STREAM_COLS + (3 * CONV_WIDTH + MIX_WIDTH) // STREAM_COLS >= STREAM_STAGES
    for a in (mem, w_kv, w_in, w_out):
        assert a.shape[1] % STREAM_ROWS == 0 and a.shape[2] % STREAM_COLS == 0
    n_chain = (Q_BLOCK // SB_TILE) * (SB_WIDTH // LANES)

    def whole(a):
        return pl.BlockSpec(a.shape, lambda s: (0,) * a.ndim)

    in_hbm = pl.BlockSpec(memory_space=pl.ANY)
    n_blk = t // Q_BLOCK
    return pl.pallas_call(
        _layer_kernel,
        grid=(n_blk + 1,),
        in_specs=[pl.BlockSpec((Q_BLOCK, d), lambda s: (jnp.minimum(s, n_blk - 1), 0)), in_hbm, whole(g_in),
                  in_hbm, whole(conv_w), whole(conv_b), whole(g_memin), in_hbm, whole(g_sb), whole(g_conv),
                  whole(g_mem), in_hbm, whole(g_final)],
        out_specs=[pl.BlockSpec((Q_BLOCK, d), lambda s: (jnp.maximum(s - 1, 0), 0)), in_hbm],
        out_shape=[jax.ShapeDtypeStruct((t, d), jnp.float32),
                   jax.ShapeDtypeStruct((2, KV_PAD + t, SB_WIDTH), jnp.bfloat16)],
        scratch_shapes=[pltpu.VMEM(w_in.shape[1:], jnp.bfloat16),
                        pltpu.VMEM(w_out.shape[1:], jnp.bfloat16),
                        pltpu.VMEM((mem.shape[1], w_kv.shape[2]), jnp.bfloat16),
                        pltpu.SemaphoreType.DMA((STREAM_STAGES,)),
                        pltpu.VMEM((KV_PREV + Q_BLOCK, SB_WIDTH), jnp.bfloat16),
                        pltpu.VMEM((KV_PREV + Q_BLOCK, SB_WIDTH), jnp.bfloat16),
                        pltpu.VMEM((Q_BLOCK, 2 * SB_WIDTH), jnp.bfloat16),
                        pltpu.VMEM((n_chain, SB_TILE, LANES), jnp.bfloat16),
                        pltpu.VMEM((n_chain, SB_TILE, LANES), jnp.bfloat16),
                        pltpu.SemaphoreType.DMA((2,)),
                        pltpu.SemaphoreType.DMA((2, n_chain)),
                        pltpu.VMEM((2, Q_BLOCK, SB_WIDTH + MEM_WIDTH), jnp.bfloat16),
                        pltpu.VMEM((2, Q_BLOCK, 3 * CONV_WIDTH + MIX_WIDTH), jnp.float32),
                        pltpu.VMEM((Q_BLOCK, d), jnp.float32),
                        pltpu.VMEM((SUBLANES + Q_BLOCK, CONV_WIDTH), jnp.float32),
                        pltpu.VMEM((n_chain, SB_TILE, LANES), jnp.float32),
                        pltpu.VMEM((n_chain, 2 * SB_TILE, 1), jnp.float32)],
        compiler_params=pltpu.CompilerParams(
            dimension_semantics=("arbitrary",), vmem_limit_bytes=VMEM_LIMIT_BYTES),
        name="layer",
    )(x2, mem, g_in, w_in, conv_w, conv_b, g_memin, w_kv, g_sb, g_conv, g_mem, w_out, g_final)[0]


def kernel(x, mem, g_in, w_in, conv_w, conv_b, g_mem, w_mem_kv, g_sb_out, g_conv_out,
           g_mem_out, w_out, g_final):
    batch, t, d = x.shape
    assert batch == 1 and g_in.shape[0] == 1
    out = _layer(x.reshape(t, d), mem, g_in[0][None, :], w_in, conv_w, conv_b[0][None, :],
                 g_mem[0][None, :], w_mem_kv, g_sb_out[0][None, :], g_conv_out[0][None, :],
                 g_mem_out[0][None, :], w_out, g_final[None, :])
    return out.reshape(batch, t, d)
```

```python
import jax
import jax.numpy as jnp
from jax import lax
from jax.experimental import pallas as pl
from jax.experimental.pallas import tpu as pltpu

HEAD_DIM = 64
SB_HEADS = 8
SB_WIDTH = SB_HEADS * HEAD_DIM
CONV_WIDTH = 4 * HEAD_DIM
MEM_HEADS = 4
MEM_WIDTH = MEM_HEADS * HEAD_DIM
CONV_K = 3
EPS = 1e-6
SCALE = HEAD_DIM ** -0.5

LANES = 128
SUBLANES = 8
MIX_WIDTH = SB_WIDTH + CONV_WIDTH + MEM_WIDTH
PROJ_COLS = 512
Q_BLOCK = 512
SB_TILE = 128
SWEEP_STOP = 104.0
TILE2_ROWS = 32
KV_PAD = Q_BLOCK
KV_PREV = 2 * SB_TILE
LOG2_E = 1.4426950408889634
LOGIT2_CLAMP = 126.0
STREAM_ROWS = 256
STREAM_COLS = 512
STREAM_STAGES = 5
VMEM_LIMIT_BYTES = 62 * 1024 * 1024


def _rms_scale(xf, eps=EPS):
    return lax.rsqrt(jnp.mean(xf * xf, axis=-1, keepdims=True) + eps)


def _dot_nt(a, b):
    return lax.dot_general(a, b, (((1,), (1,)), ((), ())), preferred_element_type=jnp.float32)


def _split_heads(a):
    first_half = lax.broadcasted_iota(jnp.int32, a.shape, 1) < HEAD_DIM
    zero = jnp.zeros_like(a)
    return jnp.concatenate([jnp.where(first_half, a, zero), jnp.where(first_half, zero, a)], axis=0)


def _sb_sweep(i, q_ref, kwin_ref, vwin_ref, kv_hbm, ktile_ref, vtile_ref, tile_sems, acc_ref, carry_ref, fillers,
              before_loop):
    n_sub = q_ref.shape[0] // SB_TILE
    n_pair = SB_WIDTH // LANES
    chains = [(sub, p) for sub in range(n_sub) for p in range(n_pair)]
    fillers = list(fillers)

    rt = lax.broadcasted_iota(jnp.int32, (2 * SB_TILE, 2 * SB_TILE), 0)
    ct = lax.broadcasted_iota(jnp.int32, (2 * SB_TILE, 2 * SB_TILE), 1)
    top, left = rt < SB_TILE, ct < SB_TILE
    tri2 = jnp.where(top, rt, rt - SB_TILE) >= jnp.where(left, ct, ct - SB_TILE)
    causal = jnp.logical_or(left, jnp.where(left, ct, ct - SB_TILE) < jnp.where(top, rt, rt - SB_TILE))

    def as_matrix(cond):
        return jnp.where(cond, 1.0, 0.0).astype(jnp.bfloat16)

    w_10 = as_matrix(jnp.logical_or(jnp.logical_and(jnp.logical_not(top), left),
                                    jnp.logical_and(tri2, jnp.logical_not(jnp.logical_xor(top, left)))))
    w_2 = as_matrix(jnp.logical_or(top, jnp.logical_or(tri2, jnp.logical_not(left))))
    w_1 = w_2[SB_TILE:]

    q_stack = [_split_heads(q_ref[sub * SB_TILE:(sub + 1) * SB_TILE, p * LANES:(p + 1) * LANES])
               for sub, p in chains]

    def window_rows(c, m, n_tiles=1):
        start = KV_PREV + (chains[c][0] - (m + n_tiles - 1)) * SB_TILE
        assert start >= 0
        return slice(start, start + n_tiles * SB_TILE)

    def head_rows(a, n):
        return jnp.concatenate([a[:n], a[SB_TILE:SB_TILE + n]], axis=0)

    def logits(q, keys):
        return jnp.minimum(_dot_nt(q, keys), LOGIT2_CLAMP)

    def window_keys(c, rows):
        p = chains[c][1]
        return kwin_ref[rows, p * LANES:(p + 1) * LANES]

    def window_values(c, rows):
        p = chains[c][1]
        return _split_heads(vwin_ref[rows, p * LANES:(p + 1) * LANES])

    def neg_log2_not_beta(z):
        return jnp.log(1.0 + jnp.exp2(z)) * LOG2_E

    def weights(z, s, mask=None):
        w = jnp.exp2(z - s)
        if mask is not None:
            w = jnp.where(mask, w, 0.0)
        wb = w.astype(jnp.bfloat16)
        n = wb.shape[0] // 2
        return jnp.concatenate([wb[:n], wb[n:]], axis=1)

    def all_done(carries):
        floor = carries[0]
        for carry in carries[1:]:
            floor = jnp.minimum(floor, carry)
        return jnp.min(floor) >= SWEEP_STOP * LOG2_E

    def first_sweep():
        n = TILE2_ROWS
        rows_10 = [window_rows(c, 0, 2) for c in range(len(chains))]
        rows_2 = [window_rows(c, 2) for c in range(len(chains))]
        zs = [(logits(q_stack[c], window_keys(c, rows_10[c])),
               logits(head_rows(q_stack[c], n), window_keys(c, rows_2[c]))) for c in range(len(chains))]
        if fillers:
            fillers.pop(0)()
        lhs_10, lhs_2 = [], []
        for z10, z2 in zs:
            l10 = jnp.where(causal, neg_log2_not_beta(z10), 0.0)
            lhs_10.append(l10.astype(jnp.bfloat16))
            lhs_2.append(jnp.concatenate(
                [head_rows(l10[:, :SB_TILE] + l10[:, SB_TILE:], n).astype(jnp.bfloat16),
                 neg_log2_not_beta(z2).astype(jnp.bfloat16)], axis=1))
        s_10 = jnp.dot(jnp.concatenate(lhs_10, axis=0), w_10, preferred_element_type=jnp.float32)
        s_2 = jnp.dot(jnp.concatenate(lhs_2, axis=0), w_2, preferred_element_type=jnp.float32)
        if fillers:
            fillers.pop(0)()
        reached, w_10s, w_2s = [], [], []
        for c, (z10, z2) in enumerate(zs):
            part = slice(c * 2 * SB_TILE, (c + 1) * 2 * SB_TILE)
            part2 = slice(c * 2 * n, (c + 1) * 2 * n)
            w_10s.append(weights(z10, s_10[part], causal))
            w_2s.append(weights(z2, s_2[part2, :SB_TILE]))
            after_1 = s_10[part, 0:1]
            after_2 = s_2[part2, SB_TILE:SB_TILE + 1]
            carry_ref[c] = after_1
            reached.append(jnp.concatenate([after_2[:n], after_1[n:SB_TILE],
                                            after_2[n:], after_1[SB_TILE + n:]], axis=0))
        done = all_done(reached)
        for c in range(len(chains)):
            acc = jnp.dot(w_10s[c], window_values(c, rows_10[c]), preferred_element_type=jnp.float32)
            tile_2 = jnp.dot(w_2s[c], window_values(c, rows_2[c]), preferred_element_type=jnp.float32)
            acc_ref[c, n:, :] = acc[n:]
            acc_ref[c, :n, :] = acc[:n] + jnp.where(done, tile_2, 0.0)
        return 2, done

    def tile_copies(m):
        copies = []
        for c, (sub, p) in enumerate(chains):
            row0 = pl.multiple_of(KV_PAD + (n_sub * i + sub - m) * SB_TILE, SB_TILE)
            for which, dst in ((0, ktile_ref), (1, vtile_ref)):
                copies.append(pltpu.make_async_copy(
                    kv_hbm.at[which, pl.ds(row0, SB_TILE), pl.ds(p * LANES, LANES)], dst.at[c],
                    tile_sems.at[which, c]))
        return copies

    def next_sweep(m):
        for copy in tile_copies(m):
            copy.start()
        for copy in tile_copies(m):
            copy.wait()
        zs = [logits(q_stack[c], ktile_ref[c]) for c in range(len(chains))]
        ls = [neg_log2_not_beta(z).astype(jnp.bfloat16) for z in zs]
        cs = jnp.dot(jnp.concatenate(ls, axis=0), w_1, preferred_element_type=jnp.float32)
        carries, w_cats = [], []
        for c, z in enumerate(zs):
            part = slice(c * 2 * SB_TILE, (c + 1) * 2 * SB_TILE)
            old = carry_ref[c]
            w_cats.append(weights(z, cs[part, :SB_TILE] + old))
            carries.append(cs[part, SB_TILE:SB_TILE + 1] + old)
            carry_ref[c] = carries[c]
        for c in range(len(chains)):
            acc_ref[c] += jnp.dot(w_cats[c], _split_heads(vtile_ref[c]), preferred_element_type=jnp.float32)
        return all_done(carries)

    m0, done0 = first_sweep()
    before_loop()
    last = n_sub * i + n_sub - 1

    def cond(state):
        m, done = state
        return jnp.logical_and(m <= last, jnp.logical_not(done))

    def body(state):
        m, _ = state
        return m + 1, next_sweep(m)

    lax.while_loop(cond, body, (jnp.int32(m0), done0))
    return jnp.concatenate(
        [jnp.concatenate([acc_ref[sub * n_pair + p] for p in range(n_pair)], axis=1)
         for sub in range(n_sub)], axis=0)


def _mem_attention(q_ref, mkv_ref, filler):
    q_rows = q_ref.shape[0]
    n_mem = mkv_ref.shape[0]
    n_pair = MEM_WIDTH // LANES
    head_ones = ((lax.broadcasted_iota(jnp.int32, (2 * n_mem, LANES), 1) < HEAD_DIM)
                 == (lax.broadcasted_iota(jnp.int32, (2 * n_mem, LANES), 0) < n_mem)
                 ).astype(jnp.float32).astype(jnp.bfloat16)
    logits = [_dot_nt(_split_heads(q_ref[:, SB_WIDTH + p * LANES:SB_WIDTH + (p + 1) * LANES]),
                      mkv_ref[:, p * LANES:(p + 1) * LANES]) for p in range(n_pair)]
    filler()
    outs = []
    for p, s in enumerate(logits):
        mv = mkv_ref[:, MEM_WIDTH + p * LANES:MEM_WIDTH + (p + 1) * LANES]
        e = jnp.exp(s - jnp.max(s, axis=-1, keepdims=True)).astype(jnp.bfloat16)
        e_cat = jnp.concatenate([e[:q_rows], e[q_rows:]], axis=1)
        num_den = jnp.dot(e_cat, jnp.concatenate([_split_heads(mv), head_ones], axis=1),
                          preferred_element_type=jnp.float32)
        outs.append(num_den[:, :LANES] / num_den[:, LANES:])
    return jnp.concatenate(outs, axis=-1)


def _stream_chunks(jobs, stages, sems):
    depth = len(stages)

    def copy(j):
        return pltpu.make_async_copy(jobs[j][0], stages[j % depth], sems.at[j % depth])

    for j in range(min(depth - 1, len(jobs))):
        copy(j).start()
    for j, (_, consume) in enumerate(jobs):
        if j + depth - 1 < len(jobs):
            copy(j + depth - 1).start()
        copy(j).wait()
        consume(stages[j % depth])


def _chunks(hbm):
    _, n_rows, n_cols = hbm.shape
    return [(r0, c0, hbm.at[0, pl.ds(r0, STREAM_ROWS), pl.ds(c0, STREAM_COLS)])
            for r0 in range(0, n_rows, STREAM_ROWS) for c0 in range(0, n_cols, STREAM_COLS)]


def _load_parameters(mem_hbm, g_mem_ref, w_kv_hbm, w_in_hbm, w_out_hbm, mkv_ref, w_in_ref, w_out_ref,
                     stages, sems):
    jobs = []
    state = {"mem": [], "h": None, "kv": None}

    def take_mem(window):
        state["mem"].append(window[...])

    def memory_kv(window, r0):
        if state["h"] is None:
            m = jnp.concatenate(state["mem"], axis=1)
            state["h"] = (m * _rms_scale(m) * g_mem_ref[...]).astype(jnp.bfloat16)
        part = jnp.dot(state["h"][:, r0:r0 + STREAM_ROWS], window[...].astype(jnp.bfloat16),
                       preferred_element_type=jnp.float32)
        state["kv"] = part if state["kv"] is None else state["kv"] + part
        if r0 + STREAM_ROWS == w_kv_hbm.shape[1]:
            mkv_ref[...] = state["kv"].astype(jnp.bfloat16)

    def cast_into(dst, r0, c0):
        def consume(window):
            dst[r0:r0 + STREAM_ROWS, c0:c0 + STREAM_COLS] = window[...].astype(jnp.bfloat16)
        return consume

    assert mem_hbm.shape[1] == STREAM_ROWS and w_kv_hbm.shape[2] == STREAM_COLS
    jobs += [(chunk, take_mem) for _, _, chunk in _chunks(mem_hbm)]
    jobs += [(chunk, lambda window, r0=r0: memory_kv(window, r0)) for r0, _, chunk in _chunks(w_kv_hbm)]
    for hbm, dst in ((w_in_hbm, w_in_ref), (w_out_hbm, w_out_ref)):
        jobs += [(chunk, cast_into(dst, r0, c0)) for r0, c0, chunk in _chunks(hbm)]
    _stream_chunks(jobs, stages, sems)


def _layer_kernel(x_ref, mem_hbm, g_in_ref, w_in_hbm, convw_ref, convb_ref, g_memin_ref, w_kv_hbm,
                  g_sb_ref, g_conv_ref, g_mem_ref, w_out_hbm, g_final_ref,
                  out_ref, kv_hbm, w_in_ref, w_out_ref, mkv_ref, stream_sems, kwin_ref, vwin_ref, kv_next_ref,
                  ktile_ref, vtile_ref, kv_sems, tile_sems, q2_ref, rest2_ref, x_prev_ref, cu_ref, acc_ref,
                  carry_ref):
    s = pl.program_id(0)
    q_rows = x_ref.shape[0]
    c_u = 3 * SB_WIDTH
    c_qm = c_u + 3 * CONV_WIDTH
    c_gate = c_qm + MEM_WIDTH

    def normed_input():
        x = x_ref[...]
        return (x * _rms_scale(x) * g_in_ref[...]).astype(jnp.bfloat16)

    def proj(h, c0, width):
        return jnp.dot(h, w_in_ref[:, c0:c0 + width], preferred_element_type=jnp.float32)

    def project_kv(h):
        kv_next_ref[...] = proj(h, SB_WIDTH, 2 * SB_WIDTH).astype(jnp.bfloat16)

    def project_q_conv(h, slot):
        q2_ref[slot, :, :SB_WIDTH] = (proj(h, 0, SB_WIDTH) * (SCALE * LOG2_E)).astype(jnp.bfloat16)
        q2_ref[slot, :, SB_WIDTH:] = (proj(h, c_qm, MEM_WIDTH) * SCALE).astype(jnp.bfloat16)
        rest2_ref[slot, :, :3 * CONV_WIDTH] = proj(h, c_u, 3 * CONV_WIDTH)

    def project_gate(h, slot):
        for c0 in range(0, MIX_WIDTH, PROJ_COLS):
            rest2_ref[slot, :, 3 * CONV_WIDTH + c0:3 * CONV_WIDTH + c0 + PROJ_COLS] = proj(h, c_gate + c0, PROJ_COLS)

    @pl.when(s == 0)
    def _():
        stages = [ref.at[pl.ds(0, STREAM_ROWS), pl.ds(c0, STREAM_COLS)] for ref in (x_prev_ref, rest2_ref.at[1])
                  for c0 in range(0, ref.shape[-1] - STREAM_COLS + 1, STREAM_COLS)]
        _load_parameters(mem_hbm, g_memin_ref, w_kv_hbm, w_in_hbm, w_out_hbm, mkv_ref, w_in_ref, w_out_ref,
                         stages[:STREAM_STAGES], stream_sems)
        h = normed_input()
        project_kv(h)
        project_q_conv(h, 0)
        project_gate(h, 0)
        x_prev_ref[...] = x_ref[...]
        cu_ref[0:SUBLANES, :] = jnp.zeros((SUBLANES, CONV_WIDTH), jnp.float32)
        kwin_ref[...] = jnp.zeros(kwin_ref.shape, jnp.bfloat16)
        vwin_ref[...] = jnp.zeros(vwin_ref.shape, jnp.bfloat16)
        zero_rows = [pltpu.make_async_copy(win.at[pl.ds(0, KV_PAD)], kv_hbm.at[which, pl.ds(0, KV_PAD)],
                                           kv_sems.at[which])
                     for which, win in ((0, kwin_ref), (1, vwin_ref))]
        for copy in zero_rows:
            copy.start()
        for copy in zero_rows:
            copy.wait()

    @pl.when(s > 0)
    def _():
        i = s - 1
        cur = lax.rem(i, 2)
        nxt = 1 - cur
        q_ref = q2_ref.at[cur]
        rest_ref = rest2_ref.at[cur]

        def history_writes(block):
            rows = pl.ds(pl.multiple_of(KV_PAD + block * q_rows, q_rows), q_rows)
            return [pltpu.make_async_copy(win.at[pl.ds(KV_PREV, q_rows)], kv_hbm.at[which, rows],
                                          kv_sems.at[which])
                    for which, win in ((0, kwin_ref), (1, vwin_ref))]

        for win, c0 in ((kwin_ref, 0), (vwin_ref, SB_WIDTH)):
            win[0:KV_PREV, :] = win[q_rows:q_rows + KV_PREV, :]
            win[KV_PREV:, :] = kv_next_ref[:, c0:c0 + SB_WIDTH]
        for copy in history_writes(i):
            copy.start()

        h_next = normed_input()
        y_sb = _sb_sweep(i, q_ref, kwin_ref, vwin_ref, kv_hbm, ktile_ref, vtile_ref, tile_sems, acc_ref, carry_ref,
                         (lambda: project_kv(h_next), lambda: project_q_conv(h_next, nxt)),
                         lambda: [copy.wait() for copy in history_writes(i)])

        u = rest_ref[:, 0:CONV_WIDTH]
        b = rest_ref[:, CONV_WIDTH:2 * CONV_WIDTH]
        c = rest_ref[:, 2 * CONV_WIDTH:3 * CONV_WIDTH]
        cu = c * u
        cu_ref[SUBLANES:, :] = cu
        cu_1 = cu_ref[SUBLANES - 1:SUBLANES - 1 + q_rows, :]
        cu_2 = cu_ref[SUBLANES - 2:SUBLANES - 2 + q_rows, :]
        cu_ref[0:SUBLANES, :] = cu[q_rows - SUBLANES:, :]
        conv = (convw_ref[0, 0:1, :] * cu_2 + convw_ref[0, 1:2, :] * cu_1 + convw_ref[0, 2:3, :] * cu
                + convb_ref[...])
        y_conv = b * conv

        y_mem = _mem_attention(q_ref, mkv_ref, lambda: project_gate(h_next, nxt))

        y = jnp.concatenate([y_sb * _rms_scale(y_sb) * g_sb_ref[...],
                             y_conv * _rms_scale(y_conv) * g_conv_ref[...],
                             y_mem * _rms_scale(y_mem) * g_mem_ref[...]], axis=-1)
        gate = rest_ref[:, 3 * CONV_WIDTH:]
        gated = (y * (gate / (1.0 + jnp.exp(-gate)))).astype(jnp.bfloat16)
        res = x_prev_ref[...] + jnp.dot(gated, w_out_ref[...], preferred_element_type=jnp.float32)
        out_ref[...] = res * _rms_scale(res) * g_final_ref[...]
        x_prev_ref[...] = x_ref[...]


def _layer(x2, mem, g_in, w_in, conv_w, conv_b, g_memin, w_kv, g_sb, g_conv, g_mem, w_out, g_final):
    t, d = x2.shape
    assert t % Q_BLOCK == 0 and Q_BLOCK % SB_TILE == 0 and KV_PREV <= Q_BLOCK
    assert Q_BLOCK >= STREAM_ROWS
    assert d // STREAM_COLS + (3 * CONV_WIDTH + MIX_WIDTH) // STREAM_COLS >= STREAM_STAGES
    for a in (mem, w_kv, w_in, w_out):
        assert a.shape[1] % STREAM_ROWS == 0 and a.shape[2] % STREAM_COLS == 0
    n_chain = (Q_BLOCK // SB_TILE) * (SB_WIDTH // LANES)

    def whole(a):
        return pl.BlockSpec(a.shape, lambda s: (0,) * a.ndim)

    in_hbm = pl.BlockSpec(memory_space=pl.ANY)
    n_blk = t // Q_BLOCK
    return pl.pallas_call(
        _layer_kernel,
        grid=(n_blk + 1,),
        in_specs=[pl.BlockSpec((Q_BLOCK, d), lambda s: (jnp.minimum(s, n_blk - 1), 0)), in_hbm, whole(g_in),
                  in_hbm, whole(conv_w), whole(conv_b), whole(g_memin), in_hbm, whole(g_sb), whole(g_conv),
                  whole(g_mem), in_hbm, whole(g_final)],
        out_specs=[pl.BlockSpec((Q_BLOCK, d), lambda s: (jnp.maximum(s - 1, 0), 0)), in_hbm],
        out_shape=[jax.ShapeDtypeStruct((t, d), jnp.float32),
                   jax.ShapeDtypeStruct((2, KV_PAD + t, SB_WIDTH), jnp.bfloat16)],
        scratch_shapes=[pltpu.VMEM(w_in.shape[1:], jnp.bfloat16),
                        pltpu.VMEM(w_out.shape[1:], jnp.bfloat16),
                        pltpu.VMEM((mem.shape[1], w_kv.shape[2]), jnp.bfloat16),
                        pltpu.SemaphoreType.DMA((STREAM_STAGES,)),
                        pltpu.VMEM((KV_PREV + Q_BLOCK, SB_WIDTH), jnp.bfloat16),
                        pltpu.VMEM((KV_PREV + Q_BLOCK, SB_WIDTH), jnp.bfloat16),
                        pltpu.VMEM((Q_BLOCK, 2 * SB_WIDTH), jnp.bfloat16),
                        pltpu.VMEM((n_chain, SB_TILE, LANES), jnp.bfloat16),
                        pltpu.VMEM((n_chain, SB_TILE, LANES), jnp.bfloat16),
                        pltpu.SemaphoreType.DMA((2,)),
                        pltpu.SemaphoreType.DMA((2, n_chain)),
                        pltpu.VMEM((2, Q_BLOCK, SB_WIDTH + MEM_WIDTH), jnp.bfloat16),
                        pltpu.VMEM((2, Q_BLOCK, 3 * CONV_WIDTH + MIX_WIDTH), jnp.float32),
                        pltpu.VMEM((Q_BLOCK, d), jnp.float32),
                        pltpu.VMEM((SUBLANES + Q_BLOCK, CONV_WIDTH), jnp.float32),
                        pltpu.VMEM((n_chain, SB_TILE, LANES), jnp.float32),
                        pltpu.VMEM((n_chain, 2 * SB_TILE, 1), jnp.float32)],
        compiler_params=pltpu.CompilerParams(
            dimension_semantics=("arbitrary",), vmem_limit_bytes=VMEM_LIMIT_BYTES),
        name="layer",
    )(x2, mem, g_in, w_in, conv_w, conv_b, g_memin, w_kv, g_sb, g_conv, g_mem, w_out, g_final)[0]


def kernel(x, mem, g_in, w_in, conv_w, conv_b, g_mem, w_mem_kv, g_sb_out, g_conv_out,
           g_mem_out, w_out, g_final):
    batch, t, d = x.shape
    assert batch == 1 and g_in.shape[0] == 1
    out = _layer(x.reshape(t, d), mem, g_in[0][None, :], w_in, conv_w, conv_b[0][None, :],
                 g_mem[0][None, :], w_mem_kv, g_sb_out[0][None, :], g_conv_out[0][None, :],
                 g_mem_out[0][None, :], w_out, g_final[None, :])
    return out.reshape(batch, t, d)
```

```python
import jax
import jax.numpy as jnp
from jax import lax
from jax.experimental import pallas as pl
from jax.experimental.pallas import tpu as pltpu

HEAD_DIM = 64
SB_HEADS = 8
SB_WIDTH = SB_HEADS * HEAD_DIM
CONV_WIDTH = 4 * HEAD_DIM
MEM_HEADS = 4
MEM_WIDTH = MEM_HEADS * HEAD_DIM
CONV_K = 3
EPS = 1e-6
SCALE = HEAD_DIM ** -0.5

LANES = 128
SUBLANES = 8
MIX_WIDTH = SB_WIDTH + CONV_WIDTH + MEM_WIDTH
PROJ_COLS = 512
Q_BLOCK = 512
SB_TILE = 128
SWEEP_STOP = 104.0
TILE2_ROWS = 32
KV_PAD = Q_BLOCK
KV_PREV = 2 * SB_TILE
LOG2_E = 1.4426950408889634
LOGIT2_CLAMP = 126.0
STREAM_ROWS = 256
STREAM_COLS = 512
STREAM_STAGES = 5
VMEM_LIMIT_BYTES = 62 * 1024 * 1024


def _rms_scale(xf, eps=EPS):
    return lax.rsqrt(jnp.mean(xf * xf, axis=-1, keepdims=True) + eps)


def _dot_nt(a, b):
    return lax.dot_general(a, b, (((1,), (1,)), ((), ())), preferred_element_type=jnp.float32)


def _split_heads(a):
    first_half = lax.broadcasted_iota(jnp.int32, a.shape, 1) < HEAD_DIM
    zero = jnp.zeros_like(a)
    return jnp.concatenate([jnp.where(first_half, a, zero), jnp.where(first_half, zero, a)], axis=0)


def _sb_sweep(i, q_ref, kwin_ref, vwin_ref, kv_hbm, ktile_ref, vtile_ref, tile_sems, acc_ref, carry_ref, fillers,
              before_loop):
    n_sub = q_ref.shape[0] // SB_TILE
    n_pair = SB_WIDTH // LANES
    chains = [(sub, p) for sub in range(n_sub) for p in range(n_pair)]
    fillers = list(fillers)

    rt = lax.broadcasted_iota(jnp.int32, (2 * SB_TILE, 2 * SB_TILE), 0)
    ct = lax.broadcasted_iota(jnp.int32, (2 * SB_TILE, 2 * SB_TILE), 1)
    top, left = rt < SB_TILE, ct < SB_TILE
    tri2 = jnp.where(top, rt, rt - SB_TILE) >= jnp.where(left, ct, ct - SB_TILE)
    causal = jnp.logical_or(left, jnp.where(left, ct, ct - SB_TILE) < jnp.where(top, rt, rt - SB_TILE))

    def as_matrix(cond):
        return jnp.where(cond, 1.0, 0.0).astype(jnp.bfloat16)

    w_10 = as_matrix(jnp.logical_or(jnp.logical_and(jnp.logical_not(top), left),
                                    jnp.logical_and(tri2, jnp.logical_not(jnp.logical_xor(top, left)))))
    w_2 = as_matrix(jnp.logical_or(top, jnp.logical_or(tri2, jnp.logical_not(left))))
    w_1 = w_2[SB_TILE:]

    q_stack = [_split_heads(q_ref[sub * SB_TILE:(sub + 1) * SB_TILE, p * LANES:(p + 1) * LANES])
               for sub, p in chains]

    def window_rows(c, m, n_tiles=1):
        start = KV_PREV + (chains[c][0] - (m + n_tiles - 1)) * SB_TILE
        assert start >= 0
        return slice(start, start + n_tiles * SB_TILE)

    def head_rows(a, n):
        return jnp.concatenate([a[:n], a[SB_TILE:SB_TILE + n]], axis=0)

    def logits(q, keys):
        return jnp.minimum(_dot_nt(q, keys), LOGIT2_CLAMP)

    def window_keys(c, rows):
        p = chains[c][1]
        return kwin_ref[rows, p * LANES:(p + 1) * LANES]

    def window_values(c, rows):
        p = chains[c][1]
        return _split_heads(vwin_ref[rows, p * LANES:(p + 1) * LANES])

    def neg_log2_not_beta(z):
        return jnp.log(1.0 + jnp.exp2(z)) * LOG2_E

    def weights(z, s, mask=None):
        w = jnp.exp2(z - s)
        if mask is not None:
            w = jnp.where(mask, w, 0.0)
        wb = w.astype(jnp.bfloat16)
        n = wb.shape[0] // 2
        return jnp.concatenate([wb[:n], wb[n:]], axis=1)

    def all_done(carries):
        floor = carries[0]
        for carry in carries[1:]:
            floor = jnp.minimum(floor, carry)
        return jnp.min(floor) >= SWEEP_STOP * LOG2_E

    def first_sweep():
        n = TILE2_ROWS
        rows_10 = [window_rows(c, 0, 2) for c in range(len(chains))]
        rows_2 = [window_rows(c, 2) for c in range(len(chains))]
        zs = [(logits(q_stack[c], window_keys(c, rows_10[c])),
               logits(head_rows(q_stack[c], n), window_keys(c, rows_2[c]))) for c in range(len(chains))]
        if fillers:
            fillers.pop(0)()
        lhs_10, lhs_2 = [], []
        for z10, z2 in zs:
            l10 = jnp.where(causal, neg_log2_not_beta(z10), 0.0)
            lhs_10.append(l10.astype(jnp.bfloat16))
            lhs_2.append(jnp.concatenate(
                [head_rows(l10[:, :SB_TILE] + l10[:, SB_TILE:], n).astype(jnp.bfloat16),
                 neg_log2_not_beta(z2).astype(jnp.bfloat16)], axis=1))
        s_10 = jnp.dot(jnp.concatenate(lhs_10, axis=0), w_10, preferred_element_type=jnp.float32)
        s_2 = jnp.dot(jnp.concatenate(lhs_2, axis=0), w_2, preferred_element_type=jnp.float32)
        if fillers:
            fillers.pop(0)()
        reached, w_10s, w_2s = [], [], []
        for c, (z10, z2) in enumerate(zs):
            part = slice(c * 2 * SB_TILE, (c + 1) * 2 * SB_TILE)
            part2 = slice(c * 2 * n, (c + 1) * 2 * n)
            w_10s.append(weights(z10, s_10[part], causal))
            w_2s.append(weights(z2, s_2[part2, :SB_TILE]))
            after_1 = s_10[part, 0:1]
            after_2 = s_2[part2, SB_TILE:SB_TILE + 1]
            carry_ref[c] = after_1
            reached.append(jnp.concatenate([after_2[:n], after_1[n:SB_TILE],
                                            after_2[n:], after_1[SB_TILE + n:]], axis=0))
        done = all_done(reached)
        for c in range(len(chains)):
            acc = jnp.dot(w_10s[c], window_values(c, rows_10[c]), preferred_element_type=jnp.float32)
            tile_2 = jnp.dot(w_2s[c], window_values(c, rows_2[c]), preferred_element_type=jnp.float32)
            acc_ref[c, n:, :] = acc[n:]
            acc_ref[c, :n, :] = acc[:n] + jnp.where(done, tile_2, 0.0)
        return 2, done

    def tile_copies(m):
        copies = []
        for c, (sub, p) in enumerate(chains):
            row0 = pl.multiple_of(KV_PAD + (n_sub * i + sub - m) * SB_TILE, SB_TILE)
            for which, dst in ((0, ktile_ref), (1, vtile_ref)):
                copies.append(pltpu.make_async_copy(
                    kv_hbm.at[which, pl.ds(row0, SB_TILE), pl.ds(p * LANES, LANES)], dst.at[c],
                    tile_sems.at[which, c]))
        return copies

    def next_sweep(m):
        for copy in tile_copies(m):
            copy.start()
        for copy in tile_copies(m):
            copy.wait()
        zs = [logits(q_stack[c], ktile_ref[c]) for c in range(len(chains))]
        ls = [neg_log2_not_beta(z).astype(jnp.bfloat16) for z in zs]
        cs = jnp.dot(jnp.concatenate(ls, axis=0), w_1, preferred_element_type=jnp.float32)
        carries, w_cats = [], []
        for c, z in enumerate(zs):
            part = slice(c * 2 * SB_TILE, (c + 1) * 2 * SB_TILE)
            old = carry_ref[c]
            w_cats.append(weights(z, cs[part, :SB_TILE] + old))
            carries.append(cs[part, SB_TILE:SB_TILE + 1] + old)
            carry_ref[c] = carries[c]
        for c in range(len(chains)):
            acc_ref[c] += jnp.dot(w_cats[c], _split_heads(vtile_ref[c]), preferred_element_type=jnp.float32)
        return all_done(carries)

    m0, done0 = first_sweep()
    before_loop()
    last = n_sub * i + n_sub - 1

    def cond(state):
        m, done = state
        return jnp.logical_and(m <= last, jnp.logical_not(done))

    def body(state):
        m, _ = state
        return m + 1, next_sweep(m)

    lax.while_loop(cond, body, (jnp.int32(m0), done0))
    return jnp.concatenate(
        [jnp.concatenate([acc_ref[sub * n_pair + p] for p in range(n_pair)], axis=1)
         for sub in range(n_sub)], axis=0)


def _mem_attention(q_ref, mkv_ref, filler):
    q_rows = q_ref.shape[0]
    n_mem = mkv_ref.shape[0]
    n_pair = MEM_WIDTH // LANES
    head_ones = ((lax.broadcasted_iota(jnp.int32, (2 * n_mem, LANES), 1) < HEAD_DIM)
                 == (lax.broadcasted_iota(jnp.int32, (2 * n_mem, LANES), 0) < n_mem)
                 ).astype(jnp.float32).astype(jnp.bfloat16)
    logits = [_dot_nt(_split_heads(q_ref[:, SB_WIDTH + p * LANES:SB_WIDTH + (p + 1) * LANES]),
                      mkv_ref[:, p * LANES:(p + 1) * LANES]) for p in range(n_pair)]
    filler()
    outs = []
    for p, s in enumerate(logits):
        mv = mkv_ref[:, MEM_WIDTH + p * LANES:MEM_WIDTH + (p + 1) * LANES]
        e = jnp.exp(s - jnp.max(s, axis=-1, keepdims=True)).astype(jnp.bfloat16)
        e_cat = jnp.concatenate([e[:q_rows], e[q_rows:]], axis=1)
        num_den = jnp.dot(e_cat, jnp.concatenate([_split_heads(mv), head_ones], axis=1),
                          preferred_element_type=jnp.float32)
        outs.append(num_den[:, :LANES] / num_den[:, LANES:])
    return jnp.concatenate(outs, axis=-1)


def _stream_chunks(jobs, stages, sems):
    depth = len(stages)

    def copy(j):
        return pltpu.make_async_copy(jobs[j][0], stages[j % depth], sems.at[j % depth])

    for j in range(min(depth - 1, len(jobs))):
        copy(j).start()
    for j, (_, consume) in enumerate(jobs):
        if j + depth - 1 < len(jobs):
            copy(j + depth - 1).start()
        copy(j).wait()
        consume(stages[j % depth])


def _chunks(hbm):
    _, n_rows, n_cols = hbm.shape
    return [(r0, c0, hbm.at[0, pl.ds(r0, STREAM_ROWS), pl.ds(c0, STREAM_COLS)])
            for r0 in range(0, n_rows, STREAM_ROWS) for c0 in range(0, n_cols, STREAM_COLS)]


def _load_parameters(mem_hbm, g_mem_ref, w_kv_hbm, w_in_hbm, w_out_hbm, mkv_ref, w_in_ref, w_out_ref,
                     stages, sems):
    jobs = []
    state = {"mem": [], "h": None, "kv": None}

    def take_mem(window):
        state["mem"].append(window[...])

    def memory_kv(window, r0):
        if state["h"] is None:
            m = jnp.concatenate(state["mem"], axis=1)
            state["h"] = (m * _rms_scale(m) * g_mem_ref[...]).astype(jnp.bfloat16)
        part = jnp.dot(state["h"][:, r0:r0 + STREAM_ROWS], window[...].astype(jnp.bfloat16),
                       preferred_element_type=jnp.float32)
        state["kv"] = part if state["kv"] is None else state["kv"] + part
        if r0 + STREAM_ROWS == w_kv_hbm.shape[1]:
            mkv_ref[...] = state["kv"].astype(jnp.bfloat16)

    def cast_into(dst, r0, c0):
        def consume(window):
            dst[r0:r0 + STREAM_ROWS, c0:c0 + STREAM_COLS] = window[...].astype(jnp.bfloat16)
        return consume

    assert mem_hbm.shape[1] == STREAM_ROWS and w_kv_hbm.shape[2] == STREAM_COLS
    jobs += [(chunk, take_mem) for _, _, chunk in _chunks(mem_hbm)]
    jobs += [(chunk, lambda window, r0=r0: memory_kv(window, r0)) for r0, _, chunk in _chunks(w_kv_hbm)]
    for hbm, dst in ((w_in_hbm, w_in_ref), (w_out_hbm, w_out_ref)):
        jobs += [(chunk, cast_into(dst, r0, c0)) for r0, c0, chunk in _chunks(hbm)]
    _stream_chunks(jobs, stages, sems)


def _layer_kernel(x_ref, mem_hbm, g_in_ref, w_in_hbm, convw_ref, convb_ref, g_memin_ref, w_kv_hbm,
                  g_sb_ref, g_conv_ref, g_mem_ref, w_out_hbm, g_final_ref,
                  out_ref, kv_hbm, w_in_ref, w_out_ref, mkv_ref, stream_sems, kwin_ref, vwin_ref, kv_next_ref,
                  ktile_ref, vtile_ref, kv_sems, tile_sems, q2_ref, rest2_ref, x_prev_ref, res_ref, cu_ref,
                  acc_ref, carry_ref):
    s = pl.program_id(0)
    n_blk = pl.num_programs(0) - 2
    q_rows = x_ref.shape[0]
    c_u = 3 * SB_WIDTH
    c_qm = c_u + 3 * CONV_WIDTH
    c_gate = c_qm + MEM_WIDTH

    def normed_input():
        x = x_ref[...]
        return (x * _rms_scale(x) * g_in_ref[...]).astype(jnp.bfloat16)

    def proj(h, c0, width):
        return jnp.dot(h, w_in_ref[:, c0:c0 + width], preferred_element_type=jnp.float32)

    def project_kv(h):
        kv_next_ref[...] = proj(h, SB_WIDTH, 2 * SB_WIDTH).astype(jnp.bfloat16)

    def project_q_conv(h, slot):
        q2_ref[slot, :, :SB_WIDTH] = (proj(h, 0, SB_WIDTH) * (SCALE * LOG2_E)).astype(jnp.bfloat16)
        q2_ref[slot, :, SB_WIDTH:] = (proj(h, c_qm, MEM_WIDTH) * SCALE).astype(jnp.bfloat16)
        rest2_ref[slot, :, :3 * CONV_WIDTH] = proj(h, c_u, 3 * CONV_WIDTH)

    def project_gate(h, slot):
        for c0 in range(0, MIX_WIDTH, PROJ_COLS):
            rest2_ref[slot, :, 3 * CONV_WIDTH + c0:3 * CONV_WIDTH + c0 + PROJ_COLS] = proj(h, c_gate + c0, PROJ_COLS)

    def final_norm():
        res = res_ref[...]
        out_ref[...] = res * _rms_scale(res) * g_final_ref[...]

    @pl.when(s == 0)
    def _():
        stages = [ref.at[pl.ds(0, STREAM_ROWS), pl.ds(c0, STREAM_COLS)] for ref in (x_prev_ref, rest2_ref.at[1])
                  for c0 in range(0, ref.shape[-1] - STREAM_COLS + 1, STREAM_COLS)]
        _load_parameters(mem_hbm, g_memin_ref, w_kv_hbm, w_in_hbm, w_out_hbm, mkv_ref, w_in_ref, w_out_ref,
                         stages[:STREAM_STAGES], stream_sems)
        h = normed_input()
        project_kv(h)
        project_q_conv(h, 0)
        project_gate(h, 0)
        x_prev_ref[...] = x_ref[...]
        res_ref[...] = jnp.zeros(res_ref.shape, jnp.float32)
        cu_ref[0:SUBLANES, :] = jnp.zeros((SUBLANES, CONV_WIDTH), jnp.float32)
        kwin_ref[...] = jnp.zeros(kwin_ref.shape, jnp.bfloat16)
        vwin_ref[...] = jnp.zeros(vwin_ref.shape, jnp.bfloat16)
        zero_rows = [pltpu.make_async_copy(win.at[pl.ds(0, KV_PAD)], kv_hbm.at[which, pl.ds(0, KV_PAD)],
                                           kv_sems.at[which])
                     for which, win in ((0, kwin_ref), (1, vwin_ref))]
        for copy in zero_rows:
            copy.start()
        for copy in zero_rows:
            copy.wait()

    @pl.when(s == n_blk + 1)
    def _():
        final_norm()

    @pl.when(jnp.logical_and(s > 0, s <= n_blk))
    def _():
        i = s - 1
        cur = lax.rem(i, 2)
        nxt = 1 - cur
        q_ref = q2_ref.at[cur]
        rest_ref = rest2_ref.at[cur]

        def history_writes(block):
            rows = pl.ds(pl.multiple_of(KV_PAD + block * q_rows, q_rows), q_rows)
            return [pltpu.make_async_copy(win.at[pl.ds(KV_PREV, q_rows)], kv_hbm.at[which, rows],
                                          kv_sems.at[which])
                    for which, win in ((0, kwin_ref), (1, vwin_ref))]

        for win, c0 in ((kwin_ref, 0), (vwin_ref, SB_WIDTH)):
            win[0:KV_PREV, :] = win[q_rows:q_rows + KV_PREV, :]
            win[KV_PREV:, :] = kv_next_ref[:, c0:c0 + SB_WIDTH]
        for copy in history_writes(i):
            copy.start()
        final_norm()

        h_next = normed_input()
        y_sb = _sb_sweep(i, q_ref, kwin_ref, vwin_ref, kv_hbm, ktile_ref, vtile_ref, tile_sems, acc_ref, carry_ref,
                         (lambda: project_kv(h_next), lambda: project_q_conv(h_next, nxt)),
                         lambda: [copy.wait() for copy in history_writes(i)])

        u = rest_ref[:, 0:CONV_WIDTH]
        b = rest_ref[:, CONV_WIDTH:2 * CONV_WIDTH]
        c = rest_ref[:, 2 * CONV_WIDTH:3 * CONV_WIDTH]
        cu = c * u
        cu_ref[SUBLANES:, :] = cu
        cu_1 = cu_ref[SUBLANES - 1:SUBLANES - 1 + q_rows, :]
        cu_2 = cu_ref[SUBLANES - 2:SUBLANES - 2 + q_rows, :]
        cu_ref[0:SUBLANES, :] = cu[q_rows - SUBLANES:, :]
        conv = (convw_ref[0, 0:1, :] * cu_2 + convw_ref[0, 1:2, :] * cu_1 + convw_ref[0, 2:3, :] * cu
                + convb_ref[...])
        y_conv = b * conv

        y_mem = _mem_attention(q_ref, mkv_ref, lambda: project_gate(h_next, nxt))

        y = jnp.concatenate([y_sb * _rms_scale(y_sb) * g_sb_ref[...],
                             y_conv * _rms_scale(y_conv) * g_conv_ref[...],
                             y_mem * _rms_scale(y_mem) * g_mem_ref[...]], axis=-1)
        gate = rest_ref[:, 3 * CONV_WIDTH:]
        gated = (y * (gate / (1.0 + jnp.exp(-gate)))).astype(jnp.bfloat16)
        res_ref[...] = x_prev_ref[...] + jnp.dot(gated, w_out_ref[...], preferred_element_type=jnp.float32)
        x_prev_ref[...] = x_ref[...]


def _layer(x2, mem, g_in, w_in, conv_w, conv_b, g_memin, w_kv, g_sb, g_conv, g_mem, w_out, g_final):
    t, d = x2.shape
    assert t % Q_BLOCK == 0 and Q_BLOCK % SB_TILE == 0 and KV_PREV <= Q_BLOCK
    assert Q_BLOCK >= STREAM_ROWS
    assert d // STREAM_COLS + (3 * CONV_WIDTH + MIX_WIDTH) // STREAM_COLS >= STREAM_STAGES
    for a in (mem, w_kv, w_in, w_out):
        assert a.shape[1] % STREAM_ROWS == 0 and a.shape[2] % STREAM_COLS == 0
    n_chain = (Q_BLOCK // SB_TILE) * (SB_WIDTH // LANES)

    def whole(a):
        return pl.BlockSpec(a.shape, lambda s: (0,) * a.ndim)

    in_hbm = pl.BlockSpec(memory_space=pl.ANY)
    n_blk = t // Q_BLOCK
    return pl.pallas_call(
        _layer_kernel,
        grid=(n_blk + 2,),
        in_specs=[pl.BlockSpec((Q_BLOCK, d), lambda s: (jnp.minimum(s, n_blk - 1), 0)), in_hbm, whole(g_in),
                  in_hbm, whole(conv_w), whole(conv_b), whole(g_memin), in_hbm, whole(g_sb), whole(g_conv),
                  whole(g_mem), in_hbm, whole(g_final)],
        out_specs=[pl.BlockSpec((Q_BLOCK, d), lambda s: (jnp.clip(s - 2, 0, n_blk - 1), 0)), in_hbm],
        out_shape=[jax.ShapeDtypeStruct((t, d), jnp.float32),
                   jax.ShapeDtypeStruct((2, KV_PAD + t, SB_WIDTH), jnp.bfloat16)],
        scratch_shapes=[pltpu.VMEM(w_in.shape[1:], jnp.bfloat16),
                        pltpu.VMEM(w_out.shape[1:], jnp.bfloat16),
                        pltpu.VMEM((mem.shape[1], w_kv.shape[2]), jnp.bfloat16),
                        pltpu.SemaphoreType.DMA((STREAM_STAGES,)),
                        pltpu.VMEM((KV_PREV + Q_BLOCK, SB_WIDTH), jnp.bfloat16),
                        pltpu.VMEM((KV_PREV + Q_BLOCK, SB_WIDTH), jnp.bfloat16),
                        pltpu.VMEM((Q_BLOCK, 2 * SB_WIDTH), jnp.bfloat16),
                        pltpu.VMEM((n_chain, SB_TILE, LANES), jnp.bfloat16),
                        pltpu.VMEM((n_chain, SB_TILE, LANES), jnp.bfloat16),
                        pltpu.SemaphoreType.DMA((2,)),
                        pltpu.SemaphoreType.DMA((2, n_chain)),
                        pltpu.VMEM((2, Q_BLOCK, SB_WIDTH + MEM_WIDTH), jnp.bfloat16),
                        pltpu.VMEM((2, Q_BLOCK, 3 * CONV_WIDTH + MIX_WIDTH), jnp.float32),
                        pltpu.VMEM((Q_BLOCK, d), jnp.float32),
                        pltpu.VMEM((Q_BLOCK, d), jnp.float32),
                        pltpu.VMEM((SUBLANES + Q_BLOCK, CONV_WIDTH), jnp.float32),
                        pltpu.VMEM((n_chain, SB_TILE, LANES), jnp.float32),
                        pltpu.VMEM((n_chain, 2 * SB_TILE, 1), jnp.float32)],
        compiler_params=pltpu.CompilerParams(
            dimension_semantics=("arbitrary",), vmem_limit_bytes=VMEM_LIMIT_BYTES),
        name="layer",
    )(x2, mem, g_in, w_in, conv_w, conv_b, g_memin, w_kv, g_sb, g_conv, g_mem, w_out, g_final)[0]


def kernel(x, mem, g_in, w_in, conv_w, conv_b, g_mem, w_mem_kv, g_sb_out, g_conv_out,
           g_mem_out, w_out, g_final):
    batch, t, d = x.shape
    assert batch == 1 and g_in.shape[0] == 1
    out = _layer(x.reshape(t, d), mem, g_in[0][None, :], w_in, conv_w, conv_b[0][None, :],
                 g_mem[0][None, :], w_mem_kv, g_sb_out[0][None, :], g_conv_out[0][None, :],
                 g_mem_out[0][None, :], w_out, g_final[None, :])
    return out.reshape(batch, t, d)
```

```python
import jax
import jax.numpy as jnp
from jax import lax
from jax.experimental import pallas as pl
from jax.experimental.pallas import tpu as pltpu

HEAD_DIM = 64
SB_HEADS = 8
SB_WIDTH = SB_HEADS * HEAD_DIM
CONV_WIDTH = 4 * HEAD_DIM
MEM_HEADS = 4
MEM_WIDTH = MEM_HEADS * HEAD_DIM
CONV_K = 3
EPS = 1e-6
SCALE = HEAD_DIM ** -0.5

LANES = 128
SUBLANES = 8
MIX_WIDTH = SB_WIDTH + CONV_WIDTH + MEM_WIDTH
PROJ_COLS = 512
Q_BLOCK = 512
SB_TILE = 128
SWEEP_STOP = 104.0
TILE2_ROWS = 32
KV_PAD = Q_BLOCK
KV_PREV = 2 * SB_TILE
LOG2_E = 1.4426950408889634
LOGIT2_CLAMP = 126.0
STREAM_ROWS = 256
STREAM_COLS = 512
STREAM_STAGES = 5
VMEM_LIMIT_BYTES = 62 * 1024 * 1024


def _rms_scale(xf, eps=EPS):
    return lax.rsqrt(jnp.mean(xf * xf, axis=-1, keepdims=True) + eps)


def _dot_nt(a, b):
    return lax.dot_general(a, b, (((1,), (1,)), ((), ())), preferred_element_type=jnp.float32)


def _split_heads(a):
    first_half = lax.broadcasted_iota(jnp.int32, a.shape, 1) < HEAD_DIM
    zero = jnp.zeros_like(a)
    return jnp.concatenate([jnp.where(first_half, a, zero), jnp.where(first_half, zero, a)], axis=0)


def _sb_sweep(i, q_ref, kwin_ref, vwin_ref, kv_hbm, ktile_ref, vtile_ref, tile_sems, acc_ref, carry_ref, fillers,
              before_loop):
    n_sub = q_ref.shape[0] // SB_TILE
    n_pair = SB_WIDTH // LANES
    chains = [(sub, p) for sub in range(n_sub) for p in range(n_pair)]
    fillers = list(fillers)

    rt = lax.broadcasted_iota(jnp.int32, (2 * SB_TILE, 2 * SB_TILE), 0)
    ct = lax.broadcasted_iota(jnp.int32, (2 * SB_TILE, 2 * SB_TILE), 1)
    top, left = rt < SB_TILE, ct < SB_TILE
    tri2 = jnp.where(top, rt, rt - SB_TILE) >= jnp.where(left, ct, ct - SB_TILE)
    causal = jnp.logical_or(left, jnp.where(left, ct, ct - SB_TILE) < jnp.where(top, rt, rt - SB_TILE))

    def as_matrix(cond):
        return jnp.where(cond, 1.0, 0.0).astype(jnp.bfloat16)

    w_10 = as_matrix(jnp.logical_or(jnp.logical_and(jnp.logical_not(top), left),
                                    jnp.logical_and(tri2, jnp.logical_not(jnp.logical_xor(top, left)))))
    w_2 = as_matrix(jnp.logical_or(top, jnp.logical_or(tri2, jnp.logical_not(left))))
    w_1 = w_2[SB_TILE:]

    q_stack = [_split_heads(q_ref[sub * SB_TILE:(sub + 1) * SB_TILE, p * LANES:(p + 1) * LANES])
               for sub, p in chains]

    def window_rows(c, m, n_tiles=1):
        start = KV_PREV + (chains[c][0] - (m + n_tiles - 1)) * SB_TILE
        assert start >= 0
        return slice(start, start + n_tiles * SB_TILE)

    def head_rows(a, n):
        return jnp.concatenate([a[:n], a[SB_TILE:SB_TILE + n]], axis=0)

    def logits(q, keys):
        return jnp.minimum(_dot_nt(q, keys), LOGIT2_CLAMP)

    def window_keys(c, rows):
        p = chains[c][1]
        return kwin_ref[rows, p * LANES:(p + 1) * LANES]

    def window_values(c, rows):
        p = chains[c][1]
        return _split_heads(vwin_ref[rows, p * LANES:(p + 1) * LANES])

    def neg_log2_not_beta(z):
        return jnp.log(1.0 + jnp.exp2(z)) * LOG2_E

    def weights(z, s, mask=None):
        w = jnp.exp2(z - s)
        if mask is not None:
            w = jnp.where(mask, w, 0.0)
        wb = w.astype(jnp.bfloat16)
        n = wb.shape[0] // 2
        return jnp.concatenate([wb[:n], wb[n:]], axis=1)

    def all_done(carries):
        floor = carries[0]
        for carry in carries[1:]:
            floor = jnp.minimum(floor, carry)
        return jnp.min(floor) >= SWEEP_STOP * LOG2_E

    def first_sweep():
        n = TILE2_ROWS
        rows_10 = [window_rows(c, 0, 2) for c in range(len(chains))]
        rows_2 = [window_rows(c, 2) for c in range(len(chains))]
        zs = [(logits(q_stack[c], window_keys(c, rows_10[c])),
               logits(head_rows(q_stack[c], n), window_keys(c, rows_2[c]))) for c in range(len(chains))]
        if fillers:
            fillers.pop(0)()
        lhs_10, lhs_2 = [], []
        for z10, z2 in zs:
            l10 = jnp.where(causal, neg_log2_not_beta(z10), 0.0)
            lhs_10.append(l10.astype(jnp.bfloat16))
            lhs_2.append(jnp.concatenate(
                [head_rows(l10[:, :SB_TILE] + l10[:, SB_TILE:], n).astype(jnp.bfloat16),
                 neg_log2_not_beta(z2).astype(jnp.bfloat16)], axis=1))
        s_10 = jnp.dot(jnp.concatenate(lhs_10, axis=0), w_10, preferred_element_type=jnp.float32)
        s_2 = jnp.dot(jnp.concatenate(lhs_2, axis=0), w_2, preferred_element_type=jnp.float32)
        if fillers:
            fillers.pop(0)()
        reached, w_10s, w_2s = [], [], []
        for c, (z10, z2) in enumerate(zs):
            part = slice(c * 2 * SB_TILE, (c + 1) * 2 * SB_TILE)
            part2 = slice(c * 2 * n, (c + 1) * 2 * n)
            w_10s.append(weights(z10, s_10[part], causal))
            w_2s.append(weights(z2, s_2[part2, :SB_TILE]))
            after_1 = s_10[part, 0:1]
            after_2 = s_2[part2, SB_TILE:SB_TILE + 1]
            carry_ref[c] = after_1
            reached.append(jnp.concatenate([after_2[:n], after_1[n:SB_TILE],
                                            after_2[n:], after_1[SB_TILE + n:]], axis=0))
        done = all_done(reached)
        for c in range(len(chains)):
            acc = jnp.dot(w_10s[c], window_values(c, rows_10[c]), preferred_element_type=jnp.float32)
            tile_2 = jnp.dot(w_2s[c], window_values(c, rows_2[c]), preferred_element_type=jnp.float32)
            acc_ref[c, n:, :] = acc[n:]
            acc_ref[c, :n, :] = acc[:n] + jnp.where(done, tile_2, 0.0)
        return 2, done

    def tile_copies(m):
        copies = []
        for c, (sub, p) in enumerate(chains):
            row0 = pl.multiple_of(KV_PAD + (n_sub * i + sub - m) * SB_TILE, SB_TILE)
            for which, dst in ((0, ktile_ref), (1, vtile_ref)):
                copies.append(pltpu.make_async_copy(
                    kv_hbm.at[which, pl.ds(row0, SB_TILE), pl.ds(p * LANES, LANES)], dst.at[c],
                    tile_sems.at[which, c]))
        return copies

    def next_sweep(m):
        for copy in tile_copies(m):
            copy.start()
        for copy in tile_copies(m):
            copy.wait()
        zs = [logits(q_stack[c], ktile_ref[c]) for c in range(len(chains))]
        ls = [neg_log2_not_beta(z).astype(jnp.bfloat16) for z in zs]
        cs = jnp.dot(jnp.concatenate(ls, axis=0), w_1, preferred_element_type=jnp.float32)
        carries, w_cats = [], []
        for c, z in enumerate(zs):
            part = slice(c * 2 * SB_TILE, (c + 1) * 2 * SB_TILE)
            old = carry_ref[c]
            w_cats.append(weights(z, cs[part, :SB_TILE] + old))
            carries.append(cs[part, SB_TILE:SB_TILE + 1] + old)
            carry_ref[c] = carries[c]
        for c in range(len(chains)):
            acc_ref[c] += jnp.dot(w_cats[c], _split_heads(vtile_ref[c]), preferred_element_type=jnp.float32)
        return all_done(carries)

    m0, done0 = first_sweep()
    before_loop()
    last = n_sub * i + n_sub - 1

    def cond(state):
        m, done = state
        return jnp.logical_and(m <= last, jnp.logical_not(done))

    def body(state):
        m, _ = state
        return m + 1, next_sweep(m)

    lax.while_loop(cond, body, (jnp.int32(m0), done0))
    return jnp.concatenate(
        [jnp.concatenate([acc_ref[sub * n_pair + p] for p in range(n_pair)], axis=1)
         for sub in range(n_sub)], axis=0)


def _mem_attention(q_ref, mkv_ref, filler):
    q_rows = q_ref.shape[0]
    n_mem = mkv_ref.shape[0]
    n_pair = MEM_WIDTH // LANES
    head_ones = ((lax.broadcasted_iota(jnp.int32, (2 * n_mem, LANES), 1) < HEAD_DIM)
                 == (lax.broadcasted_iota(jnp.int32, (2 * n_mem, LANES), 0) < n_mem)
                 ).astype(jnp.float32).astype(jnp.bfloat16)
    logits = [_dot_nt(_split_heads(q_ref[:, SB_WIDTH + p * LANES:SB_WIDTH + (p + 1) * LANES]),
                      mkv_ref[:, p * LANES:(p + 1) * LANES]) for p in range(n_pair)]
    filler()
    outs = []
    for p, s in enumerate(logits):
        mv = mkv_ref[:, MEM_WIDTH + p * LANES:MEM_WIDTH + (p + 1) * LANES]
        e = jnp.exp(s - jnp.max(s, axis=-1, keepdims=True)).astype(jnp.bfloat16)
        e_cat = jnp.concatenate([e[:q_rows], e[q_rows:]], axis=1)
        num_den = jnp.dot(e_cat, jnp.concatenate([_split_heads(mv), head_ones], axis=1),
                          preferred_element_type=jnp.float32)
        outs.append(num_den[:, :LANES] / num_den[:, LANES:])
    return jnp.concatenate(outs, axis=-1)


def _stream_chunks(jobs, stages, sems):
    depth = len(stages)

    def copy(j):
        return pltpu.make_async_copy(jobs[j][0], stages[j % depth], sems.at[j % depth])

    for j in range(min(depth - 1, len(jobs))):
        copy(j).start()
    for j, (_, consume) in enumerate(jobs):
        if j + depth - 1 < len(jobs):
            copy(j + depth - 1).start()
        copy(j).wait()
        consume(stages[j % depth])


def _chunks(hbm):
    _, n_rows, n_cols = hbm.shape
    return [(r0, c0, hbm.at[0, pl.ds(r0, STREAM_ROWS), pl.ds(c0, STREAM_COLS)])
            for c0 in range(0, n_cols, STREAM_COLS) for r0 in range(0, n_rows, STREAM_ROWS)]


def _load_parameters(mem_hbm, g_mem_ref, w_kv_hbm, w_in_hbm, w_out_hbm, mkv_ref, w_in_ref, w_out_ref,
                     stages, sems, on_w_in_columns):
    jobs = []
    state = {"mem": [], "h": None, "kv": None}

    def take_mem(window):
        state["mem"].append(window[...])

    def memory_kv(window, r0):
        if state["h"] is None:
            m = jnp.concatenate(state["mem"], axis=1)
            state["h"] = (m * _rms_scale(m) * g_mem_ref[...]).astype(jnp.bfloat16)
        part = jnp.dot(state["h"][:, r0:r0 + STREAM_ROWS], window[...].astype(jnp.bfloat16),
                       preferred_element_type=jnp.float32)
        state["kv"] = part if state["kv"] is None else state["kv"] + part
        if r0 + STREAM_ROWS == w_kv_hbm.shape[1]:
            mkv_ref[...] = state["kv"].astype(jnp.bfloat16)

    def cast_into(dst, r0, c0, then=None):
        def consume(window):
            dst[r0:r0 + STREAM_ROWS, c0:c0 + STREAM_COLS] = window[...].astype(jnp.bfloat16)
            if then is not None and r0 + STREAM_ROWS == dst.shape[0]:
                then(c0)
        return consume

    assert mem_hbm.shape[1] == STREAM_ROWS and w_kv_hbm.shape[2] == STREAM_COLS
    jobs += [(chunk, take_mem) for _, _, chunk in _chunks(mem_hbm)]
    jobs += [(chunk, lambda window, r0=r0: memory_kv(window, r0)) for r0, _, chunk in _chunks(w_kv_hbm)]
    jobs += [(chunk, cast_into(w_in_ref, r0, c0, on_w_in_columns)) for r0, c0, chunk in _chunks(w_in_hbm)]
    jobs += [(chunk, cast_into(w_out_ref, r0, c0)) for r0, c0, chunk in _chunks(w_out_hbm)]
    _stream_chunks(jobs, stages, sems)


def _layer_kernel(x_ref, mem_hbm, g_in_ref, w_in_hbm, convw_ref, convb_ref, g_memin_ref, w_kv_hbm,
                  g_sb_ref, g_conv_ref, g_mem_ref, w_out_hbm, g_final_ref,
                  out_ref, kv_hbm, w_in_ref, w_out_ref, mkv_ref, stream_sems, kwin_ref, vwin_ref, kv_next_ref,
                  ktile_ref, vtile_ref, kv_sems, tile_sems, q2_ref, rest2_ref, x_prev_ref, res_ref, cu_ref,
                  acc_ref, carry_ref):
    s = pl.program_id(0)
    n_blk = pl.num_programs(0) - 2
    q_rows = x_ref.shape[0]
    c_u = 3 * SB_WIDTH
    c_qm = c_u + 3 * CONV_WIDTH
    c_gate = c_qm + MEM_WIDTH

    def normed_input():
        x = x_ref[...]
        return (x * _rms_scale(x) * g_in_ref[...]).astype(jnp.bfloat16)

    def project(h, slot, c0, width):
        y = jnp.dot(h, w_in_ref[:, c0:c0 + width], preferred_element_type=jnp.float32)
        for w0, w_width, dst, d0, scale in (
                (0, SB_WIDTH, q2_ref.at[slot], 0, SCALE * LOG2_E),
                (SB_WIDTH, 2 * SB_WIDTH, kv_next_ref, 0, 1.0),
                (c_u, 3 * CONV_WIDTH, rest2_ref.at[slot], 0, None),
                (c_qm, MEM_WIDTH, q2_ref.at[slot], SB_WIDTH, SCALE),
                (c_gate, MIX_WIDTH, rest2_ref.at[slot], 3 * CONV_WIDTH, None)):
            lo, hi = max(c0, w0), min(c0 + width, w0 + w_width)
            if lo < hi:
                piece = y[:, lo - c0:hi - c0]
                if scale is not None:
                    piece = (piece if scale == 1.0 else piece * scale).astype(jnp.bfloat16)
                dst[:, d0 + lo - w0:d0 + hi - w0] = piece

    def project_kv(h):
        project(h, 0, SB_WIDTH, 2 * SB_WIDTH)

    def project_q_conv(h, slot):
        project(h, slot, 0, SB_WIDTH)
        project(h, slot, c_u, 3 * CONV_WIDTH + MEM_WIDTH)

    def project_gate(h, slot):
        for c0 in range(c_gate, c_gate + MIX_WIDTH, PROJ_COLS):
            project(h, slot, c0, PROJ_COLS)

    def final_norm():
        res = res_ref[...]
        out_ref[...] = res * _rms_scale(res) * g_final_ref[...]

    @pl.when(s == 0)
    def _():
        stages = [ref.at[pl.ds(0, STREAM_ROWS), pl.ds(c0, STREAM_COLS)] for ref in (x_prev_ref, rest2_ref.at[1])
                  for c0 in range(0, ref.shape[-1] - STREAM_COLS + 1, STREAM_COLS)]
        h = normed_input()
        _load_parameters(mem_hbm, g_memin_ref, w_kv_hbm, w_in_hbm, w_out_hbm, mkv_ref, w_in_ref, w_out_ref,
                         stages[:STREAM_STAGES], stream_sems,
                         lambda c0: project(h, 0, c0, STREAM_COLS))
        x_prev_ref[...] = x_ref[...]
        res_ref[...] = jnp.zeros(res_ref.shape, jnp.float32)
        cu_ref[0:SUBLANES, :] = jnp.zeros((SUBLANES, CONV_WIDTH), jnp.float32)
        kwin_ref[...] = jnp.zeros(kwin_ref.shape, jnp.bfloat16)
        vwin_ref[...] = jnp.zeros(vwin_ref.shape, jnp.bfloat16)
        zero_rows = [pltpu.make_async_copy(win.at[pl.ds(0, KV_PAD)], kv_hbm.at[which, pl.ds(0, KV_PAD)],
                                           kv_sems.at[which])
                     for which, win in ((0, kwin_ref), (1, vwin_ref))]
        for copy in zero_rows:
            copy.start()
        for copy in zero_rows:
            copy.wait()

    @pl.when(s == n_blk + 1)
    def _():
        final_norm()

    @pl.when(jnp.logical_and(s > 0, s <= n_blk))
    def _():
        i = s - 1
        cur = lax.rem(i, 2)
        nxt = 1 - cur
        q_ref = q2_ref.at[cur]
        rest_ref = rest2_ref.at[cur]

        def history_writes(block):
            rows = pl.ds(pl.multiple_of(KV_PAD + block * q_rows, q_rows), q_rows)
            return [pltpu.make_async_copy(win.at[pl.ds(KV_PREV, q_rows)], kv_hbm.at[which, rows],
                                          kv_sems.at[which])
                    for which, win in ((0, kwin_ref), (1, vwin_ref))]

        for win, c0 in ((kwin_ref, 0), (vwin_ref, SB_WIDTH)):
            win[0:KV_PREV, :] = win[q_rows:q_rows + KV_PREV, :]
            win[KV_PREV:, :] = kv_next_ref[:, c0:c0 + SB_WIDTH]
        for copy in history_writes(i):
            copy.start()
        final_norm()

        h_next = normed_input()
        y_sb = _sb_sweep(i, q_ref, kwin_ref, vwin_ref, kv_hbm, ktile_ref, vtile_ref, tile_sems, acc_ref, carry_ref,
                         (lambda: project_kv(h_next), lambda: project_q_conv(h_next, nxt)),
                         lambda: [copy.wait() for copy in history_writes(i)])

        u = rest_ref[:, 0:CONV_WIDTH]
        b = rest_ref[:, CONV_WIDTH:2 * CONV_WIDTH]
        c = rest_ref[:, 2 * CONV_WIDTH:3 * CONV_WIDTH]
        cu = c * u
        cu_ref[SUBLANES:, :] = cu
        cu_1 = cu_ref[SUBLANES - 1:SUBLANES - 1 + q_rows, :]
        cu_2 = cu_ref[SUBLANES - 2:SUBLANES - 2 + q_rows, :]
        cu_ref[0:SUBLANES, :] = cu[q_rows - SUBLANES:, :]
        conv = (convw_ref[0, 0:1, :] * cu_2 + convw_ref[0, 1:2, :] * cu_1 + convw_ref[0, 2:3, :] * cu
                + convb_ref[...])
        y_conv = b * conv

        y_mem = _mem_attention(q_ref, mkv_ref, lambda: project_gate(h_next, nxt))

        y = jnp.concatenate([y_sb * _rms_scale(y_sb) * g_sb_ref[...],
                             y_conv * _rms_scale(y_conv) * g_conv_ref[...],
                             y_mem * _rms_scale(y_mem) * g_mem_ref[...]], axis=-1)
        gate = rest_ref[:, 3 * CONV_WIDTH:]
        gated = (y * (gate / (1.0 + jnp.exp(-gate)))).astype(jnp.bfloat16)
        res_ref[...] = x_prev_ref[...] + jnp.dot(gated, w_out_ref[...], preferred_element_type=jnp.float32)
        x_prev_ref[...] = x_ref[...]


def _layer(x2, mem, g_in, w_in, conv_w, conv_b, g_memin, w_kv, g_sb, g_conv, g_mem, w_out, g_final):
    t, d = x2.shape
    assert t % Q_BLOCK == 0 and Q_BLOCK % SB_TILE == 0 and KV_PREV <= Q_BLOCK
    assert Q_BLOCK >= STREAM_ROWS
    assert d // STREAM_COLS + (3 * CONV_WIDTH + MIX_WIDTH) // STREAM_COLS >= STREAM_STAGES
    for a in (mem, w_kv, w_in, w_out):
        assert a.shape[1] % STREAM_ROWS == 0 and a.shape[2] % STREAM_COLS == 0
    n_chain = (Q_BLOCK // SB_TILE) * (SB_WIDTH // LANES)

    def whole(a):
        return pl.BlockSpec(a.shape, lambda s: (0,) * a.ndim)

    in_hbm = pl.BlockSpec(memory_space=pl.ANY)
    n_blk = t // Q_BLOCK
    return pl.pallas_call(
        _layer_kernel,
        grid=(n_blk + 2,),
        in_specs=[pl.BlockSpec((Q_BLOCK, d), lambda s: (jnp.minimum(s, n_blk - 1), 0)), in_hbm, whole(g_in),
                  in_hbm, whole(conv_w), whole(conv_b), whole(g_memin), in_hbm, whole(g_sb), whole(g_conv),
                  whole(g_mem), in_hbm, whole(g_final)],
        out_specs=[pl.BlockSpec((Q_BLOCK, d), lambda s: (jnp.clip(s - 2, 0, n_blk - 1), 0)), in_hbm],
        out_shape=[jax.ShapeDtypeStruct((t, d), jnp.float32),
                   jax.ShapeDtypeStruct((2, KV_PAD + t, SB_WIDTH), jnp.bfloat16)],
        scratch_shapes=[pltpu.VMEM(w_in.shape[1:], jnp.bfloat16),
                        pltpu.VMEM(w_out.shape[1:], jnp.bfloat16),
                        pltpu.VMEM((mem.shape[1], w_kv.shape[2]), jnp.bfloat16),
                        pltpu.SemaphoreType.DMA((STREAM_STAGES,)),
                        pltpu.VMEM((KV_PREV + Q_BLOCK, SB_WIDTH), jnp.bfloat16),
                        pltpu.VMEM((KV_PREV + Q_BLOCK, SB_WIDTH), jnp.bfloat16),
                        pltpu.VMEM((Q_BLOCK, 2 * SB_WIDTH), jnp.bfloat16),
                        pltpu.VMEM((n_chain, SB_TILE, LANES), jnp.bfloat16),
                        pltpu.VMEM((n_chain, SB_TILE, LANES), jnp.bfloat16),
                        pltpu.SemaphoreType.DMA((2,)),
                        pltpu.SemaphoreType.DMA((2, n_chain)),
                        pltpu.VMEM((2, Q_BLOCK, SB_WIDTH + MEM_WIDTH), jnp.bfloat16),
                        pltpu.VMEM((2, Q_BLOCK, 3 * CONV_WIDTH + MIX_WIDTH), jnp.float32),
                        pltpu.VMEM((Q_BLOCK, d), jnp.float32),
                        pltpu.VMEM((Q_BLOCK, d), jnp.float32),
                        pltpu.VMEM((SUBLANES + Q_BLOCK, CONV_WIDTH), jnp.float32),
                        pltpu.VMEM((n_chain, SB_TILE, LANES), jnp.float32),
                        pltpu.VMEM((n_chain, 2 * SB_TILE, 1), jnp.float32)],
        compiler_params=pltpu.CompilerParams(
            dimension_semantics=("arbitrary",), vmem_limit_bytes=VMEM_LIMIT_BYTES),
        name="layer",
    )(x2, mem, g_in, w_in, conv_w, conv_b, g_memin, w_kv, g_sb, g_conv, g_mem, w_out, g_final)[0]


def kernel(x, mem, g_in, w_in, conv_w, conv_b, g_mem, w_mem_kv, g_sb_out, g_conv_out,
           g_mem_out, w_out, g_final):
    batch, t, d = x.shape
    assert batch == 1 and g_in.shape[0] == 1
    out = _layer(x.reshape(t, d), mem, g_in[0][None, :], w_in, conv_w, conv_b[0][None, :],
                 g_mem[0][None, :], w_mem_kv, g_sb_out[0][None, :], g_conv_out[0][None, :],
                 g_mem_out[0][None, :], w_out, g_final[None, :])
    return out.reshape(batch, t, d)
```

```python
import jax
import jax.numpy as jnp
from jax import lax
from jax.experimental import pallas as pl
from jax.experimental.pallas import tpu as pltpu

HEAD_DIM = 64
SB_HEADS = 8
SB_WIDTH = SB_HEADS * HEAD_DIM
CONV_WIDTH = 4 * HEAD_DIM
MEM_HEADS = 4
MEM_WIDTH = MEM_HEADS * HEAD_DIM
CONV_K = 3
EPS = 1e-6
SCALE = HEAD_DIM ** -0.5

LANES = 128
SUBLANES = 8
MIX_WIDTH = SB_WIDTH + CONV_WIDTH + MEM_WIDTH
PROJ_COLS = 512
Q_BLOCK = 512
SB_TILE = 128
SWEEP_STOP = 104.0
TILE2_ROWS = 32
KV_PAD = Q_BLOCK
KV_PREV = 2 * SB_TILE
LOG2_E = 1.4426950408889634
LOGIT2_CLAMP = 126.0
STREAM_ROWS = 256
STREAM_COLS = 512
STREAM_STAGES = 9
VMEM_LIMIT_BYTES = 62 * 1024 * 1024


def _rms_scale(xf, eps=EPS):
    return lax.rsqrt(jnp.mean(xf * xf, axis=-1, keepdims=True) + eps)


def _dot_nt(a, b):
    return lax.dot_general(a, b, (((1,), (1,)), ((), ())), preferred_element_type=jnp.float32)


def _split_heads(a):
    first_half = lax.broadcasted_iota(jnp.int32, a.shape, 1) < HEAD_DIM
    zero = jnp.zeros_like(a)
    return jnp.concatenate([jnp.where(first_half, a, zero), jnp.where(first_half, zero, a)], axis=0)


def _sb_sweep(i, q_ref, kwin_ref, vwin_ref, kv_hbm, ktile_ref, vtile_ref, tile_sems, acc_ref, carry_ref, fillers,
              before_loop):
    n_sub = q_ref.shape[0] // SB_TILE
    n_pair = SB_WIDTH // LANES
    chains = [(sub, p) for sub in range(n_sub) for p in range(n_pair)]
    fillers = list(fillers)

    rt = lax.broadcasted_iota(jnp.int32, (2 * SB_TILE, 2 * SB_TILE), 0)
    ct = lax.broadcasted_iota(jnp.int32, (2 * SB_TILE, 2 * SB_TILE), 1)
    top, left = rt < SB_TILE, ct < SB_TILE
    tri2 = jnp.where(top, rt, rt - SB_TILE) >= jnp.where(left, ct, ct - SB_TILE)
    causal = jnp.logical_or(left, jnp.where(left, ct, ct - SB_TILE) < jnp.where(top, rt, rt - SB_TILE))

    def as_matrix(cond):
        return jnp.where(cond, 1.0, 0.0).astype(jnp.bfloat16)

    w_10 = as_matrix(jnp.logical_or(jnp.logical_and(jnp.logical_not(top), left),
                                    jnp.logical_and(tri2, jnp.logical_not(jnp.logical_xor(top, left)))))
    w_2 = as_matrix(jnp.logical_or(top, jnp.logical_or(tri2, jnp.logical_not(left))))
    w_1 = w_2[SB_TILE:]

    q_stack = [_split_heads(q_ref[sub * SB_TILE:(sub + 1) * SB_TILE, p * LANES:(p + 1) * LANES])
               for sub, p in chains]

    def window_rows(c, m, n_tiles=1):
        start = KV_PREV + (chains[c][0] - (m + n_tiles - 1)) * SB_TILE
        assert start >= 0
        return slice(start, start + n_tiles * SB_TILE)

    def head_rows(a, n):
        return jnp.concatenate([a[:n], a[SB_TILE:SB_TILE + n]], axis=0)

    def logits(q, keys):
        return jnp.minimum(_dot_nt(q, keys), LOGIT2_CLAMP)

    def window_keys(c, rows):
        p = chains[c][1]
        return kwin_ref[rows, p * LANES:(p + 1) * LANES]

    def window_values(c, rows):
        p = chains[c][1]
        return _split_heads(vwin_ref[rows, p * LANES:(p + 1) * LANES])

    def neg_log2_not_beta(z):
        return jnp.log(1.0 + jnp.exp2(z)) * LOG2_E

    def weights(z, s, mask=None):
        w = jnp.exp2(z - s)
        if mask is not None:
            w = jnp.where(mask, w, 0.0)
        wb = w.astype(jnp.bfloat16)
        n = wb.shape[0] // 2
        return jnp.concatenate([wb[:n], wb[n:]], axis=1)

    def all_done(carries):
        floor = carries[0]
        for carry in carries[1:]:
            floor = jnp.minimum(floor, carry)
        return jnp.min(floor) >= SWEEP_STOP * LOG2_E

    def first_sweep():
        n = TILE2_ROWS
        rows_10 = [window_rows(c, 0, 2) for c in range(len(chains))]
        rows_2 = [window_rows(c, 2) for c in range(len(chains))]
        zs = [(logits(q_stack[c], window_keys(c, rows_10[c])),
               logits(head_rows(q_stack[c], n), window_keys(c, rows_2[c]))) for c in range(len(chains))]
        if fillers:
            fillers.pop(0)()
        lhs_10, lhs_2 = [], []
        for z10, z2 in zs:
            l10 = jnp.where(causal, neg_log2_not_beta(z10), 0.0)
            lhs_10.append(l10.astype(jnp.bfloat16))
            lhs_2.append(jnp.concatenate(
                [head_rows(l10[:, :SB_TILE] + l10[:, SB_TILE:], n).astype(jnp.bfloat16),
                 neg_log2_not_beta(z2).astype(jnp.bfloat16)], axis=1))
        s_10 = jnp.dot(jnp.concatenate(lhs_10, axis=0), w_10, preferred_element_type=jnp.float32)
        s_2 = jnp.dot(jnp.concatenate(lhs_2, axis=0), w_2, preferred_element_type=jnp.float32)
        if fillers:
            fillers.pop(0)()
        reached, w_10s, w_2s = [], [], []
        for c, (z10, z2) in enumerate(zs):
            part = slice(c * 2 * SB_TILE, (c + 1) * 2 * SB_TILE)
            part2 = slice(c * 2 * n, (c + 1) * 2 * n)
            w_10s.append(weights(z10, s_10[part], causal))
            w_2s.append(weights(z2, s_2[part2, :SB_TILE]))
            after_1 = s_10[part, 0:1]
            after_2 = s_2[part2, SB_TILE:SB_TILE + 1]
            carry_ref[c] = after_1
            reached.append(jnp.concatenate([after_2[:n], after_1[n:SB_TILE],
                                            after_2[n:], after_1[SB_TILE + n:]], axis=0))
        done = all_done(reached)
        for c in range(len(chains)):
            acc = jnp.dot(w_10s[c], window_values(c, rows_10[c]), preferred_element_type=jnp.float32)
            tile_2 = jnp.dot(w_2s[c], window_values(c, rows_2[c]), preferred_element_type=jnp.float32)
            acc_ref[c, n:, :] = acc[n:]
            acc_ref[c, :n, :] = acc[:n] + jnp.where(done, tile_2, 0.0)
        return 2, done

    def tile_copies(m):
        copies = []
        for c, (sub, p) in enumerate(chains):
            row0 = pl.multiple_of(KV_PAD + (n_sub * i + sub - m) * SB_TILE, SB_TILE)
            for which, dst in ((0, ktile_ref), (1, vtile_ref)):
                copies.append(pltpu.make_async_copy(
                    kv_hbm.at[which, pl.ds(row0, SB_TILE), pl.ds(p * LANES, LANES)], dst.at[c],
                    tile_sems.at[which, c]))
        return copies

    def next_sweep(m):
        for copy in tile_copies(m):
            copy.start()
        for copy in tile_copies(m):
            copy.wait()
        zs = [logits(q_stack[c], ktile_ref[c]) for c in range(len(chains))]
        ls = [neg_log2_not_beta(z).astype(jnp.bfloat16) for z in zs]
        cs = jnp.dot(jnp.concatenate(ls, axis=0), w_1, preferred_element_type=jnp.float32)
        carries, w_cats = [], []
        for c, z in enumerate(zs):
            part = slice(c * 2 * SB_TILE, (c + 1) * 2 * SB_TILE)
            old = carry_ref[c]
            w_cats.append(weights(z, cs[part, :SB_TILE] + old))
            carries.append(cs[part, SB_TILE:SB_TILE + 1] + old)
            carry_ref[c] = carries[c]
        for c in range(len(chains)):
            acc_ref[c] += jnp.dot(w_cats[c], _split_heads(vtile_ref[c]), preferred_element_type=jnp.float32)
        return all_done(carries)

    m0, done0 = first_sweep()
    before_loop()
    last = n_sub * i + n_sub - 1

    def cond(state):
        m, done = state
        return jnp.logical_and(m <= last, jnp.logical_not(done))

    def body(state):
        m, _ = state
        return m + 1, next_sweep(m)

    lax.while_loop(cond, body, (jnp.int32(m0), done0))
    return jnp.concatenate(
        [jnp.concatenate([acc_ref[sub * n_pair + p] for p in range(n_pair)], axis=1)
         for sub in range(n_sub)], axis=0)


def _mem_attention(q_ref, mkv_ref, filler):
    q_rows = q_ref.shape[0]
    n_mem = mkv_ref.shape[0]
    n_pair = MEM_WIDTH // LANES
    head_ones = ((lax.broadcasted_iota(jnp.int32, (2 * n_mem, LANES), 1) < HEAD_DIM)
                 == (lax.broadcasted_iota(jnp.int32, (2 * n_mem, LANES), 0) < n_mem)
                 ).astype(jnp.float32).astype(jnp.bfloat16)
    logits = [_dot_nt(_split_heads(q_ref[:, SB_WIDTH + p * LANES:SB_WIDTH + (p + 1) * LANES]),
                      mkv_ref[:, p * LANES:(p + 1) * LANES]) for p in range(n_pair)]
    filler()
    outs = []
    for p, s in enumerate(logits):
        mv = mkv_ref[:, MEM_WIDTH + p * LANES:MEM_WIDTH + (p + 1) * LANES]
        e = jnp.exp(s - jnp.max(s, axis=-1, keepdims=True)).astype(jnp.bfloat16)
        e_cat = jnp.concatenate([e[:q_rows], e[q_rows:]], axis=1)
        num_den = jnp.dot(e_cat, jnp.concatenate([_split_heads(mv), head_ones], axis=1),
                          preferred_element_type=jnp.float32)
        outs.append(num_den[:, :LANES] / num_den[:, LANES:])
    return jnp.concatenate(outs, axis=-1)


def _stream_chunks(jobs, stages, sems):
    depth = len(stages)

    def copy(j):
        return pltpu.make_async_copy(jobs[j][0], stages[j % depth], sems.at[j % depth])

    for j in range(min(depth - 1, len(jobs))):
        copy(j).start()
    for j, (_, consume) in enumerate(jobs):
        if j + depth - 1 < len(jobs):
            copy(j + depth - 1).start()
        copy(j).wait()
        consume(stages[j % depth])


def _chunks(hbm):
    _, n_rows, n_cols = hbm.shape
    return [(r0, c0, hbm.at[0, pl.ds(r0, STREAM_ROWS), pl.ds(c0, STREAM_COLS)])
            for r0 in range(0, n_rows, STREAM_ROWS) for c0 in range(0, n_cols, STREAM_COLS)]


def _load_parameters(mem_hbm, g_mem_ref, w_kv_hbm, w_in_hbm, w_out_hbm, mkv_ref, w_in_ref, w_out_ref,
                     stages, sems):
    jobs = []
    state = {"mem": [], "h": None, "kv": None}

    def take_mem(window):
        state["mem"].append(window[...])

    def memory_kv(window, r0):
        if state["h"] is None:
            m = jnp.concatenate(state["mem"], axis=1)
            state["h"] = (m * _rms_scale(m) * g_mem_ref[...]).astype(jnp.bfloat16)
        part = jnp.dot(state["h"][:, r0:r0 + STREAM_ROWS], window[...].astype(jnp.bfloat16),
                       preferred_element_type=jnp.float32)
        state["kv"] = part if state["kv"] is None else state["kv"] + part
        if r0 + STREAM_ROWS == w_kv_hbm.shape[1]:
            mkv_ref[...] = state["kv"].astype(jnp.bfloat16)

    def cast_into(dst, r0, c0):
        def consume(window):
            dst[r0:r0 + STREAM_ROWS, c0:c0 + STREAM_COLS] = window[...].astype(jnp.bfloat16)
        return consume

    assert mem_hbm.shape[1] == STREAM_ROWS and w_kv_hbm.shape[2] == STREAM_COLS
    jobs += [(chunk, take_mem) for _, _, chunk in _chunks(mem_hbm)]
    jobs += [(chunk, lambda window, r0=r0: memory_kv(window, r0)) for r0, _, chunk in _chunks(w_kv_hbm)]
    for hbm, dst in ((w_in_hbm, w_in_ref), (w_out_hbm, w_out_ref)):
        jobs += [(chunk, cast_into(dst, r0, c0)) for r0, c0, chunk in _chunks(hbm)]
    _stream_chunks(jobs, stages, sems)


def _layer_kernel(x_ref, mem_hbm, g_in_ref, w_in_hbm, convw_ref, convb_ref, g_memin_ref, w_kv_hbm,
                  g_sb_ref, g_conv_ref, g_mem_ref, w_out_hbm, g_final_ref,
                  out_ref, kv_hbm, w_in_ref, w_out_ref, mkv_ref, stream_sems, kwin_ref, vwin_ref, kv_next_ref,
                  ktile_ref, vtile_ref, kv_sems, tile_sems, q2_ref, rest2_ref, x_prev_ref, res_ref, cu_ref,
                  acc_ref, carry_ref):
    s = pl.program_id(0)
    n_blk = pl.num_programs(0) - 2
    q_rows = x_ref.shape[0]
    c_u = 3 * SB_WIDTH
    c_qm = c_u + 3 * CONV_WIDTH
    c_gate = c_qm + MEM_WIDTH

    def normed_input():
        x = x_ref[...]
        return (x * _rms_scale(x) * g_in_ref[...]).astype(jnp.bfloat16)

    def proj(h, c0, width):
        return jnp.dot(h, w_in_ref[:, c0:c0 + width], preferred_element_type=jnp.float32)

    def project_kv(h):
        kv_next_ref[...] = proj(h, SB_WIDTH, 2 * SB_WIDTH).astype(jnp.bfloat16)

    def project_q_conv(h, slot):
        q2_ref[slot, :, :SB_WIDTH] = (proj(h, 0, SB_WIDTH) * (SCALE * LOG2_E)).astype(jnp.bfloat16)
        q2_ref[slot, :, SB_WIDTH:] = (proj(h, c_qm, MEM_WIDTH) * SCALE).astype(jnp.bfloat16)
        rest2_ref[slot, :, :3 * CONV_WIDTH] = proj(h, c_u, 3 * CONV_WIDTH)

    def project_gate(h, slot):
        for c0 in range(0, MIX_WIDTH, PROJ_COLS):
            rest2_ref[slot, :, 3 * CONV_WIDTH + c0:3 * CONV_WIDTH + c0 + PROJ_COLS] = proj(h, c_gate + c0, PROJ_COLS)

    def final_norm():
        res = res_ref[...]
        out_ref[...] = res * _rms_scale(res) * g_final_ref[...]

    @pl.when(s == 0)
    def _():
        stages = [ref.at[pl.ds(r0, STREAM_ROWS), pl.ds(c0, STREAM_COLS)] for ref in (x_prev_ref, rest2_ref.at[1])
                  for r0 in range(0, ref.shape[0] - STREAM_ROWS + 1, STREAM_ROWS)
                  for c0 in range(0, ref.shape[-1] - STREAM_COLS + 1, STREAM_COLS)]
        _load_parameters(mem_hbm, g_memin_ref, w_kv_hbm, w_in_hbm, w_out_hbm, mkv_ref, w_in_ref, w_out_ref,
                         stages[:STREAM_STAGES], stream_sems)
        h = normed_input()
        project_kv(h)
        project_q_conv(h, 0)
        project_gate(h, 0)
        x_prev_ref[...] = x_ref[...]
        res_ref[...] = jnp.zeros(res_ref.shape, jnp.float32)
        cu_ref[0:SUBLANES, :] = jnp.zeros((SUBLANES, CONV_WIDTH), jnp.float32)
        kwin_ref[...] = jnp.zeros(kwin_ref.shape, jnp.bfloat16)
        vwin_ref[...] = jnp.zeros(vwin_ref.shape, jnp.bfloat16)
        zero_rows = [pltpu.make_async_copy(win.at[pl.ds(0, KV_PAD)], kv_hbm.at[which, pl.ds(0, KV_PAD)],
                                           kv_sems.at[which])
                     for which, win in ((0, kwin_ref), (1, vwin_ref))]
        for copy in zero_rows:
            copy.start()
        for copy in zero_rows:
            copy.wait()

    @pl.when(s == n_blk + 1)
    def _():
        final_norm()

    @pl.when(jnp.logical_and(s > 0, s <= n_blk))
    def _():
        i = s - 1
        cur = lax.rem(i, 2)
        nxt = 1 - cur
        q_ref = q2_ref.at[cur]
        rest_ref = rest2_ref.at[cur]

        def history_writes(block):
            rows = pl.ds(pl.multiple_of(KV_PAD + block * q_rows, q_rows), q_rows)
            return [pltpu.make_async_copy(win.at[pl.ds(KV_PREV, q_rows)], kv_hbm.at[which, rows],
                                          kv_sems.at[which])
                    for which, win in ((0, kwin_ref), (1, vwin_ref))]

        for win, c0 in ((kwin_ref, 0), (vwin_ref, SB_WIDTH)):
            win[0:KV_PREV, :] = win[q_rows:q_rows + KV_PREV, :]
            win[KV_PREV:, :] = kv_next_ref[:, c0:c0 + SB_WIDTH]
        for copy in history_writes(i):
            copy.start()
        final_norm()

        h_next = normed_input()
        y_sb = _sb_sweep(i, q_ref, kwin_ref, vwin_ref, kv_hbm, ktile_ref, vtile_ref, tile_sems, acc_ref, carry_ref,
                         (lambda: project_kv(h_next), lambda: project_q_conv(h_next, nxt)),
                         lambda: [copy.wait() for copy in history_writes(i)])

        u = rest_ref[:, 0:CONV_WIDTH]
        b = rest_ref[:, CONV_WIDTH:2 * CONV_WIDTH]
        c = rest_ref[:, 2 * CONV_WIDTH:3 * CONV_WIDTH]
        cu = c * u
        cu_ref[SUBLANES:, :] = cu
        cu_1 = cu_ref[SUBLANES - 1:SUBLANES - 1 + q_rows, :]
        cu_2 = cu_ref[SUBLANES - 2:SUBLANES - 2 + q_rows, :]
        cu_ref[0:SUBLANES, :] = cu[q_rows - SUBLANES:, :]
        conv = (convw_ref[0] * cu_2 + convw_ref[1] * cu_1 + convw_ref[2] * cu
                + convb_ref[...])
        y_conv = b * conv

        y_mem = _mem_attention(q_ref, mkv_ref, lambda: project_gate(h_next, nxt))

        y = jnp.concatenate([y_sb * _rms_scale(y_sb) * g_sb_ref[...],
                             y_conv * _rms_scale(y_conv) * g_conv_ref[...],
                             y_mem * _rms_scale(y_mem) * g_mem_ref[...]], axis=-1)
        gate = rest_ref[:, 3 * CONV_WIDTH:]
        gated = (y * (gate / (1.0 + jnp.exp(-gate)))).astype(jnp.bfloat16)
        res_ref[...] = x_prev_ref[...] + jnp.dot(gated, w_out_ref[...], preferred_element_type=jnp.float32)
        x_prev_ref[...] = x_ref[...]


def _layer(x2, mem, g_in, w_in, conv_w, conv_b, g_memin, w_kv, g_sb, g_conv, g_mem, w_out, g_final):
    t, d = x2.shape
    assert t % Q_BLOCK == 0 and Q_BLOCK % SB_TILE == 0 and KV_PREV <= Q_BLOCK
    assert Q_BLOCK >= STREAM_ROWS
    assert (Q_BLOCK // STREAM_ROWS) * (d // STREAM_COLS + (3 * CONV_WIDTH + MIX_WIDTH) // STREAM_COLS) >= STREAM_STAGES
    for a in (mem, w_kv, w_in, w_out):
        assert a.shape[1] % STREAM_ROWS == 0 and a.shape[2] % STREAM_COLS == 0
    n_chain = (Q_BLOCK // SB_TILE) * (SB_WIDTH // LANES)

    def whole(a):
        return pl.BlockSpec(a.shape, lambda s: (0,) * a.ndim)

    in_hbm = pl.BlockSpec(memory_space=pl.ANY)
    n_blk = t // Q_BLOCK
    return pl.pallas_call(
        _layer_kernel,
        grid=(n_blk + 2,),
        in_specs=[pl.BlockSpec((Q_BLOCK, d), lambda s: (jnp.minimum(s, n_blk - 1), 0)), in_hbm, whole(g_in),
                  in_hbm, whole(conv_w), whole(conv_b), whole(g_memin), in_hbm, whole(g_sb), whole(g_conv),
                  whole(g_mem), in_hbm, whole(g_final)],
        out_specs=[pl.BlockSpec((Q_BLOCK, d), lambda s: (jnp.clip(s - 2, 0, n_blk - 1), 0)), in_hbm],
        out_shape=[jax.ShapeDtypeStruct((t, d), jnp.float32),
                   jax.ShapeDtypeStruct((2, KV_PAD + t, SB_WIDTH), jnp.bfloat16)],
        scratch_shapes=[pltpu.VMEM(w_in.shape[1:], jnp.bfloat16),
                        pltpu.VMEM(w_out.shape[1:], jnp.bfloat16),
                        pltpu.VMEM((mem.shape[1], w_kv.shape[2]), jnp.bfloat16),
                        pltpu.SemaphoreType.DMA((STREAM_STAGES,)),
                        pltpu.VMEM((KV_PREV + Q_BLOCK, SB_WIDTH), jnp.bfloat16),
                        pltpu.VMEM((KV_PREV + Q_BLOCK, SB_WIDTH), jnp.bfloat16),
                        pltpu.VMEM((Q_BLOCK, 2 * SB_WIDTH), jnp.bfloat16),
                        pltpu.VMEM((n_chain, SB_TILE, LANES), jnp.bfloat16),
                        pltpu.VMEM((n_chain, SB_TILE, LANES), jnp.bfloat16),
                        pltpu.SemaphoreType.DMA((2,)),
                        pltpu.SemaphoreType.DMA((2, n_chain)),
                        pltpu.VMEM((2, Q_BLOCK, SB_WIDTH + MEM_WIDTH), jnp.bfloat16),
                        pltpu.VMEM((2, Q_BLOCK, 3 * CONV_WIDTH + MIX_WIDTH), jnp.float32),
                        pltpu.VMEM((Q_BLOCK, d), jnp.float32),
                        pltpu.VMEM((Q_BLOCK, d), jnp.float32),
                        pltpu.VMEM((SUBLANES + Q_BLOCK, CONV_WIDTH), jnp.float32),
                        pltpu.VMEM((n_chain, SB_TILE, LANES), jnp.float32),
                        pltpu.VMEM((n_chain, 2 * SB_TILE, 1), jnp.float32)],
        compiler_params=pltpu.CompilerParams(
            dimension_semantics=("arbitrary",), vmem_limit_bytes=VMEM_LIMIT_BYTES),
        name="layer",
    )(x2, mem, g_in, w_in, conv_w, conv_b, g_memin, w_kv, g_sb, g_conv, g_mem, w_out, g_final)[0]


def kernel(x, mem, g_in, w_in, conv_w, conv_b, g_mem, w_mem_kv, g_sb_out, g_conv_out,
           g_mem_out, w_out, g_final):
    batch, t, d = x.shape
    assert batch == 1 and g_in.shape[0] == 1
    out = _layer(x.reshape(t, d), mem, g_in[0][None, :], w_in, jnp.transpose(conv_w, (1, 0, 2)), conv_b[0][None, :],
                 g_mem[0][None, :], w_mem_kv, g_sb_out[0][None, :], g_conv_out[0][None, :],
                 g_mem_out[0][None, :], w_out, g_final[None, :])
    return out.reshape(batch, t, d)
```

```python
import jax
import jax.numpy as jnp
from jax import lax
from jax.experimental import pallas as pl
from jax.experimental.pallas import tpu as pltpu

HEAD_DIM = 64
SB_HEADS = 8
SB_WIDTH = SB_HEADS * HEAD_DIM
CONV_WIDTH = 4 * HEAD_DIM
MEM_HEADS = 4
MEM_WIDTH = MEM_HEADS * HEAD_DIM
CONV_K = 3
EPS = 1e-6
SCALE = HEAD_DIM ** -0.5

LANES = 128
SUBLANES = 8
MIX_WIDTH = SB_WIDTH + CONV_WIDTH + MEM_WIDTH
PROJ_COLS = 512
Q_BLOCK = 512
SB_TILE = 128
SWEEP_STOP = 104.0
TILE2_ROWS = 32
KV_PAD = Q_BLOCK
KV_PREV = 2 * SB_TILE
LOG2_E = 1.4426950408889634
LOGIT2_CLAMP = 126.0
STREAM_ROWS = 256
STREAM_COLS = 512
STREAM_STAGES = 9
VMEM_LIMIT_BYTES = 62 * 1024 * 1024


def _rms_scale(xf, eps=EPS):
    return lax.rsqrt(jnp.mean(xf * xf, axis=-1, keepdims=True) + eps)


def _dot_nt(a, b):
    return lax.dot_general(a, b, (((1,), (1,)), ((), ())), preferred_element_type=jnp.float32)


def _split_heads(a):
    first_half = lax.broadcasted_iota(jnp.int32, a.shape, 1) < HEAD_DIM
    zero = jnp.zeros_like(a)
    return jnp.concatenate([jnp.where(first_half, a, zero), jnp.where(first_half, zero, a)], axis=0)


def _sb_sweep(i, q_ref, kwin_ref, vwin_ref, kv_hbm, ktile_ref, vtile_ref, tile_sems, acc_ref, carry_ref, fillers,
              before_loop):
    n_sub = q_ref.shape[0] // SB_TILE
    n_pair = SB_WIDTH // LANES
    chains = [(sub, p) for sub in range(n_sub) for p in range(n_pair)]
    fillers = list(fillers)

    rt = lax.broadcasted_iota(jnp.int32, (2 * SB_TILE, 2 * SB_TILE), 0)
    ct = lax.broadcasted_iota(jnp.int32, (2 * SB_TILE, 2 * SB_TILE), 1)
    top, left = rt < SB_TILE, ct < SB_TILE
    tri2 = jnp.where(top, rt, rt - SB_TILE) >= jnp.where(left, ct, ct - SB_TILE)
    r2 = lax.broadcasted_iota(jnp.int32, (2 * SB_TILE, SB_TILE), 0)
    c2 = lax.broadcasted_iota(jnp.int32, (2 * SB_TILE, SB_TILE), 1)
    causal = c2 < jnp.where(r2 < SB_TILE, r2, r2 - SB_TILE)

    def mask_diagonal(a):
        return jnp.concatenate([a[:, :SB_TILE], jnp.where(causal, a[:, SB_TILE:], 0.0)], axis=1)

    def as_matrix(cond):
        return jnp.where(cond, 1.0, 0.0).astype(jnp.bfloat16)

    w_10 = as_matrix(jnp.logical_or(jnp.logical_and(jnp.logical_not(top), left),
                                    jnp.logical_and(tri2, jnp.logical_not(jnp.logical_xor(top, left)))))
    w_2 = as_matrix(jnp.logical_or(top, jnp.logical_or(tri2, jnp.logical_not(left))))
    w_1 = w_2[SB_TILE:]

    q_stack = [_split_heads(q_ref[sub * SB_TILE:(sub + 1) * SB_TILE, p * LANES:(p + 1) * LANES])
               for sub, p in chains]

    def window_rows(c, m, n_tiles=1):
        start = KV_PREV + (chains[c][0] - (m + n_tiles - 1)) * SB_TILE
        assert start >= 0
        return slice(start, start + n_tiles * SB_TILE)

    def head_rows(a, n):
        return jnp.concatenate([a[:n], a[SB_TILE:SB_TILE + n]], axis=0)

    def logits(q, keys):
        return jnp.minimum(_dot_nt(q, keys), LOGIT2_CLAMP)

    def window_keys(c, rows):
        p = chains[c][1]
        return kwin_ref[rows, p * LANES:(p + 1) * LANES]

    def window_values(c, rows):
        p = chains[c][1]
        return _split_heads(vwin_ref[rows, p * LANES:(p + 1) * LANES])

    def neg_log2_not_beta(z):
        return jnp.log(1.0 + jnp.exp2(z)) * LOG2_E

    def weights(z, s, diagonal=False):
        w = jnp.exp2(z - s)
        if diagonal:
            w = mask_diagonal(w)
        wb = w.astype(jnp.bfloat16)
        n = wb.shape[0] // 2
        return jnp.concatenate([wb[:n], wb[n:]], axis=1)

    def all_done(carries):
        floor = carries[0]
        for carry in carries[1:]:
            floor = jnp.minimum(floor, carry)
        return jnp.min(floor) >= SWEEP_STOP * LOG2_E

    def first_sweep():
        n = TILE2_ROWS
        rows_10 = [window_rows(c, 0, 2) for c in range(len(chains))]
        rows_2 = [window_rows(c, 2) for c in range(len(chains))]
        zs = [(logits(q_stack[c], window_keys(c, rows_10[c])),
               logits(head_rows(q_stack[c], n), window_keys(c, rows_2[c]))) for c in range(len(chains))]
        if fillers:
            fillers.pop(0)()
        lhs_10, lhs_2 = [], []
        for z10, z2 in zs:
            l10 = mask_diagonal(neg_log2_not_beta(z10))
            lhs_10.append(l10.astype(jnp.bfloat16))
            lhs_2.append(jnp.concatenate(
                [(head_rows(l10[:, :SB_TILE], n) + head_rows(l10[:, SB_TILE:], n)).astype(jnp.bfloat16),
                 neg_log2_not_beta(z2).astype(jnp.bfloat16)], axis=1))
        s_10 = jnp.dot(jnp.concatenate(lhs_10, axis=0), w_10, preferred_element_type=jnp.float32)
        s_2 = jnp.dot(jnp.concatenate(lhs_2, axis=0), w_2, preferred_element_type=jnp.float32)
        if fillers:
            fillers.pop(0)()
        reached, w_10s, w_2s = [], [], []
        for c, (z10, z2) in enumerate(zs):
            part = slice(c * 2 * SB_TILE, (c + 1) * 2 * SB_TILE)
            part2 = slice(c * 2 * n, (c + 1) * 2 * n)
            w_10s.append(weights(z10, s_10[part], diagonal=True))
            w_2s.append(weights(z2, s_2[part2, :SB_TILE]))
            after_1 = s_10[part, 0:1]
            after_2 = s_2[part2, SB_TILE:SB_TILE + 1]
            carry_ref[c] = after_1
            reached.append(jnp.concatenate([after_2[:n], after_1[n:SB_TILE],
                                            after_2[n:], after_1[SB_TILE + n:]], axis=0))
        done = all_done(reached)
        for c in range(len(chains)):
            acc = jnp.dot(w_10s[c], window_values(c, rows_10[c]), preferred_element_type=jnp.float32)
            tile_2 = jnp.dot(w_2s[c], window_values(c, rows_2[c]), preferred_element_type=jnp.float32)
            acc_ref[c, n:, :] = acc[n:]
            acc_ref[c, :n, :] = acc[:n] + jnp.where(done, tile_2, 0.0)
        return 2, done

    def tile_copies(m):
        copies = []
        for c, (sub, p) in enumerate(chains):
            row0 = pl.multiple_of(KV_PAD + (n_sub * i + sub - m) * SB_TILE, SB_TILE)
            for which, dst in ((0, ktile_ref), (1, vtile_ref)):
                copies.append(pltpu.make_async_copy(
                    kv_hbm.at[which, pl.ds(row0, SB_TILE), pl.ds(p * LANES, LANES)], dst.at[c],
                    tile_sems.at[which, c]))
        return copies

    def next_sweep(m):
        for copy in tile_copies(m):
            copy.start()
        for copy in tile_copies(m):
            copy.wait()
        zs = [logits(q_stack[c], ktile_ref[c]) for c in range(len(chains))]
        ls = [neg_log2_not_beta(z).astype(jnp.bfloat16) for z in zs]
        cs = jnp.dot(jnp.concatenate(ls, axis=0), w_1, preferred_element_type=jnp.float32)
        carries, w_cats = [], []
        for c, z in enumerate(zs):
            part = slice(c * 2 * SB_TILE, (c + 1) * 2 * SB_TILE)
            old = carry_ref[c]
            w_cats.append(weights(z, cs[part, :SB_TILE] + old))
            carries.append(cs[part, SB_TILE:SB_TILE + 1] + old)
            carry_ref[c] = carries[c]
        for c in range(len(chains)):
            acc_ref[c] += jnp.dot(w_cats[c], _split_heads(vtile_ref[c]), preferred_element_type=jnp.float32)
        return all_done(carries)

    m0, done0 = first_sweep()
    before_loop()
    last = n_sub * i + n_sub - 1

    def cond(state):
        m, done = state
        return jnp.logical_and(m <= last, jnp.logical_not(done))

    def body(state):
        m, _ = state
        return m + 1, next_sweep(m)

    lax.while_loop(cond, body, (jnp.int32(m0), done0))
    return jnp.concatenate(
        [jnp.concatenate([acc_ref[sub * n_pair + p] for p in range(n_pair)], axis=1)
         for sub in range(n_sub)], axis=0)


def _mem_attention(q_ref, mkv_ref, filler):
    q_rows = q_ref.shape[0]
    n_mem = mkv_ref.shape[0]
    n_pair = MEM_WIDTH // LANES
    head_ones = ((lax.broadcasted_iota(jnp.int32, (2 * n_mem, LANES), 1) < HEAD_DIM)
                 == (lax.broadcasted_iota(jnp.int32, (2 * n_mem, LANES), 0) < n_mem)
                 ).astype(jnp.float32).astype(jnp.bfloat16)
    logits = [_dot_nt(_split_heads(q_ref[:, SB_WIDTH + p * LANES:SB_WIDTH + (p + 1) * LANES]),
                      mkv_ref[:, p * LANES:(p + 1) * LANES]) for p in range(n_pair)]
    filler()
    outs = []
    for p, s in enumerate(logits):
        mv = mkv_ref[:, MEM_WIDTH + p * LANES:MEM_WIDTH + (p + 1) * LANES]
        e = jnp.exp(s - jnp.max(s, axis=-1, keepdims=True)).astype(jnp.bfloat16)
        e_cat = jnp.concatenate([e[:q_rows], e[q_rows:]], axis=1)
        num_den = jnp.dot(e_cat, jnp.concatenate([_split_heads(mv), head_ones], axis=1),
                          preferred_element_type=jnp.float32)
        outs.append(num_den[:, :LANES] / num_den[:, LANES:])
    return jnp.concatenate(outs, axis=-1)


def _stream_chunks(jobs, stages, sems):
    depth = len(stages)

    def copy(j):
        return pltpu.make_async_copy(jobs[j][0], stages[j % depth], sems.at[j % depth])

    for j in range(min(depth - 1, len(jobs))):
        copy(j).start()
    for j, (_, consume) in enumerate(jobs):
        if j + depth - 1 < len(jobs):
            copy(j + depth - 1).start()
        copy(j).wait()
        consume(stages[j % depth])


def _chunks(hbm):
    _, n_rows, n_cols = hbm.shape
    return [(r0, c0, hbm.at[0, pl.ds(r0, STREAM_ROWS), pl.ds(c0, STREAM_COLS)])
            for r0 in range(0, n_rows, STREAM_ROWS) for c0 in range(0, n_cols, STREAM_COLS)]


def _load_parameters(mem_hbm, g_mem_ref, w_kv_hbm, w_in_hbm, w_out_hbm, mkv_ref, w_in_ref, w_out_ref,
                     stages, sems):
    jobs = []
    state = {"mem": [], "h": None, "kv": None}

    def take_mem(window):
        state["mem"].append(window[...])

    def memory_kv(window, r0):
        if state["h"] is None:
            m = jnp.concatenate(state["mem"], axis=1)
            state["h"] = (m * _rms_scale(m) * g_mem_ref[...]).astype(jnp.bfloat16)
        part = jnp.dot(state["h"][:, r0:r0 + STREAM_ROWS], window[...].astype(jnp.bfloat16),
                       preferred_element_type=jnp.float32)
        state["kv"] = part if state["kv"] is None else state["kv"] + part
        if r0 + STREAM_ROWS == w_kv_hbm.shape[1]:
            mkv_ref[...] = state["kv"].astype(jnp.bfloat16)

    def cast_into(dst, r0, c0):
        def consume(window):
            dst[r0:r0 + STREAM_ROWS, c0:c0 + STREAM_COLS] = window[...].astype(jnp.bfloat16)
        return consume

    assert mem_hbm.shape[1] == STREAM_ROWS and w_kv_hbm.shape[2] == STREAM_COLS
    jobs += [(chunk, take_mem) for _, _, chunk in _chunks(mem_hbm)]
    jobs += [(chunk, lambda window, r0=r0: memory_kv(window, r0)) for r0, _, chunk in _chunks(w_kv_hbm)]
    for hbm, dst in ((w_in_hbm, w_in_ref), (w_out_hbm, w_out_ref)):
        jobs += [(chunk, cast_into(dst, r0, c0)) for r0, c0, chunk in _chunks(hbm)]
    _stream_chunks(jobs, stages, sems)


def _layer_kernel(x_ref, mem_hbm, g_in_ref, w_in_hbm, convw_ref, convb_ref, g_memin_ref, w_kv_hbm,
                  g_sb_ref, g_conv_ref, g_mem_ref, w_out_hbm, g_final_ref,
                  out_ref, kv_hbm, w_in_ref, w_out_ref, mkv_ref, stream_sems, kwin_ref, vwin_ref, kv_next_ref,
                  ktile_ref, vtile_ref, kv_sems, tile_sems, q2_ref, rest2_ref, x_prev_ref, res_ref, cu_ref,
                  acc_ref, carry_ref):
    s = pl.program_id(0)
    n_blk = pl.num_programs(0) - 2
    q_rows = x_ref.shape[0]
    c_u = 3 * SB_WIDTH
    c_qm = c_u + 3 * CONV_WIDTH
    c_gate = c_qm + MEM_WIDTH

    def normed_input():
        x = x_ref[...]
        return (x * _rms_scale(x) * g_in_ref[...]).astype(jnp.bfloat16)

    def proj(h, c0, width):
        return jnp.dot(h, w_in_ref[:, c0:c0 + width], preferred_element_type=jnp.float32)

    def project_kv(h):
        kv_next_ref[...] = proj(h, SB_WIDTH, 2 * SB_WIDTH).astype(jnp.bfloat16)

    def project_q_conv(h, slot):
        q2_ref[slot, :, :SB_WIDTH] = (proj(h, 0, SB_WIDTH) * (SCALE * LOG2_E)).astype(jnp.bfloat16)
        q2_ref[slot, :, SB_WIDTH:] = (proj(h, c_qm, MEM_WIDTH) * SCALE).astype(jnp.bfloat16)
        rest2_ref[slot, :, :3 * CONV_WIDTH] = proj(h, c_u, 3 * CONV_WIDTH)

    def project_gate(h, slot):
        for c0 in range(0, MIX_WIDTH, PROJ_COLS):
            rest2_ref[slot, :, 3 * CONV_WIDTH + c0:3 * CONV_WIDTH + c0 + PROJ_COLS] = proj(h, c_gate + c0, PROJ_COLS)

    def final_norm():
        res = res_ref[...]
        out_ref[...] = res * _rms_scale(res) * g_final_ref[...]

    @pl.when(s == 0)
    def _():
        stages = [ref.at[pl.ds(r0, STREAM_ROWS), pl.ds(c0, STREAM_COLS)] for ref in (x_prev_ref, rest2_ref.at[1])
                  for r0 in range(0, ref.shape[0] - STREAM_ROWS + 1, STREAM_ROWS)
                  for c0 in range(0, ref.shape[-1] - STREAM_COLS + 1, STREAM_COLS)]
        _load_parameters(mem_hbm, g_memin_ref, w_kv_hbm, w_in_hbm, w_out_hbm, mkv_ref, w_in_ref, w_out_ref,
                         stages[:STREAM_STAGES], stream_sems)
        h = normed_input()
        project_kv(h)
        project_q_conv(h, 0)
        project_gate(h, 0)
        x_prev_ref[...] = x_ref[...]
        res_ref[...] = jnp.zeros(res_ref.shape, jnp.float32)
        cu_ref[0:SUBLANES, :] = jnp.zeros((SUBLANES, CONV_WIDTH), jnp.float32)
        kwin_ref[...] = jnp.zeros(kwin_ref.shape, jnp.bfloat16)
        vwin_ref[...] = jnp.zeros(vwin_ref.shape, jnp.bfloat16)
        zero_rows = [pltpu.make_async_copy(win.at[pl.ds(0, KV_PAD)], kv_hbm.at[which, pl.ds(0, KV_PAD)],
                                           kv_sems.at[which])
                     for which, win in ((0, kwin_ref), (1, vwin_ref))]
        for copy in zero_rows:
            copy.start()
        for copy in zero_rows:
            copy.wait()

    @pl.when(s == n_blk + 1)
    def _():
        final_norm()

    @pl.when(jnp.logical_and(s > 0, s <= n_blk))
    def _():
        i = s - 1
        cur = lax.rem(i, 2)
        nxt = 1 - cur
        q_ref = q2_ref.at[cur]
        rest_ref = rest2_ref.at[cur]

        def history_writes(block):
            rows = pl.ds(pl.multiple_of(KV_PAD + block * q_rows, q_rows), q_rows)
            return [pltpu.make_async_copy(win.at[pl.ds(KV_PREV, q_rows)], kv_hbm.at[which, rows],
                                          kv_sems.at[which])
                    for which, win in ((0, kwin_ref), (1, vwin_ref))]

        for win, c0 in ((kwin_ref, 0), (vwin_ref, SB_WIDTH)):
            win[0:KV_PREV, :] = win[q_rows:q_rows + KV_PREV, :]
            win[KV_PREV:, :] = kv_next_ref[:, c0:c0 + SB_WIDTH]
        for copy in history_writes(i):
            copy.start()
        final_norm()

        h_next = normed_input()
        y_sb = _sb_sweep(i, q_ref, kwin_ref, vwin_ref, kv_hbm, ktile_ref, vtile_ref, tile_sems, acc_ref, carry_ref,
                         (lambda: project_kv(h_next), lambda: project_q_conv(h_next, nxt)),
                         lambda: [copy.wait() for copy in history_writes(i)])

        u = rest_ref[:, 0:CONV_WIDTH]
        b = rest_ref[:, CONV_WIDTH:2 * CONV_WIDTH]
        c = rest_ref[:, 2 * CONV_WIDTH:3 * CONV_WIDTH]
        cu = c * u
        cu_ref[SUBLANES:, :] = cu
        cu_1 = cu_ref[SUBLANES - 1:SUBLANES - 1 + q_rows, :]
        cu_2 = cu_ref[SUBLANES - 2:SUBLANES - 2 + q_rows, :]
        cu_ref[0:SUBLANES, :] = cu[q_rows - SUBLANES:, :]
        conv = (convw_ref[0] * cu_2 + convw_ref[1] * cu_1 + convw_ref[2] * cu
                + convb_ref[...])
        y_conv = b * conv

        y_mem = _mem_attention(q_ref, mkv_ref, lambda: project_gate(h_next, nxt))

        y = jnp.concatenate([y_sb * _rms_scale(y_sb) * g_sb_ref[...],
                             y_conv * _rms_scale(y_conv) * g_conv_ref[...],
                             y_mem * _rms_scale(y_mem) * g_mem_ref[...]], axis=-1)
        gate = rest_ref[:, 3 * CONV_WIDTH:]
        gated = (y * (gate / (1.0 + jnp.exp(-gate)))).astype(jnp.bfloat16)
        res_ref[...] = x_prev_ref[...] + jnp.dot(gated, w_out_ref[...], preferred_element_type=jnp.float32)
        x_prev_ref[...] = x_ref[...]


def _layer(x2, mem, g_in, w_in, conv_w, conv_b, g_memin, w_kv, g_sb, g_conv, g_mem, w_out, g_final):
    t, d = x2.shape
    assert t % Q_BLOCK == 0 and Q_BLOCK % SB_TILE == 0 and KV_PREV <= Q_BLOCK
    assert Q_BLOCK >= STREAM_ROWS
    assert (Q_BLOCK // STREAM_ROWS) * (d // STREAM_COLS + (3 * CONV_WIDTH + MIX_WIDTH) // STREAM_COLS) >= STREAM_STAGES
    for a in (mem, w_kv, w_in, w_out):
        assert a.shape[1] % STREAM_ROWS == 0 and a.shape[2] % STREAM_COLS == 0
    n_chain = (Q_BLOCK // SB_TILE) * (SB_WIDTH // LANES)

    def whole(a):
        return pl.BlockSpec(a.shape, lambda s: (0,) * a.ndim)

    in_hbm = pl.BlockSpec(memory_space=pl.ANY)
    n_blk = t // Q_BLOCK
    return pl.pallas_call(
        _layer_kernel,
        grid=(n_blk + 2,),
        in_specs=[pl.BlockSpec((Q_BLOCK, d), lambda s: (jnp.minimum(s, n_blk - 1), 0)), in_hbm, whole(g_in),
                  in_hbm, whole(conv_w), whole(conv_b), whole(g_memin), in_hbm, whole(g_sb), whole(g_conv),
                  whole(g_mem), in_hbm, whole(g_final)],
        out_specs=[pl.BlockSpec((Q_BLOCK, d), lambda s: (jnp.clip(s - 2, 0, n_blk - 1), 0)), in_hbm],
        out_shape=[jax.ShapeDtypeStruct((t, d), jnp.float32),
                   jax.ShapeDtypeStruct((2, KV_PAD + t, SB_WIDTH), jnp.bfloat16)],
        scratch_shapes=[pltpu.VMEM(w_in.shape[1:], jnp.bfloat16),
                        pltpu.VMEM(w_out.shape[1:], jnp.bfloat16),
                        pltpu.VMEM((mem.shape[1], w_kv.shape[2]), jnp.bfloat16),
                        pltpu.SemaphoreType.DMA((STREAM_STAGES,)),
                        pltpu.VMEM((KV_PREV + Q_BLOCK, SB_WIDTH), jnp.bfloat16),
                        pltpu.VMEM((KV_PREV + Q_BLOCK, SB_WIDTH), jnp.bfloat16),
                        pltpu.VMEM((Q_BLOCK, 2 * SB_WIDTH), jnp.bfloat16),
                        pltpu.VMEM((n_chain, SB_TILE, LANES), jnp.bfloat16),
                        pltpu.VMEM((n_chain, SB_TILE, LANES), jnp.bfloat16),
                        pltpu.SemaphoreType.DMA((2,)),
                        pltpu.SemaphoreType.DMA((2, n_chain)),
                        pltpu.VMEM((2, Q_BLOCK, SB_WIDTH + MEM_WIDTH), jnp.bfloat16),
                        pltpu.VMEM((2, Q_BLOCK, 3 * CONV_WIDTH + MIX_WIDTH), jnp.float32),
                        pltpu.VMEM((Q_BLOCK, d), jnp.float32),
                        pltpu.VMEM((Q_BLOCK, d), jnp.float32),
                        pltpu.VMEM((SUBLANES + Q_BLOCK, CONV_WIDTH), jnp.float32),
                        pltpu.VMEM((n_chain, SB_TILE, LANES), jnp.float32),
                        pltpu.VMEM((n_chain, 2 * SB_TILE, 1), jnp.float32)],
        compiler_params=pltpu.CompilerParams(
            dimension_semantics=("arbitrary",), vmem_limit_bytes=VMEM_LIMIT_BYTES),
        name="layer",
    )(x2, mem, g_in, w_in, conv_w, conv_b, g_memin, w_kv, g_sb, g_conv, g_mem, w_out, g_final)[0]


def kernel(x, mem, g_in, w_in, conv_w, conv_b, g_mem, w_mem_kv, g_sb_out, g_conv_out,
           g_mem_out, w_out, g_final):
    batch, t, d = x.shape
    assert batch == 1 and g_in.shape[0] == 1
    out = _layer(x.reshape(t, d), mem, g_in[0][None, :], w_in, jnp.transpose(conv_w, (1, 0, 2)), conv_b[0][None, :],
                 g_mem[0][None, :], w_mem_kv, g_sb_out[0][None, :], g_conv_out[0][None, :],
                 g_mem_out[0][None, :], w_out, g_final[None, :])
    return out.reshape(batch, t, d)
```

```python
import jax
import jax.numpy as jnp
from jax import lax
from jax.experimental import pallas as pl
from jax.experimental.pallas import tpu as pltpu

HEAD_DIM = 64
SB_HEADS = 8
SB_WIDTH = SB_HEADS * HEAD_DIM
CONV_WIDTH = 4 * HEAD_DIM
MEM_HEADS = 4
MEM_WIDTH = MEM_HEADS * HEAD_DIM
CONV_K = 3
EPS = 1e-6
SCALE = HEAD_DIM ** -0.5

LANES = 128
SUBLANES = 8
MIX_WIDTH = SB_WIDTH + CONV_WIDTH + MEM_WIDTH
PROJ_COLS = 512
Q_BLOCK = 512
SB_TILE = 128
SWEEP_STOP = 104.0
TILE2_ROWS = 32
KV_PAD = Q_BLOCK
KV_PREV = 2 * SB_TILE
LOG2_E = 1.4426950408889634
LOGIT2_CLAMP = 126.0
STREAM_ROWS = 256
STREAM_COLS = 512
STREAM_STAGES = 9
VMEM_LIMIT_BYTES = 62 * 1024 * 1024


def _rms_scale(xf, eps=EPS):
    return lax.rsqrt(jnp.mean(xf * xf, axis=-1, keepdims=True) + eps)


def _dot_nt(a, b):
    return lax.dot_general(a, b, (((1,), (1,)), ((), ())), preferred_element_type=jnp.float32)


def _split_heads(a):
    first_half = lax.broadcasted_iota(jnp.int32, a.shape, 1) < HEAD_DIM
    zero = jnp.zeros_like(a)
    return jnp.concatenate([jnp.where(first_half, a, zero), jnp.where(first_half, zero, a)], axis=0)


def _sb_sweep(i, q_ref, kv_ref, prev_ref, kv_hbm, ktile_ref, vtile_ref, tile_sems, acc_ref, carry_ref, fillers,
              before_loop):
    n_sub = q_ref.shape[0] // SB_TILE
    n_pair = SB_WIDTH // LANES
    chains = [(sub, p) for sub in range(n_sub) for p in range(n_pair)]
    fillers = list(fillers)

    rt = lax.broadcasted_iota(jnp.int32, (2 * SB_TILE, 2 * SB_TILE), 0)
    ct = lax.broadcasted_iota(jnp.int32, (2 * SB_TILE, 2 * SB_TILE), 1)
    top, left = rt < SB_TILE, ct < SB_TILE
    tri2 = jnp.where(top, rt, rt - SB_TILE) >= jnp.where(left, ct, ct - SB_TILE)
    r2 = lax.broadcasted_iota(jnp.int32, (2 * SB_TILE, SB_TILE), 0)
    c2 = lax.broadcasted_iota(jnp.int32, (2 * SB_TILE, SB_TILE), 1)
    causal = c2 < jnp.where(r2 < SB_TILE, r2, r2 - SB_TILE)

    def mask_diagonal(a):
        return jnp.concatenate([a[:, :SB_TILE], jnp.where(causal, a[:, SB_TILE:], 0.0)], axis=1)

    def as_matrix(cond):
        return jnp.where(cond, 1.0, 0.0).astype(jnp.bfloat16)

    w_10 = as_matrix(jnp.logical_or(jnp.logical_and(jnp.logical_not(top), left),
                                    jnp.logical_and(tri2, jnp.logical_not(jnp.logical_xor(top, left)))))
    w_2 = as_matrix(jnp.logical_or(top, jnp.logical_or(tri2, jnp.logical_not(left))))
    w_1 = w_2[SB_TILE:]

    q_stack = [_split_heads(q_ref[sub * SB_TILE:(sub + 1) * SB_TILE, p * LANES:(p + 1) * LANES])
               for sub, p in chains]

    def window(c, m, n_tiles, lane0):
        sub, p = chains[c]
        pieces = []
        for t in range(m + n_tiles - 1, m - 1, -1):
            start = (sub - t) * SB_TILE
            lanes = slice(lane0 + p * LANES, lane0 + (p + 1) * LANES)
            if start >= 0:
                pieces.append(kv_ref[start:start + SB_TILE, lanes])
            else:
                pieces.append(prev_ref[KV_PREV + start:KV_PREV + start + SB_TILE, lanes])
        return pieces[0] if len(pieces) == 1 else jnp.concatenate(pieces, axis=0)

    def head_rows(a, n):
        return jnp.concatenate([a[:n], a[SB_TILE:SB_TILE + n]], axis=0)

    def logits(q, keys):
        return jnp.minimum(_dot_nt(q, keys), LOGIT2_CLAMP)

    def window_keys(c, m, n_tiles=1):
        return window(c, m, n_tiles, 0)

    def window_values(c, m, n_tiles=1):
        return _split_heads(window(c, m, n_tiles, SB_WIDTH))

    def neg_log2_not_beta(z):
        return jnp.log(1.0 + jnp.exp2(z)) * LOG2_E

    def weights(z, s, diagonal=False):
        w = jnp.exp2(z - s)
        if diagonal:
            w = mask_diagonal(w)
        wb = w.astype(jnp.bfloat16)
        n = wb.shape[0] // 2
        return jnp.concatenate([wb[:n], wb[n:]], axis=1)

    def all_done(carries):
        floor = carries[0]
        for carry in carries[1:]:
            floor = jnp.minimum(floor, carry)
        return jnp.min(floor) >= SWEEP_STOP * LOG2_E

    def first_sweep():
        n = TILE2_ROWS
        zs = [(logits(q_stack[c], window_keys(c, 0, 2)),
               logits(head_rows(q_stack[c], n), window_keys(c, 2))) for c in range(len(chains))]
        if fillers:
            fillers.pop(0)()
        lhs_10, lhs_2 = [], []
        for z10, z2 in zs:
            l10 = mask_diagonal(neg_log2_not_beta(z10))
            lhs_10.append(l10.astype(jnp.bfloat16))
            lhs_2.append(jnp.concatenate(
                [(head_rows(l10[:, :SB_TILE], n) + head_rows(l10[:, SB_TILE:], n)).astype(jnp.bfloat16),
                 neg_log2_not_beta(z2).astype(jnp.bfloat16)], axis=1))
        s_10 = jnp.dot(jnp.concatenate(lhs_10, axis=0), w_10, preferred_element_type=jnp.float32)
        s_2 = jnp.dot(jnp.concatenate(lhs_2, axis=0), w_2, preferred_element_type=jnp.float32)
        if fillers:
            fillers.pop(0)()
        reached, w_10s, w_2s = [], [], []
        for c, (z10, z2) in enumerate(zs):
            part = slice(c * 2 * SB_TILE, (c + 1) * 2 * SB_TILE)
            part2 = slice(c * 2 * n, (c + 1) * 2 * n)
            w_10s.append(weights(z10, s_10[part], diagonal=True))
            w_2s.append(weights(z2, s_2[part2, :SB_TILE]))
            after_1 = s_10[part, 0:1]
            after_2 = s_2[part2, SB_TILE:SB_TILE + 1]
            carry_ref[c] = after_1
            reached.append(jnp.concatenate([after_2[:n], after_1[n:SB_TILE],
                                            after_2[n:], after_1[SB_TILE + n:]], axis=0))
        done = all_done(reached)
        for c in range(len(chains)):
            acc = jnp.dot(w_10s[c], window_values(c, 0, 2), preferred_element_type=jnp.float32)
            tile_2 = jnp.dot(w_2s[c], window_values(c, 2), preferred_element_type=jnp.float32)
            acc_ref[c, n:, :] = acc[n:]
            acc_ref[c, :n, :] = acc[:n] + jnp.where(done, tile_2, 0.0)
        return 2, done

    def tile_copies(m):
        copies = []
        for c, (sub, p) in enumerate(chains):
            row0 = pl.multiple_of(KV_PAD + (n_sub * i + sub - m) * SB_TILE, SB_TILE)
            for which, dst in ((0, ktile_ref), (1, vtile_ref)):
                copies.append(pltpu.make_async_copy(
                    kv_hbm.at[pl.ds(row0, SB_TILE), pl.ds(which * SB_WIDTH + p * LANES, LANES)], dst.at[c],
                    tile_sems.at[which, c]))
        return copies

    def next_sweep(m):
        for copy in tile_copies(m):
            copy.start()
        for copy in tile_copies(m):
            copy.wait()
        zs = [logits(q_stack[c], ktile_ref[c]) for c in range(len(chains))]
        ls = [neg_log2_not_beta(z).astype(jnp.bfloat16) for z in zs]
        cs = jnp.dot(jnp.concatenate(ls, axis=0), w_1, preferred_element_type=jnp.float32)
        carries, w_cats = [], []
        for c, z in enumerate(zs):
            part = slice(c * 2 * SB_TILE, (c + 1) * 2 * SB_TILE)
            old = carry_ref[c]
            w_cats.append(weights(z, cs[part, :SB_TILE] + old))
            carries.append(cs[part, SB_TILE:SB_TILE + 1] + old)
            carry_ref[c] = carries[c]
        for c in range(len(chains)):
            acc_ref[c] += jnp.dot(w_cats[c], _split_heads(vtile_ref[c]), preferred_element_type=jnp.float32)
        return all_done(carries)

    m0, done0 = first_sweep()
    before_loop()
    last = n_sub * i + n_sub - 1

    def cond(state):
        m, done = state
        return jnp.logical_and(m <= last, jnp.logical_not(done))

    def body(state):
        m, _ = state
        return m + 1, next_sweep(m)

    lax.while_loop(cond, body, (jnp.int32(m0), done0))
    return jnp.concatenate(
        [jnp.concatenate([acc_ref[sub * n_pair + p] for p in range(n_pair)], axis=1)
         for sub in range(n_sub)], axis=0)


def _mem_attention(q_ref, mkv_ref, filler):
    q_rows = q_ref.shape[0]
    n_mem = mkv_ref.shape[0]
    n_pair = MEM_WIDTH // LANES
    head_ones = ((lax.broadcasted_iota(jnp.int32, (2 * n_mem, LANES), 1) < HEAD_DIM)
                 == (lax.broadcasted_iota(jnp.int32, (2 * n_mem, LANES), 0) < n_mem)
                 ).astype(jnp.float32).astype(jnp.bfloat16)
    logits = [_dot_nt(_split_heads(q_ref[:, SB_WIDTH + p * LANES:SB_WIDTH + (p + 1) * LANES]),
                      mkv_ref[:, p * LANES:(p + 1) * LANES]) for p in range(n_pair)]
    filler()
    outs = []
    for p, s in enumerate(logits):
        mv = mkv_ref[:, MEM_WIDTH + p * LANES:MEM_WIDTH + (p + 1) * LANES]
        e = jnp.exp(s - jnp.max(s, axis=-1, keepdims=True)).astype(jnp.bfloat16)
        e_cat = jnp.concatenate([e[:q_rows], e[q_rows:]], axis=1)
        num_den = jnp.dot(e_cat, jnp.concatenate([_split_heads(mv), head_ones], axis=1),
                          preferred_element_type=jnp.float32)
        outs.append(num_den[:, :LANES] / num_den[:, LANES:])
    return jnp.concatenate(outs, axis=-1)


def _stream_chunks(jobs, stages, sems):
    depth = len(stages)

    def copy(j):
        return pltpu.make_async_copy(jobs[j][0], stages[j % depth], sems.at[j % depth])

    for j in range(min(depth - 1, len(jobs))):
        copy(j).start()
    for j, (_, consume) in enumerate(jobs):
        if j + depth - 1 < len(jobs):
            copy(j + depth - 1).start()
        copy(j).wait()
        consume(stages[j % depth])


def _chunks(hbm):
    _, n_rows, n_cols = hbm.shape
    return [(r0, c0, hbm.at[0, pl.ds(r0, STREAM_ROWS), pl.ds(c0, STREAM_COLS)])
            for r0 in range(0, n_rows, STREAM_ROWS) for c0 in range(0, n_cols, STREAM_COLS)]


def _load_parameters(mem_hbm, g_mem_ref, w_kv_hbm, w_in_hbm, w_out_hbm, mkv_ref, w_in_ref, w_out_ref,
                     stages, sems):
    jobs = []
    state = {"mem": [], "h": None, "kv": None}

    def take_mem(window):
        state["mem"].append(window[...])

    def memory_kv(window, r0):
        if state["h"] is None:
            m = jnp.concatenate(state["mem"], axis=1)
            state["h"] = (m * _rms_scale(m) * g_mem_ref[...]).astype(jnp.bfloat16)
        part = jnp.dot(state["h"][:, r0:r0 + STREAM_ROWS], window[...].astype(jnp.bfloat16),
                       preferred_element_type=jnp.float32)
        state["kv"] = part if state["kv"] is None else state["kv"] + part
        if r0 + STREAM_ROWS == w_kv_hbm.shape[1]:
            mkv_ref[...] = state["kv"].astype(jnp.bfloat16)

    def cast_into(dst, r0, c0):
        def consume(window):
            dst[r0:r0 + STREAM_ROWS, c0:c0 + STREAM_COLS] = window[...].astype(jnp.bfloat16)
        return consume

    assert mem_hbm.shape[1] == STREAM_ROWS and w_kv_hbm.shape[2] == STREAM_COLS
    jobs += [(chunk, take_mem) for _, _, chunk in _chunks(mem_hbm)]
    jobs += [(chunk, lambda window, r0=r0: memory_kv(window, r0)) for r0, _, chunk in _chunks(w_kv_hbm)]
    for hbm, dst in ((w_in_hbm, w_in_ref), (w_out_hbm, w_out_ref)):
        jobs += [(chunk, cast_into(dst, r0, c0)) for r0, c0, chunk in _chunks(hbm)]
    _stream_chunks(jobs, stages, sems)


def _layer_kernel(x_ref, mem_hbm, g_in_ref, w_in_hbm, convw_ref, convb_ref, g_memin_ref, w_kv_hbm,
                  g_sb_ref, g_conv_ref, g_mem_ref, w_out_hbm, g_final_ref,
                  out_ref, kv_hbm, w_in_ref, w_out_ref, mkv_ref, stream_sems, kv2_ref, prev_ref, kv_sem,
                  ktile_ref, vtile_ref, tile_sems, q2_ref, rest2_ref, x_prev_ref, res_ref, cu_ref,
                  acc_ref, carry_ref):
    s = pl.program_id(0)
    n_blk = pl.num_programs(0) - 2
    q_rows = x_ref.shape[0]
    c_u = 3 * SB_WIDTH
    c_qm = c_u + 3 * CONV_WIDTH
    c_gate = c_qm + MEM_WIDTH

    def normed_input():
        x = x_ref[...]
        return (x * _rms_scale(x) * g_in_ref[...]).astype(jnp.bfloat16)

    def proj(h, c0, width):
        return jnp.dot(h, w_in_ref[:, c0:c0 + width], preferred_element_type=jnp.float32)

    def project_kv(h, slot):
        kv2_ref[slot] = proj(h, SB_WIDTH, 2 * SB_WIDTH).astype(jnp.bfloat16)

    def project_q_conv(h, slot):
        q2_ref[slot, :, :SB_WIDTH] = (proj(h, 0, SB_WIDTH) * (SCALE * LOG2_E)).astype(jnp.bfloat16)
        q2_ref[slot, :, SB_WIDTH:] = (proj(h, c_qm, MEM_WIDTH) * SCALE).astype(jnp.bfloat16)
        rest2_ref[slot, :, :3 * CONV_WIDTH] = proj(h, c_u, 3 * CONV_WIDTH)

    def project_gate(h, slot):
        for c0 in range(0, MIX_WIDTH, PROJ_COLS):
            rest2_ref[slot, :, 3 * CONV_WIDTH + c0:3 * CONV_WIDTH + c0 + PROJ_COLS] = proj(h, c_gate + c0, PROJ_COLS)

    def final_norm():
        res = res_ref[...]
        out_ref[...] = res * _rms_scale(res) * g_final_ref[...]

    @pl.when(s == 0)
    def _():
        stages = [ref.at[pl.ds(r0, STREAM_ROWS), pl.ds(c0, STREAM_COLS)] for ref in (x_prev_ref, rest2_ref.at[1])
                  for r0 in range(0, ref.shape[0] - STREAM_ROWS + 1, STREAM_ROWS)
                  for c0 in range(0, ref.shape[-1] - STREAM_COLS + 1, STREAM_COLS)]
        _load_parameters(mem_hbm, g_memin_ref, w_kv_hbm, w_in_hbm, w_out_hbm, mkv_ref, w_in_ref, w_out_ref,
                         stages[:STREAM_STAGES], stream_sems)
        h = normed_input()
        project_kv(h, 0)
        project_q_conv(h, 0)
        project_gate(h, 0)
        x_prev_ref[...] = x_ref[...]
        res_ref[...] = jnp.zeros(res_ref.shape, jnp.float32)
        cu_ref[0:SUBLANES, :] = jnp.zeros((SUBLANES, CONV_WIDTH), jnp.float32)
        kv2_ref[1] = jnp.zeros(kv2_ref.shape[1:], jnp.bfloat16)
        zero_rows = pltpu.make_async_copy(kv2_ref.at[1, pl.ds(0, KV_PAD)], kv_hbm.at[pl.ds(0, KV_PAD)],
                                          kv_sem.at[0])
        zero_rows.start()
        zero_rows.wait()

    @pl.when(s == n_blk + 1)
    def _():
        final_norm()

    @pl.when(jnp.logical_and(s > 0, s <= n_blk))
    def _():
        i = s - 1
        cur = lax.rem(i, 2)
        nxt = 1 - cur
        q_ref = q2_ref.at[cur]
        rest_ref = rest2_ref.at[cur]

        def history_write():
            rows = pl.ds(pl.multiple_of(KV_PAD + i * q_rows, q_rows), q_rows)
            return pltpu.make_async_copy(kv2_ref.at[cur], kv_hbm.at[rows], kv_sem.at[0])

        history_write().start()
        prev_ref[...] = kv2_ref[nxt, q_rows - KV_PREV:, :]
        final_norm()

        h_next = normed_input()
        y_sb = _sb_sweep(i, q_ref, kv2_ref.at[cur], prev_ref, kv_hbm, ktile_ref, vtile_ref, tile_sems, acc_ref,
                         carry_ref, (lambda: project_kv(h_next, nxt), lambda: project_q_conv(h_next, nxt)),
                         lambda: history_write().wait())

        u = rest_ref[:, 0:CONV_WIDTH]
        b = rest_ref[:, CONV_WIDTH:2 * CONV_WIDTH]
        c = rest_ref[:, 2 * CONV_WIDTH:3 * CONV_WIDTH]
        cu = c * u
        cu_ref[SUBLANES:, :] = cu
        cu_1 = cu_ref[SUBLANES - 1:SUBLANES - 1 + q_rows, :]
        cu_2 = cu_ref[SUBLANES - 2:SUBLANES - 2 + q_rows, :]
        cu_ref[0:SUBLANES, :] = cu[q_rows - SUBLANES:, :]
        conv = (convw_ref[0] * cu_2 + convw_ref[1] * cu_1 + convw_ref[2] * cu
                + convb_ref[...])
        y_conv = b * conv

        y_mem = _mem_attention(q_ref, mkv_ref, lambda: project_gate(h_next, nxt))

        y = jnp.concatenate([y_sb * _rms_scale(y_sb) * g_sb_ref[...],
                             y_conv * _rms_scale(y_conv) * g_conv_ref[...],
                             y_mem * _rms_scale(y_mem) * g_mem_ref[...]], axis=-1)
        gate = rest_ref[:, 3 * CONV_WIDTH:]
        gated = (y * (gate / (1.0 + jnp.exp(-gate)))).astype(jnp.bfloat16)
        res_ref[...] = x_prev_ref[...] + jnp.dot(gated, w_out_ref[...], preferred_element_type=jnp.float32)
        x_prev_ref[...] = x_ref[...]


def _layer(x2, mem, g_in, w_in, conv_w, conv_b, g_memin, w_kv, g_sb, g_conv, g_mem, w_out, g_final):
    t, d = x2.shape
    assert t % Q_BLOCK == 0 and Q_BLOCK % SB_TILE == 0 and KV_PREV <= Q_BLOCK and KV_PAD <= Q_BLOCK
    assert Q_BLOCK >= STREAM_ROWS
    assert (Q_BLOCK // STREAM_ROWS) * (d // STREAM_COLS + (3 * CONV_WIDTH + MIX_WIDTH) // STREAM_COLS) >= STREAM_STAGES
    for a in (mem, w_kv, w_in, w_out):
        assert a.shape[1] % STREAM_ROWS == 0 and a.shape[2] % STREAM_COLS == 0
    n_chain = (Q_BLOCK // SB_TILE) * (SB_WIDTH // LANES)

    def whole(a):
        return pl.BlockSpec(a.shape, lambda s: (0,) * a.ndim)

    in_hbm = pl.BlockSpec(memory_space=pl.ANY)
    n_blk = t // Q_BLOCK
    return pl.pallas_call(
        _layer_kernel,
        grid=(n_blk + 2,),
        in_specs=[pl.BlockSpec((Q_BLOCK, d), lambda s: (jnp.minimum(s, n_blk - 1), 0)), in_hbm, whole(g_in),
                  in_hbm, whole(conv_w), whole(conv_b), whole(g_memin), in_hbm, whole(g_sb), whole(g_conv),
                  whole(g_mem), in_hbm, whole(g_final)],
        out_specs=[pl.BlockSpec((Q_BLOCK, d), lambda s: (jnp.clip(s - 2, 0, n_blk - 1), 0)), in_hbm],
        out_shape=[jax.ShapeDtypeStruct((t, d), jnp.float32),
                   jax.ShapeDtypeStruct((KV_PAD + t, 2 * SB_WIDTH), jnp.bfloat16)],
        scratch_shapes=[pltpu.VMEM(w_in.shape[1:], jnp.bfloat16),
                        pltpu.VMEM(w_out.shape[1:], jnp.bfloat16),
                        pltpu.VMEM((mem.shape[1], w_kv.shape[2]), jnp.bfloat16),
                        pltpu.SemaphoreType.DMA((STREAM_STAGES,)),
                        pltpu.VMEM((2, Q_BLOCK, 2 * SB_WIDTH), jnp.bfloat16),
                        pltpu.VMEM((KV_PREV, 2 * SB_WIDTH), jnp.bfloat16),
                        pltpu.SemaphoreType.DMA((1,)),
                        pltpu.VMEM((n_chain, SB_TILE, LANES), jnp.bfloat16),
                        pltpu.VMEM((n_chain, SB_TILE, LANES), jnp.bfloat16),
                        pltpu.SemaphoreType.DMA((2, n_chain)),
                        pltpu.VMEM((2, Q_BLOCK, SB_WIDTH + MEM_WIDTH), jnp.bfloat16),
                        pltpu.VMEM((2, Q_BLOCK, 3 * CONV_WIDTH + MIX_WIDTH), jnp.float32),
                        pltpu.VMEM((Q_BLOCK, d), jnp.float32),
                        pltpu.VMEM((Q_BLOCK, d), jnp.float32),
                        pltpu.VMEM((SUBLANES + Q_BLOCK, CONV_WIDTH), jnp.float32),
                        pltpu.VMEM((n_chain, SB_TILE, LANES), jnp.float32),
                        pltpu.VMEM((n_chain, 2 * SB_TILE, 1), jnp.float32)],
        compiler_params=pltpu.CompilerParams(
            dimension_semantics=("arbitrary",), vmem_limit_bytes=VMEM_LIMIT_BYTES),
        name="layer",
    )(x2, mem, g_in, w_in, conv_w, conv_b, g_memin, w_kv, g_sb, g_conv, g_mem, w_out, g_final)[0]


def kernel(x, mem, g_in, w_in, conv_w, conv_b, g_mem, w_mem_kv, g_sb_out, g_conv_out,
           g_mem_out, w_out, g_final):
    batch, t, d = x.shape
    assert batch == 1 and g_in.shape[0] == 1
    out = _layer(x.reshape(t, d), mem, g_in[0][None, :], w_in, jnp.transpose(conv_w, (1, 0, 2)), conv_b[0][None, :],
                 g_mem[0][None, :], w_mem_kv, g_sb_out[0][None, :], g_conv_out[0][None, :],
                 g_mem_out[0][None, :], w_out, g_final[None, :])
    return out.reshape(batch, t, d)
```

```python
import jax
import jax.numpy as jnp
from jax import lax
from jax.experimental import pallas as pl
from jax.experimental.pallas import tpu as pltpu

HEAD_DIM = 64
SB_HEADS = 8
SB_WIDTH = SB_HEADS * HEAD_DIM
CONV_WIDTH = 4 * HEAD_DIM
MEM_HEADS = 4
MEM_WIDTH = MEM_HEADS * HEAD_DIM
CONV_K = 3
EPS = 1e-6
SCALE = HEAD_DIM ** -0.5

LANES = 128
SUBLANES = 8
MIX_WIDTH = SB_WIDTH + CONV_WIDTH + MEM_WIDTH
PROJ_COLS = 512
Q_BLOCK = 512
SB_TILE = 128
SWEEP_STOP = 104.0
TILE2_ROWS = 32
KV_PAD = Q_BLOCK
KV_PREV = 2 * SB_TILE
LOG2_E = 1.4426950408889634
LOGIT2_CLAMP = 126.0
STREAM_ROWS = 256
STREAM_COLS = 512
STREAM_STAGES = 9
VMEM_LIMIT_BYTES = 62 * 1024 * 1024


def _rms_scale(xf, eps=EPS):
    return lax.rsqrt(jnp.mean(xf * xf, axis=-1, keepdims=True) + eps)


def _dot_nt(a, b):
    return lax.dot_general(a, b, (((1,), (1,)), ((), ())), preferred_element_type=jnp.float32)


def _split_heads(a):
    first_half = lax.broadcasted_iota(jnp.int32, a.shape, 1) < HEAD_DIM
    zero = jnp.zeros_like(a)
    return jnp.concatenate([jnp.where(first_half, a, zero), jnp.where(first_half, zero, a)], axis=0)


def _sb_sweep(i, q_ref, kwin_ref, vwin_ref, kv_hbm, ktile_ref, vtile_ref, tile_sems, acc_ref, carry_ref, fillers,
              before_loop):
    n_sub = q_ref.shape[0] // SB_TILE
    n_pair = SB_WIDTH // LANES
    chains = [(sub, p) for sub in range(n_sub) for p in range(n_pair)]
    fillers = list(fillers)

    rt = lax.broadcasted_iota(jnp.int32, (2 * SB_TILE, 2 * SB_TILE), 0)
    ct = lax.broadcasted_iota(jnp.int32, (2 * SB_TILE, 2 * SB_TILE), 1)
    top, left = rt < SB_TILE, ct < SB_TILE
    tri2 = jnp.where(top, rt, rt - SB_TILE) >= jnp.where(left, ct, ct - SB_TILE)
    r2 = lax.broadcasted_iota(jnp.int32, (2 * SB_TILE, SB_TILE), 0)
    c2 = lax.broadcasted_iota(jnp.int32, (2 * SB_TILE, SB_TILE), 1)
    causal = c2 < jnp.where(r2 < SB_TILE, r2, r2 - SB_TILE)

    def mask_diagonal(a):
        return jnp.concatenate([a[:, :SB_TILE], jnp.where(causal, a[:, SB_TILE:], 0.0)], axis=1)

    def as_matrix(cond):
        return jnp.where(cond, 1.0, 0.0).astype(jnp.bfloat16)

    w_10 = as_matrix(jnp.logical_or(jnp.logical_and(jnp.logical_not(top), left),
                                    jnp.logical_and(tri2, jnp.logical_not(jnp.logical_xor(top, left)))))
    w_2 = as_matrix(jnp.logical_or(top, jnp.logical_or(tri2, jnp.logical_not(left))))
    w_1 = w_2[SB_TILE:]

    q_stack = [_split_heads(q_ref[sub * SB_TILE:(sub + 1) * SB_TILE, p * LANES:(p + 1) * LANES])
               for sub, p in chains]

    def window_rows(c, m, n_tiles=1):
        start = KV_PREV + (chains[c][0] - (m + n_tiles - 1)) * SB_TILE
        assert start >= 0
        return slice(start, start + n_tiles * SB_TILE)

    def head_rows(a, n):
        return jnp.concatenate([a[:n], a[SB_TILE:SB_TILE + n]], axis=0)

    def logits(q, keys):
        return jnp.minimum(_dot_nt(q, keys), LOGIT2_CLAMP)

    def window_keys(c, rows):
        p = chains[c][1]
        return kwin_ref[rows, p * LANES:(p + 1) * LANES]

    def window_values(c, rows):
        p = chains[c][1]
        return _split_heads(vwin_ref[rows, p * LANES:(p + 1) * LANES])

    def neg_log2_not_beta(z):
        return jnp.log(1.0 + jnp.exp2(z)) * LOG2_E

    def weights(z, s, diagonal=False):
        w = jnp.exp2(z - s)
        if diagonal:
            w = mask_diagonal(w)
        wb = w.astype(jnp.bfloat16)
        n = wb.shape[0] // 2
        return jnp.concatenate([wb[:n], wb[n:]], axis=1)

    def all_done(carries):
        floor = carries[0]
        for carry in carries[1:]:
            floor = jnp.minimum(floor, carry)
        return jnp.min(floor) >= SWEEP_STOP * LOG2_E

    def first_sweep():
        n = TILE2_ROWS
        rows_10 = [window_rows(c, 0, 2) for c in range(len(chains))]
        rows_2 = [window_rows(c, 2) for c in range(len(chains))]
        zs = [(logits(q_stack[c], window_keys(c, rows_10[c])),
               logits(head_rows(q_stack[c], n), window_keys(c, rows_2[c]))) for c in range(len(chains))]
        if fillers:
            fillers.pop(0)()
        lhs_10, lhs_2 = [], []
        for z10, z2 in zs:
            l10 = mask_diagonal(neg_log2_not_beta(z10))
            lhs_10.append(l10.astype(jnp.bfloat16))
            lhs_2.append(jnp.concatenate(
                [(head_rows(l10[:, :SB_TILE], n) + head_rows(l10[:, SB_TILE:], n)).astype(jnp.bfloat16),
                 neg_log2_not_beta(z2).astype(jnp.bfloat16)], axis=1))
        s_10 = jnp.dot(jnp.concatenate(lhs_10, axis=0), w_10, preferred_element_type=jnp.float32)
        s_2 = jnp.dot(jnp.concatenate(lhs_2, axis=0), w_2, preferred_element_type=jnp.float32)
        if fillers:
            fillers.pop(0)()
        reached, w_10s, w_2s = [], [], []
        for c, (z10, z2) in enumerate(zs):
            part = slice(c * 2 * SB_TILE, (c + 1) * 2 * SB_TILE)
            part2 = slice(c * 2 * n, (c + 1) * 2 * n)
            w_10s.append(weights(z10, s_10[part], diagonal=True))
            w_2s.append(weights(z2, s_2[part2, :SB_TILE]))
            after_1 = s_10[part, 0:1]
            after_2 = s_2[part2, SB_TILE:SB_TILE + 1]
            carry_ref[c] = after_1
            reached.append(jnp.concatenate([after_2[:n], after_1[n:SB_TILE],
                                            after_2[n:], after_1[SB_TILE + n:]], axis=0))
        done = all_done(reached)
        for c in range(len(chains)):
            acc = jnp.dot(w_10s[c], window_values(c, rows_10[c]), preferred_element_type=jnp.float32)
            tile_2 = jnp.dot(w_2s[c], window_values(c, rows_2[c]), preferred_element_type=jnp.float32)
            acc_ref[c, n:, :] = acc[n:]
            acc_ref[c, :n, :] = acc[:n] + jnp.where(done, tile_2, 0.0)
        return 2, done

    def tile_copies(m):
        copies = []
        for c, (sub, p) in enumerate(chains):
            row0 = pl.multiple_of(KV_PAD + (n_sub * i + sub - m) * SB_TILE, SB_TILE)
            for which, dst in ((0, ktile_ref), (1, vtile_ref)):
                copies.append(pltpu.make_async_copy(
                    kv_hbm.at[which, pl.ds(row0, SB_TILE), pl.ds(p * LANES, LANES)], dst.at[c],
                    tile_sems.at[which, c]))
        return copies

    def next_sweep(m):
        for copy in tile_copies(m):
            copy.start()
        for copy in tile_copies(m):
            copy.wait()
        zs = [logits(q_stack[c], ktile_ref[c]) for c in range(len(chains))]
        ls = [neg_log2_not_beta(z).astype(jnp.bfloat16) for z in zs]
        cs = jnp.dot(jnp.concatenate(ls, axis=0), w_1, preferred_element_type=jnp.float32)
        carries, w_cats = [], []
        for c, z in enumerate(zs):
            part = slice(c * 2 * SB_TILE, (c + 1) * 2 * SB_TILE)
            old = carry_ref[c]
            w_cats.append(weights(z, cs[part, :SB_TILE] + old))
            carries.append(cs[part, SB_TILE:SB_TILE + 1] + old)
            carry_ref[c] = carries[c]
        for c in range(len(chains)):
            acc_ref[c] += jnp.dot(w_cats[c], _split_heads(vtile_ref[c]), preferred_element_type=jnp.float32)
        return all_done(carries)

    m0, done0 = first_sweep()
    before_loop()
    last = n_sub * i + n_sub - 1

    def cond(state):
        m, done = state
        return jnp.logical_and(m <= last, jnp.logical_not(done))

    def body(state):
        m, _ = state
        return m + 1, next_sweep(m)

    lax.while_loop(cond, body, (jnp.int32(m0), done0))
    return jnp.concatenate(
        [jnp.concatenate([acc_ref[sub * n_pair + p] for p in range(n_pair)], axis=1)
         for sub in range(n_sub)], axis=0)


def _mem_attention(q_ref, mkv_ref, filler):
    q_rows = q_ref.shape[0]
    n_mem = mkv_ref.shape[0]
    n_pair = MEM_WIDTH // LANES
    head_ones = ((lax.broadcasted_iota(jnp.int32, (2 * n_mem, LANES), 1) < HEAD_DIM)
                 == (lax.broadcasted_iota(jnp.int32, (2 * n_mem, LANES), 0) < n_mem)
                 ).astype(jnp.float32).astype(jnp.bfloat16)
    logits = [_dot_nt(_split_heads(q_ref[:, SB_WIDTH + p * LANES:SB_WIDTH + (p + 1) * LANES]),
                      mkv_ref[:, p * LANES:(p + 1) * LANES]) for p in range(n_pair)]
    filler()
    outs = []
    for p, s in enumerate(logits):
        mv = mkv_ref[:, MEM_WIDTH + p * LANES:MEM_WIDTH + (p + 1) * LANES]
        e = jnp.exp(s - jnp.max(s, axis=-1, keepdims=True)).astype(jnp.bfloat16)
        e_cat = jnp.concatenate([e[:q_rows], e[q_rows:]], axis=1)
        num_den = jnp.dot(e_cat, jnp.concatenate([_split_heads(mv), head_ones], axis=1),
                          preferred_element_type=jnp.float32)
        outs.append(num_den[:, :LANES] / num_den[:, LANES:])
    return jnp.concatenate(outs, axis=-1)


def _stream_chunks(jobs, stages, sems):
    depth = len(stages)

    def copy(j):
        return pltpu.make_async_copy(jobs[j][0], stages[j % depth], sems.at[j % depth])

    for j in range(min(depth - 1, len(jobs))):
        copy(j).start()
    for j, (_, consume) in enumerate(jobs):
        if j + depth - 1 < len(jobs):
            copy(j + depth - 1).start()
        copy(j).wait()
        consume(stages[j % depth])


def _chunks(hbm):
    _, n_rows, n_cols = hbm.shape
    return [(r0, c0, hbm.at[0, pl.ds(r0, STREAM_ROWS), pl.ds(c0, STREAM_COLS)])
            for r0 in range(0, n_rows, STREAM_ROWS) for c0 in range(0, n_cols, STREAM_COLS)]


def _load_parameters(mem_hbm, g_mem_ref, w_kv_hbm, w_in_hbm, w_out_hbm, mkv_ref, w_in_ref, w_out_ref,
                     stages, sems):
    jobs = []
    state = {"mem": [], "h": None, "kv": None}

    def take_mem(window):
        state["mem"].append(window[...])

    def memory_kv(window, r0):
        if state["h"] is None:
            m = jnp.concatenate(state["mem"], axis=1)
            state["h"] = (m * _rms_scale(m) * g_mem_ref[...]).astype(jnp.bfloat16)
        part = jnp.dot(state["h"][:, r0:r0 + STREAM_ROWS], window[...].astype(jnp.bfloat16),
                       preferred_element_type=jnp.float32)
        state["kv"] = part if state["kv"] is None else state["kv"] + part
        if r0 + STREAM_ROWS == w_kv_hbm.shape[1]:
            mkv_ref[...] = state["kv"].astype(jnp.bfloat16)

    def cast_into(dst, r0, c0):
        def consume(window):
            dst[r0:r0 + STREAM_ROWS, c0:c0 + STREAM_COLS] = window[...].astype(jnp.bfloat16)
        return consume

    assert mem_hbm.shape[1] == STREAM_ROWS and w_kv_hbm.shape[2] == STREAM_COLS
    jobs += [(chunk, take_mem) for _, _, chunk in _chunks(mem_hbm)]
    jobs += [(chunk, lambda window, r0=r0: memory_kv(window, r0)) for r0, _, chunk in _chunks(w_kv_hbm)]
    for hbm, dst in ((w_in_hbm, w_in_ref), (w_out_hbm, w_out_ref)):
        jobs += [(chunk, cast_into(dst, r0, c0)) for r0, c0, chunk in _chunks(hbm)]
    _stream_chunks(jobs, stages, sems)


def _layer_kernel(x_ref, mem_hbm, g_in_ref, w_in_hbm, convw_ref, convb_ref, g_memin_ref, w_kv_hbm,
                  g_sb_ref, g_conv_ref, g_mem_ref, w_out_hbm, g_final_ref,
                  out_ref, kv_hbm, w_in_ref, w_out_ref, mkv_ref, stream_sems, kwin_ref, vwin_ref, kv_next_ref,
                  ktile_ref, vtile_ref, kv_sems, tile_sems, q2_ref, rest2_ref, x_prev_ref, res_ref, cu_ref,
                  acc_ref, carry_ref):
    s = pl.program_id(0)
    n_blk = pl.num_programs(0) - 2
    q_rows = x_ref.shape[0]
    c_u = 3 * SB_WIDTH
    c_qm = c_u + 3 * CONV_WIDTH
    c_gate = c_qm + MEM_WIDTH

    def normed_input():
        x = x_ref[...]
        return (x * _rms_scale(x) * g_in_ref[...]).astype(jnp.bfloat16)

    def proj(h, c0, width):
        return jnp.dot(h, w_in_ref[:, c0:c0 + width], preferred_element_type=jnp.float32)

    def project_kv(h):
        kv_next_ref[...] = proj(h, SB_WIDTH, 2 * SB_WIDTH).astype(jnp.bfloat16)

    def project_q_conv(h, slot):
        q2_ref[slot, :, :SB_WIDTH] = (proj(h, 0, SB_WIDTH) * (SCALE * LOG2_E)).astype(jnp.bfloat16)
        q2_ref[slot, :, SB_WIDTH:] = (proj(h, c_qm, MEM_WIDTH) * SCALE).astype(jnp.bfloat16)
        rest2_ref[slot, :, :3 * CONV_WIDTH] = proj(h, c_u, 3 * CONV_WIDTH)

    def project_gate(h, slot):
        for c0 in range(0, MIX_WIDTH, PROJ_COLS):
            rest2_ref[slot, :, 3 * CONV_WIDTH + c0:3 * CONV_WIDTH + c0 + PROJ_COLS] = proj(h, c_gate + c0, PROJ_COLS)

    def final_norm():
        res = res_ref[...]
        out_ref[...] = res * _rms_scale(res) * g_final_ref[...]

    @pl.when(s == 0)
    def _():
        stages = [ref.at[pl.ds(r0, STREAM_ROWS), pl.ds(c0, STREAM_COLS)] for ref in (x_prev_ref, rest2_ref.at[1])
                  for r0 in range(0, ref.shape[0] - STREAM_ROWS + 1, STREAM_ROWS)
                  for c0 in range(0, ref.shape[-1] - STREAM_COLS + 1, STREAM_COLS)]
        _load_parameters(mem_hbm, g_memin_ref, w_kv_hbm, w_in_hbm, w_out_hbm, mkv_ref, w_in_ref, w_out_ref,
                         stages[:STREAM_STAGES], stream_sems)
        h = normed_input()
        project_kv(h)
        project_q_conv(h, 0)
        project_gate(h, 0)
        x_prev_ref[...] = x_ref[...]
        res_ref[...] = jnp.zeros(res_ref.shape, jnp.float32)
        cu_ref[0:SUBLANES, :] = jnp.zeros((SUBLANES, CONV_WIDTH), jnp.float32)
        kwin_ref[...] = jnp.zeros(kwin_ref.shape, jnp.bfloat16)
        vwin_ref[...] = jnp.zeros(vwin_ref.shape, jnp.bfloat16)
        zero_rows = [pltpu.make_async_copy(win.at[pl.ds(0, KV_PAD)], kv_hbm.at[which, pl.ds(0, KV_PAD)],
                                           kv_sems.at[which])
                     for which, win in ((0, kwin_ref), (1, vwin_ref))]
        for copy in zero_rows:
            copy.start()
        for copy in zero_rows:
            copy.wait()

    @pl.when(s == n_blk + 1)
    def _():
        final_norm()

    @pl.when(jnp.logical_and(s > 0, s <= n_blk))
    def _():
        i = s - 1
        cur = lax.rem(i, 2)
        nxt = 1 - cur
        q_ref = q2_ref.at[cur]
        rest_ref = rest2_ref.at[cur]

        def history_writes(block):
            rows = pl.ds(pl.multiple_of(KV_PAD + block * q_rows, q_rows), q_rows)
            return [pltpu.make_async_copy(win.at[pl.ds(KV_PREV, q_rows)], kv_hbm.at[which, rows],
                                          kv_sems.at[which])
                    for which, win in ((0, kwin_ref), (1, vwin_ref))]

        for win, c0 in ((kwin_ref, 0), (vwin_ref, SB_WIDTH)):
            win[0:KV_PREV, :] = win[q_rows:q_rows + KV_PREV, :]
            win[KV_PREV:, :] = kv_next_ref[:, c0:c0 + SB_WIDTH]
        for copy in history_writes(i):
            copy.start()
        final_norm()

        h_next = normed_input()
        y_sb = _sb_sweep(i, q_ref, kwin_ref, vwin_ref, kv_hbm, ktile_ref, vtile_ref, tile_sems, acc_ref, carry_ref,
                         (lambda: project_kv(h_next), lambda: project_q_conv(h_next, nxt)),
                         lambda: [copy.wait() for copy in history_writes(i)])

        u = rest_ref[:, 0:CONV_WIDTH]
        b = rest_ref[:, CONV_WIDTH:2 * CONV_WIDTH]
        c = rest_ref[:, 2 * CONV_WIDTH:3 * CONV_WIDTH]
        cu = c * u
        cu_ref[SUBLANES:, :] = cu
        cu_1 = cu_ref[SUBLANES - 1:SUBLANES - 1 + q_rows, :]
        cu_2 = cu_ref[SUBLANES - 2:SUBLANES - 2 + q_rows, :]
        cu_ref[0:SUBLANES, :] = cu[q_rows - SUBLANES:, :]
        conv = (convw_ref[0] * cu_2 + convw_ref[1] * cu_1 + convw_ref[2] * cu
                + convb_ref[...])
        y_conv = b * conv

        y_mem = _mem_attention(q_ref, mkv_ref, lambda: project_gate(h_next, nxt))

        y = jnp.concatenate([y_sb * _rms_scale(y_sb) * g_sb_ref[...],
                             y_conv * _rms_scale(y_conv) * g_conv_ref[...],
                             y_mem * _rms_scale(y_mem) * g_mem_ref[...]], axis=-1)
        gate = rest_ref[:, 3 * CONV_WIDTH:]
        half_gate = 0.5 * gate
        gated = (y * (half_gate + half_gate * jnp.tanh(half_gate))).astype(jnp.bfloat16)
        res_ref[...] = x_prev_ref[...] + jnp.dot(gated, w_out_ref[...], preferred_element_type=jnp.float32)
        x_prev_ref[...] = x_ref[...]


def _layer(x2, mem, g_in, w_in, conv_w, conv_b, g_memin, w_kv, g_sb, g_conv, g_mem, w_out, g_final):
    t, d = x2.shape
    assert t % Q_BLOCK == 0 and Q_BLOCK % SB_TILE == 0 and KV_PREV <= Q_BLOCK
    assert Q_BLOCK >= STREAM_ROWS
    assert (Q_BLOCK // STREAM_ROWS) * (d // STREAM_COLS + (3 * CONV_WIDTH + MIX_WIDTH) // STREAM_COLS) >= STREAM_STAGES
    for a in (mem, w_kv, w_in, w_out):
        assert a.shape[1] % STREAM_ROWS == 0 and a.shape[2] % STREAM_COLS == 0
    n_chain = (Q_BLOCK // SB_TILE) * (SB_WIDTH // LANES)

    def whole(a):
        return pl.BlockSpec(a.shape, lambda s: (0,) * a.ndim)

    in_hbm = pl.BlockSpec(memory_space=pl.ANY)
    n_blk = t // Q_BLOCK
    return pl.pallas_call(
        _layer_kernel,
        grid=(n_blk + 2,),
        in_specs=[pl.BlockSpec((Q_BLOCK, d), lambda s: (jnp.minimum(s, n_blk - 1), 0)), in_hbm, whole(g_in),
                  in_hbm, whole(conv_w), whole(conv_b), whole(g_memin), in_hbm, whole(g_sb), whole(g_conv),
                  whole(g_mem), in_hbm, whole(g_final)],
        out_specs=[pl.BlockSpec((Q_BLOCK, d), lambda s: (jnp.clip(s - 2, 0, n_blk - 1), 0)), in_hbm],
        out_shape=[jax.ShapeDtypeStruct((t, d), jnp.float32),
                   jax.ShapeDtypeStruct((2, KV_PAD + t, SB_WIDTH), jnp.bfloat16)],
        scratch_shapes=[pltpu.VMEM(w_in.shape[1:], jnp.bfloat16),
                        pltpu.VMEM(w_out.shape[1:], jnp.bfloat16),
                        pltpu.VMEM((mem.shape[1], w_kv.shape[2]), jnp.bfloat16),
                        pltpu.SemaphoreType.DMA((STREAM_STAGES,)),
                        pltpu.VMEM((KV_PREV + Q_BLOCK, SB_WIDTH), jnp.bfloat16),
                        pltpu.VMEM((KV_PREV + Q_BLOCK, SB_WIDTH), jnp.bfloat16),
                        pltpu.VMEM((Q_BLOCK, 2 * SB_WIDTH), jnp.bfloat16),
                        pltpu.VMEM((n_chain, SB_TILE, LANES), jnp.bfloat16),
                        pltpu.VMEM((n_chain, SB_TILE, LANES), jnp.bfloat16),
                        pltpu.SemaphoreType.DMA((2,)),
                        pltpu.SemaphoreType.DMA((2, n_chain)),
                        pltpu.VMEM((2, Q_BLOCK, SB_WIDTH + MEM_WIDTH), jnp.bfloat16),
                        pltpu.VMEM((2, Q_BLOCK, 3 * CONV_WIDTH + MIX_WIDTH), jnp.float32),
                        pltpu.VMEM((Q_BLOCK, d), jnp.float32),
                        pltpu.VMEM((Q_BLOCK, d), jnp.float32),
                        pltpu.VMEM((SUBLANES + Q_BLOCK, CONV_WIDTH), jnp.float32),
                        pltpu.VMEM((n_chain, SB_TILE, LANES), jnp.float32),
                        pltpu.VMEM((n_chain, 2 * SB_TILE, 1), jnp.float32)],
        compiler_params=pltpu.CompilerParams(
            dimension_semantics=("arbitrary",), vmem_limit_bytes=VMEM_LIMIT_BYTES),
        name="layer",
    )(x2, mem, g_in, w_in, conv_w, conv_b, g_memin, w_kv, g_sb, g_conv, g_mem, w_out, g_final)[0]


def kernel(x, mem, g_in, w_in, conv_w, conv_b, g_mem, w_mem_kv, g_sb_out, g_conv_out,
           g_mem_out, w_out, g_final):
    batch, t, d = x.shape
    assert batch == 1 and g_in.shape[0] == 1
    out = _layer(x.reshape(t, d), mem, g_in[0][None, :], w_in, jnp.transpose(conv_w, (1, 0, 2)), conv_b[0][None, :],
                 g_mem[0][None, :], w_mem_kv, g_sb_out[0][None, :], g_conv_out[0][None, :],
                 g_mem_out[0][None, :], w_out, g_final[None, :])
    return out.reshape(batch, t, d)
```

```python
import jax
import jax.numpy as jnp
from jax import lax
from jax.experimental import pallas as pl
from jax.experimental.pallas import tpu as pltpu

HEAD_DIM = 64
SB_HEADS = 8
SB_WIDTH = SB_HEADS * HEAD_DIM
CONV_WIDTH = 4 * HEAD_DIM
MEM_HEADS = 4
MEM_WIDTH = MEM_HEADS * HEAD_DIM
CONV_K = 3
EPS = 1e-6
SCALE = HEAD_DIM ** -0.5

LANES = 128
SUBLANES = 8
MIX_WIDTH = SB_WIDTH + CONV_WIDTH + MEM_WIDTH
PROJ_COLS = 512
Q_BLOCK = 512
SB_TILE = 128
SWEEP_STOP = 104.0
TILE2_ROWS = 32
KV_PAD = Q_BLOCK
KV_PREV = 2 * SB_TILE
LOG2_E = 1.4426950408889634
LOGIT2_CLAMP = 126.0
STREAM_ROWS = 256
STREAM_COLS = 512
STREAM_STAGES = 9
VMEM_LIMIT_BYTES = 62 * 1024 * 1024


def _rms_scale(xf, eps=EPS):
    return lax.rsqrt(jnp.mean(xf * xf, axis=-1, keepdims=True) + eps)


def _dot_nt(a, b):
    return lax.dot_general(a, b, (((1,), (1,)), ((), ())), preferred_element_type=jnp.float32)


def _split_heads(a):
    first_half = lax.broadcasted_iota(jnp.int32, a.shape, 1) < HEAD_DIM
    zero = jnp.zeros_like(a)
    return jnp.concatenate([jnp.where(first_half, a, zero), jnp.where(first_half, zero, a)], axis=0)


def _sb_sweep(i, q_ref, kwin_ref, vwin_ref, kv_hbm, ktile_ref, vtile_ref, tile_sems, acc_ref, carry_ref, fillers,
              before_loop):
    n_sub = q_ref.shape[0] // SB_TILE
    n_pair = SB_WIDTH // LANES
    chains = [(sub, p) for sub in range(n_sub) for p in range(n_pair)]
    fillers = list(fillers)

    rt = lax.broadcasted_iota(jnp.int32, (2 * SB_TILE, 2 * SB_TILE), 0)
    ct = lax.broadcasted_iota(jnp.int32, (2 * SB_TILE, 2 * SB_TILE), 1)
    top, left = rt < SB_TILE, ct < SB_TILE
    tri2 = jnp.where(top, rt, rt - SB_TILE) >= jnp.where(left, ct, ct - SB_TILE)
    r2 = lax.broadcasted_iota(jnp.int32, (2 * SB_TILE, SB_TILE), 0)
    c2 = lax.broadcasted_iota(jnp.int32, (2 * SB_TILE, SB_TILE), 1)
    causal = c2 < jnp.where(r2 < SB_TILE, r2, r2 - SB_TILE)

    def mask_diagonal(a):
        return jnp.concatenate([a[:, :SB_TILE], jnp.where(causal, a[:, SB_TILE:], 0.0)], axis=1)

    def as_matrix(cond):
        return jnp.where(cond, 1.0, 0.0).astype(jnp.bfloat16)

    w_10 = as_matrix(jnp.logical_or(jnp.logical_and(jnp.logical_not(top), left),
                                    jnp.logical_and(tri2, jnp.logical_not(jnp.logical_xor(top, left)))))
    w_2 = as_matrix(jnp.logical_or(top, jnp.logical_or(tri2, jnp.logical_not(left))))
    w_1 = w_2[SB_TILE:]

    q_stack = [_split_heads(q_ref[sub * SB_TILE:(sub + 1) * SB_TILE, p * LANES:(p + 1) * LANES])
               for sub, p in chains]

    def window_rows(c, m, n_tiles=1):
        start = KV_PREV + (chains[c][0] - (m + n_tiles - 1)) * SB_TILE
        assert start >= 0
        return slice(start, start + n_tiles * SB_TILE)

    def head_rows(a, n):
        return jnp.concatenate([a[:n], a[SB_TILE:SB_TILE + n]], axis=0)

    def logits(q, keys):
        return jnp.minimum(_dot_nt(q, keys), LOGIT2_CLAMP)

    def window_keys(c, rows):
        p = chains[c][1]
        return kwin_ref[rows, p * LANES:(p + 1) * LANES]

    def window_values(c, rows):
        p = chains[c][1]
        return _split_heads(vwin_ref[rows, p * LANES:(p + 1) * LANES])

    def neg_log2_not_beta(z):
        return jnp.log(1.0 + jnp.exp2(z)) * LOG2_E

    def weights(z, s, diagonal=False):
        w = jnp.exp2(z - s)
        if diagonal:
            w = mask_diagonal(w)
        wb = w.astype(jnp.bfloat16)
        n = wb.shape[0] // 2
        return jnp.concatenate([wb[:n], wb[n:]], axis=1)

    def all_done(carries):
        floor = carries[0]
        for carry in carries[1:]:
            floor = jnp.minimum(floor, carry)
        return jnp.min(floor) >= SWEEP_STOP * LOG2_E

    def first_sweep():
        n = TILE2_ROWS
        rows_10 = [window_rows(c, 0, 2) for c in range(len(chains))]
        rows_2 = [window_rows(c, 2) for c in range(len(chains))]
        zs = [(logits(q_stack[c], window_keys(c, rows_10[c])),
               logits(head_rows(q_stack[c], n), window_keys(c, rows_2[c]))) for c in range(len(chains))]
        if fillers:
            fillers.pop(0)()
        lhs_10, lhs_2 = [], []
        for z10, z2 in zs:
            l10 = mask_diagonal(neg_log2_not_beta(z10))
            lhs_10.append(l10.astype(jnp.bfloat16))
            lhs_2.append(jnp.concatenate(
                [(head_rows(l10[:, :SB_TILE], n) + head_rows(l10[:, SB_TILE:], n)).astype(jnp.bfloat16),
                 neg_log2_not_beta(z2).astype(jnp.bfloat16)], axis=1))
        s_10 = jnp.dot(jnp.concatenate(lhs_10, axis=0), w_10, preferred_element_type=jnp.float32)
        s_2 = jnp.dot(jnp.concatenate(lhs_2, axis=0), w_2, preferred_element_type=jnp.float32)
        if fillers:
            fillers.pop(0)()
        reached, w_10s, w_2s = [], [], []
        for c, (z10, z2) in enumerate(zs):
            part = slice(c * 2 * SB_TILE, (c + 1) * 2 * SB_TILE)
            part2 = slice(c * 2 * n, (c + 1) * 2 * n)
            w_10s.append(weights(z10, s_10[part], diagonal=True))
            w_2s.append(weights(z2, s_2[part2, :SB_TILE]))
            after_1 = s_10[part, 0:1]
            after_2 = s_2[part2, SB_TILE:SB_TILE + 1]
            carry_ref[c] = after_1
            reached.append(jnp.concatenate([after_2[:n], after_1[n:SB_TILE],
                                            after_2[n:], after_1[SB_TILE + n:]], axis=0))
        done = all_done(reached)
        for c in range(len(chains)):
            acc = jnp.dot(w_10s[c], window_values(c, rows_10[c]), preferred_element_type=jnp.float32)
            tile_2 = jnp.dot(w_2s[c], window_values(c, rows_2[c]), preferred_element_type=jnp.float32)
            acc_ref[c, n:, :] = acc[n:]
            acc_ref[c, :n, :] = acc[:n] + jnp.where(done, tile_2, 0.0)
        return 2, done

    def tile_copies(m):
        copies = []
        for c, (sub, p) in enumerate(chains):
            row0 = pl.multiple_of(KV_PAD + (n_sub * i + sub - m) * SB_TILE, SB_TILE)
            for which, dst in ((0, ktile_ref), (1, vtile_ref)):
                copies.append(pltpu.make_async_copy(
                    kv_hbm.at[which, pl.ds(row0, SB_TILE), pl.ds(p * LANES, LANES)], dst.at[c],
                    tile_sems.at[which, c]))
        return copies

    def next_sweep(m):
        for copy in tile_copies(m):
            copy.start()
        for copy in tile_copies(m):
            copy.wait()
        zs = [logits(q_stack[c], ktile_ref[c]) for c in range(len(chains))]
        ls = [neg_log2_not_beta(z).astype(jnp.bfloat16) for z in zs]
        cs = jnp.dot(jnp.concatenate(ls, axis=0), w_1, preferred_element_type=jnp.float32)
        carries, w_cats = [], []
        for c, z in enumerate(zs):
            part = slice(c * 2 * SB_TILE, (c + 1) * 2 * SB_TILE)
            old = carry_ref[c]
            w_cats.append(weights(z, cs[part, :SB_TILE] + old))
            carries.append(cs[part, SB_TILE:SB_TILE + 1] + old)
            carry_ref[c] = carries[c]
        for c in range(len(chains)):
            acc_ref[c] += jnp.dot(w_cats[c], _split_heads(vtile_ref[c]), preferred_element_type=jnp.float32)
        return all_done(carries)

    m0, done0 = first_sweep()
    before_loop()
    last = n_sub * i + n_sub - 1

    def cond(state):
        m, done = state
        return jnp.logical_and(m <= last, jnp.logical_not(done))

    def body(state):
        m, _ = state
        return m + 1, next_sweep(m)

    lax.while_loop(cond, body, (jnp.int32(m0), done0))
    return jnp.concatenate(
        [jnp.concatenate([acc_ref[sub * n_pair + p] for p in range(n_pair)], axis=1)
         for sub in range(n_sub)], axis=0)


def _mem_attention(q_ref, mkv_ref, filler):
    q_rows = q_ref.shape[0]
    n_mem = mkv_ref.shape[0]
    n_pair = MEM_WIDTH // LANES
    head_ones = ((lax.broadcasted_iota(jnp.int32, (2 * n_mem, LANES), 1) < HEAD_DIM)
                 == (lax.broadcasted_iota(jnp.int32, (2 * n_mem, LANES), 0) < n_mem)
                 ).astype(jnp.float32).astype(jnp.bfloat16)
    logits = [_dot_nt(_split_heads(q_ref[:, SB_WIDTH + p * LANES:SB_WIDTH + (p + 1) * LANES]),
                      mkv_ref[:, p * LANES:(p + 1) * LANES]) for p in range(n_pair)]
    filler()
    outs = []
    for p, s in enumerate(logits):
        mv = mkv_ref[:, MEM_WIDTH + p * LANES:MEM_WIDTH + (p + 1) * LANES]
        e = jnp.exp2(s - jnp.max(s, axis=-1, keepdims=True)).astype(jnp.bfloat16)
        e_cat = jnp.concatenate([e[:q_rows], e[q_rows:]], axis=1)
        num_den = jnp.dot(e_cat, jnp.concatenate([_split_heads(mv), head_ones], axis=1),
                          preferred_element_type=jnp.float32)
        outs.append(num_den[:, :LANES] / num_den[:, LANES:])
    return jnp.concatenate(outs, axis=-1)


def _stream_chunks(jobs, stages, sems):
    depth = len(stages)

    def copy(j):
        return pltpu.make_async_copy(jobs[j][0], stages[j % depth], sems.at[j % depth])

    for j in range(min(depth - 1, len(jobs))):
        copy(j).start()
    for j, (_, consume) in enumerate(jobs):
        if j + depth - 1 < len(jobs):
            copy(j + depth - 1).start()
        copy(j).wait()
        consume(stages[j % depth])


def _chunks(hbm):
    _, n_rows, n_cols = hbm.shape
    return [(r0, c0, hbm.at[0, pl.ds(r0, STREAM_ROWS), pl.ds(c0, STREAM_COLS)])
            for r0 in range(0, n_rows, STREAM_ROWS) for c0 in range(0, n_cols, STREAM_COLS)]


def _load_parameters(mem_hbm, g_mem_ref, w_kv_hbm, w_in_hbm, w_out_hbm, mkv_ref, w_in_ref, w_out_ref,
                     stages, sems):
    jobs = []
    state = {"mem": [], "h": None, "kv": None}

    def take_mem(window):
        state["mem"].append(window[...])

    def memory_kv(window, r0):
        if state["h"] is None:
            m = jnp.concatenate(state["mem"], axis=1)
            state["h"] = (m * _rms_scale(m) * g_mem_ref[...]).astype(jnp.bfloat16)
        part = jnp.dot(state["h"][:, r0:r0 + STREAM_ROWS], window[...].astype(jnp.bfloat16),
                       preferred_element_type=jnp.float32)
        state["kv"] = part if state["kv"] is None else state["kv"] + part
        if r0 + STREAM_ROWS == w_kv_hbm.shape[1]:
            mkv_ref[...] = state["kv"].astype(jnp.bfloat16)

    def cast_into(dst, r0, c0):
        def consume(window):
            dst[r0:r0 + STREAM_ROWS, c0:c0 + STREAM_COLS] = window[...].astype(jnp.bfloat16)
        return consume

    assert mem_hbm.shape[1] == STREAM_ROWS and w_kv_hbm.shape[2] == STREAM_COLS
    jobs += [(chunk, take_mem) for _, _, chunk in _chunks(mem_hbm)]
    jobs += [(chunk, lambda window, r0=r0: memory_kv(window, r0)) for r0, _, chunk in _chunks(w_kv_hbm)]
    for hbm, dst in ((w_in_hbm, w_in_ref), (w_out_hbm, w_out_ref)):
        jobs += [(chunk, cast_into(dst, r0, c0)) for r0, c0, chunk in _chunks(hbm)]
    _stream_chunks(jobs, stages, sems)


def _layer_kernel(x_ref, mem_hbm, g_in_ref, w_in_hbm, convw_ref, convb_ref, g_memin_ref, w_kv_hbm,
                  g_sb_ref, g_conv_ref, g_mem_ref, w_out_hbm, g_final_ref,
                  out_ref, kv_hbm, w_in_ref, w_out_ref, mkv_ref, stream_sems, kwin_ref, vwin_ref, kv_next_ref,
                  ktile_ref, vtile_ref, kv_sems, tile_sems, q2_ref, rest2_ref, x_prev_ref, res_ref, cu_ref,
                  acc_ref, carry_ref):
    s = pl.program_id(0)
    n_blk = pl.num_programs(0) - 2
    q_rows = x_ref.shape[0]
    c_u = 3 * SB_WIDTH
    c_qm = c_u + 3 * CONV_WIDTH
    c_gate = c_qm + MEM_WIDTH

    def normed_input():
        x = x_ref[...]
        return (x * _rms_scale(x) * g_in_ref[...]).astype(jnp.bfloat16)

    def proj(h, c0, width):
        return jnp.dot(h, w_in_ref[:, c0:c0 + width], preferred_element_type=jnp.float32)

    def project_kv(h):
        kv_next_ref[...] = proj(h, SB_WIDTH, 2 * SB_WIDTH).astype(jnp.bfloat16)

    def project_q_conv(h, slot):
        q2_ref[slot, :, :SB_WIDTH] = (proj(h, 0, SB_WIDTH) * (SCALE * LOG2_E)).astype(jnp.bfloat16)
        q2_ref[slot, :, SB_WIDTH:] = (proj(h, c_qm, MEM_WIDTH) * (SCALE * LOG2_E)).astype(jnp.bfloat16)
        rest2_ref[slot, :, :3 * CONV_WIDTH] = proj(h, c_u, 3 * CONV_WIDTH)

    def project_gate(h, slot):
        for c0 in range(0, MIX_WIDTH, PROJ_COLS):
            rest2_ref[slot, :, 3 * CONV_WIDTH + c0:3 * CONV_WIDTH + c0 + PROJ_COLS] = proj(h, c_gate + c0, PROJ_COLS)

    def final_norm():
        res = res_ref[...]
        out_ref[...] = res * _rms_scale(res) * g_final_ref[...]

    @pl.when(s == 0)
    def _():
        stages = [ref.at[pl.ds(r0, STREAM_ROWS), pl.ds(c0, STREAM_COLS)] for ref in (x_prev_ref, rest2_ref.at[1])
                  for r0 in range(0, ref.shape[0] - STREAM_ROWS + 1, STREAM_ROWS)
                  for c0 in range(0, ref.shape[-1] - STREAM_COLS + 1, STREAM_COLS)]
        _load_parameters(mem_hbm, g_memin_ref, w_kv_hbm, w_in_hbm, w_out_hbm, mkv_ref, w_in_ref, w_out_ref,
                         stages[:STREAM_STAGES], stream_sems)
        h = normed_input()
        project_kv(h)
        project_q_conv(h, 0)
        project_gate(h, 0)
        x_prev_ref[...] = x_ref[...]
        res_ref[...] = jnp.zeros(res_ref.shape, jnp.float32)
        cu_ref[0:SUBLANES, :] = jnp.zeros((SUBLANES, CONV_WIDTH), jnp.float32)
        kwin_ref[...] = jnp.zeros(kwin_ref.shape, jnp.bfloat16)
        vwin_ref[...] = jnp.zeros(vwin_ref.shape, jnp.bfloat16)
        zero_rows = [pltpu.make_async_copy(win.at[pl.ds(0, KV_PAD)], kv_hbm.at[which, pl.ds(0, KV_PAD)],
                                           kv_sems.at[which])
                     for which, win in ((0, kwin_ref), (1, vwin_ref))]
        for copy in zero_rows:
            copy.start()
        for copy in zero_rows:
            copy.wait()

    @pl.when(s == n_blk + 1)
    def _():
        final_norm()

    @pl.when(jnp.logical_and(s > 0, s <= n_blk))
    def _():
        i = s - 1
        cur = lax.rem(i, 2)
        nxt = 1 - cur
        q_ref = q2_ref.at[cur]
        rest_ref = rest2_ref.at[cur]

        def history_writes(block):
            rows = pl.ds(pl.multiple_of(KV_PAD + block * q_rows, q_rows), q_rows)
            return [pltpu.make_async_copy(win.at[pl.ds(KV_PREV, q_rows)], kv_hbm.at[which, rows],
                                          kv_sems.at[which])
                    for which, win in ((0, kwin_ref), (1, vwin_ref))]

        for win, c0 in ((kwin_ref, 0), (vwin_ref, SB_WIDTH)):
            win[0:KV_PREV, :] = win[q_rows:q_rows + KV_PREV, :]
            win[KV_PREV:, :] = kv_next_ref[:, c0:c0 + SB_WIDTH]
        for copy in history_writes(i):
            copy.start()
        final_norm()

        h_next = normed_input()
        y_sb = _sb_sweep(i, q_ref, kwin_ref, vwin_ref, kv_hbm, ktile_ref, vtile_ref, tile_sems, acc_ref, carry_ref,
                         (lambda: project_kv(h_next), lambda: project_q_conv(h_next, nxt)),
                         lambda: [copy.wait() for copy in history_writes(i)])

        u = rest_ref[:, 0:CONV_WIDTH]
        b = rest_ref[:, CONV_WIDTH:2 * CONV_WIDTH]
        c = rest_ref[:, 2 * CONV_WIDTH:3 * CONV_WIDTH]
        cu = c * u
        cu_ref[SUBLANES:, :] = cu
        cu_1 = cu_ref[SUBLANES - 1:SUBLANES - 1 + q_rows, :]
        cu_2 = cu_ref[SUBLANES - 2:SUBLANES - 2 + q_rows, :]
        cu_ref[0:SUBLANES, :] = cu[q_rows - SUBLANES:, :]
        conv = (convw_ref[0] * cu_2 + convw_ref[1] * cu_1 + convw_ref[2] * cu
                + convb_ref[...])
        y_conv = b * conv

        y_mem = _mem_attention(q_ref, mkv_ref, lambda: project_gate(h_next, nxt))

        y = jnp.concatenate([y_sb * _rms_scale(y_sb) * g_sb_ref[...],
                             y_conv * _rms_scale(y_conv) * g_conv_ref[...],
                             y_mem * _rms_scale(y_mem) * g_mem_ref[...]], axis=-1)
        gate = rest_ref[:, 3 * CONV_WIDTH:]
        half_gate = 0.5 * gate
        gated = (y * (half_gate + half_gate * jnp.tanh(half_gate))).astype(jnp.bfloat16)
        res_ref[...] = x_prev_ref[...] + jnp.dot(gated, w_out_ref[...], preferred_element_type=jnp.float32)
        x_prev_ref[...] = x_ref[...]


def _layer(x2, mem, g_in, w_in, conv_w, conv_b, g_memin, w_kv, g_sb, g_conv, g_mem, w_out, g_final):
    t, d = x2.shape
    assert t % Q_BLOCK == 0 and Q_BLOCK % SB_TILE == 0 and KV_PREV <= Q_BLOCK
    assert Q_BLOCK >= STREAM_ROWS
    assert (Q_BLOCK // STREAM_ROWS) * (d // STREAM_COLS + (3 * CONV_WIDTH + MIX_WIDTH) // STREAM_COLS) >= STREAM_STAGES
    for a in (mem, w_kv, w_in, w_out):
        assert a.shape[1] % STREAM_ROWS == 0 and a.shape[2] % STREAM_COLS == 0
    n_chain = (Q_BLOCK // SB_TILE) * (SB_WIDTH // LANES)

    def whole(a):
        return pl.BlockSpec(a.shape, lambda s: (0,) * a.ndim)

    in_hbm = pl.BlockSpec(memory_space=pl.ANY)
    n_blk = t // Q_BLOCK
    return pl.pallas_call(
        _layer_kernel,
        grid=(n_blk + 2,),
        in_specs=[pl.BlockSpec((Q_BLOCK, d), lambda s: (jnp.minimum(s, n_blk - 1), 0)), in_hbm, whole(g_in),
                  in_hbm, whole(conv_w), whole(conv_b), whole(g_memin), in_hbm, whole(g_sb), whole(g_conv),
                  whole(g_mem), in_hbm, whole(g_final)],
        out_specs=[pl.BlockSpec((Q_BLOCK, d), lambda s: (jnp.clip(s - 2, 0, n_blk - 1), 0)), in_hbm],
        out_shape=[jax.ShapeDtypeStruct((t, d), jnp.float32),
                   jax.ShapeDtypeStruct((2, KV_PAD + t, SB_WIDTH), jnp.bfloat16)],
        scratch_shapes=[pltpu.VMEM(w_in.shape[1:], jnp.bfloat16),
                        pltpu.VMEM(w_out.shape[1:], jnp.bfloat16),
                        pltpu.VMEM((mem.shape[1], w_kv.shape[2]), jnp.bfloat16),
                        pltpu.SemaphoreType.DMA((STREAM_STAGES,)),
                        pltpu.VMEM((KV_PREV + Q_BLOCK, SB_WIDTH), jnp.bfloat16),
                        pltpu.VMEM((KV_PREV + Q_BLOCK, SB_WIDTH), jnp.bfloat16),
                        pltpu.VMEM((Q_BLOCK, 2 * SB_WIDTH), jnp.bfloat16),
                        pltpu.VMEM((n_chain, SB_TILE, LANES), jnp.bfloat16),
                        pltpu.VMEM((n_chain, SB_TILE, LANES), jnp.bfloat16),
                        pltpu.SemaphoreType.DMA((2,)),
                        pltpu.SemaphoreType.DMA((2, n_chain)),
                        pltpu.VMEM((2, Q_BLOCK, SB_WIDTH + MEM_WIDTH), jnp.bfloat16),
                        pltpu.VMEM((2, Q_BLOCK, 3 * CONV_WIDTH + MIX_WIDTH), jnp.float32),
                        pltpu.VMEM((Q_BLOCK, d), jnp.float32),
                        pltpu.VMEM((Q_BLOCK, d), jnp.float32),
                        pltpu.VMEM((SUBLANES + Q_BLOCK, CONV_WIDTH), jnp.float32),
                        pltpu.VMEM((n_chain, SB_TILE, LANES), jnp.float32),
                        pltpu.VMEM((n_chain, 2 * SB_TILE, 1), jnp.float32)],
        compiler_params=pltpu.CompilerParams(
            dimension_semantics=("arbitrary",), vmem_limit_bytes=VMEM_LIMIT_BYTES),
        name="layer",
    )(x2, mem, g_in, w_in, conv_w, conv_b, g_memin, w_kv, g_sb, g_conv, g_mem, w_out, g_final)[0]


def kernel(x, mem, g_in, w_in, conv_w, conv_b, g_mem, w_mem_kv, g_sb_out, g_conv_out,
           g_mem_out, w_out, g_final):
    batch, t, d = x.shape
    assert batch == 1 and g_in.shape[0] == 1
    out = _layer(x.reshape(t, d), mem, g_in[0][None, :], w_in, jnp.transpose(conv_w, (1, 0, 2)), conv_b[0][None, :],
                 g_mem[0][None, :], w_mem_kv, g_sb_out[0][None, :], g_conv_out[0][None, :],
                 g_mem_out[0][None, :], w_out, g_final[None, :])
    return out.reshape(batch, t, d)
```

```python
import jax
import jax.numpy as jnp
from jax import lax
from jax.experimental import pallas as pl
from jax.experimental.pallas import tpu as pltpu

HEAD_DIM = 64
SB_HEADS = 8
SB_WIDTH = SB_HEADS * HEAD_DIM
CONV_WIDTH = 4 * HEAD_DIM
MEM_HEADS = 4
MEM_WIDTH = MEM_HEADS * HEAD_DIM
CONV_K = 3
EPS = 1e-6
SCALE = HEAD_DIM ** -0.5

LANES = 128
SUBLANES = 8
MIX_WIDTH = SB_WIDTH + CONV_WIDTH + MEM_WIDTH
PROJ_COLS = 512
Q_BLOCK = 512
SB_TILE = 128
SWEEP_STOP = 104.0
TILE2_ROWS = 32
KV_PAD = Q_BLOCK
KV_PREV = 2 * SB_TILE
LOG2_E = 1.4426950408889634
LOGIT2_CLAMP = 126.0
STREAM_ROWS = 256
STREAM_COLS = 512
STREAM_STAGES = 9
VMEM_LIMIT_BYTES = 62 * 1024 * 1024


def _rms_scale(xf, eps=EPS):
    return lax.rsqrt(jnp.mean(xf * xf, axis=-1, keepdims=True) + eps)


def _dot_nt(a, b):
    return lax.dot_general(a, b, (((1,), (1,)), ((), ())), preferred_element_type=jnp.float32)


def _split_heads(a):
    first_half = lax.broadcasted_iota(jnp.int32, a.shape, 1) < HEAD_DIM
    zero = jnp.zeros_like(a)
    return jnp.concatenate([jnp.where(first_half, a, zero), jnp.where(first_half, zero, a)], axis=0)


def _sb_sweep(i, q_ref, kwin_ref, vwin_ref, kv_hbm, ktile_ref, vtile_ref, tile_sems, acc_ref, carry_ref, fillers,
              before_loop):
    n_sub = q_ref.shape[0] // SB_TILE
    n_pair = SB_WIDTH // LANES
    chains = [(sub, p) for sub in range(n_sub) for p in range(n_pair)]
    fillers = list(fillers)

    rt = lax.broadcasted_iota(jnp.int32, (2 * SB_TILE, 2 * SB_TILE), 0)
    ct = lax.broadcasted_iota(jnp.int32, (2 * SB_TILE, 2 * SB_TILE), 1)
    top, left = rt < SB_TILE, ct < SB_TILE
    tri2 = jnp.where(top, rt, rt - SB_TILE) >= jnp.where(left, ct, ct - SB_TILE)
    r2 = lax.broadcasted_iota(jnp.int32, (2 * SB_TILE, SB_TILE), 0)
    c2 = lax.broadcasted_iota(jnp.int32, (2 * SB_TILE, SB_TILE), 1)
    causal = c2 < jnp.where(r2 < SB_TILE, r2, r2 - SB_TILE)

    def mask_diagonal(a):
        return jnp.concatenate([a[:, :SB_TILE], jnp.where(causal, a[:, SB_TILE:], 0.0)], axis=1)

    def as_matrix(cond):
        return jnp.where(cond, 1.0, 0.0).astype(jnp.bfloat16)

    w_10 = as_matrix(jnp.logical_or(jnp.logical_and(jnp.logical_not(top), left),
                                    jnp.logical_and(tri2, jnp.logical_not(jnp.logical_xor(top, left)))))
    w_2 = as_matrix(jnp.logical_or(top, jnp.logical_or(tri2, jnp.logical_not(left))))
    w_1 = w_2[SB_TILE:]

    q_stack = [_split_heads(q_ref[sub * SB_TILE:(sub + 1) * SB_TILE, p * LANES:(p + 1) * LANES])
               for sub, p in chains]

    def window_rows(c, m, n_tiles=1):
        start = KV_PREV + (chains[c][0] - (m + n_tiles - 1)) * SB_TILE
        assert start >= 0
        return slice(start, start + n_tiles * SB_TILE)

    def head_rows(a, n):
        return jnp.concatenate([a[:n], a[SB_TILE:SB_TILE + n]], axis=0)

    def logits(q, keys):
        return jnp.minimum(_dot_nt(q, keys), LOGIT2_CLAMP)

    def window_keys(c, rows):
        p = chains[c][1]
        return kwin_ref[rows, p * LANES:(p + 1) * LANES]

    def window_values(c, rows):
        p = chains[c][1]
        return _split_heads(vwin_ref[rows, p * LANES:(p + 1) * LANES])

    def neg_log2_not_beta(z):
        return jnp.log(1.0 + jnp.exp2(z)) * LOG2_E

    def weights(z, s, diagonal=False):
        w = jnp.exp2(z - s)
        if diagonal:
            w = mask_diagonal(w)
        wb = w.astype(jnp.bfloat16)
        n = wb.shape[0] // 2
        return jnp.concatenate([wb[:n], wb[n:]], axis=1)

    def all_done(carries):
        floor = carries[0]
        for carry in carries[1:]:
            floor = jnp.minimum(floor, carry)
        return jnp.min(floor) >= SWEEP_STOP * LOG2_E

    def first_sweep():
        n = TILE2_ROWS
        rows_10 = [window_rows(c, 0, 2) for c in range(len(chains))]
        rows_2 = [window_rows(c, 2) for c in range(len(chains))]
        zs = [(logits(q_stack[c], window_keys(c, rows_10[c])),
               logits(head_rows(q_stack[c], n), window_keys(c, rows_2[c]))) for c in range(len(chains))]
        if fillers:
            fillers.pop(0)()
        lhs_10, lhs_2 = [], []
        for z10, z2 in zs:
            l10 = mask_diagonal(neg_log2_not_beta(z10))
            lhs_10.append(l10.astype(jnp.bfloat16))
            lhs_2.append(jnp.concatenate(
                [(head_rows(l10[:, :SB_TILE], n) + head_rows(l10[:, SB_TILE:], n)).astype(jnp.bfloat16),
                 neg_log2_not_beta(z2).astype(jnp.bfloat16)], axis=1))
        s_10 = jnp.dot(jnp.concatenate(lhs_10, axis=0), w_10, preferred_element_type=jnp.float32)
        s_2 = jnp.dot(jnp.concatenate(lhs_2, axis=0), w_2, preferred_element_type=jnp.float32)
        if fillers:
            fillers.pop(0)()
        reached, w_10s, w_2s = [], [], []
        for c, (z10, z2) in enumerate(zs):
            part = slice(c * 2 * SB_TILE, (c + 1) * 2 * SB_TILE)
            part2 = slice(c * 2 * n, (c + 1) * 2 * n)
            w_10s.append(weights(z10, s_10[part], diagonal=True))
            w_2s.append(weights(z2, s_2[part2, :SB_TILE]))
            after_1 = s_10[part, 0:1]
            after_2 = s_2[part2, SB_TILE:SB_TILE + 1]
            carry_ref[c] = after_1
            reached.append(jnp.concatenate([after_2[:n], after_1[n:SB_TILE],
                                            after_2[n:], after_1[SB_TILE + n:]], axis=0))
        done = all_done(reached)
        for c in range(len(chains)):
            acc = jnp.dot(w_10s[c], window_values(c, rows_10[c]), preferred_element_type=jnp.float32)
            tile_2 = jnp.dot(w_2s[c], window_values(c, rows_2[c]), preferred_element_type=jnp.float32)
            acc_ref[c, n:, :] = acc[n:]
            acc_ref[c, :n, :] = acc[:n] + jnp.where(done, tile_2, 0.0)
        return 2, done

    def tile_copies(m):
        copies = []
        for c, (sub, p) in enumerate(chains):
            row0 = pl.multiple_of(KV_PAD + (n_sub * i + sub - m) * SB_TILE, SB_TILE)
            for which, dst in ((0, ktile_ref), (1, vtile_ref)):
                copies.append(pltpu.make_async_copy(
                    kv_hbm.at[which, pl.ds(row0, SB_TILE), pl.ds(p * LANES, LANES)], dst.at[c],
                    tile_sems.at[which, c]))
        return copies

    def next_sweep(m):
        for copy in tile_copies(m):
            copy.start()
        for copy in tile_copies(m):
            copy.wait()
        zs = [logits(q_stack[c], ktile_ref[c]) for c in range(len(chains))]
        ls = [neg_log2_not_beta(z).astype(jnp.bfloat16) for z in zs]
        cs = jnp.dot(jnp.concatenate(ls, axis=0), w_1, preferred_element_type=jnp.float32)
        carries, w_cats = [], []
        for c, z in enumerate(zs):
            part = slice(c * 2 * SB_TILE, (c + 1) * 2 * SB_TILE)
            old = carry_ref[c]
            w_cats.append(weights(z, cs[part, :SB_TILE] + old))
            carries.append(cs[part, SB_TILE:SB_TILE + 1] + old)
            carry_ref[c] = carries[c]
        for c in range(len(chains)):
            acc_ref[c] += jnp.dot(w_cats[c], _split_heads(vtile_ref[c]), preferred_element_type=jnp.float32)
        return all_done(carries)

    m0, done0 = first_sweep()
    before_loop()
    last = n_sub * i + n_sub - 1

    def cond(state):
        m, done = state
        return jnp.logical_and(m <= last, jnp.logical_not(done))

    def body(state):
        m, _ = state
        return m + 1, next_sweep(m)

    lax.while_loop(cond, body, (jnp.int32(m0), done0))
    return jnp.concatenate(
        [jnp.concatenate([acc_ref[sub * n_pair + p] for p in range(n_pair)], axis=1)
         for sub in range(n_sub)], axis=0)


def _mem_attention(q_ref, mkv_ref, filler):
    q_rows = q_ref.shape[0]
    n_mem = mkv_ref.shape[0]
    n_pair = MEM_WIDTH // LANES
    head_ones = ((lax.broadcasted_iota(jnp.int32, (2 * n_mem, LANES), 1) < HEAD_DIM)
                 == (lax.broadcasted_iota(jnp.int32, (2 * n_mem, LANES), 0) < n_mem)
                 ).astype(jnp.float32).astype(jnp.bfloat16)
    logits = [_dot_nt(_split_heads(q_ref[:, SB_WIDTH + p * LANES:SB_WIDTH + (p + 1) * LANES]),
                      mkv_ref[:, p * LANES:(p + 1) * LANES]) for p in range(n_pair)]
    filler()
    outs = []
    for p, s in enumerate(logits):
        mv = mkv_ref[:, MEM_WIDTH + p * LANES:MEM_WIDTH + (p + 1) * LANES]
        e = jnp.exp(s - jnp.max(s, axis=-1, keepdims=True)).astype(jnp.bfloat16)
        e_cat = jnp.concatenate([e[:q_rows], e[q_rows:]], axis=1)
        num_den = jnp.dot(e_cat, jnp.concatenate([_split_heads(mv), head_ones], axis=1),
                          preferred_element_type=jnp.float32)
        outs.append(num_den[:, :LANES] / num_den[:, LANES:])
    return jnp.concatenate(outs, axis=-1)


def _stream_chunks(jobs, stages, sems):
    depth = len(stages)

    def copy(j):
        return pltpu.make_async_copy(jobs[j][0], stages[j % depth], sems.at[j % depth])

    for j in range(min(depth - 1, len(jobs))):
        copy(j).start()
    for j, (_, consume) in enumerate(jobs):
        if j + depth - 1 < len(jobs):
            copy(j + depth - 1).start()
        copy(j).wait()
        consume(stages[j % depth])


def _chunks(hbm):
    _, n_rows, n_cols = hbm.shape
    return [(r0, c0, hbm.at[0, pl.ds(r0, STREAM_ROWS), pl.ds(c0, STREAM_COLS)])
            for r0 in range(0, n_rows, STREAM_ROWS) for c0 in range(0, n_cols, STREAM_COLS)]


def _load_parameters(mem_hbm, g_mem_ref, w_kv_hbm, w_in_hbm, w_in_groups, w_out_hbm, mkv_ref, w_in_ref, w_out_ref,
                     stages, sems):
    jobs = []
    state = {"mem": [], "h": None, "kv": None}

    def take_mem(window):
        state["mem"].append(window[...])

    def memory_kv(window, r0):
        if state["h"] is None:
            m = jnp.concatenate(state["mem"], axis=1)
            state["h"] = (m * _rms_scale(m) * g_mem_ref[...]).astype(jnp.bfloat16)
        part = jnp.dot(state["h"][:, r0:r0 + STREAM_ROWS], window[...].astype(jnp.bfloat16),
                       preferred_element_type=jnp.float32)
        state["kv"] = part if state["kv"] is None else state["kv"] + part
        if r0 + STREAM_ROWS == w_kv_hbm.shape[1]:
            mkv_ref[...] = state["kv"].astype(jnp.bfloat16)

    def cast_into(dst, r0, c0, then=None):
        def consume(window):
            dst[r0:r0 + STREAM_ROWS, c0:c0 + STREAM_COLS] = window[...].astype(jnp.bfloat16)
            if then is not None:
                then()
        return consume

    assert mem_hbm.shape[1] == STREAM_ROWS and w_kv_hbm.shape[2] == STREAM_COLS
    jobs += [(chunk, take_mem) for _, _, chunk in _chunks(mem_hbm)]
    jobs += [(chunk, lambda window, r0=r0: memory_kv(window, r0)) for r0, _, chunk in _chunks(w_kv_hbm)]
    assert sorted(c0 for columns, _ in w_in_groups for c0 in columns) == list(range(0, w_in_hbm.shape[2], STREAM_COLS))
    for columns, then in w_in_groups:
        group = [job for job in _chunks(w_in_hbm) if job[1] in columns]
        jobs += [(chunk, cast_into(w_in_ref, r0, c0, then if n + 1 == len(group) else None))
                 for n, (r0, c0, chunk) in enumerate(group)]
    jobs += [(chunk, cast_into(w_out_ref, r0, c0)) for r0, c0, chunk in _chunks(w_out_hbm)]
    _stream_chunks(jobs, stages, sems)


def _layer_kernel(x_ref, mem_hbm, g_in_ref, w_in_hbm, convw_ref, convb_ref, g_memin_ref, w_kv_hbm,
                  g_sb_ref, g_conv_ref, g_mem_ref, w_out_hbm, g_final_ref,
                  out_ref, kv_hbm, w_in_ref, w_out_ref, mkv_ref, stream_sems, kwin_ref, vwin_ref, kv_next_ref,
                  ktile_ref, vtile_ref, kv_sems, tile_sems, q2_ref, rest2_ref, x_prev_ref, res_ref, cu_ref,
                  acc_ref, carry_ref):
    s = pl.program_id(0)
    n_blk = pl.num_programs(0) - 2
    q_rows = x_ref.shape[0]
    c_u = 3 * SB_WIDTH
    c_qm = c_u + 3 * CONV_WIDTH
    c_gate = c_qm + MEM_WIDTH

    def normed_input():
        x = x_ref[...]
        return (x * _rms_scale(x) * g_in_ref[...]).astype(jnp.bfloat16)

    def proj(h, c0, width):
        return jnp.dot(h, w_in_ref[:, c0:c0 + width], preferred_element_type=jnp.float32)

    def project_kv(h):
        kv_next_ref[...] = proj(h, SB_WIDTH, 2 * SB_WIDTH).astype(jnp.bfloat16)

    def project_q_conv(h, slot):
        q2_ref[slot, :, :SB_WIDTH] = (proj(h, 0, SB_WIDTH) * (SCALE * LOG2_E)).astype(jnp.bfloat16)
        q2_ref[slot, :, SB_WIDTH:] = (proj(h, c_qm, MEM_WIDTH) * SCALE).astype(jnp.bfloat16)
        rest2_ref[slot, :, :3 * CONV_WIDTH] = proj(h, c_u, 3 * CONV_WIDTH)

    def project_gate(h, slot):
        for c0 in range(0, MIX_WIDTH, PROJ_COLS):
            rest2_ref[slot, :, 3 * CONV_WIDTH + c0:3 * CONV_WIDTH + c0 + PROJ_COLS] = proj(h, c_gate + c0, PROJ_COLS)

    def final_norm():
        res = res_ref[...]
        out_ref[...] = res * _rms_scale(res) * g_final_ref[...]

    @pl.when(s == 0)
    def _():
        stages = [ref.at[pl.ds(r0, STREAM_ROWS), pl.ds(c0, STREAM_COLS)] for ref in (x_prev_ref, rest2_ref.at[1])
                  for r0 in range(0, ref.shape[0] - STREAM_ROWS + 1, STREAM_ROWS)
                  for c0 in range(0, ref.shape[-1] - STREAM_COLS + 1, STREAM_COLS)]
        h = normed_input()
        w_in_groups = [(range(SB_WIDTH, 3 * SB_WIDTH, STREAM_COLS), lambda: project_kv(h)),
                       ([0, *range(c_u, c_gate, STREAM_COLS)], lambda: project_q_conv(h, 0)),
                       (range(c_gate, c_gate + MIX_WIDTH, STREAM_COLS), lambda: project_gate(h, 0))]
        _load_parameters(mem_hbm, g_memin_ref, w_kv_hbm, w_in_hbm, w_in_groups, w_out_hbm, mkv_ref, w_in_ref,
                         w_out_ref, stages[:STREAM_STAGES], stream_sems)
        x_prev_ref[...] = x_ref[...]
        res_ref[...] = jnp.zeros(res_ref.shape, jnp.float32)
        cu_ref[0:SUBLANES, :] = jnp.zeros((SUBLANES, CONV_WIDTH), jnp.float32)
        kwin_ref[...] = jnp.zeros(kwin_ref.shape, jnp.bfloat16)
        vwin_ref[...] = jnp.zeros(vwin_ref.shape, jnp.bfloat16)
        zero_rows = [pltpu.make_async_copy(win.at[pl.ds(0, KV_PAD)], kv_hbm.at[which, pl.ds(0, KV_PAD)],
                                           kv_sems.at[which])
                     for which, win in ((0, kwin_ref), (1, vwin_ref))]
        for copy in zero_rows:
            copy.start()
        for copy in zero_rows:
            copy.wait()

    @pl.when(s == n_blk + 1)
    def _():
        final_norm()

    @pl.when(jnp.logical_and(s > 0, s <= n_blk))
    def _():
        i = s - 1
        cur = lax.rem(i, 2)
        nxt = 1 - cur
        q_ref = q2_ref.at[cur]
        rest_ref = rest2_ref.at[cur]

        def history_writes(block):
            rows = pl.ds(pl.multiple_of(KV_PAD + block * q_rows, q_rows), q_rows)
            return [pltpu.make_async_copy(win.at[pl.ds(KV_PREV, q_rows)], kv_hbm.at[which, rows],
                                          kv_sems.at[which])
                    for which, win in ((0, kwin_ref), (1, vwin_ref))]

        for win, c0 in ((kwin_ref, 0), (vwin_ref, SB_WIDTH)):
            win[0:KV_PREV, :] = win[q_rows:q_rows + KV_PREV, :]
            win[KV_PREV:, :] = kv_next_ref[:, c0:c0 + SB_WIDTH]
        for copy in history_writes(i):
            copy.start()
        final_norm()

        h_next = normed_input()
        y_sb = _sb_sweep(i, q_ref, kwin_ref, vwin_ref, kv_hbm, ktile_ref, vtile_ref, tile_sems, acc_ref, carry_ref,
                         (lambda: project_kv(h_next), lambda: project_q_conv(h_next, nxt)),
                         lambda: [copy.wait() for copy in history_writes(i)])

        u = rest_ref[:, 0:CONV_WIDTH]
        b = rest_ref[:, CONV_WIDTH:2 * CONV_WIDTH]
        c = rest_ref[:, 2 * CONV_WIDTH:3 * CONV_WIDTH]
        cu = c * u
        cu_ref[SUBLANES:, :] = cu
        cu_1 = cu_ref[SUBLANES - 1:SUBLANES - 1 + q_rows, :]
        cu_2 = cu_ref[SUBLANES - 2:SUBLANES - 2 + q_rows, :]
        cu_ref[0:SUBLANES, :] = cu[q_rows - SUBLANES:, :]
        conv = (convw_ref[0] * cu_2 + convw_ref[1] * cu_1 + convw_ref[2] * cu
                + convb_ref[...])
        y_conv = b * conv

        y_mem = _mem_attention(q_ref, mkv_ref, lambda: project_gate(h_next, nxt))

        y = jnp.concatenate([y_sb * _rms_scale(y_sb) * g_sb_ref[...],
                             y_conv * _rms_scale(y_conv) * g_conv_ref[...],
                             y_mem * _rms_scale(y_mem) * g_mem_ref[...]], axis=-1)
        gate = rest_ref[:, 3 * CONV_WIDTH:]
        half_gate = 0.5 * gate
        gated = (y * (half_gate + half_gate * jnp.tanh(half_gate))).astype(jnp.bfloat16)
        res_ref[...] = x_prev_ref[...] + jnp.dot(gated, w_out_ref[...], preferred_element_type=jnp.float32)
        x_prev_ref[...] = x_ref[...]


def _layer(x2, mem, g_in, w_in, conv_w, conv_b, g_memin, w_kv, g_sb, g_conv, g_mem, w_out, g_final):
    t, d = x2.shape
    assert t % Q_BLOCK == 0 and Q_BLOCK % SB_TILE == 0 and KV_PREV <= Q_BLOCK
    assert Q_BLOCK >= STREAM_ROWS
    assert (Q_BLOCK // STREAM_ROWS) * (d // STREAM_COLS + (3 * CONV_WIDTH + MIX_WIDTH) // STREAM_COLS) >= STREAM_STAGES
    for a in (mem, w_kv, w_in, w_out):
        assert a.shape[1] % STREAM_ROWS == 0 and a.shape[2] % STREAM_COLS == 0
    n_chain = (Q_BLOCK // SB_TILE) * (SB_WIDTH // LANES)

    def whole(a):
        return pl.BlockSpec(a.shape, lambda s: (0,) * a.ndim)

    in_hbm = pl.BlockSpec(memory_space=pl.ANY)
    n_blk = t // Q_BLOCK
    return pl.pallas_call(
        _layer_kernel,
        grid=(n_blk + 2,),
        in_specs=[pl.BlockSpec((Q_BLOCK, d), lambda s: (jnp.minimum(s, n_blk - 1), 0)), in_hbm, whole(g_in),
                  in_hbm, whole(conv_w), whole(conv_b), whole(g_memin), in_hbm, whole(g_sb), whole(g_conv),
                  whole(g_mem), in_hbm, whole(g_final)],
        out_specs=[pl.BlockSpec((Q_BLOCK, d), lambda s: (jnp.clip(s - 2, 0, n_blk - 1), 0)), in_hbm],
        out_shape=[jax.ShapeDtypeStruct((t, d), jnp.float32),
                   jax.ShapeDtypeStruct((2, KV_PAD + t, SB_WIDTH), jnp.bfloat16)],
        scratch_shapes=[pltpu.VMEM(w_in.shape[1:], jnp.bfloat16),
                        pltpu.VMEM(w_out.shape[1:], jnp.bfloat16),
                        pltpu.VMEM((mem.shape[1], w_kv.shape[2]), jnp.bfloat16),
                        pltpu.SemaphoreType.DMA((STREAM_STAGES,)),
                        pltpu.VMEM((KV_PREV + Q_BLOCK, SB_WIDTH), jnp.bfloat16),
                        pltpu.VMEM((KV_PREV + Q_BLOCK, SB_WIDTH), jnp.bfloat16),
                        pltpu.VMEM((Q_BLOCK, 2 * SB_WIDTH), jnp.bfloat16),
                        pltpu.VMEM((n_chain, SB_TILE, LANES), jnp.bfloat16),
                        pltpu.VMEM((n_chain, SB_TILE, LANES), jnp.bfloat16),
                        pltpu.SemaphoreType.DMA((2,)),
                        pltpu.SemaphoreType.DMA((2, n_chain)),
                        pltpu.VMEM((2, Q_BLOCK, SB_WIDTH + MEM_WIDTH), jnp.bfloat16),
                        pltpu.VMEM((2, Q_BLOCK, 3 * CONV_WIDTH + MIX_WIDTH), jnp.float32),
                        pltpu.VMEM((Q_BLOCK, d), jnp.float32),
                        pltpu.VMEM((Q_BLOCK, d), jnp.float32),
                        pltpu.VMEM((SUBLANES + Q_BLOCK, CONV_WIDTH), jnp.float32),
                        pltpu.VMEM((n_chain, SB_TILE, LANES), jnp.float32),
                        pltpu.VMEM((n_chain, 2 * SB_TILE, 1), jnp.float32)],
        compiler_params=pltpu.CompilerParams(
            dimension_semantics=("arbitrary",), vmem_limit_bytes=VMEM_LIMIT_BYTES),
        name="layer",
    )(x2, mem, g_in, w_in, conv_w, conv_b, g_memin, w_kv, g_sb, g_conv, g_mem, w_out, g_final)[0]


def kernel(x, mem, g_in, w_in, conv_w, conv_b, g_mem, w_mem_kv, g_sb_out, g_conv_out,
           g_mem_out, w_out, g_final):
    batch, t, d = x.shape
    assert batch == 1 and g_in.shape[0] == 1
    out = _layer(x.reshape(t, d), mem, g_in[0][None, :], w_in, jnp.transpose(conv_w, (1, 0, 2)), conv_b[0][None, :],
                 g_mem[0][None, :], w_mem_kv, g_sb_out[0][None, :], g_conv_out[0][None, :],
                 g_mem_out[0][None, :], w_out, g_final[None, :])
    return out.reshape(batch, t, d)
```

```python
import jax
import jax.numpy as jnp
from jax import lax
from jax.experimental import pallas as pl
from jax.experimental.pallas import tpu as pltpu

HEAD_DIM = 64
SB_HEADS = 8
SB_WIDTH = SB_HEADS * HEAD_DIM
CONV_WIDTH = 4 * HEAD_DIM
MEM_HEADS = 4
MEM_WIDTH = MEM_HEADS * HEAD_DIM
CONV_K = 3
EPS = 1e-6
SCALE = HEAD_DIM ** -0.5

LANES = 128
SUBLANES = 8
MIX_WIDTH = SB_WIDTH + CONV_WIDTH + MEM_WIDTH
PROJ_COLS = 512
Q_BLOCK = 512
SB_TILE = 128
SWEEP_STOP = 104.0
TILE2_ROWS = 32
KV_PAD = Q_BLOCK
KV_PREV = 2 * SB_TILE
LOG2_E = 1.4426950408889634
LOGIT2_CLAMP = 126.0
STREAM_ROWS = 256
STREAM_COLS = 512
STREAM_STAGES = 9
VMEM_LIMIT_BYTES = 62 * 1024 * 1024


def _rms_scale(xf, eps=EPS):
    return lax.rsqrt(jnp.mean(xf * xf, axis=-1, keepdims=True) + eps)


def _dot_nt(a, b):
    return lax.dot_general(a, b, (((1,), (1,)), ((), ())), preferred_element_type=jnp.float32)


def _split_heads(a):
    first_half = lax.broadcasted_iota(jnp.int32, a.shape, 1) < HEAD_DIM
    zero = jnp.zeros_like(a)
    return jnp.concatenate([jnp.where(first_half, a, zero), jnp.where(first_half, zero, a)], axis=0)


def _sb_sweep(i, q_ref, kwin_ref, vwin_ref, kv_hbm, ktile_ref, vtile_ref, tile_sems, acc_ref, carry_ref, fillers,
              before_loop):
    n_sub = q_ref.shape[0] // SB_TILE
    n_pair = SB_WIDTH // LANES
    chains = [(sub, p) for sub in range(n_sub) for p in range(n_pair)]
    fillers = list(fillers)

    rt = lax.broadcasted_iota(jnp.int32, (2 * SB_TILE, 2 * SB_TILE), 0)
    ct = lax.broadcasted_iota(jnp.int32, (2 * SB_TILE, 2 * SB_TILE), 1)
    top, left = rt < SB_TILE, ct < SB_TILE
    tri2 = jnp.where(top, rt, rt - SB_TILE) >= jnp.where(left, ct, ct - SB_TILE)
    r2 = lax.broadcasted_iota(jnp.int32, (2 * SB_TILE, SB_TILE), 0)
    c2 = lax.broadcasted_iota(jnp.int32, (2 * SB_TILE, SB_TILE), 1)
    causal = c2 < jnp.where(r2 < SB_TILE, r2, r2 - SB_TILE)

    def mask_diagonal(a):
        return jnp.concatenate([a[:, :SB_TILE], jnp.where(causal, a[:, SB_TILE:], 0.0)], axis=1)

    def as_matrix(cond):
        return jnp.where(cond, 1.0, 0.0).astype(jnp.bfloat16)

    w_10 = as_matrix(jnp.logical_or(jnp.logical_and(jnp.logical_not(top), left),
                                    jnp.logical_and(tri2, jnp.logical_not(jnp.logical_xor(top, left)))))
    w_2 = as_matrix(jnp.logical_or(top, jnp.logical_or(tri2, jnp.logical_not(left))))
    w_1 = w_2[SB_TILE:]

    q_stack = [_split_heads(q_ref[sub * SB_TILE:(sub + 1) * SB_TILE, p * LANES:(p + 1) * LANES])
               for sub, p in chains]

    def window_rows(c, m, n_tiles=1):
        start = KV_PREV + (chains[c][0] - (m + n_tiles - 1)) * SB_TILE
        assert start >= 0
        return slice(start, start + n_tiles * SB_TILE)

    def head_rows(a, n):
        return jnp.concatenate([a[:n], a[SB_TILE:SB_TILE + n]], axis=0)

    def logits(q, keys):
        return jnp.minimum(_dot_nt(q, keys), LOGIT2_CLAMP)

    def window_keys(c, rows):
        p = chains[c][1]
        return kwin_ref[rows, p * LANES:(p + 1) * LANES]

    def window_values(c, rows):
        p = chains[c][1]
        return _split_heads(vwin_ref[rows, p * LANES:(p + 1) * LANES])

    def neg_log2_not_beta(z):
        return jnp.log(1.0 + jnp.exp2(z)) * LOG2_E

    def weights(z, s, diagonal=False):
        w = jnp.exp2(z - s)
        if diagonal:
            w = mask_diagonal(w)
        wb = w.astype(jnp.bfloat16)
        n = wb.shape[0] // 2
        return jnp.concatenate([wb[:n], wb[n:]], axis=1)

    def all_done(carries):
        floor = carries[0]
        for carry in carries[1:]:
            floor = jnp.minimum(floor, carry)
        return jnp.min(floor) >= SWEEP_STOP * LOG2_E

    def first_sweep():
        n = TILE2_ROWS
        rows_10 = [window_rows(c, 0, 2) for c in range(len(chains))]
        rows_2 = [window_rows(c, 2) for c in range(len(chains))]
        zs = [(logits(q_stack[c], window_keys(c, rows_10[c])),
               logits(head_rows(q_stack[c], n), window_keys(c, rows_2[c]))) for c in range(len(chains))]
        if fillers:
            fillers.pop(0)()
        lhs_10, lhs_2 = [], []
        for z10, z2 in zs:
            l10 = mask_diagonal(neg_log2_not_beta(z10))
            lhs_10.append(l10.astype(jnp.bfloat16))
            lhs_2.append(jnp.concatenate(
                [(head_rows(l10[:, :SB_TILE], n) + head_rows(l10[:, SB_TILE:], n)).astype(jnp.bfloat16),
                 neg_log2_not_beta(z2).astype(jnp.bfloat16)], axis=1))
        s_10 = jnp.dot(jnp.concatenate(lhs_10, axis=0), w_10, preferred_element_type=jnp.float32)
        s_2 = jnp.dot(jnp.concatenate(lhs_2, axis=0), w_2, preferred_element_type=jnp.float32)
        if fillers:
            fillers.pop(0)()
        reached, w_10s, w_2s = [], [], []
        for c, (z10, z2) in enumerate(zs):
            part = slice(c * 2 * SB_TILE, (c + 1) * 2 * SB_TILE)
            part2 = slice(c * 2 * n, (c + 1) * 2 * n)
            w_10s.append(weights(z10, s_10[part], diagonal=True))
            w_2s.append(weights(z2, s_2[part2, :SB_TILE]))
            after_1 = s_10[part, 0:1]
            after_2 = s_2[part2, SB_TILE:SB_TILE + 1]
            carry_ref[c] = after_1
            reached.append(jnp.concatenate([after_2[:n], after_1[n:SB_TILE],
                                            after_2[n:], after_1[SB_TILE + n:]], axis=0))
        done = all_done(reached)
        for c in range(len(chains)):
            acc = jnp.dot(w_10s[c], window_values(c, rows_10[c]), preferred_element_type=jnp.float32)
            tile_2 = jnp.dot(w_2s[c], window_values(c, rows_2[c]), preferred_element_type=jnp.float32)
            acc_ref[c, n:, :] = acc[n:]
            acc_ref[c, :n, :] = acc[:n] + jnp.where(done, tile_2, 0.0)
        return 2, done

    def tile_copies(m):
        copies = []
        for c, (sub, p) in enumerate(chains):
            row0 = pl.multiple_of(KV_PAD + (n_sub * i + sub - m) * SB_TILE, SB_TILE)
            for which, dst in ((0, ktile_ref), (1, vtile_ref)):
                copies.append(pltpu.make_async_copy(
                    kv_hbm.at[which, pl.ds(row0, SB_TILE), pl.ds(p * LANES, LANES)], dst.at[c],
                    tile_sems.at[which, c]))
        return copies

    def next_sweep(m):
        for copy in tile_copies(m):
            copy.start()
        for copy in tile_copies(m):
            copy.wait()
        zs = [logits(q_stack[c], ktile_ref[c]) for c in range(len(chains))]
        ls = [neg_log2_not_beta(z).astype(jnp.bfloat16) for z in zs]
        cs = jnp.dot(jnp.concatenate(ls, axis=0), w_1, preferred_element_type=jnp.float32)
        carries, w_cats = [], []
        for c, z in enumerate(zs):
            part = slice(c * 2 * SB_TILE, (c + 1) * 2 * SB_TILE)
            old = carry_ref[c]
            w_cats.append(weights(z, cs[part, :SB_TILE] + old))
            carries.append(cs[part, SB_TILE:SB_TILE + 1] + old)
            carry_ref[c] = carries[c]
        for c in range(len(chains)):
            acc_ref[c] += jnp.dot(w_cats[c], _split_heads(vtile_ref[c]), preferred_element_type=jnp.float32)
        return all_done(carries)

    m0, done0 = first_sweep()
    before_loop()
    last = n_sub * i + n_sub - 1

    def cond(state):
        m, done = state
        return jnp.logical_and(m <= last, jnp.logical_not(done))

    def body(state):
        m, _ = state
        return m + 1, next_sweep(m)

    lax.while_loop(cond, body, (jnp.int32(m0), done0))
    return jnp.concatenate(
        [jnp.concatenate([acc_ref[sub * n_pair + p] for p in range(n_pair)], axis=1)
         for sub in range(n_sub)], axis=0)


def _mem_attention(q_ref, mkv_ref, filler):
    q_rows = q_ref.shape[0]
    n_mem = mkv_ref.shape[0]
    n_pair = MEM_WIDTH // LANES
    head_ones = ((lax.broadcasted_iota(jnp.int32, (2 * n_mem, LANES), 1) < HEAD_DIM)
                 == (lax.broadcasted_iota(jnp.int32, (2 * n_mem, LANES), 0) < n_mem)
                 ).astype(jnp.float32).astype(jnp.bfloat16)
    logits = [_dot_nt(_split_heads(q_ref[:, SB_WIDTH + p * LANES:SB_WIDTH + (p + 1) * LANES]),
                      mkv_ref[:, p * LANES:(p + 1) * LANES]) for p in range(n_pair)]
    filler()
    outs = []
    for p, s in enumerate(logits):
        mv = mkv_ref[:, MEM_WIDTH + p * LANES:MEM_WIDTH + (p + 1) * LANES]
        e = jnp.exp(s - jnp.max(s, axis=-1, keepdims=True)).astype(jnp.bfloat16)
        e_cat = jnp.concatenate([e[:q_rows], e[q_rows:]], axis=1)
        num_den = jnp.dot(e_cat, jnp.concatenate([_split_heads(mv), head_ones], axis=1),
                          preferred_element_type=jnp.float32)
        outs.append(num_den[:, :LANES] / num_den[:, LANES:])
    return jnp.concatenate(outs, axis=-1)


def _stream_chunks(jobs, stages, sems):
    depth = len(stages)

    def copy(j):
        return pltpu.make_async_copy(jobs[j][0], stages[j % depth], sems.at[j % depth])

    for j in range(min(depth - 1, len(jobs))):
        copy(j).start()
    for j, (_, consume) in enumerate(jobs):
        if j + depth - 1 < len(jobs):
            copy(j + depth - 1).start()
        copy(j).wait()
        consume(stages[j % depth])


def _chunks(hbm):
    _, n_rows, n_cols = hbm.shape
    return [(r0, c0, hbm.at[0, pl.ds(r0, STREAM_ROWS), pl.ds(c0, STREAM_COLS)])
            for r0 in range(0, n_rows, STREAM_ROWS) for c0 in range(0, n_cols, STREAM_COLS)]


def _load_parameters(mem_hbm, g_mem_ref, w_kv_hbm, w_in_hbm, w_in_groups, w_out_hbm, mkv_ref, w_in_ref, w_out_ref,
                     stages, sems):
    jobs = []
    state = {"mem": [], "h": None, "kv": None}

    def take_mem(window):
        state["mem"].append(window[...])

    def memory_kv(window, r0):
        if state["h"] is None:
            m = jnp.concatenate(state["mem"], axis=1)
            state["h"] = (m * _rms_scale(m) * g_mem_ref[...]).astype(jnp.bfloat16)
        part = jnp.dot(state["h"][:, r0:r0 + STREAM_ROWS], window[...].astype(jnp.bfloat16),
                       preferred_element_type=jnp.float32)
        state["kv"] = part if state["kv"] is None else state["kv"] + part
        if r0 + STREAM_ROWS == w_kv_hbm.shape[1]:
            mkv_ref[...] = state["kv"].astype(jnp.bfloat16)

    def cast_into(dst, r0, c0, then=None):
        def consume(window):
            dst[r0:r0 + STREAM_ROWS, c0:c0 + STREAM_COLS] = window[...].astype(jnp.bfloat16)
            if then is not None:
                then()
        return consume

    assert mem_hbm.shape[1] == STREAM_ROWS and w_kv_hbm.shape[2] == STREAM_COLS
    jobs += [(chunk, take_mem) for _, _, chunk in _chunks(mem_hbm)]
    jobs += [(chunk, lambda window, r0=r0: memory_kv(window, r0)) for r0, _, chunk in _chunks(w_kv_hbm)]
    assert sorted(c0 for columns, _ in w_in_groups for c0 in columns) == list(range(0, w_in_hbm.shape[2], STREAM_COLS))
    for columns, then in w_in_groups:
        group = [job for job in _chunks(w_in_hbm) if job[1] in columns]
        jobs += [(chunk, cast_into(w_in_ref, r0, c0, then if n + 1 == len(group) else None))
                 for n, (r0, c0, chunk) in enumerate(group)]
    jobs += [(chunk, cast_into(w_out_ref, r0, c0)) for r0, c0, chunk in _chunks(w_out_hbm)]
    _stream_chunks(jobs, stages, sems)


def _layer_kernel(x_ref, x_res_ref, mem_hbm, g_in_ref, w_in_hbm, convw_ref, convb_ref, g_memin_ref, w_kv_hbm,
                  g_sb_ref, g_conv_ref, g_mem_ref, w_out_hbm, g_final_ref,
                  out_ref, kv_hbm, w_in_ref, w_out_ref, mkv_ref, stream_sems, kwin_ref, vwin_ref, kv_next_ref,
                  ktile_ref, vtile_ref, kv_sems, tile_sems, q2_ref, rest2_ref, res_ref, cu_ref,
                  acc_ref, carry_ref):
    s = pl.program_id(0)
    n_blk = pl.num_programs(0) - 2
    q_rows = x_ref.shape[0]
    c_u = 3 * SB_WIDTH
    c_qm = c_u + 3 * CONV_WIDTH
    c_gate = c_qm + MEM_WIDTH

    def normed_input():
        x = x_ref[...]
        return (x * _rms_scale(x) * g_in_ref[...]).astype(jnp.bfloat16)

    def proj(h, c0, width):
        return jnp.dot(h, w_in_ref[:, c0:c0 + width], preferred_element_type=jnp.float32)

    def project_kv(h):
        kv_next_ref[...] = proj(h, SB_WIDTH, 2 * SB_WIDTH).astype(jnp.bfloat16)

    def project_q_conv(h, slot):
        q2_ref[slot, :, :SB_WIDTH] = (proj(h, 0, SB_WIDTH) * (SCALE * LOG2_E)).astype(jnp.bfloat16)
        q2_ref[slot, :, SB_WIDTH:] = (proj(h, c_qm, MEM_WIDTH) * SCALE).astype(jnp.bfloat16)
        rest2_ref[slot, :, :3 * CONV_WIDTH] = proj(h, c_u, 3 * CONV_WIDTH)

    def project_gate(h, slot):
        for c0 in range(0, MIX_WIDTH, PROJ_COLS):
            rest2_ref[slot, :, 3 * CONV_WIDTH + c0:3 * CONV_WIDTH + c0 + PROJ_COLS] = proj(h, c_gate + c0, PROJ_COLS)

    def final_norm():
        res = res_ref[...]
        out_ref[...] = res * _rms_scale(res) * g_final_ref[...]

    @pl.when(s == 0)
    def _():
        stages = [ref.at[pl.ds(r0, STREAM_ROWS), pl.ds(c0, STREAM_COLS)] for ref in (res_ref, rest2_ref.at[1])
                  for r0 in range(0, ref.shape[0] - STREAM_ROWS + 1, STREAM_ROWS)
                  for c0 in range(0, ref.shape[-1] - STREAM_COLS + 1, STREAM_COLS)]
        h = normed_input()
        w_in_groups = [(range(SB_WIDTH, 3 * SB_WIDTH, STREAM_COLS), lambda: project_kv(h)),
                       ([0, *range(c_u, c_gate, STREAM_COLS)], lambda: project_q_conv(h, 0)),
                       (range(c_gate, c_gate + MIX_WIDTH, STREAM_COLS), lambda: project_gate(h, 0))]
        _load_parameters(mem_hbm, g_memin_ref, w_kv_hbm, w_in_hbm, w_in_groups, w_out_hbm, mkv_ref, w_in_ref,
                         w_out_ref, stages[:STREAM_STAGES], stream_sems)
        res_ref[...] = jnp.zeros(res_ref.shape, jnp.float32)
        cu_ref[0:SUBLANES, :] = jnp.zeros((SUBLANES, CONV_WIDTH), jnp.float32)
        kwin_ref[...] = jnp.zeros(kwin_ref.shape, jnp.bfloat16)
        vwin_ref[...] = jnp.zeros(vwin_ref.shape, jnp.bfloat16)
        zero_rows = [pltpu.make_async_copy(win.at[pl.ds(0, KV_PAD)], kv_hbm.at[which, pl.ds(0, KV_PAD)],
                                           kv_sems.at[which])
                     for which, win in ((0, kwin_ref), (1, vwin_ref))]
        for copy in zero_rows:
            copy.start()
        for copy in zero_rows:
            copy.wait()

    @pl.when(s == n_blk + 1)
    def _():
        final_norm()

    @pl.when(jnp.logical_and(s > 0, s <= n_blk))
    def _():
        i = s - 1
        cur = lax.rem(i, 2)
        nxt = 1 - cur
        q_ref = q2_ref.at[cur]
        rest_ref = rest2_ref.at[cur]

        def history_writes(block):
            rows = pl.ds(pl.multiple_of(KV_PAD + block * q_rows, q_rows), q_rows)
            return [pltpu.make_async_copy(win.at[pl.ds(KV_PREV, q_rows)], kv_hbm.at[which, rows],
                                          kv_sems.at[which])
                    for which, win in ((0, kwin_ref), (1, vwin_ref))]

        for win, c0 in ((kwin_ref, 0), (vwin_ref, SB_WIDTH)):
            win[0:KV_PREV, :] = win[q_rows:q_rows + KV_PREV, :]
            win[KV_PREV:, :] = kv_next_ref[:, c0:c0 + SB_WIDTH]
        for copy in history_writes(i):
            copy.start()
        final_norm()

        h_next = normed_input()
        y_sb = _sb_sweep(i, q_ref, kwin_ref, vwin_ref, kv_hbm, ktile_ref, vtile_ref, tile_sems, acc_ref, carry_ref,
                         (lambda: project_kv(h_next), lambda: project_q_conv(h_next, nxt)),
                         lambda: [copy.wait() for copy in history_writes(i)])

        u = rest_ref[:, 0:CONV_WIDTH]
        b = rest_ref[:, CONV_WIDTH:2 * CONV_WIDTH]
        c = rest_ref[:, 2 * CONV_WIDTH:3 * CONV_WIDTH]
        cu = c * u
        cu_ref[SUBLANES:, :] = cu
        cu_1 = cu_ref[SUBLANES - 1:SUBLANES - 1 + q_rows, :]
        cu_2 = cu_ref[SUBLANES - 2:SUBLANES - 2 + q_rows, :]
        cu_ref[0:SUBLANES, :] = cu[q_rows - SUBLANES:, :]
        conv = (convw_ref[0] * cu_2 + convw_ref[1] * cu_1 + convw_ref[2] * cu
                + convb_ref[...])
        y_conv = b * conv

        y_mem = _mem_attention(q_ref, mkv_ref, lambda: project_gate(h_next, nxt))

        y = jnp.concatenate([y_sb * _rms_scale(y_sb) * g_sb_ref[...],
                             y_conv * _rms_scale(y_conv) * g_conv_ref[...],
                             y_mem * _rms_scale(y_mem) * g_mem_ref[...]], axis=-1)
        gate = rest_ref[:, 3 * CONV_WIDTH:]
        half_gate = 0.5 * gate
        gated = (y * (half_gate + half_gate * jnp.tanh(half_gate))).astype(jnp.bfloat16)
        res_ref[...] = x_res_ref[...] + jnp.dot(gated, w_out_ref[...], preferred_element_type=jnp.float32)


def _layer(x2, mem, g_in, w_in, conv_w, conv_b, g_memin, w_kv, g_sb, g_conv, g_mem, w_out, g_final):
    t, d = x2.shape
    assert t % Q_BLOCK == 0 and Q_BLOCK % SB_TILE == 0 and KV_PREV <= Q_BLOCK
    assert Q_BLOCK >= STREAM_ROWS
    assert (Q_BLOCK // STREAM_ROWS) * (d // STREAM_COLS + (3 * CONV_WIDTH + MIX_WIDTH) // STREAM_COLS) >= STREAM_STAGES
    for a in (mem, w_kv, w_in, w_out):
        assert a.shape[1] % STREAM_ROWS == 0 and a.shape[2] % STREAM_COLS == 0
    n_chain = (Q_BLOCK // SB_TILE) * (SB_WIDTH // LANES)

    def whole(a):
        return pl.BlockSpec(a.shape, lambda s: (0,) * a.ndim)

    in_hbm = pl.BlockSpec(memory_space=pl.ANY)
    n_blk = t // Q_BLOCK
    return pl.pallas_call(
        _layer_kernel,
        grid=(n_blk + 2,),
        in_specs=[pl.BlockSpec((Q_BLOCK, d), lambda s: (jnp.minimum(s, n_blk - 1), 0)),
                  pl.BlockSpec((Q_BLOCK, d), lambda s: (jnp.clip(s - 1, 0, n_blk - 1), 0)),
                  in_hbm, whole(g_in),
                  in_hbm, whole(conv_w), whole(conv_b), whole(g_memin), in_hbm, whole(g_sb), whole(g_conv),
                  whole(g_mem), in_hbm, whole(g_final)],
        out_specs=[pl.BlockSpec((Q_BLOCK, d), lambda s: (jnp.clip(s - 2, 0, n_blk - 1), 0)), in_hbm],
        out_shape=[jax.ShapeDtypeStruct((t, d), jnp.float32),
                   jax.ShapeDtypeStruct((2, KV_PAD + t, SB_WIDTH), jnp.bfloat16)],
        scratch_shapes=[pltpu.VMEM(w_in.shape[1:], jnp.bfloat16),
                        pltpu.VMEM(w_out.shape[1:], jnp.bfloat16),
                        pltpu.VMEM((mem.shape[1], w_kv.shape[2]), jnp.bfloat16),
                        pltpu.SemaphoreType.DMA((STREAM_STAGES,)),
                        pltpu.VMEM((KV_PREV + Q_BLOCK, SB_WIDTH), jnp.bfloat16),
                        pltpu.VMEM((KV_PREV + Q_BLOCK, SB_WIDTH), jnp.bfloat16),
                        pltpu.VMEM((Q_BLOCK, 2 * SB_WIDTH), jnp.bfloat16),
                        pltpu.VMEM((n_chain, SB_TILE, LANES), jnp.bfloat16),
                        pltpu.VMEM((n_chain, SB_TILE, LANES), jnp.bfloat16),
                        pltpu.SemaphoreType.DMA((2,)),
                        pltpu.SemaphoreType.DMA((2, n_chain)),
                        pltpu.VMEM((2, Q_BLOCK, SB_WIDTH + MEM_WIDTH), jnp.bfloat16),
                        pltpu.VMEM((2, Q_BLOCK, 3 * CONV_WIDTH + MIX_WIDTH), jnp.float32),
                        pltpu.VMEM((Q_BLOCK, d), jnp.float32),
                        pltpu.VMEM((SUBLANES + Q_BLOCK, CONV_WIDTH), jnp.float32),
                        pltpu.VMEM((n_chain, SB_TILE, LANES), jnp.float32),
                        pltpu.VMEM((n_chain, 2 * SB_TILE, 1), jnp.float32)],
        compiler_params=pltpu.CompilerParams(
            dimension_semantics=("arbitrary",), vmem_limit_bytes=VMEM_LIMIT_BYTES),
        name="layer",
    )(x2, x2, mem, g_in, w_in, conv_w, conv_b, g_memin, w_kv, g_sb, g_conv, g_mem, w_out, g_final)[0]


def kernel(x, mem, g_in, w_in, conv_w, conv_b, g_mem, w_mem_kv, g_sb_out, g_conv_out,
           g_mem_out, w_out, g_final):
    batch, t, d = x.shape
    assert batch == 1 and g_in.shape[0] == 1
    out = _layer(x.reshape(t, d), mem, g_in[0][None, :], w_in, jnp.transpose(conv_w, (1, 0, 2)), conv_b[0][None, :],
                 g_mem[0][None, :], w_mem_kv, g_sb_out[0][None, :], g_conv_out[0][None, :],
                 g_mem_out[0][None, :], w_out, g_final[None, :])
    return out.reshape(batch, t, d)
```

```python
import jax
import jax.numpy as jnp
from jax import lax
from jax.experimental import pallas as pl
from jax.experimental.pallas import tpu as pltpu

HEAD_DIM = 64
SB_HEADS = 8
SB_WIDTH = SB_HEADS * HEAD_DIM
CONV_WIDTH = 4 * HEAD_DIM
MEM_HEADS = 4
MEM_WIDTH = MEM_HEADS * HEAD_DIM
CONV_K = 3
EPS = 1e-6
SCALE = HEAD_DIM ** -0.5

LANES = 128
SUBLANES = 8
MIX_WIDTH = SB_WIDTH + CONV_WIDTH + MEM_WIDTH
PROJ_COLS = 512
Q_BLOCK = 512
SB_TILE = 128
SWEEP_STOP = 104.0
TILE2_ROWS = 32
KV_PAD = Q_BLOCK
KV_PREV = 2 * SB_TILE
LOG2_E = 1.4426950408889634
LOGIT2_CLAMP = 126.0
STREAM_ROWS = 256
STREAM_COLS = 512
STREAM_EXTRA_STAGES = 4
VMEM_LIMIT_BYTES = 62 * 1024 * 1024


def _rms_scale(xf, eps=EPS):
    return lax.rsqrt(jnp.mean(xf * xf, axis=-1, keepdims=True) + eps)


def _dot_nt(a, b):
    return lax.dot_general(a, b, (((1,), (1,)), ((), ())), preferred_element_type=jnp.float32)


def _split_heads(a):
    first_half = lax.broadcasted_iota(jnp.int32, a.shape, 1) < HEAD_DIM
    zero = jnp.zeros_like(a)
    return jnp.concatenate([jnp.where(first_half, a, zero), jnp.where(first_half, zero, a)], axis=0)


def _sb_sweep(i, q_ref, kwin_ref, vwin_ref, kv_hbm, ktile_ref, vtile_ref, tile_sems, acc_ref, carry_ref, fillers,
              before_loop):
    n_sub = q_ref.shape[0] // SB_TILE
    n_pair = SB_WIDTH // LANES
    chains = [(sub, p) for sub in range(n_sub) for p in range(n_pair)]
    fillers = list(fillers)

    rt = lax.broadcasted_iota(jnp.int32, (2 * SB_TILE, 2 * SB_TILE), 0)
    ct = lax.broadcasted_iota(jnp.int32, (2 * SB_TILE, 2 * SB_TILE), 1)
    top, left = rt < SB_TILE, ct < SB_TILE
    tri2 = jnp.where(top, rt, rt - SB_TILE) >= jnp.where(left, ct, ct - SB_TILE)
    r2 = lax.broadcasted_iota(jnp.int32, (2 * SB_TILE, SB_TILE), 0)
    c2 = lax.broadcasted_iota(jnp.int32, (2 * SB_TILE, SB_TILE), 1)
    causal = c2 < jnp.where(r2 < SB_TILE, r2, r2 - SB_TILE)

    def mask_diagonal(a):
        return jnp.concatenate([a[:, :SB_TILE], jnp.where(causal, a[:, SB_TILE:], 0.0)], axis=1)

    def as_matrix(cond):
        return jnp.where(cond, 1.0, 0.0).astype(jnp.bfloat16)

    w_10 = as_matrix(jnp.logical_or(jnp.logical_and(jnp.logical_not(top), left),
                                    jnp.logical_and(tri2, jnp.logical_not(jnp.logical_xor(top, left)))))
    w_2 = as_matrix(jnp.logical_or(top, jnp.logical_or(tri2, jnp.logical_not(left))))
    w_1 = w_2[SB_TILE:]

    q_stack = [_split_heads(q_ref[sub * SB_TILE:(sub + 1) * SB_TILE, p * LANES:(p + 1) * LANES])
               for sub, p in chains]

    def window_rows(c, m, n_tiles=1):
        start = KV_PREV + (chains[c][0] - (m + n_tiles - 1)) * SB_TILE
        assert start >= 0
        return slice(start, start + n_tiles * SB_TILE)

    def head_rows(a, n):
        return jnp.concatenate([a[:n], a[SB_TILE:SB_TILE + n]], axis=0)

    def logits(q, keys):
        return jnp.minimum(_dot_nt(q, keys), LOGIT2_CLAMP)

    def window_keys(c, rows):
        p = chains[c][1]
        return kwin_ref[rows, p * LANES:(p + 1) * LANES]

    def window_values(c, rows):
        p = chains[c][1]
        return _split_heads(vwin_ref[rows, p * LANES:(p + 1) * LANES])

    def neg_log2_not_beta(z):
        return jnp.log(1.0 + jnp.exp2(z)) * LOG2_E

    def weights(z, s, diagonal=False):
        w = jnp.exp2(z - s)
        if diagonal:
            w = mask_diagonal(w)
        wb = w.astype(jnp.bfloat16)
        n = wb.shape[0] // 2
        return jnp.concatenate([wb[:n], wb[n:]], axis=1)

    def all_done(carries):
        floor = carries[0]
        for carry in carries[1:]:
            floor = jnp.minimum(floor, carry)
        return jnp.min(floor) >= SWEEP_STOP * LOG2_E

    def first_sweep():
        n = TILE2_ROWS
        rows_10 = [window_rows(c, 0, 2) for c in range(len(chains))]
        rows_2 = [window_rows(c, 2) for c in range(len(chains))]
        zs = [(logits(q_stack[c], window_keys(c, rows_10[c])),
               logits(head_rows(q_stack[c], n), window_keys(c, rows_2[c]))) for c in range(len(chains))]
        if fillers:
            fillers.pop(0)()
        lhs_10, lhs_2 = [], []
        for z10, z2 in zs:
            l10 = mask_diagonal(neg_log2_not_beta(z10))
            lhs_10.append(l10.astype(jnp.bfloat16))
            lhs_2.append(jnp.concatenate(
                [(head_rows(l10[:, :SB_TILE], n) + head_rows(l10[:, SB_TILE:], n)).astype(jnp.bfloat16),
                 neg_log2_not_beta(z2).astype(jnp.bfloat16)], axis=1))
        s_10 = jnp.dot(jnp.concatenate(lhs_10, axis=0), w_10, preferred_element_type=jnp.float32)
        s_2 = jnp.dot(jnp.concatenate(lhs_2, axis=0), w_2, preferred_element_type=jnp.float32)
        if fillers:
            fillers.pop(0)()
        reached, w_10s, w_2s = [], [], []
        for c, (z10, z2) in enumerate(zs):
            part = slice(c * 2 * SB_TILE, (c + 1) * 2 * SB_TILE)
            part2 = slice(c * 2 * n, (c + 1) * 2 * n)
            w_10s.append(weights(z10, s_10[part], diagonal=True))
            w_2s.append(weights(z2, s_2[part2, :SB_TILE]))
            after_1 = s_10[part, 0:1]
            after_2 = s_2[part2, SB_TILE:SB_TILE + 1]
            carry_ref[c] = after_1
            reached.append(jnp.concatenate([after_2[:n], after_1[n:SB_TILE],
                                            after_2[n:], after_1[SB_TILE + n:]], axis=0))
        done = all_done(reached)
        for c in range(len(chains)):
            acc = jnp.dot(w_10s[c], window_values(c, rows_10[c]), preferred_element_type=jnp.float32)
            tile_2 = jnp.dot(w_2s[c], window_values(c, rows_2[c]), preferred_element_type=jnp.float32)
            acc_ref[c, n:, :] = acc[n:]
            acc_ref[c, :n, :] = acc[:n] + jnp.where(done, tile_2, 0.0)
        return 2, done

    def tile_copies(m):
        copies = []
        for c, (sub, p) in enumerate(chains):
            row0 = pl.multiple_of(KV_PAD + (n_sub * i + sub - m) * SB_TILE, SB_TILE)
            for which, dst in ((0, ktile_ref), (1, vtile_ref)):
                copies.append(pltpu.make_async_copy(
                    kv_hbm.at[which, pl.ds(row0, SB_TILE), pl.ds(p * LANES, LANES)], dst.at[c],
                    tile_sems.at[which, c]))
        return copies

    def next_sweep(m):
        for copy in tile_copies(m):
            copy.start()
        for copy in tile_copies(m):
            copy.wait()
        zs = [logits(q_stack[c], ktile_ref[c]) for c in range(len(chains))]
        ls = [neg_log2_not_beta(z).astype(jnp.bfloat16) for z in zs]
        cs = jnp.dot(jnp.concatenate(ls, axis=0), w_1, preferred_element_type=jnp.float32)
        carries, w_cats = [], []
        for c, z in enumerate(zs):
            part = slice(c * 2 * SB_TILE, (c + 1) * 2 * SB_TILE)
            old = carry_ref[c]
            w_cats.append(weights(z, cs[part, :SB_TILE] + old))
            carries.append(cs[part, SB_TILE:SB_TILE + 1] + old)
            carry_ref[c] = carries[c]
        for c in range(len(chains)):
            acc_ref[c] += jnp.dot(w_cats[c], _split_heads(vtile_ref[c]), preferred_element_type=jnp.float32)
        return all_done(carries)

    m0, done0 = first_sweep()
    before_loop()
    last = n_sub * i + n_sub - 1

    def cond(state):
        m, done = state
        return jnp.logical_and(m <= last, jnp.logical_not(done))

    def body(state):
        m, _ = state
        return m + 1, next_sweep(m)

    lax.while_loop(cond, body, (jnp.int32(m0), done0))
    return jnp.concatenate(
        [jnp.concatenate([acc_ref[sub * n_pair + p] for p in range(n_pair)], axis=1)
         for sub in range(n_sub)], axis=0)


def _mem_attention(q_ref, mkv_ref, filler):
    q_rows = q_ref.shape[0]
    n_mem = mkv_ref.shape[0]
    n_pair = MEM_WIDTH // LANES
    head_ones = ((lax.broadcasted_iota(jnp.int32, (2 * n_mem, LANES), 1) < HEAD_DIM)
                 == (lax.broadcasted_iota(jnp.int32, (2 * n_mem, LANES), 0) < n_mem)
                 ).astype(jnp.float32).astype(jnp.bfloat16)
    logits = [_dot_nt(_split_heads(q_ref[:, SB_WIDTH + p * LANES:SB_WIDTH + (p + 1) * LANES]),
                      mkv_ref[:, p * LANES:(p + 1) * LANES]) for p in range(n_pair)]
    filler()
    outs = []
    for p, s in enumerate(logits):
        mv = mkv_ref[:, MEM_WIDTH + p * LANES:MEM_WIDTH + (p + 1) * LANES]
        e = jnp.exp(s - jnp.max(s, axis=-1, keepdims=True)).astype(jnp.bfloat16)
        e_cat = jnp.concatenate([e[:q_rows], e[q_rows:]], axis=1)
        num_den = jnp.dot(e_cat, jnp.concatenate([_split_heads(mv), head_ones], axis=1),
                          preferred_element_type=jnp.float32)
        outs.append(num_den[:, :LANES] / num_den[:, LANES:])
    return jnp.concatenate(outs, axis=-1)


def _stream_chunks(jobs, stages, sems):
    depth = len(stages)

    def copy(j):
        return pltpu.make_async_copy(jobs[j][0], stages[j % depth], sems.at[j % depth])

    for j in range(min(depth - 1, len(jobs))):
        copy(j).start()
    for j, (_, consume) in enumerate(jobs):
        if j + depth - 1 < len(jobs):
            copy(j + depth - 1).start()
        copy(j).wait()
        consume(stages[j % depth])


def _chunks(hbm):
    _, n_rows, n_cols = hbm.shape
    return [(r0, c0, hbm.at[0, pl.ds(r0, STREAM_ROWS), pl.ds(c0, STREAM_COLS)])
            for r0 in range(0, n_rows, STREAM_ROWS) for c0 in range(0, n_cols, STREAM_COLS)]


def _load_parameters(mem_hbm, g_mem_ref, w_kv_hbm, w_in_hbm, w_in_groups, w_out_hbm, mkv_ref, w_in_ref, w_out_ref,
                     stages, sems):
    jobs = []
    state = {"mem": [], "h": None, "kv": None}

    def take_mem(window):
        state["mem"].append(window[...])

    def memory_kv(window, r0):
        if state["h"] is None:
            m = jnp.concatenate(state["mem"], axis=1)
            state["h"] = (m * _rms_scale(m) * g_mem_ref[...]).astype(jnp.bfloat16)
        part = jnp.dot(state["h"][:, r0:r0 + STREAM_ROWS], window[...].astype(jnp.bfloat16),
                       preferred_element_type=jnp.float32)
        state["kv"] = part if state["kv"] is None else state["kv"] + part
        if r0 + STREAM_ROWS == w_kv_hbm.shape[1]:
            mkv_ref[...] = state["kv"].astype(jnp.bfloat16)

    def cast_into(dst, r0, c0, then=None):
        def consume(window):
            dst[r0:r0 + STREAM_ROWS, c0:c0 + STREAM_COLS] = window[...].astype(jnp.bfloat16)
            if then is not None:
                then()
        return consume

    assert mem_hbm.shape[1] == STREAM_ROWS and w_kv_hbm.shape[2] == STREAM_COLS
    jobs += [(chunk, take_mem) for _, _, chunk in _chunks(mem_hbm)]
    jobs += [(chunk, lambda window, r0=r0: memory_kv(window, r0)) for r0, _, chunk in _chunks(w_kv_hbm)]
    assert sorted(c0 for columns, _ in w_in_groups for c0 in columns) == list(range(0, w_in_hbm.shape[2], STREAM_COLS))
    for columns, then in w_in_groups:
        group = [job for job in _chunks(w_in_hbm) if job[1] in columns]
        jobs += [(chunk, cast_into(w_in_ref, r0, c0, then if n + 1 == len(group) else None))
                 for n, (r0, c0, chunk) in enumerate(group)]
    jobs += [(chunk, cast_into(w_out_ref, r0, c0)) for r0, c0, chunk in _chunks(w_out_hbm)]
    _stream_chunks(jobs, stages, sems)


def _layer_kernel(x_ref, x_res_ref, mem_hbm, g_in_ref, w_in_hbm, convw_ref, convb_ref, g_memin_ref, w_kv_hbm,
                  g_sb_ref, g_conv_ref, g_mem_ref, w_out_hbm, g_final_ref,
                  out_ref, kv_hbm, w_in_ref, w_out_ref, mkv_ref, stream_sems, stage_ref, kwin_ref, vwin_ref, kv_next_ref,
                  ktile_ref, vtile_ref, kv_sems, tile_sems, q2_ref, rest2_ref, res_ref, cu_ref,
                  acc_ref, carry_ref):
    s = pl.program_id(0)
    n_blk = pl.num_programs(0) - 2
    q_rows = x_ref.shape[0]
    c_u = 3 * SB_WIDTH
    c_qm = c_u + 3 * CONV_WIDTH
    c_gate = c_qm + MEM_WIDTH

    def normed_input():
        x = x_ref[...]
        return (x * _rms_scale(x) * g_in_ref[...]).astype(jnp.bfloat16)

    def proj(h, c0, width):
        return jnp.dot(h, w_in_ref[:, c0:c0 + width], preferred_element_type=jnp.float32)

    def project_kv(h):
        kv_next_ref[...] = proj(h, SB_WIDTH, 2 * SB_WIDTH).astype(jnp.bfloat16)

    def project_q_conv(h, slot):
        q2_ref[slot, :, :SB_WIDTH] = (proj(h, 0, SB_WIDTH) * (SCALE * LOG2_E)).astype(jnp.bfloat16)
        q2_ref[slot, :, SB_WIDTH:] = (proj(h, c_qm, MEM_WIDTH) * SCALE).astype(jnp.bfloat16)
        rest2_ref[slot, :, :3 * CONV_WIDTH] = proj(h, c_u, 3 * CONV_WIDTH)

    def project_gate(h, slot):
        for c0 in range(0, MIX_WIDTH, PROJ_COLS):
            rest2_ref[slot, :, 3 * CONV_WIDTH + c0:3 * CONV_WIDTH + c0 + PROJ_COLS] = proj(h, c_gate + c0, PROJ_COLS)

    def final_norm():
        res = res_ref[...]
        out_ref[...] = res * _rms_scale(res) * g_final_ref[...]

    @pl.when(s == 0)
    def _():
        stages = [ref.at[pl.ds(r0, STREAM_ROWS), pl.ds(c0, STREAM_COLS)] for ref in (res_ref, rest2_ref.at[1])
                  for r0 in range(0, ref.shape[0] - STREAM_ROWS + 1, STREAM_ROWS)
                  for c0 in range(0, ref.shape[-1] - STREAM_COLS + 1, STREAM_COLS)]
        stages += [stage_ref.at[k] for k in range(stage_ref.shape[0])]
        assert len(stages) == stream_sems.shape[0]
        h = normed_input()
        w_in_groups = [(range(SB_WIDTH, 3 * SB_WIDTH, STREAM_COLS), lambda: project_kv(h)),
                       ([0, *range(c_u, c_gate, STREAM_COLS)], lambda: project_q_conv(h, 0)),
                       (range(c_gate, c_gate + MIX_WIDTH, STREAM_COLS), lambda: project_gate(h, 0))]
        _load_parameters(mem_hbm, g_memin_ref, w_kv_hbm, w_in_hbm, w_in_groups, w_out_hbm, mkv_ref, w_in_ref,
                         w_out_ref, stages, stream_sems)
        res_ref[...] = jnp.zeros(res_ref.shape, jnp.float32)
        cu_ref[0:SUBLANES, :] = jnp.zeros((SUBLANES, CONV_WIDTH), jnp.float32)
        kwin_ref[...] = jnp.zeros(kwin_ref.shape, jnp.bfloat16)
        vwin_ref[...] = jnp.zeros(vwin_ref.shape, jnp.bfloat16)
        zero_rows = [pltpu.make_async_copy(win.at[pl.ds(0, KV_PAD)], kv_hbm.at[which, pl.ds(0, KV_PAD)],
                                           kv_sems.at[which])
                     for which, win in ((0, kwin_ref), (1, vwin_ref))]
        for copy in zero_rows:
            copy.start()
        for copy in zero_rows:
            copy.wait()

    @pl.when(s == n_blk + 1)
    def _():
        final_norm()

    @pl.when(jnp.logical_and(s > 0, s <= n_blk))
    def _():
        i = s - 1
        cur = lax.rem(i, 2)
        nxt = 1 - cur
        q_ref = q2_ref.at[cur]
        rest_ref = rest2_ref.at[cur]

        def history_writes(block):
            rows = pl.ds(pl.multiple_of(KV_PAD + block * q_rows, q_rows), q_rows)
            return [pltpu.make_async_copy(win.at[pl.ds(KV_PREV, q_rows)], kv_hbm.at[which, rows],
                                          kv_sems.at[which])
                    for which, win in ((0, kwin_ref), (1, vwin_ref))]

        for win, c0 in ((kwin_ref, 0), (vwin_ref, SB_WIDTH)):
            win[0:KV_PREV, :] = win[q_rows:q_rows + KV_PREV, :]
            win[KV_PREV:, :] = kv_next_ref[:, c0:c0 + SB_WIDTH]
        for copy in history_writes(i):
            copy.start()
        final_norm()

        h_next = normed_input()
        y_sb = _sb_sweep(i, q_ref, kwin_ref, vwin_ref, kv_hbm, ktile_ref, vtile_ref, tile_sems, acc_ref, carry_ref,
                         (lambda: project_kv(h_next), lambda: project_q_conv(h_next, nxt)),
                         lambda: [copy.wait() for copy in history_writes(i)])

        u = rest_ref[:, 0:CONV_WIDTH]
        b = rest_ref[:, CONV_WIDTH:2 * CONV_WIDTH]
        c = rest_ref[:, 2 * CONV_WIDTH:3 * CONV_WIDTH]
        cu = c * u
        cu_ref[SUBLANES:, :] = cu
        cu_1 = cu_ref[SUBLANES - 1:SUBLANES - 1 + q_rows, :]
        cu_2 = cu_ref[SUBLANES - 2:SUBLANES - 2 + q_rows, :]
        cu_ref[0:SUBLANES, :] = cu[q_rows - SUBLANES:, :]
        conv = (convw_ref[0] * cu_2 + convw_ref[1] * cu_1 + convw_ref[2] * cu
                + convb_ref[...])
        y_conv = b * conv

        y_mem = _mem_attention(q_ref, mkv_ref, lambda: project_gate(h_next, nxt))

        y = jnp.concatenate([y_sb * _rms_scale(y_sb) * g_sb_ref[...],
                             y_conv * _rms_scale(y_conv) * g_conv_ref[...],
                             y_mem * _rms_scale(y_mem) * g_mem_ref[...]], axis=-1)
        gate = rest_ref[:, 3 * CONV_WIDTH:]
        half_gate = 0.5 * gate
        gated = (y * (half_gate + half_gate * jnp.tanh(half_gate))).astype(jnp.bfloat16)
        res_ref[...] = x_res_ref[...] + jnp.dot(gated, w_out_ref[...], preferred_element_type=jnp.float32)


def _layer(x2, mem, g_in, w_in, conv_w, conv_b, g_memin, w_kv, g_sb, g_conv, g_mem, w_out, g_final):
    t, d = x2.shape
    assert t % Q_BLOCK == 0 and Q_BLOCK % SB_TILE == 0 and KV_PREV <= Q_BLOCK
    assert Q_BLOCK >= STREAM_ROWS
    for a in (mem, w_kv, w_in, w_out):
        assert a.shape[1] % STREAM_ROWS == 0 and a.shape[2] % STREAM_COLS == 0
    n_chain = (Q_BLOCK // SB_TILE) * (SB_WIDTH // LANES)
    n_idle_stages = (Q_BLOCK // STREAM_ROWS) * (d // STREAM_COLS + (3 * CONV_WIDTH + MIX_WIDTH) // STREAM_COLS)

    def whole(a):
        return pl.BlockSpec(a.shape, lambda s: (0,) * a.ndim)

    in_hbm = pl.BlockSpec(memory_space=pl.ANY)
    n_blk = t // Q_BLOCK
    return pl.pallas_call(
        _layer_kernel,
        grid=(n_blk + 2,),
        in_specs=[pl.BlockSpec((Q_BLOCK, d), lambda s: (jnp.minimum(s, n_blk - 1), 0)),
                  pl.BlockSpec((Q_BLOCK, d), lambda s: (jnp.clip(s - 1, 0, n_blk - 1), 0)),
                  in_hbm, whole(g_in),
                  in_hbm, whole(conv_w), whole(conv_b), whole(g_memin), in_hbm, whole(g_sb), whole(g_conv),
                  whole(g_mem), in_hbm, whole(g_final)],
        out_specs=[pl.BlockSpec((Q_BLOCK, d), lambda s: (jnp.clip(s - 2, 0, n_blk - 1), 0)), in_hbm],
        out_shape=[jax.ShapeDtypeStruct((t, d), jnp.float32),
                   jax.ShapeDtypeStruct((2, KV_PAD + t, SB_WIDTH), jnp.bfloat16)],
        scratch_shapes=[pltpu.VMEM(w_in.shape[1:], jnp.bfloat16),
                        pltpu.VMEM(w_out.shape[1:], jnp.bfloat16),
                        pltpu.VMEM((mem.shape[1], w_kv.shape[2]), jnp.bfloat16),
                        pltpu.SemaphoreType.DMA((n_idle_stages + STREAM_EXTRA_STAGES,)),
                        pltpu.VMEM((STREAM_EXTRA_STAGES, STREAM_ROWS, STREAM_COLS), jnp.float32),
                        pltpu.VMEM((KV_PREV + Q_BLOCK, SB_WIDTH), jnp.bfloat16),
                        pltpu.VMEM((KV_PREV + Q_BLOCK, SB_WIDTH), jnp.bfloat16),
                        pltpu.VMEM((Q_BLOCK, 2 * SB_WIDTH), jnp.bfloat16),
                        pltpu.VMEM((n_chain, SB_TILE, LANES), jnp.bfloat16),
                        pltpu.VMEM((n_chain, SB_TILE, LANES), jnp.bfloat16),
                        pltpu.SemaphoreType.DMA((2,)),
                        pltpu.SemaphoreType.DMA((2, n_chain)),
                        pltpu.VMEM((2, Q_BLOCK, SB_WIDTH + MEM_WIDTH), jnp.bfloat16),
                        pltpu.VMEM((2, Q_BLOCK, 3 * CONV_WIDTH + MIX_WIDTH), jnp.float32),
                        pltpu.VMEM((Q_BLOCK, d), jnp.float32),
                        pltpu.VMEM((SUBLANES + Q_BLOCK, CONV_WIDTH), jnp.float32),
                        pltpu.VMEM((n_chain, SB_TILE, LANES), jnp.float32),
                        pltpu.VMEM((n_chain, 2 * SB_TILE, 1), jnp.float32)],
        compiler_params=pltpu.CompilerParams(
            dimension_semantics=("arbitrary",), vmem_limit_bytes=VMEM_LIMIT_BYTES),
        name="layer",
    )(x2, x2, mem, g_in, w_in, conv_w, conv_b, g_memin, w_kv, g_sb, g_conv, g_mem, w_out, g_final)[0]


def kernel(x, mem, g_in, w_in, conv_w, conv_b, g_mem, w_mem_kv, g_sb_out, g_conv_out,
           g_mem_out, w_out, g_final):
    batch, t, d = x.shape
    assert batch == 1 and g_in.shape[0] == 1
    out = _layer(x.reshape(t, d), mem, g_in[0][None, :], w_in, jnp.transpose(conv_w, (1, 0, 2)), conv_b[0][None, :],
                 g_mem[0][None, :], w_mem_kv, g_sb_out[0][None, :], g_conv_out[0][None, :],
                 g_mem_out[0][None, :], w_out, g_final[None, :])
    return out.reshape(batch, t, d)
```

```python
import jax
import jax.numpy as jnp
from jax import lax
from jax.experimental import pallas as pl
from jax.experimental.pallas import tpu as pltpu

HEAD_DIM = 64
SB_HEADS = 8
SB_WIDTH = SB_HEADS * HEAD_DIM
CONV_WIDTH = 4 * HEAD_DIM
MEM_HEADS = 4
MEM_WIDTH = MEM_HEADS * HEAD_DIM
CONV_K = 3
EPS = 1e-6
SCALE = HEAD_DIM ** -0.5

LANES = 128
SUBLANES = 8
MIX_WIDTH = SB_WIDTH + CONV_WIDTH + MEM_WIDTH
PROJ_COLS = 512
Q_BLOCK = 512
SB_TILE = 128
SWEEP_STOP = 104.0
TILE2_ROWS = 32
KV_PAD = Q_BLOCK
KV_PREV = 2 * SB_TILE
LOG2_E = 1.4426950408889634
LOGIT2_CLAMP = 126.0
STREAM_ROWS = 256
STREAM_COLS = 512
STREAM_EXTRA_STAGES = 4
VMEM_LIMIT_BYTES = 62 * 1024 * 1024


def _rms_scale(xf, eps=EPS):
    return lax.rsqrt(jnp.mean(xf * xf, axis=-1, keepdims=True) + eps)


def _dot_nt(a, b):
    return lax.dot_general(a, b, (((1,), (1,)), ((), ())), preferred_element_type=jnp.float32)


def _split_heads(a):
    first_half = lax.broadcasted_iota(jnp.int32, a.shape, 1) < HEAD_DIM
    zero = jnp.zeros_like(a)
    return jnp.concatenate([jnp.where(first_half, a, zero), jnp.where(first_half, zero, a)], axis=0)


def _sb_sweep(i, q_ref, kwin_ref, vwin_ref, kv_hbm, ktile_ref, vtile_ref, tile_sems, acc_ref, carry_ref, fillers,
              before_loop):
    n_sub = q_ref.shape[0] // SB_TILE
    n_pair = SB_WIDTH // LANES
    chains = [(sub, p) for sub in range(n_sub) for p in range(n_pair)]
    fillers = list(fillers)

    rt = lax.broadcasted_iota(jnp.int32, (2 * SB_TILE, 2 * SB_TILE), 0)
    ct = lax.broadcasted_iota(jnp.int32, (2 * SB_TILE, 2 * SB_TILE), 1)
    top, left = rt < SB_TILE, ct < SB_TILE
    tri2 = jnp.where(top, rt, rt - SB_TILE) >= jnp.where(left, ct, ct - SB_TILE)
    r2 = lax.broadcasted_iota(jnp.int32, (2 * SB_TILE, SB_TILE), 0)
    c2 = lax.broadcasted_iota(jnp.int32, (2 * SB_TILE, SB_TILE), 1)
    causal = c2 < jnp.where(r2 < SB_TILE, r2, r2 - SB_TILE)

    def mask_diagonal(a):
        return jnp.concatenate([a[:, :SB_TILE], jnp.where(causal, a[:, SB_TILE:], 0.0)], axis=1)

    def as_matrix(cond):
        return jnp.where(cond, 1.0, 0.0).astype(jnp.bfloat16)

    w_10 = as_matrix(jnp.logical_or(jnp.logical_and(jnp.logical_not(top), left),
                                    jnp.logical_and(tri2, jnp.logical_not(jnp.logical_xor(top, left)))))
    w_2 = as_matrix(jnp.logical_or(top, jnp.logical_or(tri2, jnp.logical_not(left))))
    w_1 = w_2[SB_TILE:]

    q_stack = [_split_heads(q_ref[sub * SB_TILE:(sub + 1) * SB_TILE, p * LANES:(p + 1) * LANES])
               for sub, p in chains]

    def window_rows(c, m, n_tiles=1):
        start = KV_PREV + (chains[c][0] - (m + n_tiles - 1)) * SB_TILE
        assert start >= 0
        return slice(start, start + n_tiles * SB_TILE)

    def head_rows(a, n):
        return jnp.concatenate([a[:n], a[SB_TILE:SB_TILE + n]], axis=0)

    def logits(q, keys):
        return jnp.minimum(_dot_nt(q, keys), LOGIT2_CLAMP)

    def window_keys(c, rows):
        p = chains[c][1]
        return kwin_ref[rows, p * LANES:(p + 1) * LANES]

    def window_values(c, rows):
        p = chains[c][1]
        return _split_heads(vwin_ref[rows, p * LANES:(p + 1) * LANES])

    def neg_log2_not_beta(z):
        return jnp.log(1.0 + jnp.exp2(z)) * LOG2_E

    def weights(z, s, diagonal=False):
        w = jnp.exp2(z - s)
        if diagonal:
            w = mask_diagonal(w)
        wb = w.astype(jnp.bfloat16)
        n = wb.shape[0] // 2
        return jnp.concatenate([wb[:n], wb[n:]], axis=1)

    def all_done(carries):
        floor = carries[0]
        for carry in carries[1:]:
            floor = jnp.minimum(floor, carry)
        return jnp.min(floor) >= SWEEP_STOP * LOG2_E

    def first_sweep():
        n = TILE2_ROWS
        rows_10 = [window_rows(c, 0, 2) for c in range(len(chains))]
        rows_2 = [window_rows(c, 2) for c in range(len(chains))]
        zs = [(logits(q_stack[c], window_keys(c, rows_10[c])),
               logits(head_rows(q_stack[c], n), window_keys(c, rows_2[c]))) for c in range(len(chains))]
        if fillers:
            fillers.pop(0)()
        lhs_10, lhs_2 = [], []
        for z10, z2 in zs:
            l10 = mask_diagonal(neg_log2_not_beta(z10))
            lhs_10.append(l10.astype(jnp.bfloat16))
            lhs_2.append(jnp.concatenate(
                [(head_rows(l10[:, :SB_TILE], n) + head_rows(l10[:, SB_TILE:], n)).astype(jnp.bfloat16),
                 neg_log2_not_beta(z2).astype(jnp.bfloat16)], axis=1))
        s_10 = jnp.dot(jnp.concatenate(lhs_10, axis=0), w_10, preferred_element_type=jnp.float32)
        s_2 = jnp.dot(jnp.concatenate(lhs_2, axis=0), w_2, preferred_element_type=jnp.float32)
        if fillers:
            fillers.pop(0)()
        reached, w_10s, w_2s = [], [], []
        for c, (z10, z2) in enumerate(zs):
            part = slice(c * 2 * SB_TILE, (c + 1) * 2 * SB_TILE)
            part2 = slice(c * 2 * n, (c + 1) * 2 * n)
            w_10s.append(weights(z10, s_10[part], diagonal=True))
            w_2s.append(weights(z2, s_2[part2, :SB_TILE]))
            after_1 = s_10[part, 0:1]
            after_2 = s_2[part2, SB_TILE:SB_TILE + 1]
            carry_ref[c] = after_1
            reached.append(jnp.concatenate([after_2[:n], after_1[n:SB_TILE],
                                            after_2[n:], after_1[SB_TILE + n:]], axis=0))
        done = all_done(reached)
        for c in range(len(chains)):
            acc = jnp.dot(w_10s[c], window_values(c, rows_10[c]), preferred_element_type=jnp.float32)
            tile_2 = jnp.dot(w_2s[c], window_values(c, rows_2[c]), preferred_element_type=jnp.float32)
            acc_ref[c, n:, :] = acc[n:]
            acc_ref[c, :n, :] = acc[:n] + jnp.where(done, tile_2, 0.0)
        return 2, done

    def tile_copies(m):
        copies = []
        for c, (sub, p) in enumerate(chains):
            row0 = pl.multiple_of(KV_PAD + (n_sub * i + sub - m) * SB_TILE, SB_TILE)
            for which, dst in ((0, ktile_ref), (1, vtile_ref)):
                copies.append(pltpu.make_async_copy(
                    kv_hbm.at[which, pl.ds(row0, SB_TILE), pl.ds(p * LANES, LANES)], dst.at[c],
                    tile_sems.at[which, c]))
        return copies

    def next_sweep(m):
        for copy in tile_copies(m):
            copy.start()
        for copy in tile_copies(m):
            copy.wait()
        zs = [logits(q_stack[c], ktile_ref[c]) for c in range(len(chains))]
        ls = [neg_log2_not_beta(z).astype(jnp.bfloat16) for z in zs]
        cs = jnp.dot(jnp.concatenate(ls, axis=0), w_1, preferred_element_type=jnp.float32)
        carries, w_cats = [], []
        for c, z in enumerate(zs):
            part = slice(c * 2 * SB_TILE, (c + 1) * 2 * SB_TILE)
            old = carry_ref[c]
            w_cats.append(weights(z, cs[part, :SB_TILE] + old))
            carries.append(cs[part, SB_TILE:SB_TILE + 1] + old)
            carry_ref[c] = carries[c]
        for c in range(len(chains)):
            acc_ref[c] += jnp.dot(w_cats[c], _split_heads(vtile_ref[c]), preferred_element_type=jnp.float32)
        return all_done(carries)

    m0, done0 = first_sweep()
    before_loop()
    last = n_sub * i + n_sub - 1

    def cond(state):
        m, done = state
        return jnp.logical_and(m <= last, jnp.logical_not(done))

    def body(state):
        m, _ = state
        return m + 1, next_sweep(m)

    lax.while_loop(cond, body, (jnp.int32(m0), done0))
    return jnp.concatenate(
        [jnp.concatenate([acc_ref[sub * n_pair + p] for p in range(n_pair)], axis=1)
         for sub in range(n_sub)], axis=0)


def _mem_attention(q_ref, mkv_ref, filler):
    q_rows = q_ref.shape[0]
    n_mem = mkv_ref.shape[0]
    n_pair = MEM_WIDTH // LANES
    head_ones = ((lax.broadcasted_iota(jnp.int32, (2 * n_mem, LANES), 1) < HEAD_DIM)
                 == (lax.broadcasted_iota(jnp.int32, (2 * n_mem, LANES), 0) < n_mem)
                 ).astype(jnp.float32).astype(jnp.bfloat16)
    logits = [_dot_nt(_split_heads(q_ref[:, SB_WIDTH + p * LANES:SB_WIDTH + (p + 1) * LANES]),
                      mkv_ref[:, p * LANES:(p + 1) * LANES]) for p in range(n_pair)]
    filler()
    outs = []
    for p, s in enumerate(logits):
        mv = mkv_ref[:, MEM_WIDTH + p * LANES:MEM_WIDTH + (p + 1) * LANES]
        e = jnp.exp(s - jnp.max(s, axis=-1, keepdims=True)).astype(jnp.bfloat16)
        e_cat = jnp.concatenate([e[:q_rows], e[q_rows:]], axis=1)
        num_den = jnp.dot(e_cat, jnp.concatenate([_split_heads(mv), head_ones], axis=1),
                          preferred_element_type=jnp.float32)
        outs.append(num_den[:, :LANES] / num_den[:, LANES:])
    return jnp.concatenate(outs, axis=-1)


def _stream_chunks(jobs, stages, sems):
    depth = len(stages)

    def copy(j):
        return pltpu.make_async_copy(jobs[j][0], stages[j % depth], sems.at[j % depth])

    for j in range(min(depth - 1, len(jobs))):
        copy(j).start()
    for j, (_, consume) in enumerate(jobs):
        if j + depth - 1 < len(jobs):
            copy(j + depth - 1).start()
        copy(j).wait()
        consume(stages[j % depth])


def _chunks(hbm):
    _, n_rows, n_cols = hbm.shape
    return [(r0, c0, hbm.at[0, pl.ds(r0, STREAM_ROWS), pl.ds(c0, STREAM_COLS)])
            for r0 in range(0, n_rows, STREAM_ROWS) for c0 in range(0, n_cols, STREAM_COLS)]


def _load_parameters(mem_hbm, g_mem_ref, w_kv_hbm, w_in_hbm, w_in_groups, w_out_hbm, mkv_ref, w_in_ref, w_out_ref,
                     stages, sems):
    jobs = []
    state = {"mem": [], "h": None, "kv": None}

    def take_mem(window):
        state["mem"].append(window[...])

    def memory_kv(window, r0):
        if state["h"] is None:
            m = jnp.concatenate(state["mem"], axis=1)
            state["h"] = (m * _rms_scale(m) * g_mem_ref[...]).astype(jnp.bfloat16)
        part = jnp.dot(state["h"][:, r0:r0 + STREAM_ROWS], window[...].astype(jnp.bfloat16),
                       preferred_element_type=jnp.float32)
        state["kv"] = part if state["kv"] is None else state["kv"] + part
        if r0 + STREAM_ROWS == w_kv_hbm.shape[1]:
            mkv_ref[...] = state["kv"].astype(jnp.bfloat16)

    def cast_into(dst, r0, c0, then=None):
        def consume(window):
            dst[r0:r0 + STREAM_ROWS, c0:c0 + STREAM_COLS] = window[...].astype(jnp.bfloat16)
            if then is not None:
                then()
        return consume

    assert mem_hbm.shape[1] == STREAM_ROWS and w_kv_hbm.shape[2] == STREAM_COLS
    jobs += [(chunk, take_mem) for _, _, chunk in _chunks(mem_hbm)]
    jobs += [(chunk, lambda window, r0=r0: memory_kv(window, r0)) for r0, _, chunk in _chunks(w_kv_hbm)]
    assert sorted(c0 for columns, _ in w_in_groups for c0 in columns) == list(range(0, w_in_hbm.shape[2], STREAM_COLS))
    for columns, then in w_in_groups:
        group = [job for job in _chunks(w_in_hbm) if job[1] in columns]
        jobs += [(chunk, cast_into(w_in_ref, r0, c0, then if n + 1 == len(group) else None))
                 for n, (r0, c0, chunk) in enumerate(group)]
    jobs += [(chunk, cast_into(w_out_ref, r0, c0)) for r0, c0, chunk in _chunks(w_out_hbm)]
    _stream_chunks(jobs, stages, sems)


def _layer_kernel(x_ref, x_res_ref, mem_hbm, g_in_ref, w_in_hbm, convw_ref, convb_ref, g_memin_ref, w_kv_hbm,
                  g_sb_ref, g_conv_ref, g_mem_ref, w_out_hbm, g_final_ref,
                  out_ref, kv_hbm, w_in_ref, w_out_ref, mkv_ref, stream_sems, stage_ref, kwin_ref, vwin_ref, kv_next_ref,
                  ktile_ref, vtile_ref, kv_sems, tile_sems, q2_ref, rest2_ref, res_ref, cu_ref,
                  acc_ref, carry_ref):
    s = pl.program_id(0)
    n_blk = pl.num_programs(0) - 2
    q_rows = x_ref.shape[0]
    c_u = 3 * SB_WIDTH
    c_qm = c_u + 3 * CONV_WIDTH
    c_gate = c_qm + MEM_WIDTH

    def normed_input():
        x = x_ref[...]
        return (x * _rms_scale(x) * g_in_ref[...]).astype(jnp.bfloat16)

    def proj(h, c0, width):
        return jnp.dot(h, w_in_ref[:, c0:c0 + width], preferred_element_type=jnp.float32)

    def project_kv(h):
        kv_next_ref[...] = proj(h, SB_WIDTH, 2 * SB_WIDTH).astype(jnp.bfloat16)

    def project_q_conv(h, slot):
        q2_ref[slot, :, :SB_WIDTH] = (proj(h, 0, SB_WIDTH) * (SCALE * LOG2_E)).astype(jnp.bfloat16)
        q2_ref[slot, :, SB_WIDTH:] = (proj(h, c_qm, MEM_WIDTH) * SCALE).astype(jnp.bfloat16)
        rest2_ref[slot, :, :3 * CONV_WIDTH] = proj(h, c_u, 3 * CONV_WIDTH)

    def project_gate(h, slot):
        for c0 in range(0, MIX_WIDTH, PROJ_COLS):
            rest2_ref[slot, :, 3 * CONV_WIDTH + c0:3 * CONV_WIDTH + c0 + PROJ_COLS] = proj(h, c_gate + c0, PROJ_COLS)

    def final_norm():
        res = res_ref[...]
        out_ref[...] = res * _rms_scale(res) * g_final_ref[...]

    @pl.when(s == 0)
    def _():
        stages = [ref.at[pl.ds(r0, STREAM_ROWS), pl.ds(c0, STREAM_COLS)] for ref in (res_ref, rest2_ref.at[1])
                  for r0 in range(0, ref.shape[0] - STREAM_ROWS + 1, STREAM_ROWS)
                  for c0 in range(0, ref.shape[-1] - STREAM_COLS + 1, STREAM_COLS)]
        stages += [stage_ref.at[k] for k in range(stage_ref.shape[0])]
        assert len(stages) == stream_sems.shape[0]
        h = normed_input()
        w_in_groups = [(range(SB_WIDTH, 3 * SB_WIDTH, STREAM_COLS), lambda: project_kv(h)),
                       ([0, *range(c_u, c_gate, STREAM_COLS)], lambda: project_q_conv(h, 0)),
                       (range(c_gate, c_gate + MIX_WIDTH, STREAM_COLS), lambda: project_gate(h, 0))]
        _load_parameters(mem_hbm, g_memin_ref, w_kv_hbm, w_in_hbm, w_in_groups, w_out_hbm, mkv_ref, w_in_ref,
                         w_out_ref, stages, stream_sems)
        res_ref[...] = jnp.zeros(res_ref.shape, jnp.float32)
        cu_ref[0:SUBLANES, :] = jnp.zeros((SUBLANES, CONV_WIDTH), jnp.float32)
        kwin_ref[...] = jnp.zeros(kwin_ref.shape, jnp.bfloat16)
        vwin_ref[...] = jnp.zeros(vwin_ref.shape, jnp.bfloat16)
        zero_rows = [pltpu.make_async_copy(win.at[pl.ds(0, KV_PAD)], kv_hbm.at[which, pl.ds(0, KV_PAD)],
                                           kv_sems.at[which])
                     for which, win in ((0, kwin_ref), (1, vwin_ref))]
        for copy in zero_rows:
            copy.start()
        for copy in zero_rows:
            copy.wait()

    @pl.when(s == n_blk + 1)
    def _():
        final_norm()

    def mix_block(project_next):
        i = s - 1
        cur = lax.rem(i, 2)
        nxt = 1 - cur
        q_ref = q2_ref.at[cur]
        rest_ref = rest2_ref.at[cur]

        def history_writes(block):
            rows = pl.ds(pl.multiple_of(KV_PAD + block * q_rows, q_rows), q_rows)
            return [pltpu.make_async_copy(win.at[pl.ds(KV_PREV, q_rows)], kv_hbm.at[which, rows],
                                          kv_sems.at[which])
                    for which, win in ((0, kwin_ref), (1, vwin_ref))]

        for win, c0 in ((kwin_ref, 0), (vwin_ref, SB_WIDTH)):
            win[0:KV_PREV, :] = win[q_rows:q_rows + KV_PREV, :]
            win[KV_PREV:, :] = kv_next_ref[:, c0:c0 + SB_WIDTH]
        for copy in history_writes(i):
            copy.start()
        final_norm()

        h_next = normed_input() if project_next else None
        y_sb = _sb_sweep(i, q_ref, kwin_ref, vwin_ref, kv_hbm, ktile_ref, vtile_ref, tile_sems, acc_ref, carry_ref,
                         (lambda: project_kv(h_next), lambda: project_q_conv(h_next, nxt)) if project_next else (),
                         lambda: [copy.wait() for copy in history_writes(i)])

        u = rest_ref[:, 0:CONV_WIDTH]
        b = rest_ref[:, CONV_WIDTH:2 * CONV_WIDTH]
        c = rest_ref[:, 2 * CONV_WIDTH:3 * CONV_WIDTH]
        cu = c * u
        cu_ref[SUBLANES:, :] = cu
        cu_1 = cu_ref[SUBLANES - 1:SUBLANES - 1 + q_rows, :]
        cu_2 = cu_ref[SUBLANES - 2:SUBLANES - 2 + q_rows, :]
        cu_ref[0:SUBLANES, :] = cu[q_rows - SUBLANES:, :]
        conv = (convw_ref[0] * cu_2 + convw_ref[1] * cu_1 + convw_ref[2] * cu
                + convb_ref[...])
        y_conv = b * conv

        y_mem = _mem_attention(q_ref, mkv_ref,
                               (lambda: project_gate(h_next, nxt)) if project_next else (lambda: None))

        y = jnp.concatenate([y_sb * _rms_scale(y_sb) * g_sb_ref[...],
                             y_conv * _rms_scale(y_conv) * g_conv_ref[...],
                             y_mem * _rms_scale(y_mem) * g_mem_ref[...]], axis=-1)
        gate = rest_ref[:, 3 * CONV_WIDTH:]
        half_gate = 0.5 * gate
        gated = (y * (half_gate + half_gate * jnp.tanh(half_gate))).astype(jnp.bfloat16)
        res_ref[...] = x_res_ref[...] + jnp.dot(gated, w_out_ref[...], preferred_element_type=jnp.float32)

    @pl.when(jnp.logical_and(s > 0, s < n_blk))
    def _():
        mix_block(True)

    @pl.when(s == n_blk)
    def _():
        mix_block(False)


def _layer(x2, mem, g_in, w_in, conv_w, conv_b, g_memin, w_kv, g_sb, g_conv, g_mem, w_out, g_final):
    t, d = x2.shape
    assert t % Q_BLOCK == 0 and Q_BLOCK % SB_TILE == 0 and KV_PREV <= Q_BLOCK
    assert Q_BLOCK >= STREAM_ROWS
    for a in (mem, w_kv, w_in, w_out):
        assert a.shape[1] % STREAM_ROWS == 0 and a.shape[2] % STREAM_COLS == 0
    n_chain = (Q_BLOCK // SB_TILE) * (SB_WIDTH // LANES)
    n_idle_stages = (Q_BLOCK // STREAM_ROWS) * (d // STREAM_COLS + (3 * CONV_WIDTH + MIX_WIDTH) // STREAM_COLS)

    def whole(a):
        return pl.BlockSpec(a.shape, lambda s: (0,) * a.ndim)

    in_hbm = pl.BlockSpec(memory_space=pl.ANY)
    n_blk = t // Q_BLOCK
    return pl.pallas_call(
        _layer_kernel,
        grid=(n_blk + 2,),
        in_specs=[pl.BlockSpec((Q_BLOCK, d), lambda s: (jnp.minimum(s, n_blk - 1), 0)),
                  pl.BlockSpec((Q_BLOCK, d), lambda s: (jnp.clip(s - 1, 0, n_blk - 1), 0)),
                  in_hbm, whole(g_in),
                  in_hbm, whole(conv_w), whole(conv_b), whole(g_memin), in_hbm, whole(g_sb), whole(g_conv),
                  whole(g_mem), in_hbm, whole(g_final)],
        out_specs=[pl.BlockSpec((Q_BLOCK, d), lambda s: (jnp.clip(s - 2, 0, n_blk - 1), 0)), in_hbm],
        out_shape=[jax.ShapeDtypeStruct((t, d), jnp.float32),
                   jax.ShapeDtypeStruct((2, KV_PAD + t, SB_WIDTH), jnp.bfloat16)],
        scratch_shapes=[pltpu.VMEM(w_in.shape[1:], jnp.bfloat16),
                        pltpu.VMEM(w_out.shape[1:], jnp.bfloat16),
                        pltpu.VMEM((mem.shape[1], w_kv.shape[2]), jnp.bfloat16),
                        pltpu.SemaphoreType.DMA((n_idle_stages + STREAM_EXTRA_STAGES,)),
                        pltpu.VMEM((STREAM_EXTRA_STAGES, STREAM_ROWS, STREAM_COLS), jnp.float32),
                        pltpu.VMEM((KV_PREV + Q_BLOCK, SB_WIDTH), jnp.bfloat16),
                        pltpu.VMEM((KV_PREV + Q_BLOCK, SB_WIDTH), jnp.bfloat16),
                        pltpu.VMEM((Q_BLOCK, 2 * SB_WIDTH), jnp.bfloat16),
                        pltpu.VMEM((n_chain, SB_TILE, LANES), jnp.bfloat16),
                        pltpu.VMEM((n_chain, SB_TILE, LANES), jnp.bfloat16),
                        pltpu.SemaphoreType.DMA((2,)),
                        pltpu.SemaphoreType.DMA((2, n_chain)),
                        pltpu.VMEM((2, Q_BLOCK, SB_WIDTH + MEM_WIDTH), jnp.bfloat16),
                        pltpu.VMEM((2, Q_BLOCK, 3 * CONV_WIDTH + MIX_WIDTH), jnp.float32),
                        pltpu.VMEM((Q_BLOCK, d), jnp.float32),
                        pltpu.VMEM((SUBLANES + Q_BLOCK, CONV_WIDTH), jnp.float32),
                        pltpu.VMEM((n_chain, SB_TILE, LANES), jnp.float32),
                        pltpu.VMEM((n_chain, 2 * SB_TILE, 1), jnp.float32)],
        compiler_params=pltpu.CompilerParams(
            dimension_semantics=("arbitrary",), vmem_limit_bytes=VMEM_LIMIT_BYTES),
        name="layer",
    )(x2, x2, mem, g_in, w_in, conv_w, conv_b, g_memin, w_kv, g_sb, g_conv, g_mem, w_out, g_final)[0]


def kernel(x, mem, g_in, w_in, conv_w, conv_b, g_mem, w_mem_kv, g_sb_out, g_conv_out,
           g_mem_out, w_out, g_final):
    batch, t, d = x.shape
    assert batch == 1 and g_in.shape[0] == 1
    out = _layer(x.reshape(t, d), mem, g_in[0][None, :], w_in, jnp.transpose(conv_w, (1, 0, 2)), conv_b[0][None, :],
                 g_mem[0][None, :], w_mem_kv, g_sb_out[0][None, :], g_conv_out[0][None, :],
                 g_mem_out[0][None, :], w_out, g_final[None, :])
    return out.reshape(batch, t, d)
```

```python
import jax
import jax.numpy as jnp
from jax import lax
from jax.experimental import pallas as pl
from jax.experimental.pallas import tpu as pltpu

HEAD_DIM = 64
SB_HEADS = 8
SB_WIDTH = SB_HEADS * HEAD_DIM
CONV_WIDTH = 4 * HEAD_DIM
MEM_HEADS = 4
MEM_WIDTH = MEM_HEADS * HEAD_DIM
CONV_K = 3
EPS = 1e-6
SCALE = HEAD_DIM ** -0.5

LANES = 128
SUBLANES = 8
MIX_WIDTH = SB_WIDTH + CONV_WIDTH + MEM_WIDTH
PROJ_COLS = 512
Q_BLOCK = 512
SB_TILE = 128
SWEEP_STOP = 104.0
TILE2_ROWS = 32
KV_PAD = Q_BLOCK
KV_PREV = 2 * SB_TILE
LOG2_E = 1.4426950408889634
LOGIT2_CLAMP = 126.0
STREAM_ROWS = 256
STREAM_COLS = 512
STREAM_EXTRA_STAGES = 4
VMEM_LIMIT_BYTES = 62 * 1024 * 1024


def _rms_scale(xf, eps=EPS):
    return lax.rsqrt(jnp.mean(xf * xf, axis=-1, keepdims=True) + eps)


def _dot_nt(a, b):
    return lax.dot_general(a, b, (((1,), (1,)), ((), ())), preferred_element_type=jnp.float32)


def _split_heads(a):
    first_half = lax.broadcasted_iota(jnp.int32, a.shape, 1) < HEAD_DIM
    zero = jnp.zeros_like(a)
    return jnp.concatenate([jnp.where(first_half, a, zero), jnp.where(first_half, zero, a)], axis=0)


def _sb_sweep(i, q_ref, kwin_ref, vwin_ref, kv_hbm, ktile_ref, vtile_ref, tile_sems, acc_ref, carry_ref, fillers,
              before_loop):
    n_sub = q_ref.shape[0] // SB_TILE
    n_pair = SB_WIDTH // LANES
    chains = [(sub, p) for sub in range(n_sub) for p in range(n_pair)]
    fillers = list(fillers)

    rt = lax.broadcasted_iota(jnp.int32, (2 * SB_TILE, 2 * SB_TILE), 0)
    ct = lax.broadcasted_iota(jnp.int32, (2 * SB_TILE, 2 * SB_TILE), 1)
    top, left = rt < SB_TILE, ct < SB_TILE
    tri2 = jnp.where(top, rt, rt - SB_TILE) >= jnp.where(left, ct, ct - SB_TILE)
    r2 = lax.broadcasted_iota(jnp.int32, (2 * SB_TILE, SB_TILE), 0)
    c2 = lax.broadcasted_iota(jnp.int32, (2 * SB_TILE, SB_TILE), 1)
    causal = c2 < jnp.where(r2 < SB_TILE, r2, r2 - SB_TILE)

    def mask_diagonal(a):
        return jnp.concatenate([a[:, :SB_TILE], jnp.where(causal, a[:, SB_TILE:], 0.0)], axis=1)

    def as_matrix(cond):
        return jnp.where(cond, 1.0, 0.0).astype(jnp.bfloat16)

    w_10 = as_matrix(jnp.logical_or(jnp.logical_and(jnp.logical_not(top), left),
                                    jnp.logical_and(tri2, jnp.logical_not(jnp.logical_xor(top, left)))))
    w_2 = as_matrix(jnp.logical_or(top, jnp.logical_or(tri2, jnp.logical_not(left))))
    w_1 = w_2[SB_TILE:]

    q_stack = [_split_heads(q_ref[sub * SB_TILE:(sub + 1) * SB_TILE, p * LANES:(p + 1) * LANES])
               for sub, p in chains]

    def window_rows(c, m, n_tiles=1):
        start = KV_PREV + (chains[c][0] - (m + n_tiles - 1)) * SB_TILE
        assert start >= 0
        return slice(start, start + n_tiles * SB_TILE)

    def head_rows(a, n):
        return jnp.concatenate([a[:n], a[SB_TILE:SB_TILE + n]], axis=0)

    def logits(q, keys):
        return jnp.minimum(_dot_nt(q, keys), LOGIT2_CLAMP)

    def window_keys(c, rows):
        p = chains[c][1]
        return kwin_ref[rows, p * LANES:(p + 1) * LANES]

    def window_values(c, rows):
        p = chains[c][1]
        return _split_heads(vwin_ref[rows, p * LANES:(p + 1) * LANES])

    def neg_log2_not_beta(z):
        return jnp.log(1.0 + jnp.exp2(z)) * LOG2_E

    def weights(z, s, diagonal=False):
        w = jnp.exp2(z - s)
        if diagonal:
            w = mask_diagonal(w)
        wb = w.astype(jnp.bfloat16)
        n = wb.shape[0] // 2
        return jnp.concatenate([wb[:n], wb[n:]], axis=1)

    def all_done(carries):
        floor = carries[0]
        for carry in carries[1:]:
            floor = jnp.minimum(floor, carry)
        return jnp.min(floor) >= SWEEP_STOP * LOG2_E

    def first_sweep():
        n = TILE2_ROWS
        rows_10 = [window_rows(c, 0, 2) for c in range(len(chains))]
        rows_2 = [window_rows(c, 2) for c in range(len(chains))]
        zs = [(logits(q_stack[c], window_keys(c, rows_10[c])),
               logits(head_rows(q_stack[c], n), window_keys(c, rows_2[c]))) for c in range(len(chains))]
        if fillers:
            fillers.pop(0)()
        lhs_10, lhs_2 = [], []
        for z10, z2 in zs:
            l10 = mask_diagonal(neg_log2_not_beta(z10))
            lhs_10.append(l10.astype(jnp.bfloat16))
            lhs_2.append(jnp.concatenate(
                [(head_rows(l10[:, :SB_TILE], n) + head_rows(l10[:, SB_TILE:], n)).astype(jnp.bfloat16),
                 neg_log2_not_beta(z2).astype(jnp.bfloat16)], axis=1))
        s_10 = jnp.dot(jnp.concatenate(lhs_10, axis=0), w_10, preferred_element_type=jnp.float32)
        s_2 = jnp.dot(jnp.concatenate(lhs_2, axis=0), w_2, preferred_element_type=jnp.float32)
        if fillers:
            fillers.pop(0)()
        reached, w_10s, w_2s = [], [], []
        for c, (z10, z2) in enumerate(zs):
            part = slice(c * 2 * SB_TILE, (c + 1) * 2 * SB_TILE)
            part2 = slice(c * 2 * n, (c + 1) * 2 * n)
            w_10s.append(weights(z10, s_10[part], diagonal=True))
            w_2s.append(weights(z2, s_2[part2, :SB_TILE]))
            after_1 = s_10[part, 0:1]
            after_2 = s_2[part2, SB_TILE:SB_TILE + 1]
            carry_ref[c] = after_1
            reached.append(jnp.concatenate([after_2[:n], after_1[n:SB_TILE],
                                            after_2[n:], after_1[SB_TILE + n:]], axis=0))
        done = all_done(reached)
        for c in range(len(chains)):
            acc = jnp.dot(w_10s[c], window_values(c, rows_10[c]), preferred_element_type=jnp.float32)
            tile_2 = jnp.dot(w_2s[c], window_values(c, rows_2[c]), preferred_element_type=jnp.float32)
            acc_ref[c, n:, :] = acc[n:]
            acc_ref[c, :n, :] = acc[:n] + jnp.where(done, tile_2, 0.0)
        return 2, done

    def tile_copies(m):
        copies = []
        for c, (sub, p) in enumerate(chains):
            row0 = pl.multiple_of(KV_PAD + (n_sub * i + sub - m) * SB_TILE, SB_TILE)
            for which, dst in ((0, ktile_ref), (1, vtile_ref)):
                copies.append(pltpu.make_async_copy(
                    kv_hbm.at[which, pl.ds(row0, SB_TILE), pl.ds(p * LANES, LANES)], dst.at[c],
                    tile_sems.at[which, c]))
        return copies

    def next_sweep(m):
        for copy in tile_copies(m):
            copy.start()
        for copy in tile_copies(m):
            copy.wait()
        zs = [logits(q_stack[c], ktile_ref[c]) for c in range(len(chains))]
        ls = [neg_log2_not_beta(z).astype(jnp.bfloat16) for z in zs]
        cs = jnp.dot(jnp.concatenate(ls, axis=0), w_1, preferred_element_type=jnp.float32)
        carries, w_cats = [], []
        for c, z in enumerate(zs):
            part = slice(c * 2 * SB_TILE, (c + 1) * 2 * SB_TILE)
            old = carry_ref[c]
            w_cats.append(weights(z, cs[part, :SB_TILE] + old))
            carries.append(cs[part, SB_TILE:SB_TILE + 1] + old)
            carry_ref[c] = carries[c]
        for c in range(len(chains)):
            acc_ref[c] += jnp.dot(w_cats[c], _split_heads(vtile_ref[c]), preferred_element_type=jnp.float32)
        return all_done(carries)

    m0, done0 = first_sweep()
    before_loop()
    last = n_sub * i + n_sub - 1

    def cond(state):
        m, done = state
        return jnp.logical_and(m <= last, jnp.logical_not(done))

    def body(state):
        m, _ = state
        return m + 1, next_sweep(m)

    lax.while_loop(cond, body, (jnp.int32(m0), done0))
    return jnp.concatenate(
        [jnp.concatenate([acc_ref[sub * n_pair + p] for p in range(n_pair)], axis=1)
         for sub in range(n_sub)], axis=0)


def _mem_attention(q_ref, mkv_ref, filler):
    q_rows = q_ref.shape[0]
    n_mem = mkv_ref.shape[0]
    n_pair = MEM_WIDTH // LANES
    head_ones = ((lax.broadcasted_iota(jnp.int32, (2 * n_mem, LANES), 1) < HEAD_DIM)
                 == (lax.broadcasted_iota(jnp.int32, (2 * n_mem, LANES), 0) < n_mem)
                 ).astype(jnp.float32).astype(jnp.bfloat16)
    logits = [_dot_nt(_split_heads(q_ref[:, SB_WIDTH + p * LANES:SB_WIDTH + (p + 1) * LANES]),
                      mkv_ref[:, p * LANES:(p + 1) * LANES]) for p in range(n_pair)]
    filler()
    outs = []
    for p, s in enumerate(logits):
        mv = mkv_ref[:, MEM_WIDTH + p * LANES:MEM_WIDTH + (p + 1) * LANES]
        e = jnp.exp(s - jnp.max(s, axis=-1, keepdims=True)).astype(jnp.bfloat16)
        e_cat = jnp.concatenate([e[:q_rows], e[q_rows:]], axis=1)
        num_den = jnp.dot(e_cat, jnp.concatenate([_split_heads(mv), head_ones], axis=1),
                          preferred_element_type=jnp.float32)
        outs.append(num_den[:, :LANES] / num_den[:, LANES:])
    return jnp.concatenate(outs, axis=-1)


def _stream_chunks(jobs, stages, sems):
    depth = len(stages)

    def copy(j):
        return pltpu.make_async_copy(jobs[j][0], stages[j % depth], sems.at[j % depth])

    for j in range(min(depth - 1, len(jobs))):
        copy(j).start()
    for j, (_, consume) in enumerate(jobs):
        if j + depth - 1 < len(jobs):
            copy(j + depth - 1).start()
        copy(j).wait()
        consume(stages[j % depth])


def _chunks(hbm):
    _, n_rows, n_cols = hbm.shape
    return [(r0, c0, hbm.at[0, pl.ds(r0, STREAM_ROWS), pl.ds(c0, STREAM_COLS)])
            for r0 in range(0, n_rows, STREAM_ROWS) for c0 in range(0, n_cols, STREAM_COLS)]


def _load_parameters(mem_hbm, g_mem_ref, w_kv_hbm, w_in_hbm, w_in_groups, w_out_hbm, mkv_ref, w_in_ref, w_out_ref,
                     stages, sems):
    jobs = []
    state = {"mem": [], "h": None, "kv": None}

    def take_mem(window):
        state["mem"].append(window[...])

    def memory_kv(window, r0):
        if state["h"] is None:
            m = jnp.concatenate(state["mem"], axis=1)
            state["h"] = (m * _rms_scale(m) * g_mem_ref[...]).astype(jnp.bfloat16)
        part = jnp.dot(state["h"][:, r0:r0 + STREAM_ROWS], window[...].astype(jnp.bfloat16),
                       preferred_element_type=jnp.float32)
        state["kv"] = part if state["kv"] is None else state["kv"] + part
        if r0 + STREAM_ROWS == w_kv_hbm.shape[1]:
            mkv_ref[...] = state["kv"].astype(jnp.bfloat16)

    def cast_into(dst, r0, c0, scale=1.0, then=None):
        def consume(window):
            w = window[...] if scale == 1.0 else window[...] * scale
            dst[r0:r0 + STREAM_ROWS, c0:c0 + STREAM_COLS] = w.astype(jnp.bfloat16)
            if then is not None:
                then()
        return consume

    assert mem_hbm.shape[1] == STREAM_ROWS and w_kv_hbm.shape[2] == STREAM_COLS
    jobs += [(chunk, take_mem) for _, _, chunk in _chunks(mem_hbm)]
    jobs += [(chunk, lambda window, r0=r0: memory_kv(window, r0)) for r0, _, chunk in _chunks(w_kv_hbm)]
    assert (sorted(c0 for columns, _, _ in w_in_groups for c0 in columns)
            == list(range(0, w_in_hbm.shape[2], STREAM_COLS)))
    for columns, scale, then in w_in_groups:
        group = [job for job in _chunks(w_in_hbm) if job[1] in columns]
        jobs += [(chunk, cast_into(w_in_ref, r0, c0, scale, then if n + 1 == len(group) else None))
                 for n, (r0, c0, chunk) in enumerate(group)]
    jobs += [(chunk, cast_into(w_out_ref, r0, c0)) for r0, c0, chunk in _chunks(w_out_hbm)]
    _stream_chunks(jobs, stages, sems)


def _layer_kernel(x_ref, x_res_ref, mem_hbm, g_in_ref, w_in_hbm, convw_ref, convb_ref, g_memin_ref, w_kv_hbm,
                  g_sb_ref, g_conv_ref, g_mem_ref, w_out_hbm, g_final_ref,
                  out_ref, kv_hbm, w_in_ref, w_out_ref, mkv_ref, stream_sems, stage_ref, kwin_ref, vwin_ref, kv_next_ref,
                  ktile_ref, vtile_ref, kv_sems, tile_sems, q2_ref, rest2_ref, res_ref, cu_ref,
                  acc_ref, carry_ref):
    s = pl.program_id(0)
    n_blk = pl.num_programs(0) - 2
    q_rows = x_ref.shape[0]
    c_u = 3 * SB_WIDTH
    c_qm = c_u + 3 * CONV_WIDTH
    c_gate = c_qm + MEM_WIDTH

    def normed_input():
        x = x_ref[...]
        return (x * _rms_scale(x) * g_in_ref[...]).astype(jnp.bfloat16)

    def proj(h, c0, width):
        return jnp.dot(h, w_in_ref[:, c0:c0 + width], preferred_element_type=jnp.float32)

    def project_kv(h):
        kv_next_ref[...] = proj(h, SB_WIDTH, 2 * SB_WIDTH).astype(jnp.bfloat16)

    def project_q_conv(h, slot):
        q2_ref[slot, :, :SB_WIDTH] = (proj(h, 0, SB_WIDTH) * (SCALE * LOG2_E)).astype(jnp.bfloat16)
        q2_ref[slot, :, SB_WIDTH:] = (proj(h, c_qm, MEM_WIDTH) * SCALE).astype(jnp.bfloat16)
        rest2_ref[slot, :, :3 * CONV_WIDTH] = proj(h, c_u, 3 * CONV_WIDTH)

    def project_gate(h, slot):
        for c0 in range(0, MIX_WIDTH, PROJ_COLS):
            rest2_ref[slot, :, 3 * CONV_WIDTH + c0:3 * CONV_WIDTH + c0 + PROJ_COLS] = proj(h, c_gate + c0, PROJ_COLS)

    def final_norm():
        res = res_ref[...]
        out_ref[...] = res * _rms_scale(res) * g_final_ref[...]

    @pl.when(s == 0)
    def _():
        stages = [ref.at[pl.ds(r0, STREAM_ROWS), pl.ds(c0, STREAM_COLS)] for ref in (res_ref, rest2_ref.at[1])
                  for r0 in range(0, ref.shape[0] - STREAM_ROWS + 1, STREAM_ROWS)
                  for c0 in range(0, ref.shape[-1] - STREAM_COLS + 1, STREAM_COLS)]
        stages += [stage_ref.at[k] for k in range(stage_ref.shape[0])]
        assert len(stages) == stream_sems.shape[0]
        h = normed_input()
        w_in_groups = [(range(SB_WIDTH, 3 * SB_WIDTH, STREAM_COLS), 1.0, lambda: project_kv(h)),
                       ([0, *range(c_u, c_gate, STREAM_COLS)], 1.0, lambda: project_q_conv(h, 0)),
                       (range(c_gate, c_gate + MIX_WIDTH, STREAM_COLS), 0.5, lambda: project_gate(h, 0))]
        _load_parameters(mem_hbm, g_memin_ref, w_kv_hbm, w_in_hbm, w_in_groups, w_out_hbm, mkv_ref, w_in_ref,
                         w_out_ref, stages, stream_sems)
        res_ref[...] = jnp.zeros(res_ref.shape, jnp.float32)
        cu_ref[0:SUBLANES, :] = jnp.zeros((SUBLANES, CONV_WIDTH), jnp.float32)
        kwin_ref[...] = jnp.zeros(kwin_ref.shape, jnp.bfloat16)
        vwin_ref[...] = jnp.zeros(vwin_ref.shape, jnp.bfloat16)
        zero_rows = [pltpu.make_async_copy(win.at[pl.ds(0, KV_PAD)], kv_hbm.at[which, pl.ds(0, KV_PAD)],
                                           kv_sems.at[which])
                     for which, win in ((0, kwin_ref), (1, vwin_ref))]
        for copy in zero_rows:
            copy.start()
        for copy in zero_rows:
            copy.wait()

    @pl.when(s == n_blk + 1)
    def _():
        final_norm()

    def mix_block(project_next):
        i = s - 1
        cur = lax.rem(i, 2)
        nxt = 1 - cur
        q_ref = q2_ref.at[cur]
        rest_ref = rest2_ref.at[cur]

        def history_writes(block):
            rows = pl.ds(pl.multiple_of(KV_PAD + block * q_rows, q_rows), q_rows)
            return [pltpu.make_async_copy(win.at[pl.ds(KV_PREV, q_rows)], kv_hbm.at[which, rows],
                                          kv_sems.at[which])
                    for which, win in ((0, kwin_ref), (1, vwin_ref))]

        for win, c0 in ((kwin_ref, 0), (vwin_ref, SB_WIDTH)):
            win[0:KV_PREV, :] = win[q_rows:q_rows + KV_PREV, :]
            win[KV_PREV:, :] = kv_next_ref[:, c0:c0 + SB_WIDTH]
        for copy in history_writes(i):
            copy.start()
        final_norm()

        h_next = normed_input() if project_next else None
        y_sb = _sb_sweep(i, q_ref, kwin_ref, vwin_ref, kv_hbm, ktile_ref, vtile_ref, tile_sems, acc_ref, carry_ref,
                         (lambda: project_kv(h_next), lambda: project_q_conv(h_next, nxt)) if project_next else (),
                         lambda: [copy.wait() for copy in history_writes(i)])

        u = rest_ref[:, 0:CONV_WIDTH]
        b = rest_ref[:, CONV_WIDTH:2 * CONV_WIDTH]
        c = rest_ref[:, 2 * CONV_WIDTH:3 * CONV_WIDTH]
        cu = c * u
        cu_ref[SUBLANES:, :] = cu
        cu_1 = cu_ref[SUBLANES - 1:SUBLANES - 1 + q_rows, :]
        cu_2 = cu_ref[SUBLANES - 2:SUBLANES - 2 + q_rows, :]
        cu_ref[0:SUBLANES, :] = cu[q_rows - SUBLANES:, :]
        conv = (convw_ref[0] * cu_2 + convw_ref[1] * cu_1 + convw_ref[2] * cu
                + convb_ref[...])
        y_conv = b * conv

        y_mem = _mem_attention(q_ref, mkv_ref,
                               (lambda: project_gate(h_next, nxt)) if project_next else (lambda: None))

        y = jnp.concatenate([y_sb * _rms_scale(y_sb) * g_sb_ref[...],
                             y_conv * _rms_scale(y_conv) * g_conv_ref[...],
                             y_mem * _rms_scale(y_mem) * g_mem_ref[...]], axis=-1)
        half_gate = rest_ref[:, 3 * CONV_WIDTH:]
        gated = (y * (half_gate + half_gate * jnp.tanh(half_gate))).astype(jnp.bfloat16)
        res_ref[...] = x_res_ref[...] + jnp.dot(gated, w_out_ref[...], preferred_element_type=jnp.float32)

    @pl.when(jnp.logical_and(s > 0, s < n_blk))
    def _():
        mix_block(True)

    @pl.when(s == n_blk)
    def _():
        mix_block(False)


def _layer(x2, mem, g_in, w_in, conv_w, conv_b, g_memin, w_kv, g_sb, g_conv, g_mem, w_out, g_final):
    t, d = x2.shape
    assert t % Q_BLOCK == 0 and Q_BLOCK % SB_TILE == 0 and KV_PREV <= Q_BLOCK
    assert Q_BLOCK >= STREAM_ROWS
    for a in (mem, w_kv, w_in, w_out):
        assert a.shape[1] % STREAM_ROWS == 0 and a.shape[2] % STREAM_COLS == 0
    n_chain = (Q_BLOCK // SB_TILE) * (SB_WIDTH // LANES)
    n_idle_stages = (Q_BLOCK // STREAM_ROWS) * (d // STREAM_COLS + (3 * CONV_WIDTH + MIX_WIDTH) // STREAM_COLS)

    def whole(a):
        return pl.BlockSpec(a.shape, lambda s: (0,) * a.ndim)

    in_hbm = pl.BlockSpec(memory_space=pl.ANY)
    n_blk = t // Q_BLOCK
    return pl.pallas_call(
        _layer_kernel,
        grid=(n_blk + 2,),
        in_specs=[pl.BlockSpec((Q_BLOCK, d), lambda s: (jnp.minimum(s, n_blk - 1), 0)),
                  pl.BlockSpec((Q_BLOCK, d), lambda s: (jnp.clip(s - 1, 0, n_blk - 1), 0)),
                  in_hbm, whole(g_in),
                  in_hbm, whole(conv_w), whole(conv_b), whole(g_memin), in_hbm, whole(g_sb), whole(g_conv),
                  whole(g_mem), in_hbm, whole(g_final)],
        out_specs=[pl.BlockSpec((Q_BLOCK, d), lambda s: (jnp.clip(s - 2, 0, n_blk - 1), 0)), in_hbm],
        out_shape=[jax.ShapeDtypeStruct((t, d), jnp.float32),
                   jax.ShapeDtypeStruct((2, KV_PAD + t, SB_WIDTH), jnp.bfloat16)],
        scratch_shapes=[pltpu.VMEM(w_in.shape[1:], jnp.bfloat16),
                        pltpu.VMEM(w_out.shape[1:], jnp.bfloat16),
                        pltpu.VMEM((mem.shape[1], w_kv.shape[2]), jnp.bfloat16),
                        pltpu.SemaphoreType.DMA((n_idle_stages + STREAM_EXTRA_STAGES,)),
                        pltpu.VMEM((STREAM_EXTRA_STAGES, STREAM_ROWS, STREAM_COLS), jnp.float32),
                        pltpu.VMEM((KV_PREV + Q_BLOCK, SB_WIDTH), jnp.bfloat16),
                        pltpu.VMEM((KV_PREV + Q_BLOCK, SB_WIDTH), jnp.bfloat16),
                        pltpu.VMEM((Q_BLOCK, 2 * SB_WIDTH), jnp.bfloat16),
                        pltpu.VMEM((n_chain, SB_TILE, LANES), jnp.bfloat16),
                        pltpu.VMEM((n_chain, SB_TILE, LANES), jnp.bfloat16),
                        pltpu.SemaphoreType.DMA((2,)),
                        pltpu.SemaphoreType.DMA((2, n_chain)),
                        pltpu.VMEM((2, Q_BLOCK, SB_WIDTH + MEM_WIDTH), jnp.bfloat16),
                        pltpu.VMEM((2, Q_BLOCK, 3 * CONV_WIDTH + MIX_WIDTH), jnp.float32),
                        pltpu.VMEM((Q_BLOCK, d), jnp.float32),
                        pltpu.VMEM((SUBLANES + Q_BLOCK, CONV_WIDTH), jnp.float32),
                        pltpu.VMEM((n_chain, SB_TILE, LANES), jnp.float32),
                        pltpu.VMEM((n_chain, 2 * SB_TILE, 1), jnp.float32)],
        compiler_params=pltpu.CompilerParams(
            dimension_semantics=("arbitrary",), vmem_limit_bytes=VMEM_LIMIT_BYTES),
        name="layer",
    )(x2, x2, mem, g_in, w_in, conv_w, conv_b, g_memin, w_kv, g_sb, g_conv, g_mem, w_out, g_final)[0]


def kernel(x, mem, g_in, w_in, conv_w, conv_b, g_mem, w_mem_kv, g_sb_out, g_conv_out,
           g_mem_out, w_out, g_final):
    batch, t, d = x.shape
    assert batch == 1 and g_in.shape[0] == 1
    out = _layer(x.reshape(t, d), mem, g_in[0][None, :], w_in, jnp.transpose(conv_w, (1, 0, 2)), conv_b[0][None, :],
                 g_mem[0][None, :], w_mem_kv, g_sb_out[0][None, :], g_conv_out[0][None, :],
                 g_mem_out[0][None, :], w_out, g_final[None, :])
    return out.reshape(batch, t, d)
```

```python
import jax
import jax.numpy as jnp
from jax import lax
from jax.experimental import pallas as pl
from jax.experimental.pallas import tpu as pltpu

HEAD_DIM = 64
SB_HEADS = 8
SB_WIDTH = SB_HEADS * HEAD_DIM
CONV_WIDTH = 4 * HEAD_DIM
MEM_HEADS = 4
MEM_WIDTH = MEM_HEADS * HEAD_DIM
CONV_K = 3
EPS = 1e-6
SCALE = HEAD_DIM ** -0.5

LANES = 128
SUBLANES = 8
MIX_WIDTH = SB_WIDTH + CONV_WIDTH + MEM_WIDTH
PROJ_COLS = 512
Q_BLOCK = 512
SB_TILE = 128
SWEEP_STOP = 104.0
TILE2_ROWS = 32
KV_PAD = Q_BLOCK
KV_PREV = 2 * SB_TILE
LOG2_E = 1.4426950408889634
LOGIT2_CLAMP = 126.0
STREAM_ROWS = 256
STREAM_COLS = 512
STREAM_EXTRA_STAGES = 4
VMEM_LIMIT_BYTES = 62 * 1024 * 1024


def _rms_scale(xf, eps=EPS):
    return lax.rsqrt(jnp.mean(xf * xf, axis=-1, keepdims=True) + eps)


def _dot_nt(a, b):
    return lax.dot_general(a, b, (((1,), (1,)), ((), ())), preferred_element_type=jnp.float32)


def _split_heads(a):
    first_half = lax.broadcasted_iota(jnp.int32, a.shape, 1) < HEAD_DIM
    zero = jnp.zeros_like(a)
    return jnp.concatenate([jnp.where(first_half, a, zero), jnp.where(first_half, zero, a)], axis=0)


def _sb_sweep(i, q_ref, kwin_ref, vwin_ref, kv_hbm, ktile_ref, vtile_ref, tile_sems, acc_ref, carry_ref, fillers,
              before_loop):
    n_sub = q_ref.shape[0] // SB_TILE
    n_pair = SB_WIDTH // LANES
    chains = [(sub, p) for sub in range(n_sub) for p in range(n_pair)]
    fillers = list(fillers)

    rt = lax.broadcasted_iota(jnp.int32, (2 * SB_TILE, 2 * SB_TILE), 0)
    ct = lax.broadcasted_iota(jnp.int32, (2 * SB_TILE, 2 * SB_TILE), 1)
    top, left = rt < SB_TILE, ct < SB_TILE
    tri2 = jnp.where(top, rt, rt - SB_TILE) >= jnp.where(left, ct, ct - SB_TILE)
    r2 = lax.broadcasted_iota(jnp.int32, (2 * SB_TILE, SB_TILE), 0)
    c2 = lax.broadcasted_iota(jnp.int32, (2 * SB_TILE, SB_TILE), 1)
    causal = c2 < jnp.where(r2 < SB_TILE, r2, r2 - SB_TILE)

    def mask_diagonal(a):
        return jnp.concatenate([a[:, :SB_TILE], jnp.where(causal, a[:, SB_TILE:], 0.0)], axis=1)

    def as_matrix(cond):
        return jnp.where(cond, 1.0, 0.0).astype(jnp.bfloat16)

    w_10 = as_matrix(jnp.logical_or(jnp.logical_and(jnp.logical_not(top), left),
                                    jnp.logical_and(tri2, jnp.logical_not(jnp.logical_xor(top, left)))))
    w_2 = as_matrix(jnp.logical_or(top, jnp.logical_or(tri2, jnp.logical_not(left))))
    w_1 = w_2[SB_TILE:]

    q_stack = [_split_heads(q_ref[sub * SB_TILE:(sub + 1) * SB_TILE, p * LANES:(p + 1) * LANES])
               for sub, p in chains]

    def window_rows(c, m, n_tiles=1):
        start = KV_PREV + (chains[c][0] - (m + n_tiles - 1)) * SB_TILE
        assert start >= 0
        return slice(start, start + n_tiles * SB_TILE)

    def head_rows(a, n):
        return jnp.concatenate([a[:n], a[SB_TILE:SB_TILE + n]], axis=0)

    def logits(q, keys):
        return jnp.minimum(_dot_nt(q, keys), LOGIT2_CLAMP)

    def window_keys(c, rows):
        p = chains[c][1]
        return kwin_ref[rows, p * LANES:(p + 1) * LANES]

    def window_values(c, rows):
        p = chains[c][1]
        return _split_heads(vwin_ref[rows, p * LANES:(p + 1) * LANES])

    def neg_log2_not_beta(z):
        return jnp.log(1.0 + jnp.exp2(z)) * LOG2_E

    def weights(z, s, diagonal=False):
        w = jnp.exp2(z - s)
        if diagonal:
            w = mask_diagonal(w)
        wb = w.astype(jnp.bfloat16)
        n = wb.shape[0] // 2
        return jnp.concatenate([wb[:n], wb[n:]], axis=1)

    def all_done(carries):
        floor = carries[0]
        for carry in carries[1:]:
            floor = jnp.minimum(floor, carry)
        return jnp.min(floor) >= SWEEP_STOP * LOG2_E

    def first_sweep():
        n = TILE2_ROWS
        rows_10 = [window_rows(c, 0, 2) for c in range(len(chains))]
        rows_2 = [window_rows(c, 2) for c in range(len(chains))]
        zs = [(logits(q_stack[c], window_keys(c, rows_10[c])),
               logits(head_rows(q_stack[c], n), window_keys(c, rows_2[c]))) for c in range(len(chains))]
        if fillers:
            fillers.pop(0)()
        lhs_10, lhs_2 = [], []
        for z10, z2 in zs:
            l10 = mask_diagonal(neg_log2_not_beta(z10))
            lhs_10.append(l10.astype(jnp.bfloat16))
            lhs_2.append(jnp.concatenate(
                [(head_rows(l10[:, :SB_TILE], n) + head_rows(l10[:, SB_TILE:], n)).astype(jnp.bfloat16),
                 neg_log2_not_beta(z2).astype(jnp.bfloat16)], axis=1))
        s_10 = jnp.dot(jnp.concatenate(lhs_10, axis=0), w_10, preferred_element_type=jnp.float32)
        s_2 = jnp.dot(jnp.concatenate(lhs_2, axis=0), w_2, preferred_element_type=jnp.float32)
        if fillers:
            fillers.pop(0)()
        reached, w_10s, w_2s = [], [], []
        for c, (z10, z2) in enumerate(zs):
            part = slice(c * 2 * SB_TILE, (c + 1) * 2 * SB_TILE)
            part2 = slice(c * 2 * n, (c + 1) * 2 * n)
            w_10s.append(weights(z10, s_10[part], diagonal=True))
            w_2s.append(weights(z2, s_2[part2, :SB_TILE]))
            after_1 = s_10[part, 0:1]
            after_2 = s_2[part2, SB_TILE:SB_TILE + 1]
            carry_ref[c] = after_1
            reached.append(jnp.concatenate([after_2[:n], after_1[n:SB_TILE],
                                            after_2[n:], after_1[SB_TILE + n:]], axis=0))
        done = all_done(reached)
        for c in range(len(chains)):
            acc = jnp.dot(w_10s[c], window_values(c, rows_10[c]), preferred_element_type=jnp.float32)
            tile_2 = jnp.dot(w_2s[c], window_values(c, rows_2[c]), preferred_element_type=jnp.float32)
            acc_ref[c, n:, :] = acc[n:]
            acc_ref[c, :n, :] = acc[:n] + jnp.where(done, tile_2, 0.0)
        return 2, done

    def tile_copies(m):
        copies = []
        for c, (sub, p) in enumerate(chains):
            row0 = pl.multiple_of(KV_PAD + (n_sub * i + sub - m) * SB_TILE, SB_TILE)
            for which, dst in ((0, ktile_ref), (1, vtile_ref)):
                copies.append(pltpu.make_async_copy(
                    kv_hbm.at[which, pl.ds(row0, SB_TILE), pl.ds(p * LANES, LANES)], dst.at[c],
                    tile_sems.at[which, c]))
        return copies

    def next_sweep(m):
        for copy in tile_copies(m):
            copy.start()
        for copy in tile_copies(m):
            copy.wait()
        zs = [logits(q_stack[c], ktile_ref[c]) for c in range(len(chains))]
        ls = [neg_log2_not_beta(z).astype(jnp.bfloat16) for z in zs]
        cs = jnp.dot(jnp.concatenate(ls, axis=0), w_1, preferred_element_type=jnp.float32)
        carries, w_cats = [], []
        for c, z in enumerate(zs):
            part = slice(c * 2 * SB_TILE, (c + 1) * 2 * SB_TILE)
            old = carry_ref[c]
            w_cats.append(weights(z, cs[part, :SB_TILE] + old))
            carries.append(cs[part, SB_TILE:SB_TILE + 1] + old)
            carry_ref[c] = carries[c]
        for c in range(len(chains)):
            acc_ref[c] += jnp.dot(w_cats[c], _split_heads(vtile_ref[c]), preferred_element_type=jnp.float32)
        return all_done(carries)

    m0, done0 = first_sweep()
    before_loop()
    last = n_sub * i + n_sub - 1

    def cond(state):
        m, done = state
        return jnp.logical_and(m <= last, jnp.logical_not(done))

    def body(state):
        m, _ = state
        return m + 1, next_sweep(m)

    lax.while_loop(cond, body, (jnp.int32(m0), done0))
    return jnp.concatenate(
        [jnp.concatenate([acc_ref[sub * n_pair + p] for p in range(n_pair)], axis=1)
         for sub in range(n_sub)], axis=0)


def _mem_attention(q_ref, mkv_ref, filler):
    q_rows = q_ref.shape[0]
    n_mem = mkv_ref.shape[0]
    n_pair = MEM_WIDTH // LANES
    head_ones = ((lax.broadcasted_iota(jnp.int32, (2 * n_mem, LANES), 1) < HEAD_DIM)
                 == (lax.broadcasted_iota(jnp.int32, (2 * n_mem, LANES), 0) < n_mem)
                 ).astype(jnp.float32).astype(jnp.bfloat16)
    logits = [_dot_nt(_split_heads(q_ref[:, SB_WIDTH + p * LANES:SB_WIDTH + (p + 1) * LANES]),
                      mkv_ref[:, p * LANES:(p + 1) * LANES]) for p in range(n_pair)]
    filler()
    outs = []
    for p, s in enumerate(logits):
        mv = mkv_ref[:, MEM_WIDTH + p * LANES:MEM_WIDTH + (p + 1) * LANES]
        e = jnp.exp(s - jnp.max(s, axis=-1, keepdims=True)).astype(jnp.bfloat16)
        e_cat = jnp.concatenate([e[:q_rows], e[q_rows:]], axis=1)
        num_den = jnp.dot(e_cat, jnp.concatenate([_split_heads(mv), head_ones], axis=1),
                          preferred_element_type=jnp.float32)
        outs.append(num_den[:, :LANES] / num_den[:, LANES:])
    return jnp.concatenate(outs, axis=-1)


def _stream_chunks(jobs, stages, sems):
    depth = len(stages)

    def copy(j):
        return pltpu.make_async_copy(jobs[j][0], stages[j % depth], sems.at[j % depth])

    for j in range(min(depth - 1, len(jobs))):
        copy(j).start()
    for j, (_, consume) in enumerate(jobs):
        if j + depth - 1 < len(jobs):
            copy(j + depth - 1).start()
        copy(j).wait()
        consume(stages[j % depth])


def _chunks(hbm):
    _, n_rows, n_cols = hbm.shape
    return [(r0, c0, hbm.at[0, pl.ds(r0, STREAM_ROWS), pl.ds(c0, STREAM_COLS)])
            for r0 in range(0, n_rows, STREAM_ROWS) for c0 in range(0, n_cols, STREAM_COLS)]


def _load_parameters(mem_hbm, g_mem_ref, w_kv_hbm, w_in_hbm, w_in_groups, w_out_hbm, mkv_ref, w_in_ref, w_out_ref,
                     stages, sems):
    jobs = []
    state = {"mem": [], "h": None, "kv": None}

    def take_mem(window):
        state["mem"].append(window[...])

    def memory_kv(window, r0):
        if state["h"] is None:
            m = jnp.concatenate(state["mem"], axis=1)
            state["h"] = (m * _rms_scale(m) * g_mem_ref[...]).astype(jnp.bfloat16)
        part = jnp.dot(state["h"][:, r0:r0 + STREAM_ROWS], window[...].astype(jnp.bfloat16),
                       preferred_element_type=jnp.float32)
        state["kv"] = part if state["kv"] is None else state["kv"] + part
        if r0 + STREAM_ROWS == w_kv_hbm.shape[1]:
            mkv_ref[...] = state["kv"].astype(jnp.bfloat16)

    def cast_into(dst, r0, c0, then=None):
        def consume(window):
            dst[r0:r0 + STREAM_ROWS, c0:c0 + STREAM_COLS] = window[...].astype(jnp.bfloat16)
            if then is not None:
                then()
        return consume

    assert mem_hbm.shape[1] == STREAM_ROWS and w_kv_hbm.shape[2] == STREAM_COLS
    jobs += [(chunk, take_mem) for _, _, chunk in _chunks(mem_hbm)]
    jobs += [(chunk, lambda window, r0=r0: memory_kv(window, r0)) for r0, _, chunk in _chunks(w_kv_hbm)]
    assert sorted(c0 for columns, _ in w_in_groups for c0 in columns) == list(range(0, w_in_hbm.shape[2], STREAM_COLS))
    for columns, then in w_in_groups:
        group = [job for job in _chunks(w_in_hbm) if job[1] in columns]
        jobs += [(chunk, cast_into(w_in_ref, r0, c0, then if n + 1 == len(group) else None))
                 for n, (r0, c0, chunk) in enumerate(group)]
    jobs += [(chunk, cast_into(w_out_ref, r0, c0)) for r0, c0, chunk in _chunks(w_out_hbm)]
    _stream_chunks(jobs, stages, sems)


def _layer_kernel(x_ref, mem_hbm, g_in_ref, w_in_hbm, convw_ref, convb_ref, g_memin_ref, w_kv_hbm,
                  g_sb_ref, g_conv_ref, g_mem_ref, w_out_hbm, g_final_ref,
                  out_ref, kv_hbm, w_in_ref, w_out_ref, mkv_ref, stream_sems, stage_ref, kwin_ref, vwin_ref, kv_next_ref,
                  ktile_ref, vtile_ref, kv_sems, tile_sems, q2_ref, rest2_ref, x_stash_ref, stash_sem, res_ref, cu_ref,
                  acc_ref, carry_ref):
    s = pl.program_id(0)
    n_blk = pl.num_programs(0) - 2
    q_rows = x_ref.shape[0]
    c_u = 3 * SB_WIDTH
    c_qm = c_u + 3 * CONV_WIDTH
    c_gate = c_qm + MEM_WIDTH

    def normed_input():
        x = x_ref[...]
        return (x * _rms_scale(x) * g_in_ref[...]).astype(jnp.bfloat16)

    def proj(h, c0, width):
        return jnp.dot(h, w_in_ref[:, c0:c0 + width], preferred_element_type=jnp.float32)

    def project_kv(h):
        kv_next_ref[...] = proj(h, SB_WIDTH, 2 * SB_WIDTH).astype(jnp.bfloat16)

    def project_q_conv(h, slot):
        q2_ref[slot, :, :SB_WIDTH] = (proj(h, 0, SB_WIDTH) * (SCALE * LOG2_E)).astype(jnp.bfloat16)
        q2_ref[slot, :, SB_WIDTH:] = (proj(h, c_qm, MEM_WIDTH) * SCALE).astype(jnp.bfloat16)
        rest2_ref[slot, :, :3 * CONV_WIDTH] = proj(h, c_u, 3 * CONV_WIDTH)

    def project_gate(h, slot):
        for c0 in range(0, MIX_WIDTH, PROJ_COLS):
            rest2_ref[slot, :, 3 * CONV_WIDTH + c0:3 * CONV_WIDTH + c0 + PROJ_COLS] = proj(h, c_gate + c0, PROJ_COLS)

    def stash_copy(slot):
        return pltpu.make_async_copy(x_ref, x_stash_ref.at[slot], stash_sem.at[0])

    def final_norm():
        res = res_ref[...]
        out_ref[...] = res * _rms_scale(res) * g_final_ref[...]

    @pl.when(s == 0)
    def _():
        stages = [ref.at[pl.ds(r0, STREAM_ROWS), pl.ds(c0, STREAM_COLS)] for ref in (res_ref, rest2_ref.at[1])
                  for r0 in range(0, ref.shape[0] - STREAM_ROWS + 1, STREAM_ROWS)
                  for c0 in range(0, ref.shape[-1] - STREAM_COLS + 1, STREAM_COLS)]
        stages += [stage_ref.at[k] for k in range(stage_ref.shape[0])]
        assert len(stages) == stream_sems.shape[0]
        stash_copy(0).start()
        h = normed_input()
        w_in_groups = [(range(SB_WIDTH, 3 * SB_WIDTH, STREAM_COLS), lambda: project_kv(h)),
                       ([0, *range(c_u, c_gate, STREAM_COLS)], lambda: project_q_conv(h, 0)),
                       (range(c_gate, c_gate + MIX_WIDTH, STREAM_COLS), lambda: project_gate(h, 0))]
        _load_parameters(mem_hbm, g_memin_ref, w_kv_hbm, w_in_hbm, w_in_groups, w_out_hbm, mkv_ref, w_in_ref,
                         w_out_ref, stages, stream_sems)
        res_ref[...] = jnp.zeros(res_ref.shape, jnp.float32)
        cu_ref[0:SUBLANES, :] = jnp.zeros((SUBLANES, CONV_WIDTH), jnp.float32)
        kwin_ref[...] = jnp.zeros(kwin_ref.shape, jnp.bfloat16)
        vwin_ref[...] = jnp.zeros(vwin_ref.shape, jnp.bfloat16)
        zero_rows = [pltpu.make_async_copy(win.at[pl.ds(0, KV_PAD)], kv_hbm.at[which, pl.ds(0, KV_PAD)],
                                           kv_sems.at[which])
                     for which, win in ((0, kwin_ref), (1, vwin_ref))]
        for copy in zero_rows:
            copy.start()
        for copy in zero_rows:
            copy.wait()
        stash_copy(0).wait()

    @pl.when(s == n_blk + 1)
    def _():
        final_norm()

    def mix_block(project_next):
        i = s - 1
        cur = lax.rem(i, 2)
        nxt = 1 - cur
        q_ref = q2_ref.at[cur]
        rest_ref = rest2_ref.at[cur]
        if project_next:
            stash_copy(nxt).start()

        def history_writes(block):
            rows = pl.ds(pl.multiple_of(KV_PAD + block * q_rows, q_rows), q_rows)
            return [pltpu.make_async_copy(win.at[pl.ds(KV_PREV, q_rows)], kv_hbm.at[which, rows],
                                          kv_sems.at[which])
                    for which, win in ((0, kwin_ref), (1, vwin_ref))]

        for win, c0 in ((kwin_ref, 0), (vwin_ref, SB_WIDTH)):
            win[0:KV_PREV, :] = win[q_rows:q_rows + KV_PREV, :]
            win[KV_PREV:, :] = kv_next_ref[:, c0:c0 + SB_WIDTH]
        for copy in history_writes(i):
            copy.start()
        final_norm()

        h_next = normed_input() if project_next else None
        y_sb = _sb_sweep(i, q_ref, kwin_ref, vwin_ref, kv_hbm, ktile_ref, vtile_ref, tile_sems, acc_ref, carry_ref,
                         (lambda: project_kv(h_next), lambda: project_q_conv(h_next, nxt)) if project_next else (),
                         lambda: [copy.wait() for copy in history_writes(i)])

        u = rest_ref[:, 0:CONV_WIDTH]
        b = rest_ref[:, CONV_WIDTH:2 * CONV_WIDTH]
        c = rest_ref[:, 2 * CONV_WIDTH:3 * CONV_WIDTH]
        cu = c * u
        cu_ref[SUBLANES:, :] = cu
        cu_1 = cu_ref[SUBLANES - 1:SUBLANES - 1 + q_rows, :]
        cu_2 = cu_ref[SUBLANES - 2:SUBLANES - 2 + q_rows, :]
        cu_ref[0:SUBLANES, :] = cu[q_rows - SUBLANES:, :]
        conv = (convw_ref[0] * cu_2 + convw_ref[1] * cu_1 + convw_ref[2] * cu
                + convb_ref[...])
        y_conv = b * conv

        y_mem = _mem_attention(q_ref, mkv_ref,
                               (lambda: project_gate(h_next, nxt)) if project_next else (lambda: None))

        y = jnp.concatenate([y_sb * _rms_scale(y_sb) * g_sb_ref[...],
                             y_conv * _rms_scale(y_conv) * g_conv_ref[...],
                             y_mem * _rms_scale(y_mem) * g_mem_ref[...]], axis=-1)
        gate = rest_ref[:, 3 * CONV_WIDTH:]
        half_gate = 0.5 * gate
        gated = (y * (half_gate + half_gate * jnp.tanh(half_gate))).astype(jnp.bfloat16)
        res_ref[...] = x_stash_ref[cur] + jnp.dot(gated, w_out_ref[...], preferred_element_type=jnp.float32)
        if project_next:
            stash_copy(nxt).wait()

    @pl.when(jnp.logical_and(s > 0, s < n_blk))
    def _():
        mix_block(True)

    @pl.when(s == n_blk)
    def _():
        mix_block(False)


def _layer(x2, mem, g_in, w_in, conv_w, conv_b, g_memin, w_kv, g_sb, g_conv, g_mem, w_out, g_final):
    t, d = x2.shape
    assert t % Q_BLOCK == 0 and Q_BLOCK % SB_TILE == 0 and KV_PREV <= Q_BLOCK
    assert Q_BLOCK >= STREAM_ROWS
    for a in (mem, w_kv, w_in, w_out):
        assert a.shape[1] % STREAM_ROWS == 0 and a.shape[2] % STREAM_COLS == 0
    n_chain = (Q_BLOCK // SB_TILE) * (SB_WIDTH // LANES)
    n_idle_stages = (Q_BLOCK // STREAM_ROWS) * (d // STREAM_COLS + (3 * CONV_WIDTH + MIX_WIDTH) // STREAM_COLS)

    def whole(a):
        return pl.BlockSpec(a.shape, lambda s: (0,) * a.ndim)

    in_hbm = pl.BlockSpec(memory_space=pl.ANY)
    n_blk = t // Q_BLOCK
    return pl.pallas_call(
        _layer_kernel,
        grid=(n_blk + 2,),
        in_specs=[pl.BlockSpec((Q_BLOCK, d), lambda s: (jnp.minimum(s, n_blk - 1), 0)), in_hbm, whole(g_in),
                  in_hbm, whole(conv_w), whole(conv_b), whole(g_memin), in_hbm, whole(g_sb), whole(g_conv),
                  whole(g_mem), in_hbm, whole(g_final)],
        out_specs=[pl.BlockSpec((Q_BLOCK, d), lambda s: (jnp.clip(s - 2, 0, n_blk - 1), 0)), in_hbm],
        out_shape=[jax.ShapeDtypeStruct((t, d), jnp.float32),
                   jax.ShapeDtypeStruct((2, KV_PAD + t, SB_WIDTH), jnp.bfloat16)],
        scratch_shapes=[pltpu.VMEM(w_in.shape[1:], jnp.bfloat16),
                        pltpu.VMEM(w_out.shape[1:], jnp.bfloat16),
                        pltpu.VMEM((mem.shape[1], w_kv.shape[2]), jnp.bfloat16),
                        pltpu.SemaphoreType.DMA((n_idle_stages + STREAM_EXTRA_STAGES,)),
                        pltpu.VMEM((STREAM_EXTRA_STAGES, STREAM_ROWS, STREAM_COLS), jnp.float32),
                        pltpu.VMEM((KV_PREV + Q_BLOCK, SB_WIDTH), jnp.bfloat16),
                        pltpu.VMEM((KV_PREV + Q_BLOCK, SB_WIDTH), jnp.bfloat16),
                        pltpu.VMEM((Q_BLOCK, 2 * SB_WIDTH), jnp.bfloat16),
                        pltpu.VMEM((n_chain, SB_TILE, LANES), jnp.bfloat16),
                        pltpu.VMEM((n_chain, SB_TILE, LANES), jnp.bfloat16),
                        pltpu.SemaphoreType.DMA((2,)),
                        pltpu.SemaphoreType.DMA((2, n_chain)),
                        pltpu.VMEM((2, Q_BLOCK, SB_WIDTH + MEM_WIDTH), jnp.bfloat16),
                        pltpu.VMEM((2, Q_BLOCK, 3 * CONV_WIDTH + MIX_WIDTH), jnp.float32),
                        pltpu.VMEM((2, Q_BLOCK, d), jnp.float32),
                        pltpu.SemaphoreType.DMA((1,)),
                        pltpu.VMEM((Q_BLOCK, d), jnp.float32),
                        pltpu.VMEM((SUBLANES + Q_BLOCK, CONV_WIDTH), jnp.float32),
                        pltpu.VMEM((n_chain, SB_TILE, LANES), jnp.float32),
                        pltpu.VMEM((n_chain, 2 * SB_TILE, 1), jnp.float32)],
        compiler_params=pltpu.CompilerParams(
            dimension_semantics=("arbitrary",), vmem_limit_bytes=VMEM_LIMIT_BYTES),
        name="layer",
    )(x2, mem, g_in, w_in, conv_w, conv_b, g_memin, w_kv, g_sb, g_conv, g_mem, w_out, g_final)[0]


def kernel(x, mem, g_in, w_in, conv_w, conv_b, g_mem, w_mem_kv, g_sb_out, g_conv_out,
           g_mem_out, w_out, g_final):
    batch, t, d = x.shape
    assert batch == 1 and g_in.shape[0] == 1
    out = _layer(x.reshape(t, d), mem, g_in[0][None, :], w_in, jnp.transpose(conv_w, (1, 0, 2)), conv_b[0][None, :],
                 g_mem[0][None, :], w_mem_kv, g_sb_out[0][None, :], g_conv_out[0][None, :],
                 g_mem_out[0][None, :], w_out, g_final[None, :])
    return out.reshape(batch, t, d)
```

```python
import jax
import jax.numpy as jnp
from jax import lax
from jax.experimental import pallas as pl
from jax.experimental.pallas import tpu as pltpu

HEAD_DIM = 64
SB_HEADS = 8
SB_WIDTH = SB_HEADS * HEAD_DIM
CONV_WIDTH = 4 * HEAD_DIM
MEM_HEADS = 4
MEM_WIDTH = MEM_HEADS * HEAD_DIM
CONV_K = 3
EPS = 1e-6
SCALE = HEAD_DIM ** -0.5

LANES = 128
SUBLANES = 8
MIX_WIDTH = SB_WIDTH + CONV_WIDTH + MEM_WIDTH
PROJ_COLS = 512
Q_BLOCK = 512
SB_TILE = 128
SWEEP_STOP = 104.0
TILE2_ROWS = 32
KV_PAD = Q_BLOCK
KV_PREV = 2 * SB_TILE
LOG2_E = 1.4426950408889634
LOGIT2_CLAMP = 126.0
STREAM_ROWS = 256
STREAM_COLS = 512
STREAM_EXTRA_STAGES = 4
VMEM_LIMIT_BYTES = 62 * 1024 * 1024


def _rms_scale(xf, eps=EPS):
    return lax.rsqrt(jnp.mean(xf * xf, axis=-1, keepdims=True) + eps)


def _dot_nt(a, b):
    return lax.dot_general(a, b, (((1,), (1,)), ((), ())), preferred_element_type=jnp.float32)


def _split_heads(a):
    first_half = lax.broadcasted_iota(jnp.int32, a.shape, 1) < HEAD_DIM
    zero = jnp.zeros_like(a)
    return jnp.concatenate([jnp.where(first_half, a, zero), jnp.where(first_half, zero, a)], axis=0)


def _sb_sweep(i, q_ref, kwin_ref, vwin_ref, kv_hbm, ktile_ref, vtile_ref, tile_sems, acc_ref, carry_ref, fillers,
              before_loop):
    n_sub = q_ref.shape[0] // SB_TILE
    n_pair = SB_WIDTH // LANES
    chains = [(sub, p) for sub in range(n_sub) for p in range(n_pair)]
    fillers = list(fillers)

    rt = lax.broadcasted_iota(jnp.int32, (2 * SB_TILE, 2 * SB_TILE), 0)
    ct = lax.broadcasted_iota(jnp.int32, (2 * SB_TILE, 2 * SB_TILE), 1)
    top, left = rt < SB_TILE, ct < SB_TILE
    tri2 = jnp.where(top, rt, rt - SB_TILE) >= jnp.where(left, ct, ct - SB_TILE)
    r2 = lax.broadcasted_iota(jnp.int32, (2 * SB_TILE, SB_TILE), 0)
    c2 = lax.broadcasted_iota(jnp.int32, (2 * SB_TILE, SB_TILE), 1)
    causal = c2 < jnp.where(r2 < SB_TILE, r2, r2 - SB_TILE)

    def mask_diagonal(a):
        return jnp.concatenate([a[:, :SB_TILE], jnp.where(causal, a[:, SB_TILE:], 0.0)], axis=1)

    def as_matrix(cond):
        return jnp.where(cond, 1.0, 0.0).astype(jnp.bfloat16)

    w_10 = as_matrix(jnp.logical_or(jnp.logical_and(jnp.logical_not(top), left),
                                    jnp.logical_and(tri2, jnp.logical_not(jnp.logical_xor(top, left)))))
    w_2 = as_matrix(jnp.logical_or(top, jnp.logical_or(tri2, jnp.logical_not(left))))
    w_1 = w_2[SB_TILE:]

    q_stack = [_split_heads(q_ref[sub * SB_TILE:(sub + 1) * SB_TILE, p * LANES:(p + 1) * LANES])
               for sub, p in chains]

    def window_rows(c, m, n_tiles=1):
        start = KV_PREV + (chains[c][0] - (m + n_tiles - 1)) * SB_TILE
        assert start >= 0
        return slice(start, start + n_tiles * SB_TILE)

    def head_rows(a, n):
        return jnp.concatenate([a[:n], a[SB_TILE:SB_TILE + n]], axis=0)

    def logits(q, keys):
        return jnp.minimum(_dot_nt(q, keys), LOGIT2_CLAMP)

    def window_keys(c, rows):
        p = chains[c][1]
        return kwin_ref[rows, p * LANES:(p + 1) * LANES]

    def window_values(c, rows):
        p = chains[c][1]
        return _split_heads(vwin_ref[rows, p * LANES:(p + 1) * LANES])

    def neg_log2_not_beta(z):
        return jnp.log(1.0 + jnp.exp2(z)) * LOG2_E

    def weights(z, s, diagonal=False):
        w = jnp.exp2(z - s)
        if diagonal:
            w = mask_diagonal(w)
        wb = w.astype(jnp.bfloat16)
        n = wb.shape[0] // 2
        return jnp.concatenate([wb[:n], wb[n:]], axis=1)

    def all_done(carries):
        floor = carries[0]
        for carry in carries[1:]:
            floor = jnp.minimum(floor, carry)
        return jnp.min(floor) >= SWEEP_STOP * LOG2_E

    def first_sweep():
        n = TILE2_ROWS
        rows_10 = [window_rows(c, 0, 2) for c in range(len(chains))]
        rows_2 = [window_rows(c, 2) for c in range(len(chains))]
        zs = [(logits(q_stack[c], window_keys(c, rows_10[c])),
               logits(head_rows(q_stack[c], n), window_keys(c, rows_2[c]))) for c in range(len(chains))]
        if fillers:
            fillers.pop(0)()
        lhs_10, lhs_2 = [], []
        for z10, z2 in zs:
            l10 = mask_diagonal(neg_log2_not_beta(z10))
            lhs_10.append(l10.astype(jnp.bfloat16))
            lhs_2.append(jnp.concatenate(
                [(head_rows(l10[:, :SB_TILE], n) + head_rows(l10[:, SB_TILE:], n)).astype(jnp.bfloat16),
                 neg_log2_not_beta(z2).astype(jnp.bfloat16)], axis=1))
        s_10 = jnp.dot(jnp.concatenate(lhs_10, axis=0), w_10, preferred_element_type=jnp.float32)
        s_2 = jnp.dot(jnp.concatenate(lhs_2, axis=0), w_2, preferred_element_type=jnp.float32)
        if fillers:
            fillers.pop(0)()
        reached, w_10s, w_2s = [], [], []
        for c, (z10, z2) in enumerate(zs):
            part = slice(c * 2 * SB_TILE, (c + 1) * 2 * SB_TILE)
            part2 = slice(c * 2 * n, (c + 1) * 2 * n)
            w_10s.append(weights(z10, s_10[part], diagonal=True))
            w_2s.append(weights(z2, s_2[part2, :SB_TILE]))
            after_1 = s_10[part, 0:1]
            after_2 = s_2[part2, SB_TILE:SB_TILE + 1]
            carry_ref[c] = after_1
            reached.append(jnp.concatenate([after_2[:n], after_1[n:SB_TILE],
                                            after_2[n:], after_1[SB_TILE + n:]], axis=0))
        done = all_done(reached)
        for c in range(len(chains)):
            acc = jnp.dot(w_10s[c], window_values(c, rows_10[c]), preferred_element_type=jnp.float32)
            tile_2 = jnp.dot(w_2s[c], window_values(c, rows_2[c]), preferred_element_type=jnp.float32)
            acc_ref[c, n:, :] = acc[n:]
            acc_ref[c, :n, :] = acc[:n] + jnp.where(done, tile_2, 0.0)
        return 2, done

    def tile_copies(m):
        copies = []
        for c, (sub, p) in enumerate(chains):
            row0 = pl.multiple_of(KV_PAD + (n_sub * i + sub - m) * SB_TILE, SB_TILE)
            for which, dst in ((0, ktile_ref), (1, vtile_ref)):
                copies.append(pltpu.make_async_copy(
                    kv_hbm.at[which, pl.ds(row0, SB_TILE), pl.ds(p * LANES, LANES)], dst.at[c],
                    tile_sems.at[which, c]))
        return copies

    def next_sweep(m):
        for copy in tile_copies(m):
            copy.start()
        for copy in tile_copies(m):
            copy.wait()
        zs = [logits(q_stack[c], ktile_ref[c]) for c in range(len(chains))]
        ls = [neg_log2_not_beta(z).astype(jnp.bfloat16) for z in zs]
        cs = jnp.dot(jnp.concatenate(ls, axis=0), w_1, preferred_element_type=jnp.float32)
        carries, w_cats = [], []
        for c, z in enumerate(zs):
            part = slice(c * 2 * SB_TILE, (c + 1) * 2 * SB_TILE)
            old = carry_ref[c]
            w_cats.append(weights(z, cs[part, :SB_TILE] + old))
            carries.append(cs[part, SB_TILE:SB_TILE + 1] + old)
            carry_ref[c] = carries[c]
        for c in range(len(chains)):
            acc_ref[c] += jnp.dot(w_cats[c], _split_heads(vtile_ref[c]), preferred_element_type=jnp.float32)
        return all_done(carries)

    m0, done0 = first_sweep()
    before_loop()
    last = n_sub * i + n_sub - 1

    def cond(state):
        m, done = state
        return jnp.logical_and(m <= last, jnp.logical_not(done))

    def body(state):
        m, _ = state
        return m + 1, next_sweep(m)

    lax.while_loop(cond, body, (jnp.int32(m0), done0))
    return jnp.concatenate(
        [jnp.concatenate([acc_ref[sub * n_pair + p] for p in range(n_pair)], axis=1)
         for sub in range(n_sub)], axis=0)


def _mem_attention(q_ref, mkv_ref, filler):
    q_rows = q_ref.shape[0]
    n_mem = mkv_ref.shape[0]
    n_pair = MEM_WIDTH // LANES
    head_ones = ((lax.broadcasted_iota(jnp.int32, (2 * n_mem, LANES), 1) < HEAD_DIM)
                 == (lax.broadcasted_iota(jnp.int32, (2 * n_mem, LANES), 0) < n_mem)
                 ).astype(jnp.float32).astype(jnp.bfloat16)
    logits = [_dot_nt(_split_heads(q_ref[:, SB_WIDTH + p * LANES:SB_WIDTH + (p + 1) * LANES]),
                      mkv_ref[:, p * LANES:(p + 1) * LANES]) for p in range(n_pair)]
    filler()
    outs = []
    for p, s in enumerate(logits):
        mv = mkv_ref[:, MEM_WIDTH + p * LANES:MEM_WIDTH + (p + 1) * LANES]
        e = jnp.exp(s - jnp.max(s, axis=-1, keepdims=True)).astype(jnp.bfloat16)
        e_cat = jnp.concatenate([e[:q_rows], e[q_rows:]], axis=1)
        num_den = jnp.dot(e_cat, jnp.concatenate([_split_heads(mv), head_ones], axis=1),
                          preferred_element_type=jnp.float32)
        outs.append(num_den[:, :LANES] / num_den[:, LANES:])
    return jnp.concatenate(outs, axis=-1)


def _stream_chunks(jobs, stages, sems):
    depth = len(stages)

    def copy(j):
        return pltpu.make_async_copy(jobs[j][0], stages[j % depth], sems.at[j % depth])

    for j in range(min(depth - 1, len(jobs))):
        copy(j).start(priority=j % 2)
    for j, (_, consume) in enumerate(jobs):
        if j + depth - 1 < len(jobs):
            copy(j + depth - 1).start(priority=(j + depth - 1) % 2)
        copy(j).wait()
        consume(stages[j % depth])


def _chunks(hbm):
    _, n_rows, n_cols = hbm.shape
    return [(r0, c0, hbm.at[0, pl.ds(r0, STREAM_ROWS), pl.ds(c0, STREAM_COLS)])
            for r0 in range(0, n_rows, STREAM_ROWS) for c0 in range(0, n_cols, STREAM_COLS)]


def _load_parameters(mem_hbm, g_mem_ref, w_kv_hbm, w_in_hbm, w_in_groups, w_out_hbm, mkv_ref, w_in_ref, w_out_ref,
                     stages, sems):
    jobs = []
    state = {"mem": [], "h": None, "kv": None}

    def take_mem(window):
        state["mem"].append(window[...])

    def memory_kv(window, r0):
        if state["h"] is None:
            m = jnp.concatenate(state["mem"], axis=1)
            state["h"] = (m * _rms_scale(m) * g_mem_ref[...]).astype(jnp.bfloat16)
        part = jnp.dot(state["h"][:, r0:r0 + STREAM_ROWS], window[...].astype(jnp.bfloat16),
                       preferred_element_type=jnp.float32)
        state["kv"] = part if state["kv"] is None else state["kv"] + part
        if r0 + STREAM_ROWS == w_kv_hbm.shape[1]:
            mkv_ref[...] = state["kv"].astype(jnp.bfloat16)

    def cast_into(dst, r0, c0, then=None):
        def consume(window):
            dst[r0:r0 + STREAM_ROWS, c0:c0 + STREAM_COLS] = window[...].astype(jnp.bfloat16)
            if then is not None:
                then()
        return consume

    assert mem_hbm.shape[1] == STREAM_ROWS and w_kv_hbm.shape[2] == STREAM_COLS
    jobs += [(chunk, take_mem) for _, _, chunk in _chunks(mem_hbm)]
    jobs += [(chunk, lambda window, r0=r0: memory_kv(window, r0)) for r0, _, chunk in _chunks(w_kv_hbm)]
    assert sorted(c0 for columns, _ in w_in_groups for c0 in columns) == list(range(0, w_in_hbm.shape[2], STREAM_COLS))
    for columns, then in w_in_groups:
        group = [job for job in _chunks(w_in_hbm) if job[1] in columns]
        jobs += [(chunk, cast_into(w_in_ref, r0, c0, then if n + 1 == len(group) else None))
                 for n, (r0, c0, chunk) in enumerate(group)]
    jobs += [(chunk, cast_into(w_out_ref, r0, c0)) for r0, c0, chunk in _chunks(w_out_hbm)]
    _stream_chunks(jobs, stages, sems)


def _layer_kernel(x_ref, x_res_ref, mem_hbm, g_in_ref, w_in_hbm, convw_ref, convb_ref, g_memin_ref, w_kv_hbm,
                  g_sb_ref, g_conv_ref, g_mem_ref, w_out_hbm, g_final_ref,
                  out_ref, kv_hbm, w_in_ref, w_out_ref, mkv_ref, stream_sems, stage_ref, kwin_ref, vwin_ref, kv_next_ref,
                  ktile_ref, vtile_ref, kv_sems, tile_sems, q2_ref, rest2_ref, res_ref, cu_ref,
                  acc_ref, carry_ref):
    s = pl.program_id(0)
    n_blk = pl.num_programs(0) - 2
    q_rows = x_ref.shape[0]
    c_u = 3 * SB_WIDTH
    c_qm = c_u + 3 * CONV_WIDTH
    c_gate = c_qm + MEM_WIDTH

    def normed_input():
        x = x_ref[...]
        return (x * _rms_scale(x) * g_in_ref[...]).astype(jnp.bfloat16)

    def proj(h, c0, width):
        return jnp.dot(h, w_in_ref[:, c0:c0 + width], preferred_element_type=jnp.float32)

    def project_kv(h):
        kv_next_ref[...] = proj(h, SB_WIDTH, 2 * SB_WIDTH).astype(jnp.bfloat16)

    def project_q_conv(h, slot):
        q2_ref[slot, :, :SB_WIDTH] = (proj(h, 0, SB_WIDTH) * (SCALE * LOG2_E)).astype(jnp.bfloat16)
        q2_ref[slot, :, SB_WIDTH:] = (proj(h, c_qm, MEM_WIDTH) * SCALE).astype(jnp.bfloat16)
        rest2_ref[slot, :, :3 * CONV_WIDTH] = proj(h, c_u, 3 * CONV_WIDTH)

    def project_gate(h, slot):
        for c0 in range(0, MIX_WIDTH, PROJ_COLS):
            rest2_ref[slot, :, 3 * CONV_WIDTH + c0:3 * CONV_WIDTH + c0 + PROJ_COLS] = proj(h, c_gate + c0, PROJ_COLS)

    def final_norm():
        res = res_ref[...]
        out_ref[...] = res * _rms_scale(res) * g_final_ref[...]

    @pl.when(s == 0)
    def _():
        stages = [ref.at[pl.ds(r0, STREAM_ROWS), pl.ds(c0, STREAM_COLS)] for ref in (res_ref, rest2_ref.at[1])
                  for r0 in range(0, ref.shape[0] - STREAM_ROWS + 1, STREAM_ROWS)
                  for c0 in range(0, ref.shape[-1] - STREAM_COLS + 1, STREAM_COLS)]
        stages += [stage_ref.at[k] for k in range(stage_ref.shape[0])]
        assert len(stages) == stream_sems.shape[0]
        h = normed_input()
        w_in_groups = [(range(SB_WIDTH, 3 * SB_WIDTH, STREAM_COLS), lambda: project_kv(h)),
                       ([0, *range(c_u, c_gate, STREAM_COLS)], lambda: project_q_conv(h, 0)),
                       (range(c_gate, c_gate + MIX_WIDTH, STREAM_COLS), lambda: project_gate(h, 0))]
        _load_parameters(mem_hbm, g_memin_ref, w_kv_hbm, w_in_hbm, w_in_groups, w_out_hbm, mkv_ref, w_in_ref,
                         w_out_ref, stages, stream_sems)
        res_ref[...] = jnp.zeros(res_ref.shape, jnp.float32)
        cu_ref[0:SUBLANES, :] = jnp.zeros((SUBLANES, CONV_WIDTH), jnp.float32)
        kwin_ref[...] = jnp.zeros(kwin_ref.shape, jnp.bfloat16)
        vwin_ref[...] = jnp.zeros(vwin_ref.shape, jnp.bfloat16)
        zero_rows = [pltpu.make_async_copy(win.at[pl.ds(0, KV_PAD)], kv_hbm.at[which, pl.ds(0, KV_PAD)],
                                           kv_sems.at[which])
                     for which, win in ((0, kwin_ref), (1, vwin_ref))]
        for copy in zero_rows:
            copy.start()
        for copy in zero_rows:
            copy.wait()

    @pl.when(s == n_blk + 1)
    def _():
        final_norm()

    def mix_block(project_next):
        i = s - 1
        cur = lax.rem(i, 2)
        nxt = 1 - cur
        q_ref = q2_ref.at[cur]
        rest_ref = rest2_ref.at[cur]

        def history_writes(block):
            rows = pl.ds(pl.multiple_of(KV_PAD + block * q_rows, q_rows), q_rows)
            return [pltpu.make_async_copy(win.at[pl.ds(KV_PREV, q_rows)], kv_hbm.at[which, rows],
                                          kv_sems.at[which])
                    for which, win in ((0, kwin_ref), (1, vwin_ref))]

        for win, c0 in ((kwin_ref, 0), (vwin_ref, SB_WIDTH)):
            win[0:KV_PREV, :] = win[q_rows:q_rows + KV_PREV, :]
            win[KV_PREV:, :] = kv_next_ref[:, c0:c0 + SB_WIDTH]
        for copy in history_writes(i):
            copy.start()
        final_norm()

        h_next = normed_input() if project_next else None
        y_sb = _sb_sweep(i, q_ref, kwin_ref, vwin_ref, kv_hbm, ktile_ref, vtile_ref, tile_sems, acc_ref, carry_ref,
                         (lambda: project_kv(h_next), lambda: project_q_conv(h_next, nxt)) if project_next else (),
                         lambda: [copy.wait() for copy in history_writes(i)])

        u = rest_ref[:, 0:CONV_WIDTH]
        b = rest_ref[:, CONV_WIDTH:2 * CONV_WIDTH]
        c = rest_ref[:, 2 * CONV_WIDTH:3 * CONV_WIDTH]
        cu = c * u
        cu_ref[SUBLANES:, :] = cu
        cu_1 = cu_ref[SUBLANES - 1:SUBLANES - 1 + q_rows, :]
        cu_2 = cu_ref[SUBLANES - 2:SUBLANES - 2 + q_rows, :]
        cu_ref[0:SUBLANES, :] = cu[q_rows - SUBLANES:, :]
        conv = (convw_ref[0] * cu_2 + convw_ref[1] * cu_1 + convw_ref[2] * cu
                + convb_ref[...])
        y_conv = b * conv

        y_mem = _mem_attention(q_ref, mkv_ref,
                               (lambda: project_gate(h_next, nxt)) if project_next else (lambda: None))

        y = jnp.concatenate([y_sb * _rms_scale(y_sb) * g_sb_ref[...],
                             y_conv * _rms_scale(y_conv) * g_conv_ref[...],
                             y_mem * _rms_scale(y_mem) * g_mem_ref[...]], axis=-1)
        gate = rest_ref[:, 3 * CONV_WIDTH:]
        half_gate = 0.5 * gate
        gated = (y * (half_gate + half_gate * jnp.tanh(half_gate))).astype(jnp.bfloat16)
        res_ref[...] = x_res_ref[...] + jnp.dot(gated, w_out_ref[...], preferred_element_type=jnp.float32)

    @pl.when(jnp.logical_and(s > 0, s < n_blk))
    def _():
        mix_block(True)

    @pl.when(s == n_blk)
    def _():
        mix_block(False)


def _layer(x2, mem, g_in, w_in, conv_w, conv_b, g_memin, w_kv, g_sb, g_conv, g_mem, w_out, g_final):
    t, d = x2.shape
    assert t % Q_BLOCK == 0 and Q_BLOCK % SB_TILE == 0 and KV_PREV <= Q_BLOCK
    assert Q_BLOCK >= STREAM_ROWS
    for a in (mem, w_kv, w_in, w_out):
        assert a.shape[1] % STREAM_ROWS == 0 and a.shape[2] % STREAM_COLS == 0
    n_chain = (Q_BLOCK // SB_TILE) * (SB_WIDTH // LANES)
    n_idle_stages = (Q_BLOCK // STREAM_ROWS) * (d // STREAM_COLS + (3 * CONV_WIDTH + MIX_WIDTH) // STREAM_COLS)

    def whole(a):
        return pl.BlockSpec(a.shape, lambda s: (0,) * a.ndim)

    in_hbm = pl.BlockSpec(memory_space=pl.ANY)
    n_blk = t // Q_BLOCK
    return pl.pallas_call(
        _layer_kernel,
        grid=(n_blk + 2,),
        in_specs=[pl.BlockSpec((Q_BLOCK, d), lambda s: (jnp.minimum(s, n_blk - 1), 0)),
                  pl.BlockSpec((Q_BLOCK, d), lambda s: (jnp.clip(s - 1, 0, n_blk - 1), 0)),
                  in_hbm, whole(g_in),
                  in_hbm, whole(conv_w), whole(conv_b), whole(g_memin), in_hbm, whole(g_sb), whole(g_conv),
                  whole(g_mem), in_hbm, whole(g_final)],
        out_specs=[pl.BlockSpec((Q_BLOCK, d), lambda s: (jnp.clip(s - 2, 0, n_blk - 1), 0)), in_hbm],
        out_shape=[jax.ShapeDtypeStruct((t, d), jnp.float32),
                   jax.ShapeDtypeStruct((2, KV_PAD + t, SB_WIDTH), jnp.bfloat16)],
        scratch_shapes=[pltpu.VMEM(w_in.shape[1:], jnp.bfloat16),
                        pltpu.VMEM(w_out.shape[1:], jnp.bfloat16),
                        pltpu.VMEM((mem.shape[1], w_kv.shape[2]), jnp.bfloat16),
                        pltpu.SemaphoreType.DMA((n_idle_stages + STREAM_EXTRA_STAGES,)),
                        pltpu.VMEM((STREAM_EXTRA_STAGES, STREAM_ROWS, STREAM_COLS), jnp.float32),
                        pltpu.VMEM((KV_PREV + Q_BLOCK, SB_WIDTH), jnp.bfloat16),
                        pltpu.VMEM((KV_PREV + Q_BLOCK, SB_WIDTH), jnp.bfloat16),
                        pltpu.VMEM((Q_BLOCK, 2 * SB_WIDTH), jnp.bfloat16),
                        pltpu.VMEM((n_chain, SB_TILE, LANES), jnp.bfloat16),
                        pltpu.VMEM((n_chain, SB_TILE, LANES), jnp.bfloat16),
                        pltpu.SemaphoreType.DMA((2,)),
                        pltpu.SemaphoreType.DMA((2, n_chain)),
                        pltpu.VMEM((2, Q_BLOCK, SB_WIDTH + MEM_WIDTH), jnp.bfloat16),
                        pltpu.VMEM((2, Q_BLOCK, 3 * CONV_WIDTH + MIX_WIDTH), jnp.float32),
                        pltpu.VMEM((Q_BLOCK, d), jnp.float32),
                        pltpu.VMEM((SUBLANES + Q_BLOCK, CONV_WIDTH), jnp.float32),
                        pltpu.VMEM((n_chain, SB_TILE, LANES), jnp.float32),
                        pltpu.VMEM((n_chain, 2 * SB_TILE, 1), jnp.float32)],
        compiler_params=pltpu.CompilerParams(
            dimension_semantics=("arbitrary",), vmem_limit_bytes=VMEM_LIMIT_BYTES),
        name="layer",
    )(x2, x2, mem, g_in, w_in, conv_w, conv_b, g_memin, w_kv, g_sb, g_conv, g_mem, w_out, g_final)[0]


def kernel(x, mem, g_in, w_in, conv_w, conv_b, g_mem, w_mem_kv, g_sb_out, g_conv_out,
           g_mem_out, w_out, g_final):
    batch, t, d = x.shape
    assert batch == 1 and g_in.shape[0] == 1
    out = _layer(x.reshape(t, d), mem, g_in[0][None, :], w_in, jnp.transpose(conv_w, (1, 0, 2)), conv_b[0][None, :],
                 g_mem[0][None, :], w_mem_kv, g_sb_out[0][None, :], g_conv_out[0][None, :],
                 g_mem_out[0][None, :], w_out, g_final[None, :])
    return out.reshape(batch, t, d)
```
